```python
import jax
import jax.numpy as jnp
from jax import lax
import numpy as np

D_MODEL = 1024
BATCH = 8
SEQ = 4096
DEPTH = 2

EPS = 1e-6
D_PLE = 256

GLA_HEADS = 4
GLA_DK = 32
GLA_DV = 64
GLA_GATE_RANK = 16
GLA_TAU = 16.0
GLA_CHUNK = 64
GLA_W = GLA_HEADS * GLA_DV

MLA_HEADS = 8
MLA_Q_RANK = 256
MLA_KV_RANK = 128
MLA_NOPE = 64
MLA_ROPE = 32
MLA_QK = MLA_NOPE + MLA_ROPE
MLA_V = 64
MLA_W = MLA_HEADS * MLA_V
ROPE_BASE = 10000.0
Q_BLOCK = 128

POOL_WINDOWS = (2, 4, 8, 16)
N_POOL = 4
POOL_GROUP = 64
POOL_W = N_POOL * POOL_GROUP

D_MIX = GLA_W + MLA_W + POOL_W

IN_SPLITS = (GLA_HEADS * GLA_DK, GLA_HEADS * GLA_DK, GLA_W, GLA_GATE_RANK, GLA_W,
             MLA_Q_RANK, MLA_KV_RANK, MLA_ROPE, POOL_W)
D_IN = 1456

N_EXPERTS = 32
TOP_K = 4
D_FF = 1024
SWIGLU_LIMIT = 7.0
SWIGLU_ALPHA = 1.702
MOE_BLOCK = 256

kernel_name = 'hybrid_gla_mla_pool_moe_ple'


def rms_norm(x, g):
    xf = x.astype(jnp.float32)
    y = xf * lax.rsqrt(jnp.mean(xf * xf, axis=-1, keepdims=True) + EPS)
    return (y * g.astype(jnp.float32)).astype(x.dtype)


def split_cols(z, sizes):
    return jnp.split(z, list(np.cumsum(sizes)[:-1]), axis=-1)


def rope_tables(positions):
    inv = ROPE_BASE ** (-jnp.arange(0, MLA_ROPE, 2, dtype=jnp.float32) / MLA_ROPE)
    ang = positions.astype(jnp.float32)[..., None] * inv
    return jnp.cos(ang), jnp.sin(ang)


def apply_rope(x, cos, sin):
    x1, x2 = jnp.split(x, 2, axis=-1)
    return jnp.concatenate([x1 * cos - x2 * sin, x2 * cos + x1 * sin], axis=-1)


def gla_mixer(q, k, v, g_low, r, w_gate, b_gate, out_norm):
    B, S, _ = q.shape
    H, C, DK, DV = GLA_HEADS, GLA_CHUNK, GLA_DK, GLA_DV
    N = S // C
    logit = (g_low @ w_gate + b_gate).astype(jnp.float32)
    log_a = jax.nn.log_sigmoid(logit) / GLA_TAU

    def chunked(t, d):
        return t.astype(jnp.float32).reshape(B, N, C, H, d).transpose(0, 3, 1, 2, 4)

    qc = chunked(q, DK) * (DK ** -0.5)
    kc = chunked(k, DK)
    vc = chunked(v, DV)
    bc = jnp.cumsum(chunked(log_a, DK), axis=3)
    b_last = bc[:, :, :, -1:, :]
    q_dec = qc * jnp.exp(bc)
    k_dec = kc * jnp.exp(-bc)
    k_end = kc * jnp.exp(b_last - bc)
    causal = jnp.tril(jnp.ones((C, C), dtype=bool))
    att = jnp.where(causal, jnp.einsum('bhnik,bhnjk->bhnij', q_dec, k_dec), 0.0)
    o_intra = jnp.einsum('bhnij,bhnjv->bhniv', att, vc)
    upd = jnp.einsum('bhnjk,bhnjv->bhnkv', k_end, vc)
    decay = jnp.exp(b_last[:, :, :, 0, :])

    def step(state, inp):
        d, u = inp
        return d[..., None] * state + u, state

    s0 = jnp.zeros((B, H, DK, DV), jnp.float32)
    _, s_prev = lax.scan(step, s0, (jnp.moveaxis(decay, 2, 0), jnp.moveaxis(upd, 2, 0)))
    s_prev = jnp.moveaxis(s_prev, 0, 2)
    o_inter = jnp.einsum('bhnik,bhnkv->bhniv', q_dec, s_prev)
    o = (o_intra + o_inter).transpose(0, 2, 3, 1, 4).reshape(B, S, H, DV)
    o = rms_norm(o, out_norm) * jax.nn.silu(r.astype(jnp.float32).reshape(B, S, H, DV))
    return o.reshape(B, S, GLA_W).astype(q.dtype)


def mla_mixer(c_q, c_kv, k_rope_raw, cos, sin, q_norm, w_uq, kv_norm, w_ukv, qk_q_norm, qk_k_norm):
    B, S, _ = c_q.shape
    H = MLA_HEADS
    q = (rms_norm(c_q, q_norm) @ w_uq).reshape(B, S, H, MLA_QK)
    kv = (rms_norm(c_kv, kv_norm) @ w_ukv).reshape(B, S, H, MLA_NOPE + MLA_V)
    k_nope, v = kv[..., :MLA_NOPE], kv[..., MLA_NOPE:]
    k_r = jnp.broadcast_to(k_rope_raw[:, :, None, :], (B, S, H, MLA_ROPE))
    k = jnp.concatenate([k_nope, k_r], axis=-1)
    q = rms_norm(q, qk_q_norm)
    k = rms_norm(k, qk_k_norm)
    cos_h, sin_h = cos[:, :, None, :], sin[:, :, None, :]
    q = jnp.concatenate([q[..., :MLA_NOPE], apply_rope(q[..., MLA_NOPE:], cos_h, sin_h)], axis=-1)
    k = jnp.concatenate([k[..., :MLA_NOPE], apply_rope(k[..., MLA_NOPE:], cos_h, sin_h)], axis=-1)
    q = q.transpose(0, 2, 1, 3)
    k = k.transpose(0, 2, 1, 3)
    v = v.transpose(0, 2, 1, 3)
    scale = MLA_QK ** -0.5
    key_pos = jnp.arange(S)

    def block(i):
        start = i * Q_BLOCK
        qb = lax.dynamic_slice_in_dim(q, start, Q_BLOCK, axis=2)
        s = jnp.einsum('bhqd,bhkd->bhqk', qb, k).astype(jnp.float32) * scale
        q_pos = start + jnp.arange(Q_BLOCK)
        s = jnp.where(key_pos[None, :] <= q_pos[:, None], s, -jnp.inf)
        pr = jax.nn.softmax(s, axis=-1)
        return jnp.einsum('bhqk,bhkd->bhqd', pr.astype(v.dtype), v)

    o = lax.map(block, jnp.arange(S // Q_BLOCK))
    return o.transpose(1, 0, 3, 2, 4).reshape(B, S, MLA_W)


def pool_mixer(u, w_pool, scale):
    B, S, _ = u.shape
    uf = u.astype(jnp.float32).reshape(B, S, N_POOL, POOL_GROUP)
    csum = jnp.cumsum(uf, axis=1)
    t = jnp.arange(S)
    pooled = []
    for gi, w in enumerate(POOL_WINDOWS):
        cg = csum[:, :, gi]
        lagged = jnp.pad(cg, ((0, 0), (w, 0), (0, 0)))[:, :S]
        cnt = jnp.minimum(t + 1, w).astype(jnp.float32)[None, :, None]
        pooled.append((cg - lagged) / cnt)
    pooled = jnp.stack(pooled, axis=2)
    y = jnp.einsum('bsgc,gcd->bsgd', pooled - uf, w_pool.astype(jnp.float32))
    return (y.reshape(B, S, POOL_W) * scale).astype(u.dtype)


def moe_ffn(xn, router_w, router_b, w_gate, b_gate, w_up, b_up, w_down, b_down):
    B, S, D = xn.shape
    T = B * S
    A = T * TOP_K
    x2 = xn.reshape(T, D)
    logits = (x2 @ router_w + router_b).astype(jnp.float32)
    top_val, top_idx = lax.top_k(logits, TOP_K)
    gates = jax.nn.softmax(top_val, axis=-1)
    flat_e = top_idx.reshape(A)
    flat_tok = jnp.arange(A, dtype=jnp.int32) // TOP_K
    flat_gate = gates.reshape(A)
    order = jnp.argsort(flat_e)
    sorted_e = flat_e[order]
    counts = jnp.bincount(flat_e, length=N_EXPERTS)
    padded = (counts + MOE_BLOCK - 1) // MOE_BLOCK * MOE_BLOCK
    pad_end = jnp.cumsum(padded)
    pad_start = pad_end - padded
    start = jnp.cumsum(counts) - counts
    dest = pad_start[sorted_e] + jnp.arange(A) - start[sorted_e]
    n_blocks = (A + N_EXPERTS * (MOE_BLOCK - 1) + MOE_BLOCK - 1) // MOE_BLOCK
    n_rows = n_blocks * MOE_BLOCK
    row_tok = jnp.full((n_rows,), T, jnp.int32).at[dest].set(flat_tok[order])
    row_gate = jnp.zeros((n_rows,), jnp.float32).at[dest].set(flat_gate[order])
    block_e = jnp.minimum(jnp.searchsorted(pad_end, jnp.arange(n_blocks) * MOE_BLOCK, side='right'),
                          N_EXPERTS - 1)
    x_pad = jnp.concatenate([x2, jnp.zeros((1, D), x2.dtype)], axis=0)
    xs = x_pad[row_tok].reshape(n_blocks, MOE_BLOCK, D)

    def run_block(args):
        xb, e = args
        g = jnp.minimum(xb @ w_gate[e] + b_gate[e], SWIGLU_LIMIT)
        up = jnp.clip(xb @ w_up[e] + b_up[e], -SWIGLU_LIMIT, SWIGLU_LIMIT)
        hb = (up + 1.0) * (g * jax.nn.sigmoid(SWIGLU_ALPHA * g))
        return hb @ w_down[e] + b_down[e]

    ys = lax.map(run_block, (xs, block_e)).reshape(n_rows, D)
    ys = ys * row_gate[:, None].astype(ys.dtype)
    out = jnp.zeros((T + 1, D), ys.dtype).at[row_tok].add(ys)[:T]
    return out.reshape(B, S, D)


def _normal(key, shape, scale):
    return jax.random.normal(key, shape, jnp.float32) * scale


def setup_inputs(seed: int = 0) -> dict:
    key = jax.random.key(seed)
    ks = list(jax.random.split(key, 32))
    L, D, E, F = DEPTH, D_MODEL, N_EXPERTS, D_FF

    def gain(k, n, s=0.02):
        return 1.0 + _normal(k, (L, n), s)

    offs = jax.random.randint(ks[2], (BATCH, 1), 0, 1024, dtype=jnp.int32)
    return {
        'x': _normal(ks[0], (BATCH, SEQ, D), 1.0),
        'p': _normal(ks[1], (DEPTH, BATCH, SEQ, D_PLE), 1.0),
        'positions': offs + jnp.arange(SEQ, dtype=jnp.int32)[None, :],
        'mix_norm': gain(ks[3], D),
        'w_in': _normal(ks[4], (L, D, D_IN), D ** -0.5),
        'gla_w_gate': _normal(ks[5], (L, GLA_GATE_RANK, GLA_HEADS * GLA_DK), GLA_GATE_RANK ** -0.5),
        'gla_b_gate': _normal(ks[6], (L, GLA_HEADS * GLA_DK), 0.02),
        'gla_out_norm': gain(ks[7], GLA_DV),
        'mla_q_norm': gain(ks[8], MLA_Q_RANK),
        'mla_w_uq': _normal(ks[9], (L, MLA_Q_RANK, MLA_HEADS * MLA_QK), MLA_Q_RANK ** -0.5),
        'mla_kv_norm': gain(ks[10], MLA_KV_RANK),
        'mla_w_ukv': _normal(ks[11], (L, MLA_KV_RANK, MLA_HEADS * (MLA_NOPE + MLA_V)), MLA_KV_RANK ** -0.5),
        'mla_qk_q_norm': gain(ks[12], MLA_QK),
        'mla_qk_k_norm': gain(ks[13], MLA_QK),
        'pool_w': _normal(ks[14], (L, N_POOL, POOL_GROUP, POOL_GROUP), POOL_GROUP ** -0.5),
        'pool_scale': gain(ks[15], POOL_W, 0.1),
        'w_out': _normal(ks[16], (L, D_MIX, D), D_MIX ** -0.5),
        'ffn_norm': gain(ks[17], D),
        'router_w': _normal(ks[18], (L, D, E), D ** -0.5),
        'router_b': _normal(ks[19], (L, E), 0.01),
        'moe_w_gate': _normal(ks[20], (L, E, D, F), D ** -0.5),
        'moe_b_gate': _normal(ks[21], (L, E, F), 0.02),
        'moe_w_up': _normal(ks[22], (L, E, D, F), D ** -0.5),
        'moe_b_up': _normal(ks[23], (L, E, F), 0.02),
        'moe_w_down': _normal(ks[24], (L, E, F, D), F ** -0.5),
        'moe_b_down': _normal(ks[25], (L, E, D), 0.02),
        'ple_w_proj': _normal(ks[26], (L, D_PLE, D), D_PLE ** -0.5),
        'ple_gate_norm': gain(ks[27], D),
        'ple_w_gate': _normal(ks[28], (L, D, D), D ** -0.5),
        'ple_post_norm': gain(ks[29], D),
    }


def reference(x, p, positions, mix_norm, w_in, gla_w_gate, gla_b_gate, gla_out_norm,
              mla_q_norm, mla_w_uq, mla_kv_norm, mla_w_ukv, mla_qk_q_norm, mla_qk_k_norm,
              pool_w, pool_scale, w_out, ffn_norm, router_w, router_b,
              moe_w_gate, moe_b_gate, moe_w_up, moe_b_up, moe_w_down, moe_b_down,
              ple_w_proj, ple_gate_norm, ple_w_gate, ple_post_norm):
    cos, sin = rope_tables(positions)
    h = x
    for i in range(DEPTH):
        hn = rms_norm(h, mix_norm[i])
        z = hn @ w_in[i]
        gq, gk, gv, g_low, g_r, c_q, c_kv, k_rope_raw, u_pool = split_cols(z, IN_SPLITS)
        y_gla = gla_mixer(gq, gk, gv, g_low, g_r, gla_w_gate[i], gla_b_gate[i], gla_out_norm[i])
        y_mla = mla_mixer(c_q, c_kv, k_rope_raw, cos, sin, mla_q_norm[i], mla_w_uq[i],
                          mla_kv_norm[i], mla_w_ukv[i], mla_qk_q_norm[i], mla_qk_k_norm[i])
        y_pool = pool_mixer(u_pool, pool_w[i], pool_scale[i])
        y_mix = jnp.concatenate([y_gla.astype(y_mla.dtype), y_mla, y_pool.astype(y_mla.dtype)], axis=-1)
        h = h + y_mix @ w_out[i]
        hn = rms_norm(h, ffn_norm[i])
        h = h + moe_ffn(hn, router_w[i], router_b[i], moe_w_gate[i], moe_b_gate[i],
                        moe_w_up[i], moe_b_up[i], moe_w_down[i], moe_b_down[i])
        e = p[i] @ ple_w_proj[i]
        gate = jax.nn.sigmoid(rms_norm(h, ple_gate_norm[i]) @ ple_w_gate[i])
        h = h + rms_norm(e * gate, ple_post_norm[i])
    return h
```

```python
import functools

import jax
import jax.numpy as jnp
import numpy as np
from jax import lax
from jax.experimental import pallas as pl
from jax.experimental.pallas import tpu as pltpu

F32 = jnp.float32
BF16 = jnp.bfloat16

D_MODEL = 1024
EPS = 1e-6
D_PLE = 256

GLA_HEADS = 4
GLA_DK = 32
GLA_DV = 64
GLA_GATE_RANK = 16
GLA_TAU = 16.0
GLA_CHUNK = 64
GLA_K = GLA_HEADS * GLA_DK
GLA_W = GLA_HEADS * GLA_DV

MLA_HEADS = 8
MLA_Q_RANK = 256
MLA_KV_RANK = 128
MLA_NOPE = 64
MLA_ROPE = 32
MLA_QK = MLA_NOPE + MLA_ROPE
MLA_V = 64
MLA_W = MLA_HEADS * MLA_V
ROPE_BASE = 10000.0
HEAD_PAD = 128
MLA_QK_PAD = MLA_HEADS * HEAD_PAD

POOL_WINDOWS = (2, 4, 8, 16)
POOL_GROUP = 64
POOL_W = 256
POOL_HALO = 16

N_EXPERTS = 32
TOP_K = 4
D_FF = 1024
SWIGLU_LIMIT = 7.0
SWIGLU_ALPHA = 1.702

COL_GQ, COL_GK, COL_GV, COL_GR, COL_CQ, COL_POOL, COL_CKV, COL_MISC = 0, 128, 256, 512, 768, 1024, 1280, 1408
D_IN_PAD = 1536
MISC_GLOW = 0
MISC_ROPE = 16

TOKEN_TILE = 512
GLA_TILE = 512
ATTN_TILE = 512
MOE_BLOCK = 256
VMEM_LIMIT = 56 * 1024 * 1024
NEG_BIG = -1e30


def _cparams(n_axes):
    return pltpu.CompilerParams(dimension_semantics=("arbitrary",) * n_axes,
                                vmem_limit_bytes=VMEM_LIMIT)


def _rms(x, g):
    return x * lax.rsqrt(jnp.mean(x * x, axis=-1, keepdims=True) + EPS) * g


def _dot(a, b):
    return jnp.dot(a, b, preferred_element_type=F32)


def _dot_nt(a, b):
    return lax.dot_general(a, b, (((1,), (1,)), ((), ())), preferred_element_type=F32)


def _dot_tn(a, b):
    return lax.dot_general(a, b, (((0,), (0,)), ((), ())), preferred_element_type=F32)


def _split3(x):
    hi = x.astype(BF16)
    r = x - hi.astype(F32)
    mid = r.astype(BF16)
    lo = (r - mid.astype(F32)).astype(BF16)
    return hi, mid, lo


def _split2(x):
    hi = x.astype(BF16)
    lo = (x - hi.astype(F32)).astype(BF16)
    return hi, lo


def _full(shape):
    nd = len(shape)
    return pl.BlockSpec(shape, lambda *_: (0,) * nd)


def _rope(x, c, s1, s2):
    return x * c + pltpu.roll(x, HEAD_PAD - 16, 1) * s1 + pltpu.roll(x, 16, 1) * s2


def _mix_pre_kernel(h_ref, mixn_ref, win_ref, wgate_ref, bgate_ref, qn_ref, wuq_ref, kvn_ref,
                    wukvk_ref, wukvv_ref, gq_ref, gk_ref, rc_ref, rs1_ref, rs2_ref,
                    wpool_ref, pscale_ref,
                    zg_ref, la_ref, q_ref, k_ref, v_ref, yp_ref, carry_ref, *, tiles_per_seq):
    tm = h_ref.shape[0]
    i = pl.program_id(0)
    hn = _rms(h_ref[...], mixn_ref[...]).astype(BF16)
    z = _dot(hn, win_ref[...])
    zg_ref[...] = z[:, COL_GQ:COL_CQ]
    zm = z[:, COL_MISC:COL_MISC + 128]

    logit = _dot(zm.astype(BF16), wgate_ref[...]) + bgate_ref[...]
    la_ref[...] = (jnp.minimum(logit, 0.0) - jnp.log(1.0 + jnp.exp(-jnp.abs(logit)))) * (1.0 / GLA_TAU)

    cqn = _rms(z[:, COL_CQ:COL_CQ + MLA_Q_RANK], qn_ref[...]).astype(BF16)
    qf = _dot(cqn, wuq_ref[...])
    ckvn = _rms(z[:, COL_CKV:COL_CKV + MLA_KV_RANK], kvn_ref[...]).astype(BF16)
    kn = _dot(ckvn, wukvk_ref[...])
    v_ref[...] = _dot(ckvn, wukvv_ref[...]).astype(BF16)

    rc, rs1, rs2 = rc_ref[...], rs1_ref[...], rs2_ref[...]
    lane = lax.broadcasted_iota(jnp.int32, (tm, HEAD_PAD), 1)
    in_rope = (lane >= MLA_NOPE) & (lane < MLA_QK)
    kr = jnp.where(in_rope, pltpu.roll(zm, MLA_NOPE - MISC_ROPE, 1), 0.0)
    kr_ss = jnp.sum(kr * kr, axis=-1, keepdims=True)
    gq, gk = gq_ref[...], gk_ref[...]
    krr = _rope(kr * gk, rc, rs1, rs2)
    q_scale = MLA_QK ** -0.5
    for hh in range(MLA_HEADS):
        sl = slice(hh * HEAD_PAD, (hh + 1) * HEAD_PAD)
        qh = qf[:, sl]
        sq = lax.rsqrt(jnp.sum(qh * qh, axis=-1, keepdims=True) * (1.0 / MLA_QK) + EPS)
        qh = _rope(qh * sq * gq, rc, rs1, rs2)
        q_ref[:, sl] = (qh * q_scale).astype(BF16)
        kh = kn[:, sl]
        sk = lax.rsqrt((jnp.sum(kh * kh, axis=-1, keepdims=True) + kr_ss) * (1.0 / MLA_QK) + EPS)
        k_ref[:, sl] = (sk * (kh * gk + krr)).astype(BF16)

    u = z[:, COL_POOL:COL_POOL + POOL_W]
    seq_tile = i % tiles_per_seq

    @pl.when(seq_tile == 0)
    def _():
        carry_ref[...] = jnp.zeros_like(carry_ref)

    xe = jnp.concatenate([carry_ref[...], u], axis=0)
    carry_ref[...] = u[tm - POOL_HALO:, :]
    s2 = xe + pltpu.roll(xe, 1, 0)
    s4 = s2 + pltpu.roll(s2, 2, 0)
    s8 = s4 + pltpu.roll(s4, 4, 0)
    s16 = s8 + pltpu.roll(s8, 8, 0)
    lane_p = lax.broadcasted_iota(jnp.int32, (tm, POOL_W), 1)
    row_p = lax.broadcasted_iota(jnp.int32, (tm, POOL_W), 0)
    g0, g1, g2 = lane_p < 64, lane_p < 128, lane_p < 192
    pooled = jnp.where(g0, s2[POOL_HALO:], jnp.where(g1, s4[POOL_HALO:],
                       jnp.where(g2, s8[POOL_HALO:], s16[POOL_HALO:])))
    win = jnp.where(g0, 2.0, jnp.where(g1, 4.0, jnp.where(g2, 8.0, 16.0)))
    cnt = jnp.minimum((seq_tile * tm + row_p + 1).astype(F32), win)
    d = pooled / cnt - u
    yp_ref[...] = (_dot(d.astype(BF16), wpool_ref[...]) * pscale_ref[...]).astype(BF16)


def _mix_pre(h, w, rope_c, rope_s1, rope_s2, seq_len):
    T = h.shape[0]
    tm = TOKEN_TILE
    row = lambda n: pl.BlockSpec((tm, n), lambda i: (i, 0))
    ins = [h, w["mix_norm"], w["w_in"], w["gla_w_gate"], w["gla_b_gate"], w["mla_q_norm"], w["mla_w_uq"],
           w["mla_kv_norm"], w["mla_w_ukv_k"], w["mla_w_ukv_v"], w["mla_gq"], w["mla_gk"],
           rope_c, rope_s1, rope_s2, w["pool_w"], w["pool_scale"]]
    in_specs = [row(D_MODEL)] + [_full(a.shape) for a in ins[1:12]] + [row(HEAD_PAD)] * 3 + \
               [_full(w["pool_w"].shape), _full(w["pool_scale"].shape)]
    out_shape = [jax.ShapeDtypeStruct((T, COL_CQ), F32), jax.ShapeDtypeStruct((T, GLA_K), F32),
                 jax.ShapeDtypeStruct((T, MLA_QK_PAD), BF16), jax.ShapeDtypeStruct((T, MLA_QK_PAD), BF16),
                 jax.ShapeDtypeStruct((T, MLA_W), BF16), jax.ShapeDtypeStruct((T, POOL_W), BF16)]
    out_specs = [row(COL_CQ), row(GLA_K), row(MLA_QK_PAD), row(MLA_QK_PAD), row(MLA_W), row(POOL_W)]
    return pl.pallas_call(
        functools.partial(_mix_pre_kernel, tiles_per_seq=seq_len // tm),
        grid=(T // tm,), in_specs=in_specs, out_specs=out_specs, out_shape=out_shape,
        scratch_shapes=[pltpu.VMEM((POOL_HALO, POOL_W), F32)],
        compiler_params=_cparams(1), name="mix_pre")(*ins)


def _gla_kernel(zg_ref, la_ref, gn_ref, y_ref, state_ref, o_ref):
    tg = zg_ref.shape[0]
    C = GLA_CHUNK

    @pl.when(pl.program_id(1) == 0)
    def _():
        state_ref[...] = jnp.zeros_like(state_ref)

    r_i = lax.broadcasted_iota(jnp.int32, (C, C), 0)
    c_i = lax.broadcasted_iota(jnp.int32, (C, C), 1)
    tri = (r_i >= c_i).astype(BF16)
    ones = jnp.ones((C, GLA_W), BF16)
    head_k = lax.broadcasted_iota(jnp.int32, (C, GLA_K), 1) // GLA_DK
    head_v = lax.broadcasted_iota(jnp.int32, (C, GLA_W), 1) // GLA_DV
    ar = lax.broadcasted_iota(jnp.int32, (GLA_HEADS * C, C), 0)
    ac = lax.broadcasted_iota(jnp.int32, (GLA_HEADS * C, C), 1)
    causal = (ar % C) >= ac
    sk = lax.broadcasted_iota(jnp.int32, (GLA_K, GLA_W), 0) // GLA_DK
    sv = lax.broadcasted_iota(jnp.int32, (GLA_K, GLA_W), 1) // GLA_DV
    blockdiag = sk == sv

    for c in range(tg // C):
        rows = slice(c * C, (c + 1) * C)
        q = zg_ref[rows, COL_GQ:COL_GQ + GLA_K] * (GLA_DK ** -0.5)
        k = zg_ref[rows, COL_GK:COL_GK + GLA_K]
        v = zg_ref[rows, COL_GV:COL_GV + GLA_W].astype(BF16)
        la3 = _split3(la_ref[rows, :])
        bc = _dot(tri, la3[0]) + _dot(tri, la3[1]) + _dot(tri, la3[2])
        b_last = bc[C - 1:C, :]
        q_dec = (q * jnp.exp(bc)).astype(BF16)
        k_dec = (k * jnp.exp(-bc)).astype(BF16)
        k_end = (k * jnp.exp(b_last - bc)).astype(BF16)
        decay = jnp.exp(_dot_tn(la3[0], ones) + _dot_tn(la3[1], ones) + _dot_tn(la3[2], ones))
        zero = jnp.zeros_like(q_dec)
        qs = jnp.concatenate([jnp.where(head_k == hh, q_dec, zero) for hh in range(GLA_HEADS)], axis=0)
        att = jnp.where(causal, _dot_nt(qs, k_dec), 0.0).astype(BF16)
        o_full = _dot(att, v)
        o = _dot(q_dec, state_ref[...].astype(BF16))
        for hh in range(GLA_HEADS):
            o = o + jnp.where(head_v == hh, o_full[hh * C:(hh + 1) * C, :], 0.0)
        upd = _dot_tn(k_end, v)
        state_ref[...] = decay * state_ref[...] + jnp.where(blockdiag, upd, 0.0)
        o_ref[rows, :] = o

    o = o_ref[...]
    gr = lax.broadcasted_iota(jnp.int32, (GLA_W, GLA_W), 0) // GLA_DV
    gc = lax.broadcasted_iota(jnp.int32, (GLA_W, GLA_W), 1) // GLA_DV
    group = (gr == gc).astype(BF16)
    oo = _split2(o * o)
    ms = (_dot(oo[0], group) + _dot(oo[1], group)) * (1.0 / GLA_DV)
    r = zg_ref[:, COL_GR:COL_GR + GLA_W]
    y = o * lax.rsqrt(ms + EPS) * gn_ref[...] * (r / (1.0 + jnp.exp(-r)))
    y_ref[...] = y.astype(BF16)


def _gla(zg, la, gn, batch, seq_len):
    T = zg.shape[0]
    tg = GLA_TILE
    nt = seq_len // tg
    return pl.pallas_call(
        _gla_kernel, grid=(batch, nt),
        in_specs=[pl.BlockSpec((tg, COL_CQ), lambda b, s: (b * nt + s, 0)),
                  pl.BlockSpec((tg, GLA_K), lambda b, s: (b * nt + s, 0)),
                  _full(gn.shape)],
        out_specs=pl.BlockSpec((tg, GLA_W), lambda b, s: (b * nt + s, 0)),
        out_shape=jax.ShapeDtypeStruct((T, GLA_W), BF16),
        scratch_shapes=[pltpu.VMEM((GLA_K, GLA_W), F32), pltpu.VMEM((tg, GLA_W), F32)],
        compiler_params=_cparams(2), name="gla")(zg, la, gn)


def _attn_kernel(q_ref, k_ref, v_ref, o_ref, m_ref, l_ref, acc_ref):
    tq = q_ref.shape[0]
    tk = tq
    i = pl.program_id(2)
    row = lax.broadcasted_iota(jnp.int32, (tq, tk), 0)
    col = lax.broadcasted_iota(jnp.int32, (tq, tk), 1)
    outs = []
    for hh in range(2):
        hs = slice(hh * HEAD_PAD, (hh + 1) * HEAD_PAD)
        qh = q_ref[:, hs]
        m_ref[...] = jnp.full_like(m_ref, NEG_BIG)
        l_ref[...] = jnp.zeros_like(l_ref)
        acc_ref[...] = jnp.zeros_like(acc_ref)

        def step(j, masked):
            kj = k_ref[pl.ds(pl.multiple_of(j * tk, tk), tk), hs]
            vj = v_ref[pl.ds(pl.multiple_of(j * tk, tk), tk), :]
            s = _dot_nt(qh, kj)
            if masked:
                s = jnp.where(col <= row, s, NEG_BIG)
            m_old = m_ref[...]
            m_new = jnp.maximum(m_old, jnp.max(s, axis=-1, keepdims=True))
            p = jnp.exp(s - m_new)
            alpha = jnp.exp(m_old - m_new)
            l_ref[...] = alpha * l_ref[...] + jnp.sum(p, axis=-1, keepdims=True)
            acc_ref[...] = alpha * acc_ref[...] + _dot(p.astype(BF16), vj)
            m_ref[...] = m_new

        def body(j, carry):
            step(j, False)
            return carry

        lax.fori_loop(0, i, body, 0)
        step(i, True)
        outs.append(acc_ref[...] / l_ref[...])
    lane = lax.broadcasted_iota(jnp.int32, (tq, HEAD_PAD), 1)
    o_ref[...] = jnp.where(lane < MLA_V, outs[0], outs[1]).astype(BF16)


def _attn(q, k, v, batch, seq_len):
    T = q.shape[0]
    tq = ATTN_TILE
    nq = seq_len // tq
    pairs = MLA_HEADS // 2
    return pl.pallas_call(
        _attn_kernel, grid=(batch, pairs, nq),
        in_specs=[pl.BlockSpec((tq, 2 * HEAD_PAD), lambda b, p, i: (b * nq + i, p)),
                  pl.BlockSpec((seq_len, 2 * HEAD_PAD), lambda b, p, i: (b, p)),
                  pl.BlockSpec((seq_len, 2 * MLA_V), lambda b, p, i: (b, p))],
        out_specs=pl.BlockSpec((tq, 2 * MLA_V), lambda b, p, i: (b * nq + i, p)),
        out_shape=jax.ShapeDtypeStruct((T, MLA_W), BF16),
        scratch_shapes=[pltpu.VMEM((tq, 1), F32), pltpu.VMEM((tq, 1), F32), pltpu.VMEM((tq, HEAD_PAD), F32)],
        compiler_params=_cparams(3), name="attn")(q, k, v)


def _out_router_kernel(h_ref, yg_ref, ym_ref, yp_ref, wo_ref, fn_ref, rw_hi_ref, rw_lo_ref, rb_ref,
                       h1_ref, hn_ref, idx_ref, gate_ref):
    tm = h_ref.shape[0]
    h1 = (h_ref[...] + _dot(yg_ref[...], wo_ref[0:GLA_W, :])
          + _dot(ym_ref[...], wo_ref[GLA_W:GLA_W + MLA_W, :])
          + _dot(yp_ref[...], wo_ref[GLA_W + MLA_W:, :]))
    h1_ref[...] = h1
    hn = _rms(h1, fn_ref[...])
    hi, lo = _split2(hn)
    hn_ref[...] = hi
    logits = _dot(hi, rw_hi_ref[...]) + _dot(lo, rw_hi_ref[...]) + _dot(hi, rw_lo_ref[...]) + rb_ref[...]
    lane = lax.broadcasted_iota(jnp.int32, (tm, 128), 1)
    cur = jnp.where(lane < N_EXPERTS, logits, NEG_BIG)
    idx_out = jnp.zeros((tm, 128), jnp.int32)
    val_out = jnp.zeros((tm, 128), F32)
    top0 = None
    for kk in range(TOP_K):
        m = jnp.max(cur, axis=-1, keepdims=True)
        sel = jnp.min(jnp.where(cur == m, lane, 128), axis=-1, keepdims=True)
        if kk == 0:
            top0 = m
        idx_out = jnp.where(lane == kk, sel, idx_out)
        val_out = jnp.where(lane == kk, jnp.exp(m - top0), val_out)
        cur = jnp.where(lane == sel, NEG_BIG, cur)
    idx_ref[...] = idx_out
    gate_ref[...] = val_out / jnp.sum(val_out, axis=-1, keepdims=True)


def _out_router(h, yg, ym, yp, w):
    T = h.shape[0]
    tm = TOKEN_TILE
    row = lambda n: pl.BlockSpec((tm, n), lambda i: (i, 0))
    ins = [h, yg, ym, yp, w["w_out"], w["ffn_norm"], w["router_w_hi"], w["router_w_lo"], w["router_b"]]
    return pl.pallas_call(
        _out_router_kernel, grid=(T // tm,),
        in_specs=[row(D_MODEL), row(GLA_W), row(MLA_W), row(POOL_W)] + [_full(a.shape) for a in ins[4:]],
        out_specs=[row(D_MODEL), row(D_MODEL), row(128), row(128)],
        out_shape=[jax.ShapeDtypeStruct((T, D_MODEL), F32), jax.ShapeDtypeStruct((T, D_MODEL), BF16),
                   jax.ShapeDtypeStruct((T, 128), jnp.int32), jax.ShapeDtypeStruct((T, 128), F32)],
        compiler_params=_cparams(1), name="out_router")(*ins)


def _moe_kernel(be_ref, nb_ref, x_ref, wg_ref, bg_ref, wu_ref, bu_ref, wd_ref, bd_ref, y_ref):
    i = pl.program_id(0)

    @pl.when(i < nb_ref[0])
    def _():
        x = x_ref[...]
        g = jnp.minimum(_dot(x, wg_ref[0]) + bg_ref[0], SWIGLU_LIMIT)
        up = jnp.clip(_dot(x, wu_ref[0]) + bu_ref[0], -SWIGLU_LIMIT, SWIGLU_LIMIT)
        hb = (up + 1.0) * (g / (1.0 + jnp.exp(-SWIGLU_ALPHA * g)))
        y_ref[...] = _dot(hb.astype(BF16), wd_ref[0]) + bd_ref[0]

    @pl.when(i >= nb_ref[0])
    def _():
        y_ref[...] = jnp.zeros_like(y_ref)


def _moe(xs, block_e, n_used, w):
    n_rows = xs.shape[0]
    bm = MOE_BLOCK
    wspec = lambda shp: pl.BlockSpec((1,) + shp, lambda i, be, nb: (be[i], 0, 0))
    grid_spec = pltpu.PrefetchScalarGridSpec(
        num_scalar_prefetch=2, grid=(n_rows // bm,),
        in_specs=[pl.BlockSpec((bm, D_MODEL), lambda i, be, nb: (i, 0)),
                  wspec((D_MODEL, D_FF)), wspec((1, D_FF)), wspec((D_MODEL, D_FF)), wspec((1, D_FF)),
                  wspec((D_FF, D_MODEL)), wspec((1, D_MODEL))],
        out_specs=pl.BlockSpec((bm, D_MODEL), lambda i, be, nb: (i, 0)))
    return pl.pallas_call(
        _moe_kernel, grid_spec=grid_spec,
        out_shape=jax.ShapeDtypeStruct((n_rows, D_MODEL), F32),
        compiler_params=_cparams(1), name="moe")(
            block_e, n_used, xs, w["moe_w_gate"], w["moe_b_gate"], w["moe_w_up"], w["moe_b_up"],
            w["moe_w_down"], w["moe_b_down"])


def _ple_kernel(h1_ref, moe_ref, p_ref, wple_ref, gn_ref, wpg_ref, pn_ref, o_ref):
    h2 = h1_ref[...] + moe_ref[...]
    e = _dot(p_ref[...].astype(BF16), wple_ref[...])
    a = _dot(_rms(h2, gn_ref[...]).astype(BF16), wpg_ref[...])
    gate = 1.0 / (1.0 + jnp.exp(-a))
    o_ref[...] = h2 + _rms(e * gate, pn_ref[...])


def _ple(h1, moe_out, p, w):
    T = h1.shape[0]
    tm = TOKEN_TILE
    row = lambda n: pl.BlockSpec((tm, n), lambda i: (i, 0))
    ins = [h1, moe_out, p, w["ple_w_proj"], w["ple_gate_norm"], w["ple_w_gate"], w["ple_post_norm"]]
    return pl.pallas_call(
        _ple_kernel, grid=(T // tm,),
        in_specs=[row(D_MODEL), row(D_MODEL), row(D_PLE)] + [_full(a.shape) for a in ins[3:]],
        out_specs=row(D_MODEL), out_shape=jax.ShapeDtypeStruct((T, D_MODEL), F32),
        compiler_params=_cparams(1), name="ple")(*ins)


def _pad_heads(wm, per_head, n_heads=MLA_HEADS):
    kdim = wm.shape[0]
    w3 = wm.reshape(kdim, n_heads, per_head)
    return jnp.pad(w3, ((0, 0), (0, 0), (0, HEAD_PAD - per_head))).reshape(kdim, n_heads * HEAD_PAD)


def _layer_params(i, mix_norm, w_in, gla_w_gate, gla_b_gate, gla_out_norm, mla_q_norm, mla_w_uq, mla_kv_norm,
                  mla_w_ukv, mla_qk_q_norm, mla_qk_k_norm, pool_w, pool_scale, w_out, ffn_norm, router_w,
                  router_b, moe_w_gate, moe_b_gate, moe_w_up, moe_b_up, moe_w_down, moe_b_down,
                  ple_w_proj, ple_gate_norm, ple_w_gate, ple_post_norm):
    wi = w_in[i]
    c = np.cumsum((0, 128, 128, 256, 16, 256, 256, 128, 32, 256))
    gq, gk, gv, glow, gr, cq, ckv, krope, upool = [wi[:, c[j]:c[j + 1]] for j in range(9)]
    misc = jnp.concatenate([glow, krope, jnp.zeros((D_MODEL, 128 - 48), F32)], axis=1)
    w_in_p = jnp.concatenate([gq, gk, gv, gr, cq, upool, ckv, misc], axis=1).astype(BF16)
    wgate_p = jnp.zeros((128, GLA_K), F32).at[MISC_GLOW:MISC_GLOW + GLA_GATE_RANK].set(gla_w_gate[i]).astype(BF16)
    ukv = mla_w_ukv[i].reshape(MLA_KV_RANK, MLA_HEADS, MLA_NOPE + MLA_V)
    ukv_k = _pad_heads(ukv[:, :, :MLA_NOPE].reshape(MLA_KV_RANK, MLA_HEADS * MLA_NOPE), MLA_NOPE)
    ukv_v = ukv[:, :, MLA_NOPE:].reshape(MLA_KV_RANK, MLA_W)
    pw = pool_w[i]
    pool_bd = jnp.zeros((POOL_W, POOL_W), F32)
    for g in range(4):
        pool_bd = pool_bd.at[g * 64:(g + 1) * 64, g * 64:(g + 1) * 64].set(pw[g])
    rw = jnp.pad(router_w[i], ((0, 0), (0, 128 - N_EXPERTS)))
    rw_hi = rw.astype(BF16)
    rw_lo = (rw - rw_hi.astype(F32)).astype(BF16)
    row = lambda a: a.reshape(1, -1)
    pad96 = lambda a: jnp.pad(a, (0, HEAD_PAD - MLA_QK)).reshape(1, HEAD_PAD)
    return {
        "mix_norm": row(mix_norm[i]), "w_in": w_in_p, "gla_w_gate": wgate_p, "gla_b_gate": row(gla_b_gate[i]),
        "gla_out_norm": row(jnp.tile(gla_out_norm[i], GLA_HEADS)),
        "mla_q_norm": row(mla_q_norm[i]), "mla_w_uq": _pad_heads(mla_w_uq[i], MLA_QK).astype(BF16),
        "mla_kv_norm": row(mla_kv_norm[i]), "mla_w_ukv_k": ukv_k.astype(BF16), "mla_w_ukv_v": ukv_v.astype(BF16),
        "mla_gq": pad96(mla_qk_q_norm[i]), "mla_gk": pad96(mla_qk_k_norm[i]),
        "pool_w": pool_bd.astype(BF16), "pool_scale": row(pool_scale[i]),
        "w_out": w_out[i].astype(BF16), "ffn_norm": row(ffn_norm[i]),
        "router_w_hi": rw_hi, "router_w_lo": rw_lo,
        "router_b": row(jnp.pad(router_b[i], (0, 128 - N_EXPERTS))),
        "moe_w_gate": moe_w_gate[i].astype(BF16), "moe_b_gate": moe_b_gate[i].reshape(N_EXPERTS, 1, D_FF),
        "moe_w_up": moe_w_up[i].astype(BF16), "moe_b_up": moe_b_up[i].reshape(N_EXPERTS, 1, D_FF),
        "moe_w_down": moe_w_down[i].astype(BF16), "moe_b_down": moe_b_down[i].reshape(N_EXPERTS, 1, D_MODEL),
        "ple_w_proj": ple_w_proj[i].astype(BF16), "ple_gate_norm": row(ple_gate_norm[i]),
        "ple_w_gate": ple_w_gate[i].astype(BF16), "ple_post_norm": row(ple_post_norm[i]),
    }


def _rope_tables(positions):
    T = positions.size
    inv = ROPE_BASE ** (-jnp.arange(0, MLA_ROPE, 2, dtype=F32) / MLA_ROPE)
    ang = positions.reshape(T, 1).astype(F32) * inv
    cos, sin = jnp.cos(ang), jnp.sin(ang)
    z16 = jnp.zeros((T, 16), F32)
    tail = jnp.zeros((T, HEAD_PAD - MLA_QK), F32)
    c = jnp.concatenate([jnp.ones((T, MLA_NOPE), F32), cos, cos, tail], axis=1)
    s1 = jnp.concatenate([jnp.zeros((T, MLA_NOPE), F32), -sin, z16, tail], axis=1)
    s2 = jnp.concatenate([jnp.zeros((T, MLA_NOPE), F32), z16, sin, tail], axis=1)
    return c, s1, s2


def _route(top_idx, T):
    bm = MOE_BLOCK
    A = T * TOP_K
    onehot = jnp.sum(jax.nn.one_hot(top_idx, N_EXPERTS, dtype=jnp.int32), axis=1)
    before = jnp.cumsum(onehot, axis=0) - onehot
    counts = jnp.sum(onehot, axis=0)
    padded = (counts + bm - 1) // bm * bm
    pad_end = jnp.cumsum(padded)
    pad_start = pad_end - padded
    dest = pad_start[top_idx] + jnp.take_along_axis(before, top_idx, axis=1)
    n_blocks = (A + N_EXPERTS * (bm - 1) + bm - 1) // bm
    n_rows = n_blocks * bm
    tok = jnp.broadcast_to(jnp.arange(T, dtype=jnp.int32)[:, None], (T, TOP_K))
    row_tok = jnp.zeros((n_rows,), jnp.int32).at[dest.reshape(A)].set(tok.reshape(A))
    block_e = jnp.minimum(jnp.searchsorted(pad_end, jnp.arange(n_blocks) * bm, side="right"),
                          N_EXPERTS - 1).astype(jnp.int32)
    n_used = (pad_end[-1] // bm).astype(jnp.int32).reshape(1)
    return dest, row_tok, block_e, n_used


def kernel(x, p, positions, mix_norm, w_in, gla_w_gate, gla_b_gate, gla_out_norm, mla_q_norm, mla_w_uq,
           mla_kv_norm, mla_w_ukv, mla_qk_q_norm, mla_qk_k_norm, pool_w, pool_scale, w_out, ffn_norm,
           router_w, router_b, moe_w_gate, moe_b_gate, moe_w_up, moe_b_up, moe_w_down, moe_b_down,
           ple_w_proj, ple_gate_norm, ple_w_gate, ple_post_norm):
    B, S, D = x.shape
    T = B * S
    depth = p.shape[0]
    params = (mix_norm, w_in, gla_w_gate, gla_b_gate, gla_out_norm, mla_q_norm, mla_w_uq, mla_kv_norm,
              mla_w_ukv, mla_qk_q_norm, mla_qk_k_norm, pool_w, pool_scale, w_out, ffn_norm, router_w,
              router_b, moe_w_gate, moe_b_gate, moe_w_up, moe_b_up, moe_w_down, moe_b_down,
              ple_w_proj, ple_gate_norm, ple_w_gate, ple_post_norm)
    rope_c, rope_s1, rope_s2 = _rope_tables(positions)
    h = x.reshape(T, D)
    for i in range(depth):
        w = _layer_params(i, *params)
        zg, la, q, k, v, y_pool = _mix_pre(h, w, rope_c, rope_s1, rope_s2, S)
        y_gla = _gla(zg, la, w["gla_out_norm"], B, S)
        y_mla = _attn(q, k, v, B, S)
        h1, hn, top_idx, gates = _out_router(h, y_gla, y_mla, y_pool, w)
        top_idx, gates = top_idx[:, :TOP_K], gates[:, :TOP_K]
        dest, row_tok, block_e, n_used = _route(top_idx, T)
        ys = _moe(hn[row_tok], block_e, n_used, w)
        moe_out = jnp.sum(ys[dest] * gates[:, :, None], axis=1)
        h = _ple(h1, moe_out, p[i].reshape(T, D_PLE), w)
    return h.reshape(B, S, D)
```

```python
import functools

import jax
import jax.numpy as jnp
import numpy as np
from jax import lax
from jax.experimental import pallas as pl
from jax.experimental.pallas import tpu as pltpu

F32 = jnp.float32
BF16 = jnp.bfloat16

D_MODEL = 1024
EPS = 1e-6
D_PLE = 256

GLA_HEADS = 4
GLA_DK = 32
GLA_DV = 64
GLA_GATE_RANK = 16
GLA_TAU = 16.0
GLA_CHUNK = 64
GLA_K = GLA_HEADS * GLA_DK
GLA_W = GLA_HEADS * GLA_DV

MLA_HEADS = 8
MLA_Q_RANK = 256
MLA_KV_RANK = 128
MLA_NOPE = 64
MLA_ROPE = 32
MLA_QK = MLA_NOPE + MLA_ROPE
MLA_V = 64
MLA_W = MLA_HEADS * MLA_V
ROPE_BASE = 10000.0
HEAD_PAD = 128
MLA_QK_PAD = MLA_HEADS * HEAD_PAD

POOL_WINDOWS = (2, 4, 8, 16)
POOL_GROUP = 64
POOL_W = 256
POOL_HALO = 16

N_EXPERTS = 32
TOP_K = 4
D_FF = 1024
SWIGLU_LIMIT = 7.0
SWIGLU_ALPHA = 1.702

COL_GQ, COL_GK, COL_GV, COL_GR, COL_CQ, COL_POOL, COL_CKV, COL_MISC = 0, 128, 256, 512, 768, 1024, 1280, 1408
D_IN_PAD = 1536
MISC_GLOW = 0
MISC_ROPE = 16

TOKEN_TILE = 512
GLA_TILE = 512
ATTN_TILE = 512
ATTN_SUB = 256
MOE_BLOCK = 256
MOE_CAST_ROWS = 256
VMEM_LIMIT = 56 * 1024 * 1024
NEG_BIG = -1e30


def _cparams(n_axes):
    return pltpu.CompilerParams(dimension_semantics=("arbitrary",) * n_axes,
                                vmem_limit_bytes=VMEM_LIMIT)


def _rms(x, g):
    return x * lax.rsqrt(jnp.mean(x * x, axis=-1, keepdims=True) + EPS) * g


def _dot(a, b):
    return jnp.dot(a, b, preferred_element_type=F32)


def _dot_nt(a, b):
    return lax.dot_general(a, b, (((1,), (1,)), ((), ())), preferred_element_type=F32)


def _dot_tn(a, b):
    return lax.dot_general(a, b, (((0,), (0,)), ((), ())), preferred_element_type=F32)


def _split3(x):
    hi = x.astype(BF16)
    r = x - hi.astype(F32)
    mid = r.astype(BF16)
    lo = (r - mid.astype(F32)).astype(BF16)
    return hi, mid, lo


def _split2(x):
    hi = x.astype(BF16)
    lo = (x - hi.astype(F32)).astype(BF16)
    return hi, lo


def _full(shape):
    nd = len(shape)
    return pl.BlockSpec(shape, lambda *_: (0,) * nd)


def _rope(x, c, s1, s2):
    return x * c + pltpu.roll(x, HEAD_PAD - 16, 1) * s1 + pltpu.roll(x, 16, 1) * s2


def _mix_pre_kernel(h_ref, mixn_ref, win_ref, wgate_ref, bgate_ref, qn_ref, wuq_ref, kvn_ref,
                    wukvk_ref, wukvv_ref, gq_ref, gk_ref, rc_ref, rs1_ref, rs2_ref,
                    wpool_ref, pscale_ref,
                    zg_ref, la_ref, q_ref, k_ref, v_ref, yp_ref, carry_ref, *, tiles_per_seq):
    tm = h_ref.shape[0]
    i = pl.program_id(0)
    hn = _rms(h_ref[...], mixn_ref[...]).astype(BF16)
    z = _dot(hn, win_ref[...])
    zg_ref[...] = z[:, COL_GQ:COL_CQ]
    zm = z[:, COL_MISC:COL_MISC + 128]

    logit = _dot(zm.astype(BF16), wgate_ref[...]) + bgate_ref[...]
    la_ref[...] = (jnp.minimum(logit, 0.0) - jnp.log(1.0 + jnp.exp(-jnp.abs(logit)))) * (1.0 / GLA_TAU)

    cqn = _rms(z[:, COL_CQ:COL_CQ + MLA_Q_RANK], qn_ref[...]).astype(BF16)
    qf = _dot(cqn, wuq_ref[...])
    ckvn = _rms(z[:, COL_CKV:COL_CKV + MLA_KV_RANK], kvn_ref[...]).astype(BF16)
    kn = _dot(ckvn, wukvk_ref[...])
    lane_v = lax.broadcasted_iota(jnp.int32, (tm, MLA_QK_PAD), 1)
    v_ref[...] = jnp.where(lane_v % HEAD_PAD == MLA_V, 1.0, _dot(ckvn, wukvv_ref[...])).astype(BF16)

    rc, rs1, rs2 = rc_ref[...], rs1_ref[...], rs2_ref[...]
    lane = lax.broadcasted_iota(jnp.int32, (tm, HEAD_PAD), 1)
    in_rope = (lane >= MLA_NOPE) & (lane < MLA_QK)
    kr = jnp.where(in_rope, pltpu.roll(zm, MLA_NOPE - MISC_ROPE, 1), 0.0)
    kr_ss = jnp.sum(kr * kr, axis=-1, keepdims=True)
    gq, gk = gq_ref[...], gk_ref[...]
    krr = _rope(kr * gk, rc, rs1, rs2)
    q_scale = MLA_QK ** -0.5
    for hh in range(MLA_HEADS):
        sl = slice(hh * HEAD_PAD, (hh + 1) * HEAD_PAD)
        qh = qf[:, sl]
        sq = lax.rsqrt(jnp.sum(qh * qh, axis=-1, keepdims=True) * (1.0 / MLA_QK) + EPS)
        qh = _rope(qh * sq * gq, rc, rs1, rs2)
        q_ref[:, sl] = (qh * q_scale).astype(BF16)
        kh = kn[:, sl]
        sk = lax.rsqrt((jnp.sum(kh * kh, axis=-1, keepdims=True) + kr_ss) * (1.0 / MLA_QK) + EPS)
        k_ref[:, sl] = (sk * (kh * gk + krr)).astype(BF16)

    u = z[:, COL_POOL:COL_POOL + POOL_W]
    seq_tile = i % tiles_per_seq

    @pl.when(seq_tile == 0)
    def _():
        carry_ref[...] = jnp.zeros_like(carry_ref)

    xe = jnp.concatenate([carry_ref[...], u], axis=0)
    carry_ref[...] = u[tm - POOL_HALO:, :]
    s2 = xe + pltpu.roll(xe, 1, 0)
    s4 = s2 + pltpu.roll(s2, 2, 0)
    s8 = s4 + pltpu.roll(s4, 4, 0)
    s16 = s8 + pltpu.roll(s8, 8, 0)
    lane_p = lax.broadcasted_iota(jnp.int32, (tm, POOL_W), 1)
    row_p = lax.broadcasted_iota(jnp.int32, (tm, POOL_W), 0)
    g0, g1, g2 = lane_p < 64, lane_p < 128, lane_p < 192
    pooled = jnp.where(g0, s2[POOL_HALO:], jnp.where(g1, s4[POOL_HALO:],
                       jnp.where(g2, s8[POOL_HALO:], s16[POOL_HALO:])))
    win = jnp.where(g0, 2.0, jnp.where(g1, 4.0, jnp.where(g2, 8.0, 16.0)))
    cnt = jnp.minimum((seq_tile * tm + row_p + 1).astype(F32), win)
    d = pooled / cnt - u
    yp_ref[...] = (_dot(d.astype(BF16), wpool_ref[...]) * pscale_ref[...]).astype(BF16)


def _mix_pre(h, w, rope_c, rope_s1, rope_s2, seq_len):
    T = h.shape[0]
    tm = TOKEN_TILE
    row = lambda n: pl.BlockSpec((tm, n), lambda i: (i, 0))
    ins = [h, w["mix_norm"], w["w_in"], w["gla_w_gate"], w["gla_b_gate"], w["mla_q_norm"], w["mla_w_uq"],
           w["mla_kv_norm"], w["mla_w_ukv_k"], w["mla_w_ukv_v"], w["mla_gq"], w["mla_gk"],
           rope_c, rope_s1, rope_s2, w["pool_w"], w["pool_scale"]]
    in_specs = [row(D_MODEL)] + [_full(a.shape) for a in ins[1:12]] + [row(HEAD_PAD)] * 3 + \
               [_full(w["pool_w"].shape), _full(w["pool_scale"].shape)]
    out_shape = [jax.ShapeDtypeStruct((T, COL_CQ), F32), jax.ShapeDtypeStruct((T, GLA_K), F32),
                 jax.ShapeDtypeStruct((T, MLA_QK_PAD), BF16), jax.ShapeDtypeStruct((T, MLA_QK_PAD), BF16),
                 jax.ShapeDtypeStruct((T, MLA_QK_PAD), BF16), jax.ShapeDtypeStruct((T, POOL_W), BF16)]
    out_specs = [row(COL_CQ), row(GLA_K), row(MLA_QK_PAD), row(MLA_QK_PAD), row(MLA_QK_PAD), row(POOL_W)]
    return pl.pallas_call(
        functools.partial(_mix_pre_kernel, tiles_per_seq=seq_len // tm),
        grid=(T // tm,), in_specs=in_specs, out_specs=out_specs, out_shape=out_shape,
        scratch_shapes=[pltpu.VMEM((POOL_HALO, POOL_W), F32)],
        compiler_params=_cparams(1), name="mix_pre")(*ins)


def _gla_kernel(zg_ref, la_ref, gn_ref, y_ref, state_ref, o_ref):
    tg = zg_ref.shape[0]
    C = GLA_CHUNK

    @pl.when(pl.program_id(1) == 0)
    def _():
        state_ref[...] = jnp.zeros_like(state_ref)

    r_i = lax.broadcasted_iota(jnp.int32, (C, C), 0)
    c_i = lax.broadcasted_iota(jnp.int32, (C, C), 1)
    tri = (r_i >= c_i).astype(BF16)
    ones = jnp.ones((C, GLA_W), BF16)
    head_k = lax.broadcasted_iota(jnp.int32, (C, GLA_K), 1) // GLA_DK
    head_v = lax.broadcasted_iota(jnp.int32, (C, GLA_W), 1) // GLA_DV
    ar = lax.broadcasted_iota(jnp.int32, (GLA_HEADS * C, C), 0)
    ac = lax.broadcasted_iota(jnp.int32, (GLA_HEADS * C, C), 1)
    causal = (ar % C) >= ac
    sk = lax.broadcasted_iota(jnp.int32, (GLA_K, GLA_W), 0) // GLA_DK
    sv = lax.broadcasted_iota(jnp.int32, (GLA_K, GLA_W), 1) // GLA_DV
    blockdiag = sk == sv

    for c in range(tg // C):
        rows = slice(c * C, (c + 1) * C)
        q = zg_ref[rows, COL_GQ:COL_GQ + GLA_K] * (GLA_DK ** -0.5)
        k = zg_ref[rows, COL_GK:COL_GK + GLA_K]
        v = zg_ref[rows, COL_GV:COL_GV + GLA_W].astype(BF16)
        la3 = _split3(la_ref[rows, :])
        bc = _dot(tri, la3[0]) + _dot(tri, la3[1]) + _dot(tri, la3[2])
        b_last = bc[C - 1:C, :]
        q_dec = (q * jnp.exp(bc)).astype(BF16)
        k_dec = (k * jnp.exp(-bc)).astype(BF16)
        k_end = (k * jnp.exp(b_last - bc)).astype(BF16)
        decay = jnp.exp(_dot_tn(la3[0], ones) + _dot_tn(la3[1], ones) + _dot_tn(la3[2], ones))
        zero = jnp.zeros_like(q_dec)
        qs = jnp.concatenate([jnp.where(head_k == hh, q_dec, zero) for hh in range(GLA_HEADS)], axis=0)
        att = jnp.where(causal, _dot_nt(qs, k_dec), 0.0).astype(BF16)
        o_full = _dot(att, v)
        o = _dot(q_dec, state_ref[...].astype(BF16))
        for hh in range(GLA_HEADS):
            o = o + jnp.where(head_v == hh, o_full[hh * C:(hh + 1) * C, :], 0.0)
        upd = _dot_tn(k_end, v)
        state_ref[...] = decay * state_ref[...] + jnp.where(blockdiag, upd, 0.0)
        o_ref[rows, :] = o

    o = o_ref[...]
    gr = lax.broadcasted_iota(jnp.int32, (GLA_W, GLA_W), 0) // GLA_DV
    gc = lax.broadcasted_iota(jnp.int32, (GLA_W, GLA_W), 1) // GLA_DV
    group = (gr == gc).astype(BF16)
    oo = _split2(o * o)
    ms = (_dot(oo[0], group) + _dot(oo[1], group)) * (1.0 / GLA_DV)
    r = zg_ref[:, COL_GR:COL_GR + GLA_W]
    y = o * lax.rsqrt(ms + EPS) * gn_ref[...] * (r / (1.0 + jnp.exp(-r)))
    y_ref[...] = y.astype(BF16)


def _gla(zg, la, gn, batch, seq_len):
    T = zg.shape[0]
    tg = GLA_TILE
    nt = seq_len // tg
    return pl.pallas_call(
        _gla_kernel, grid=(batch, nt),
        in_specs=[pl.BlockSpec((tg, COL_CQ), lambda b, s: (b * nt + s, 0)),
                  pl.BlockSpec((tg, GLA_K), lambda b, s: (b * nt + s, 0)),
                  _full(gn.shape)],
        out_specs=pl.BlockSpec((tg, GLA_W), lambda b, s: (b * nt + s, 0)),
        out_shape=jax.ShapeDtypeStruct((T, GLA_W), BF16),
        scratch_shapes=[pltpu.VMEM((GLA_K, GLA_W), F32), pltpu.VMEM((tg, GLA_W), F32)],
        compiler_params=_cparams(2), name="gla")(zg, la, gn)


def _attn_kernel(q_ref, k_ref, v_ref, o_ref, m_ref, acc_ref):
    tq = q_ref.shape[0]
    ts = ATTN_SUB
    i = pl.program_id(2)
    m_ref[...] = jnp.full_like(m_ref, NEG_BIG)
    acc_ref[...] = jnp.zeros_like(acc_ref)

    def sub_block(hh, start, r0, mask_off):
        hs = slice(hh * HEAD_PAD, (hh + 1) * HEAD_PAD)
        kj = k_ref[pl.ds(start, ts), hs]
        vj = v_ref[pl.ds(start, ts), hs]
        s = _dot_nt(q_ref[r0:, hs], kj)
        if mask_off is not None:
            row = lax.broadcasted_iota(jnp.int32, s.shape, 0) + r0
            col = lax.broadcasted_iota(jnp.int32, s.shape, 1) + mask_off
            s = jnp.where(col <= row, s, NEG_BIG)
        m_old = m_ref[hh, r0:, :]
        parts = [s[:, c * 128:(c + 1) * 128] for c in range(ts // 128)]
        m_new = jnp.maximum(m_old, jnp.max(functools.reduce(jnp.maximum, parts), axis=-1, keepdims=True))
        p = jnp.concatenate([jnp.exp(x - m_new) for x in parts], axis=1).astype(BF16)
        acc_ref[hh, r0:, :] = jnp.exp(m_old - m_new) * acc_ref[hh, r0:, :] + _dot(p, vj)
        m_ref[hh, r0:, :] = m_new

    def body(j, carry):
        base = pl.multiple_of(j * tq, tq)
        for sb in range(tq // ts):
            for hh in range(2):
                sub_block(hh, base + sb * ts, 0, None)
        return carry

    lax.fori_loop(0, i, body, 0)
    base = pl.multiple_of(i * tq, tq)
    for sb in range(tq // ts):
        for hh in range(2):
            sub_block(hh, base + sb * ts, sb * ts, sb * ts)
    outs = []
    for hh in range(2):
        a = acc_ref[hh]
        outs.append(a / a[:, MLA_V:MLA_V + 1])
    lane = lax.broadcasted_iota(jnp.int32, (tq, HEAD_PAD), 1)
    o_ref[...] = jnp.where(lane < MLA_V, outs[0], pltpu.roll(outs[1], MLA_V, 1)).astype(BF16)


def _attn(q, k, v, batch, seq_len):
    T = q.shape[0]
    tq = ATTN_TILE
    nq = seq_len // tq
    pairs = MLA_HEADS // 2
    return pl.pallas_call(
        _attn_kernel, grid=(batch, pairs, nq),
        in_specs=[pl.BlockSpec((tq, 2 * HEAD_PAD), lambda b, p, i: (b * nq + i, p)),
                  pl.BlockSpec((seq_len, 2 * HEAD_PAD), lambda b, p, i: (b, p)),
                  pl.BlockSpec((seq_len, 2 * HEAD_PAD), lambda b, p, i: (b, p))],
        out_specs=pl.BlockSpec((tq, 2 * MLA_V), lambda b, p, i: (b * nq + i, p)),
        out_shape=jax.ShapeDtypeStruct((T, MLA_W), BF16),
        scratch_shapes=[pltpu.VMEM((2, tq, HEAD_PAD), F32), pltpu.VMEM((2, tq, HEAD_PAD), F32)],
        compiler_params=_cparams(3), name="attn")(q, k, v)


def _out_router_kernel(h_ref, yg_ref, ym_ref, yp_ref, wo_ref, fn_ref, rw_hi_ref, rw_lo_ref, rb_ref,
                       h1_ref, hn_ref, idx_ref, gate_ref, cnt_ref, carry_ref):
    tm = h_ref.shape[0]

    @pl.when(pl.program_id(0) == 0)
    def _():
        carry_ref[...] = jnp.zeros_like(carry_ref)

    h1 = (h_ref[...] + _dot(yg_ref[...], wo_ref[0:GLA_W, :])
          + _dot(ym_ref[...], wo_ref[GLA_W:GLA_W + MLA_W, :])
          + _dot(yp_ref[...], wo_ref[GLA_W + MLA_W:, :]))
    h1_ref[...] = h1
    hn = _rms(h1, fn_ref[...])
    hi, lo = _split2(hn)
    hn_ref[...] = hi
    logits = _dot(hi, rw_hi_ref[...]) + _dot(lo, rw_hi_ref[...]) + _dot(hi, rw_lo_ref[...]) + rb_ref[...]
    lane = lax.broadcasted_iota(jnp.int32, (tm, 128), 1)
    cur = jnp.where(lane < N_EXPERTS, logits, NEG_BIG)
    idx_out = jnp.zeros((tm, 128), jnp.int32)
    val_out = jnp.zeros((tm, 128), F32)
    chosen = jnp.zeros((tm, 128), F32)
    top0 = None
    sels = []
    for kk in range(TOP_K):
        m = jnp.max(cur, axis=-1, keepdims=True)
        sel = jnp.min(jnp.where(cur == m, lane, 128), axis=-1, keepdims=True)
        if kk == 0:
            top0 = m
        sels.append(sel)
        idx_out = jnp.where(lane == kk, sel, idx_out)
        val_out = jnp.where(lane == kk, jnp.exp(m - top0), val_out)
        chosen = jnp.where(lane == sel, 1.0, chosen)
        cur = jnp.where(lane == sel, NEG_BIG, cur)
    gate_ref[...] = val_out / jnp.sum(val_out, axis=-1, keepdims=True)

    r_i = lax.broadcasted_iota(jnp.int32, (tm, tm), 0)
    c_i = lax.broadcasted_iota(jnp.int32, (tm, tm), 1)
    incl = _dot((r_i >= c_i).astype(BF16), chosen.astype(BF16))
    before = carry_ref[0:1, :] + incl - chosen
    for kk in range(TOP_K):
        rank = jnp.sum(jnp.where(lane == sels[kk], before, 0.0), axis=-1, keepdims=True)
        idx_out = jnp.where(lane == TOP_K + kk, rank.astype(jnp.int32), idx_out)
    idx_ref[...] = idx_out
    total = carry_ref[...] + incl[tm - 1:tm, :]
    carry_ref[...] = total
    cnt_ref[...] = total.astype(jnp.int32)


def _out_router(h, yg, ym, yp, w):
    T = h.shape[0]
    tm = TOKEN_TILE
    row = lambda n: pl.BlockSpec((tm, n), lambda i: (i, 0))
    ins = [h, yg, ym, yp, w["w_out"], w["ffn_norm"], w["router_w_hi"], w["router_w_lo"], w["router_b"]]
    return pl.pallas_call(
        _out_router_kernel, grid=(T // tm,),
        in_specs=[row(D_MODEL), row(GLA_W), row(MLA_W), row(POOL_W)] + [_full(a.shape) for a in ins[4:]],
        out_specs=[row(D_MODEL), row(D_MODEL), row(128), row(128), _full((8, 128))],
        out_shape=[jax.ShapeDtypeStruct((T, D_MODEL), F32), jax.ShapeDtypeStruct((T, D_MODEL), BF16),
                   jax.ShapeDtypeStruct((T, 128), jnp.int32), jax.ShapeDtypeStruct((T, 128), F32),
                   jax.ShapeDtypeStruct((8, 128), jnp.int32)],
        scratch_shapes=[pltpu.VMEM((8, 128), F32)],
        compiler_params=_cparams(1), name="out_router")(*ins)


def _moe_kernel(be_ref, nb_ref, x_ref, wg_ref, bg_ref, wu_ref, bu_ref, wd_ref, bd_ref, y_ref,
                wg_bf, wu_bf, wd_bf):
    i = pl.program_id(0)
    used = i < nb_ref[0]
    new_expert = (i == 0) | (be_ref[i] != be_ref[jnp.maximum(i - 1, 0)])

    @pl.when(used & new_expert)
    def _():
        for src, dst in ((wg_ref, wg_bf), (wu_ref, wu_bf), (wd_ref, wd_bf)):
            for r in range(0, src.shape[2], MOE_CAST_ROWS):
                dst[r:r + MOE_CAST_ROWS, :] = src[0, 0, r:r + MOE_CAST_ROWS, :].astype(BF16)

    @pl.when(used)
    def _():
        x = x_ref[...]
        g = jnp.minimum(_dot(x, wg_bf[...]) + bg_ref[0], SWIGLU_LIMIT)
        up = jnp.clip(_dot(x, wu_bf[...]) + bu_ref[0], -SWIGLU_LIMIT, SWIGLU_LIMIT)
        hb = (up + 1.0) * (g / (1.0 + jnp.exp(-SWIGLU_ALPHA * g)))
        y_ref[...] = _dot(hb.astype(BF16), wd_bf[...]) + bd_ref[0]

    @pl.when(jnp.logical_not(used))
    def _():
        y_ref[...] = jnp.zeros_like(y_ref)


def _moe(xs, block_e, n_used, w):
    n_rows = xs.shape[0]
    bm = MOE_BLOCK
    layer = w["layer"]
    wspec = lambda shp: pl.BlockSpec((1, 1) + shp, lambda i, be, nb: (layer, be[i], 0, 0))
    bspec = lambda shp: pl.BlockSpec((1,) + shp, lambda i, be, nb: (be[i], 0, 0))
    grid_spec = pltpu.PrefetchScalarGridSpec(
        num_scalar_prefetch=2, grid=(n_rows // bm,),
        in_specs=[pl.BlockSpec((bm, D_MODEL), lambda i, be, nb: (i, 0)),
                  wspec((D_MODEL, D_FF)), bspec((1, D_FF)), wspec((D_MODEL, D_FF)), bspec((1, D_FF)),
                  wspec((D_FF, D_MODEL)), bspec((1, D_MODEL))],
        out_specs=pl.BlockSpec((bm, D_MODEL), lambda i, be, nb: (i, 0)),
        scratch_shapes=[pltpu.VMEM((D_MODEL, D_FF), BF16), pltpu.VMEM((D_MODEL, D_FF), BF16),
                        pltpu.VMEM((D_FF, D_MODEL), BF16)])
    return pl.pallas_call(
        _moe_kernel, grid_spec=grid_spec,
        out_shape=jax.ShapeDtypeStruct((n_rows, D_MODEL), F32),
        compiler_params=_cparams(1), name="moe")(
            block_e, n_used, xs, w["moe_w_gate"], w["moe_b_gate"], w["moe_w_up"], w["moe_b_up"],
            w["moe_w_down"], w["moe_b_down"])


def _ple_kernel(h1_ref, y0_ref, y1_ref, y2_ref, y3_ref, gate_ref, p_ref, wple_ref, gn_ref, wpg_ref, pn_ref, o_ref):
    gates = gate_ref[...]
    h2 = h1_ref[...]
    for kk, y_ref in enumerate((y0_ref, y1_ref, y2_ref, y3_ref)):
        h2 = h2 + gates[:, kk:kk + 1] * y_ref[...]
    e = _dot(p_ref[...].astype(BF16), wple_ref[...])
    a = _dot(_rms(h2, gn_ref[...]).astype(BF16), wpg_ref[...])
    gate = 1.0 / (1.0 + jnp.exp(-a))
    o_ref[...] = h2 + _rms(e * gate, pn_ref[...])


def _ple(h1, ys_k, gates, p, w):
    T = h1.shape[0]
    tm = TOKEN_TILE
    row = lambda n: pl.BlockSpec((tm, n), lambda i: (i, 0))
    ins = [h1, *ys_k, gates, p, w["ple_w_proj"], w["ple_gate_norm"], w["ple_w_gate"], w["ple_post_norm"]]
    return pl.pallas_call(
        _ple_kernel, grid=(T // tm,),
        in_specs=[row(D_MODEL)] * (1 + TOP_K) + [row(128), row(D_PLE)] + [_full(a.shape) for a in ins[7:]],
        out_specs=row(D_MODEL), out_shape=jax.ShapeDtypeStruct((T, D_MODEL), F32),
        compiler_params=_cparams(1), name="ple")(*ins)


def _pad_heads(wm, per_head, n_heads=MLA_HEADS):
    kdim = wm.shape[0]
    w3 = wm.reshape(kdim, n_heads, per_head)
    return jnp.pad(w3, ((0, 0), (0, 0), (0, HEAD_PAD - per_head))).reshape(kdim, n_heads * HEAD_PAD)


def _layer_params(i, mix_norm, w_in, gla_w_gate, gla_b_gate, gla_out_norm, mla_q_norm, mla_w_uq, mla_kv_norm,
                  mla_w_ukv, mla_qk_q_norm, mla_qk_k_norm, pool_w, pool_scale, w_out, ffn_norm, router_w,
                  router_b, moe_w_gate, moe_b_gate, moe_w_up, moe_b_up, moe_w_down, moe_b_down,
                  ple_w_proj, ple_gate_norm, ple_w_gate, ple_post_norm):
    wi = w_in[i]
    c = np.cumsum((0, 128, 128, 256, 16, 256, 256, 128, 32, 256))
    gq, gk, gv, glow, gr, cq, ckv, krope, upool = [wi[:, c[j]:c[j + 1]] for j in range(9)]
    misc = jnp.concatenate([glow, krope, jnp.zeros((D_MODEL, 128 - 48), F32)], axis=1)
    w_in_p = jnp.concatenate([gq, gk, gv, gr, cq, upool, ckv, misc], axis=1).astype(BF16)
    wgate_p = jnp.zeros((128, GLA_K), F32).at[MISC_GLOW:MISC_GLOW + GLA_GATE_RANK].set(gla_w_gate[i]).astype(BF16)
    ukv = mla_w_ukv[i].reshape(MLA_KV_RANK, MLA_HEADS, MLA_NOPE + MLA_V)
    ukv_k = _pad_heads(ukv[:, :, :MLA_NOPE].reshape(MLA_KV_RANK, MLA_HEADS * MLA_NOPE), MLA_NOPE)
    ukv_v = _pad_heads(ukv[:, :, MLA_NOPE:].reshape(MLA_KV_RANK, MLA_W), MLA_V)
    pw = pool_w[i]
    pool_bd = jnp.zeros((POOL_W, POOL_W), F32)
    for g in range(4):
        pool_bd = pool_bd.at[g * 64:(g + 1) * 64, g * 64:(g + 1) * 64].set(pw[g])
    rw = jnp.pad(router_w[i], ((0, 0), (0, 128 - N_EXPERTS)))
    rw_hi = rw.astype(BF16)
    rw_lo = (rw - rw_hi.astype(F32)).astype(BF16)
    row = lambda a: a.reshape(1, -1)
    pad96 = lambda a: jnp.pad(a, (0, HEAD_PAD - MLA_QK)).reshape(1, HEAD_PAD)
    return {
        "mix_norm": row(mix_norm[i]), "w_in": w_in_p, "gla_w_gate": wgate_p, "gla_b_gate": row(gla_b_gate[i]),
        "gla_out_norm": row(jnp.tile(gla_out_norm[i], GLA_HEADS)),
        "mla_q_norm": row(mla_q_norm[i]), "mla_w_uq": _pad_heads(mla_w_uq[i], MLA_QK).astype(BF16),
        "mla_kv_norm": row(mla_kv_norm[i]), "mla_w_ukv_k": ukv_k.astype(BF16), "mla_w_ukv_v": ukv_v.astype(BF16),
        "mla_gq": pad96(mla_qk_q_norm[i]), "mla_gk": pad96(mla_qk_k_norm[i]),
        "pool_w": pool_bd.astype(BF16), "pool_scale": row(pool_scale[i]),
        "w_out": w_out[i].astype(BF16), "ffn_norm": row(ffn_norm[i]),
        "router_w_hi": rw_hi, "router_w_lo": rw_lo,
        "router_b": row(jnp.pad(router_b[i], (0, 128 - N_EXPERTS))),
        "layer": i,
        "moe_w_gate": moe_w_gate, "moe_b_gate": moe_b_gate[i].reshape(N_EXPERTS, 1, D_FF),
        "moe_w_up": moe_w_up, "moe_b_up": moe_b_up[i].reshape(N_EXPERTS, 1, D_FF),
        "moe_w_down": moe_w_down, "moe_b_down": moe_b_down[i].reshape(N_EXPERTS, 1, D_MODEL),
        "ple_w_proj": ple_w_proj[i].astype(BF16), "ple_gate_norm": row(ple_gate_norm[i]),
        "ple_w_gate": ple_w_gate[i].astype(BF16), "ple_post_norm": row(ple_post_norm[i]),
    }


def _rope_tables(positions):
    T = positions.size
    inv = ROPE_BASE ** (-jnp.arange(0, MLA_ROPE, 2, dtype=F32) / MLA_ROPE)
    ang = positions.reshape(T, 1).astype(F32) * inv
    cos, sin = jnp.cos(ang), jnp.sin(ang)
    z16 = jnp.zeros((T, 16), F32)
    tail = jnp.zeros((T, HEAD_PAD - MLA_QK), F32)
    c = jnp.concatenate([jnp.ones((T, MLA_NOPE), F32), cos, cos, tail], axis=1)
    s1 = jnp.concatenate([jnp.zeros((T, MLA_NOPE), F32), -sin, z16, tail], axis=1)
    s2 = jnp.concatenate([jnp.zeros((T, MLA_NOPE), F32), z16, sin, tail], axis=1)
    return c, s1, s2


def _route(top_idx, rank, counts, T):
    bm = MOE_BLOCK
    A = T * TOP_K
    padded = (counts + bm - 1) // bm * bm
    pad_end = jnp.cumsum(padded)
    pad_start = pad_end - padded
    experts = jnp.arange(N_EXPERTS, dtype=jnp.int32)
    dest = rank + jnp.sum(jnp.where(top_idx[:, :, None] == experts, pad_start, 0), axis=-1)
    n_blocks = (A + N_EXPERTS * (bm - 1) + bm - 1) // bm
    n_rows = n_blocks * bm
    tok = jnp.broadcast_to(jnp.arange(T, dtype=jnp.int32)[:, None], (T, TOP_K))
    row_tok = jnp.zeros((n_rows,), jnp.int32).at[dest.reshape(A)].set(tok.reshape(A))
    block_start = jnp.arange(n_blocks, dtype=jnp.int32) * bm
    block_e = jnp.minimum(jnp.sum((pad_end[None, :] <= block_start[:, None]).astype(jnp.int32), axis=1),
                          N_EXPERTS - 1)
    n_used = (pad_end[-1] // bm).astype(jnp.int32).reshape(1)
    return dest, row_tok, block_e, n_used


def kernel(x, p, positions, mix_norm, w_in, gla_w_gate, gla_b_gate, gla_out_norm, mla_q_norm, mla_w_uq,
           mla_kv_norm, mla_w_ukv, mla_qk_q_norm, mla_qk_k_norm, pool_w, pool_scale, w_out, ffn_norm,
           router_w, router_b, moe_w_gate, moe_b_gate, moe_w_up, moe_b_up, moe_w_down, moe_b_down,
           ple_w_proj, ple_gate_norm, ple_w_gate, ple_post_norm):
    B, S, D = x.shape
    T = B * S
    depth = p.shape[0]
    params = (mix_norm, w_in, gla_w_gate, gla_b_gate, gla_out_norm, mla_q_norm, mla_w_uq, mla_kv_norm,
              mla_w_ukv, mla_qk_q_norm, mla_qk_k_norm, pool_w, pool_scale, w_out, ffn_norm, router_w,
              router_b, moe_w_gate, moe_b_gate, moe_w_up, moe_b_up, moe_w_down, moe_b_down,
              ple_w_proj, ple_gate_norm, ple_w_gate, ple_post_norm)
    rope_c, rope_s1, rope_s2 = _rope_tables(positions)
    h = x.reshape(T, D)
    for i in range(depth):
        w = _layer_params(i, *params)
        zg, la, q, k, v, y_pool = _mix_pre(h, w, rope_c, rope_s1, rope_s2, S)
        y_gla = _gla(zg, la, w["gla_out_norm"], B, S)
        y_mla = _attn(q, k, v, B, S)
        h1, hn, route, gates, counts = _out_router(h, y_gla, y_mla, y_pool, w)
        dest, row_tok, block_e, n_used = _route(route[:, :TOP_K], route[:, TOP_K:2 * TOP_K],
                                                counts[0, :N_EXPERTS], T)
        ys = _moe(hn[row_tok], block_e, n_used, w)
        h = _ple(h1, [ys[dest[:, kk]] for kk in range(TOP_K)], gates, p[i].reshape(T, D_PLE), w)
    return h.reshape(B, S, D)
```

```python
import functools

import jax
import jax.numpy as jnp
import numpy as np
from jax import lax
from jax.experimental import pallas as pl
from jax.experimental.pallas import tpu as pltpu

F32 = jnp.float32
BF16 = jnp.bfloat16

D_MODEL = 1024
EPS = 1e-6
D_PLE = 256

GLA_HEADS = 4
GLA_DK = 32
GLA_DV = 64
GLA_GATE_RANK = 16
GLA_TAU = 16.0
GLA_CHUNK = 64
GLA_K = GLA_HEADS * GLA_DK
GLA_W = GLA_HEADS * GLA_DV

MLA_HEADS = 8
MLA_Q_RANK = 256
MLA_KV_RANK = 128
MLA_NOPE = 64
MLA_ROPE = 32
MLA_QK = MLA_NOPE + MLA_ROPE
MLA_V = 64
MLA_W = MLA_HEADS * MLA_V
ROPE_BASE = 10000.0
HEAD_PAD = 128
MLA_QK_PAD = MLA_HEADS * HEAD_PAD

POOL_WINDOWS = (2, 4, 8, 16)
POOL_GROUP = 64
POOL_W = 256
POOL_HALO = 16

N_EXPERTS = 32
TOP_K = 4
D_FF = 1024
SWIGLU_LIMIT = 7.0
SWIGLU_ALPHA = 1.702

COL_GQ, COL_GK, COL_GV, COL_GR, COL_CQ, COL_POOL, COL_CKV, COL_MISC = 0, 128, 256, 512, 768, 1024, 1280, 1408
D_IN_PAD = 1536
MISC_GLOW = 0
MISC_ROPE = 16

LOG2E = 1.4426950408889634
TOKEN_TILE = 512
MIX_SUB, ROUTER_SUB, PLE_SUB = 512, 512, 256
GLA_TILE = 512
ATTN_TILE = 512
ATTN_SUB = 256
MOE_BLOCK = 512
MOE_CAST_ROWS = 256
VMEM_LIMIT = 56 * 1024 * 1024
NEG_BIG = -1e30


def _cparams(n_axes):
    return pltpu.CompilerParams(dimension_semantics=("arbitrary",) * n_axes,
                                vmem_limit_bytes=VMEM_LIMIT)


def _rms(x, g):
    return x * lax.rsqrt(jnp.mean(x * x, axis=-1, keepdims=True) + EPS) * g


def _dot(a, b):
    return jnp.dot(a, b, preferred_element_type=F32)


def _dot_nt(a, b):
    return lax.dot_general(a, b, (((1,), (1,)), ((), ())), preferred_element_type=F32)


def _dot_tn(a, b):
    return lax.dot_general(a, b, (((0,), (0,)), ((), ())), preferred_element_type=F32)


def _split3(x):
    hi = x.astype(BF16)
    r = x - hi.astype(F32)
    mid = r.astype(BF16)
    lo = (r - mid.astype(F32)).astype(BF16)
    return hi, mid, lo


def _split2(x):
    hi = x.astype(BF16)
    lo = (x - hi.astype(F32)).astype(BF16)
    return hi, lo


def _full(shape):
    nd = len(shape)
    return pl.BlockSpec(shape, lambda *_: (0,) * nd)


def _rope(x, c, s1, s2):
    return x * c + pltpu.roll(x, HEAD_PAD - 16, 1) * s1 + pltpu.roll(x, 16, 1) * s2


def _mix_pre_kernel(h_ref, mixn_ref, win_ref, wgate_ref, bgate_ref, qn_ref, wuq_ref, kvn_ref,
                    wukvk_ref, wukvv_ref, gq_ref, gk_ref, rc_ref, rs1_ref, rs2_ref,
                    wpool_ref, pscale_ref,
                    zg_ref, la_ref, q_ref, k_ref, v_ref, yp_ref, carry_ref, *, tiles_per_seq):
    tm = h_ref.shape[0]
    sub = MIX_SUB
    seq_tile = pl.program_id(0) % tiles_per_seq

    @pl.when(seq_tile == 0)
    def _():
        carry_ref[...] = jnp.zeros_like(carry_ref)

    lane = lax.broadcasted_iota(jnp.int32, (sub, HEAD_PAD), 1)
    in_rope = (lane >= MLA_NOPE) & (lane < MLA_QK)
    lane_v = lax.broadcasted_iota(jnp.int32, (sub, MLA_QK_PAD), 1)
    ones_lane = lane_v % HEAD_PAD == MLA_V
    lane_p = lax.broadcasted_iota(jnp.int32, (sub, POOL_W), 1)
    row_p = lax.broadcasted_iota(jnp.int32, (sub, POOL_W), 0)
    g0, g1, g2 = lane_p < 64, lane_p < 128, lane_p < 192
    win = jnp.where(g0, 2.0, jnp.where(g1, 4.0, jnp.where(g2, 8.0, 16.0)))
    gq, gq_sw, gk = gq_ref[0:1, :], gq_ref[1:2, :], gk_ref[...]

    for r0 in range(0, tm, sub):
        rows = slice(r0, r0 + sub)
        hn = _rms(h_ref[rows, :], mixn_ref[...]).astype(BF16)
        z = _dot(hn, win_ref[...])
        zg_ref[rows, :] = z[:, COL_GQ:COL_CQ]
        zm = z[:, COL_MISC:COL_MISC + 128]

        logit = _dot(zm.astype(BF16), wgate_ref[...]) + bgate_ref[...]
        la_ref[rows, :] = (jnp.minimum(logit, 0.0) - jnp.log(1.0 + jnp.exp(-jnp.abs(logit)))) * (1.0 / GLA_TAU)

        cqn = _rms(z[:, COL_CQ:COL_CQ + MLA_Q_RANK], qn_ref[...]).astype(BF16)
        qf = _dot(cqn, wuq_ref[...])
        ckvn = _rms(z[:, COL_CKV:COL_CKV + MLA_KV_RANK], kvn_ref[...]).astype(BF16)
        kn = _dot(ckvn, wukvk_ref[...])
        v_ref[rows, :] = jnp.where(ones_lane, 1.0, _dot(ckvn, wukvv_ref[...])).astype(BF16)

        rc, rs1, rs2 = rc_ref[rows, :], rs1_ref[rows, :], rs2_ref[rows, :]
        kr = jnp.where(in_rope, pltpu.roll(zm, MLA_NOPE - MISC_ROPE, 1), 0.0)
        kr_ss = jnp.sum(kr * kr, axis=-1, keepdims=True)
        krr = _rope(kr * gk, rc, rs1, rs2)
        cq = rc * gq
        sq_tab = (rs1 + rs2) * gq_sw
        for hh in range(MLA_HEADS):
            sl = slice(hh * HEAD_PAD, (hh + 1) * HEAD_PAD)
            qh = qf[:, sl]
            qsw = qf[:, MLA_QK_PAD + hh * HEAD_PAD:MLA_QK_PAD + (hh + 1) * HEAD_PAD]
            sq = lax.rsqrt(jnp.sum(qh * qh, axis=-1, keepdims=True) * (1.0 / MLA_QK) + EPS)
            q_ref[rows, sl] = ((qh * cq + qsw * sq_tab) * sq).astype(BF16)
            kh = kn[:, sl]
            sk = lax.rsqrt((jnp.sum(kh * kh, axis=-1, keepdims=True) + kr_ss) * (1.0 / MLA_QK) + EPS)
            k_ref[rows, sl] = (sk * (kh * gk + krr)).astype(BF16)

        u = z[:, COL_POOL:COL_POOL + POOL_W]
        xe = jnp.concatenate([carry_ref[...], u], axis=0)
        carry_ref[...] = u[sub - POOL_HALO:, :]
        s2 = xe + pltpu.roll(xe, 1, 0)
        s4 = s2 + pltpu.roll(s2, 2, 0)
        s8 = s4 + pltpu.roll(s4, 4, 0)
        s16 = s8 + pltpu.roll(s8, 8, 0)
        pooled = jnp.where(g0, s2[POOL_HALO:], jnp.where(g1, s4[POOL_HALO:],
                           jnp.where(g2, s8[POOL_HALO:], s16[POOL_HALO:])))
        cnt = jnp.minimum((seq_tile * tm + r0 + row_p + 1).astype(F32), win)
        d = pooled / cnt - u
        yp_ref[rows, :] = (_dot(d.astype(BF16), wpool_ref[...]) * pscale_ref[...]).astype(BF16)


def _mix_pre(h, w, rope_c, rope_s1, rope_s2, seq_len):
    T = h.shape[0]
    tm = TOKEN_TILE
    row = lambda n: pl.BlockSpec((tm, n), lambda i: (i, 0))
    ins = [h, w["mix_norm"], w["w_in"], w["gla_w_gate"], w["gla_b_gate"], w["mla_q_norm"], w["mla_w_uq"],
           w["mla_kv_norm"], w["mla_w_ukv_k"], w["mla_w_ukv_v"], w["mla_gq"], w["mla_gk"],
           rope_c, rope_s1, rope_s2, w["pool_w"], w["pool_scale"]]
    in_specs = [row(D_MODEL)] + [_full(a.shape) for a in ins[1:12]] + [row(HEAD_PAD)] * 3 + \
               [_full(w["pool_w"].shape), _full(w["pool_scale"].shape)]
    out_shape = [jax.ShapeDtypeStruct((T, COL_CQ), F32), jax.ShapeDtypeStruct((T, GLA_K), F32),
                 jax.ShapeDtypeStruct((T, MLA_QK_PAD), BF16), jax.ShapeDtypeStruct((T, MLA_QK_PAD), BF16),
                 jax.ShapeDtypeStruct((T, MLA_QK_PAD), BF16), jax.ShapeDtypeStruct((T, POOL_W), BF16)]
    out_specs = [row(COL_CQ), row(GLA_K), row(MLA_QK_PAD), row(MLA_QK_PAD), row(MLA_QK_PAD), row(POOL_W)]
    return pl.pallas_call(
        functools.partial(_mix_pre_kernel, tiles_per_seq=seq_len // tm),
        grid=(T // tm,), in_specs=in_specs, out_specs=out_specs, out_shape=out_shape,
        scratch_shapes=[pltpu.VMEM((POOL_HALO, POOL_W), F32)],
        compiler_params=_cparams(1), name="mix_pre")(*ins)


def _gla_kernel(zg_ref, la_ref, gn_ref, y_ref, state_ref, o_ref):
    tg = zg_ref.shape[0]
    C = GLA_CHUNK

    @pl.when(pl.program_id(1) == 0)
    def _():
        state_ref[...] = jnp.zeros_like(state_ref)

    r_i = lax.broadcasted_iota(jnp.int32, (C, C), 0)
    c_i = lax.broadcasted_iota(jnp.int32, (C, C), 1)
    tri = (r_i >= c_i).astype(BF16)
    ones = jnp.ones((C, GLA_W), BF16)
    head_k = lax.broadcasted_iota(jnp.int32, (C, GLA_K), 1) // GLA_DK
    head_v = lax.broadcasted_iota(jnp.int32, (C, GLA_W), 1) // GLA_DV
    ar = lax.broadcasted_iota(jnp.int32, (GLA_HEADS * C, C), 0)
    ac = lax.broadcasted_iota(jnp.int32, (GLA_HEADS * C, C), 1)
    causal = (ar % C) >= ac
    sk = lax.broadcasted_iota(jnp.int32, (GLA_K, GLA_W), 0) // GLA_DK
    sv = lax.broadcasted_iota(jnp.int32, (GLA_K, GLA_W), 1) // GLA_DV
    blockdiag = sk == sv

    for c in range(tg // C):
        rows = slice(c * C, (c + 1) * C)
        q = zg_ref[rows, COL_GQ:COL_GQ + GLA_K] * (GLA_DK ** -0.5)
        k = zg_ref[rows, COL_GK:COL_GK + GLA_K]
        v = zg_ref[rows, COL_GV:COL_GV + GLA_W].astype(BF16)
        la3 = _split3(la_ref[rows, :])
        bc = _dot(tri, la3[0]) + _dot(tri, la3[1]) + _dot(tri, la3[2])
        b_last = bc[C - 1:C, :]
        q_dec = (q * jnp.exp(bc)).astype(BF16)
        k_dec = (k * jnp.exp(-bc)).astype(BF16)
        k_end = (k * jnp.exp(b_last - bc)).astype(BF16)
        decay = jnp.exp(_dot_tn(la3[0], ones) + _dot_tn(la3[1], ones) + _dot_tn(la3[2], ones))
        zero = jnp.zeros_like(q_dec)
        qs = jnp.concatenate([jnp.where(head_k == hh, q_dec, zero) for hh in range(GLA_HEADS)], axis=0)
        att = jnp.where(causal, _dot_nt(qs, k_dec), 0.0).astype(BF16)
        o_full = _dot(att, v)
        o = _dot(q_dec, state_ref[...].astype(BF16))
        for hh in range(GLA_HEADS):
            o = o + jnp.where(head_v == hh, o_full[hh * C:(hh + 1) * C, :], 0.0)
        upd = _dot_tn(k_end, v)
        state_ref[...] = decay * state_ref[...] + jnp.where(blockdiag, upd, 0.0)
        o_ref[rows, :] = o

    o = o_ref[...]
    gr = lax.broadcasted_iota(jnp.int32, (GLA_W, GLA_W), 0) // GLA_DV
    gc = lax.broadcasted_iota(jnp.int32, (GLA_W, GLA_W), 1) // GLA_DV
    group = (gr == gc).astype(BF16)
    oo = _split2(o * o)
    ms = (_dot(oo[0], group) + _dot(oo[1], group)) * (1.0 / GLA_DV)
    r = zg_ref[:, COL_GR:COL_GR + GLA_W]
    y = o * lax.rsqrt(ms + EPS) * gn_ref[...] * (r / (1.0 + jnp.exp(-r)))
    y_ref[...] = y.astype(BF16)


def _gla(zg, la, gn, batch, seq_len):
    T = zg.shape[0]
    tg = GLA_TILE
    nt = seq_len // tg
    return pl.pallas_call(
        _gla_kernel, grid=(batch, nt),
        in_specs=[pl.BlockSpec((tg, COL_CQ), lambda b, s: (b * nt + s, 0)),
                  pl.BlockSpec((tg, GLA_K), lambda b, s: (b * nt + s, 0)),
                  _full(gn.shape)],
        out_specs=pl.BlockSpec((tg, GLA_W), lambda b, s: (b * nt + s, 0)),
        out_shape=jax.ShapeDtypeStruct((T, GLA_W), BF16),
        scratch_shapes=[pltpu.VMEM((GLA_K, GLA_W), F32), pltpu.VMEM((tg, GLA_W), F32)],
        compiler_params=_cparams(2), name="gla")(zg, la, gn)


def _attn_kernel(q_ref, k_ref, v_ref, o_ref, m_ref, acc_ref):
    tq = q_ref.shape[0]
    ts = ATTN_SUB
    i = pl.program_id(2)
    m_ref[...] = jnp.full_like(m_ref, NEG_BIG)
    acc_ref[...] = jnp.zeros_like(acc_ref)

    def sub_block(hh, start, r0, mask_off):
        hs = slice(hh * HEAD_PAD, (hh + 1) * HEAD_PAD)
        kj = k_ref[pl.ds(start, ts), hs]
        vj = v_ref[pl.ds(start, ts), hs]
        s = _dot_nt(q_ref[r0:, hs], kj)
        if mask_off is not None:
            row = lax.broadcasted_iota(jnp.int32, s.shape, 0) + r0
            col = lax.broadcasted_iota(jnp.int32, s.shape, 1) + mask_off
            s = jnp.where(col <= row, s, NEG_BIG)
        m_old = m_ref[hh, r0:, :]
        parts = [s[:, c * 128:(c + 1) * 128] for c in range(ts // 128)]
        m_new = jnp.maximum(m_old, jnp.max(functools.reduce(jnp.maximum, parts), axis=-1, keepdims=True))
        p = jnp.concatenate([jnp.exp2(x - m_new) for x in parts], axis=1).astype(BF16)
        acc_ref[hh, r0:, :] = jnp.exp2(m_old - m_new) * acc_ref[hh, r0:, :] + _dot(p, vj)
        m_ref[hh, r0:, :] = m_new

    def body(j, carry):
        base = pl.multiple_of(j * tq, tq)
        for sb in range(tq // ts):
            for hh in range(2):
                sub_block(hh, base + sb * ts, 0, None)
        return carry

    lax.fori_loop(0, i, body, 0)
    base = pl.multiple_of(i * tq, tq)
    for sb in range(tq // ts):
        for hh in range(2):
            sub_block(hh, base + sb * ts, sb * ts, sb * ts)
    outs = []
    for hh in range(2):
        a = acc_ref[hh]
        outs.append(a / a[:, MLA_V:MLA_V + 1])
    lane = lax.broadcasted_iota(jnp.int32, (tq, HEAD_PAD), 1)
    o_ref[...] = jnp.where(lane < MLA_V, outs[0], pltpu.roll(outs[1], MLA_V, 1)).astype(BF16)


def _attn(q, k, v, batch, seq_len):
    T = q.shape[0]
    tq = ATTN_TILE
    nq = seq_len // tq
    pairs = MLA_HEADS // 2
    return pl.pallas_call(
        _attn_kernel, grid=(batch, pairs, nq),
        in_specs=[pl.BlockSpec((tq, 2 * HEAD_PAD), lambda b, p, i: (b * nq + i, p)),
                  pl.BlockSpec((seq_len, 2 * HEAD_PAD), lambda b, p, i: (b, p)),
                  pl.BlockSpec((seq_len, 2 * HEAD_PAD), lambda b, p, i: (b, p))],
        out_specs=pl.BlockSpec((tq, 2 * MLA_V), lambda b, p, i: (b * nq + i, p)),
        out_shape=jax.ShapeDtypeStruct((T, MLA_W), BF16),
        scratch_shapes=[pltpu.VMEM((2, tq, HEAD_PAD), F32), pltpu.VMEM((2, tq, HEAD_PAD), F32)],
        compiler_params=_cparams(3), name="attn")(q, k, v)


def _out_router_kernel(h_ref, yg_ref, ym_ref, yp_ref, wo_ref, fn_ref, rw_hi_ref, rw_lo_ref, rb_ref,
                       h1_ref, hn_ref, idx_ref, gate_ref, cnt_ref, carry_ref):
    tm = h_ref.shape[0]

    @pl.when(pl.program_id(0) == 0)
    def _():
        carry_ref[...] = jnp.zeros_like(carry_ref)

    sub = ROUTER_SUB
    lane = lax.broadcasted_iota(jnp.int32, (sub, 128), 1)
    r_i = lax.broadcasted_iota(jnp.int32, (sub, sub), 0)
    c_i = lax.broadcasted_iota(jnp.int32, (sub, sub), 1)
    tri = (r_i >= c_i).astype(BF16)
    for r0 in range(0, tm, sub):
        rows = slice(r0, r0 + sub)
        h1 = (h_ref[rows, :] + _dot(yg_ref[rows, :], wo_ref[0:GLA_W, :])
              + _dot(ym_ref[rows, :], wo_ref[GLA_W:GLA_W + MLA_W, :])
              + _dot(yp_ref[rows, :], wo_ref[GLA_W + MLA_W:, :]))
        h1_ref[rows, :] = h1
        hn = _rms(h1, fn_ref[...])
        hi, lo = _split2(hn)
        hn_ref[rows, :] = hi.astype(F32)
        logits = _dot(hi, rw_hi_ref[...]) + _dot(lo, rw_hi_ref[...]) + _dot(hi, rw_lo_ref[...]) + rb_ref[...]
        cur = jnp.where(lane < N_EXPERTS, logits, NEG_BIG)
        idx_out = jnp.zeros((sub, 128), jnp.int32)
        val_out = jnp.zeros((sub, 128), F32)
        chosen = jnp.zeros((sub, 128), F32)
        top0 = None
        sels = []
        for kk in range(TOP_K):
            m = jnp.max(cur, axis=-1, keepdims=True)
            sel = jnp.min(jnp.where(cur == m, lane, 128), axis=-1, keepdims=True)
            if kk == 0:
                top0 = m
            sels.append(sel)
            idx_out = jnp.where(lane == kk, sel, idx_out)
            val_out = jnp.where(lane == kk, jnp.exp(m - top0), val_out)
            chosen = jnp.where(lane == sel, 1.0, chosen)
            cur = jnp.where(lane == sel, NEG_BIG, cur)
        gate_ref[rows, :] = val_out / jnp.sum(val_out, axis=-1, keepdims=True)

        incl = _dot(tri, chosen.astype(BF16))
        before = carry_ref[0:1, :] + incl - chosen
        for kk in range(TOP_K):
            rank = jnp.sum(jnp.where(lane == sels[kk], before, 0.0), axis=-1, keepdims=True)
            idx_out = jnp.where(lane == TOP_K + kk, rank.astype(jnp.int32), idx_out)
        idx_ref[rows, :] = idx_out
        carry_ref[...] = carry_ref[...] + incl[sub - 1:sub, :]
    cnt_ref[...] = carry_ref[...].astype(jnp.int32)


def _out_router(h, yg, ym, yp, w):
    T = h.shape[0]
    tm = TOKEN_TILE
    row = lambda n: pl.BlockSpec((tm, n), lambda i: (i, 0))
    ins = [h, yg, ym, yp, w["w_out"], w["ffn_norm"], w["router_w_hi"], w["router_w_lo"], w["router_b"]]
    return pl.pallas_call(
        _out_router_kernel, grid=(T // tm,),
        in_specs=[row(D_MODEL), row(GLA_W), row(MLA_W), row(POOL_W)] + [_full(a.shape) for a in ins[4:]],
        out_specs=[row(D_MODEL), row(D_MODEL), row(128), row(128), _full((8, 128))],
        out_shape=[jax.ShapeDtypeStruct((T, D_MODEL), F32), jax.ShapeDtypeStruct((T, D_MODEL), F32),
                   jax.ShapeDtypeStruct((T, 128), jnp.int32), jax.ShapeDtypeStruct((T, 128), F32),
                   jax.ShapeDtypeStruct((8, 128), jnp.int32)],
        scratch_shapes=[pltpu.VMEM((8, 128), F32)],
        compiler_params=_cparams(1), name="out_router")(*ins)


def _moe_kernel(be_ref, nb_ref, x_ref, wg_ref, bg_ref, wu_ref, bu_ref, wd_ref, bd_ref, y_ref,
                wg_bf, wu_bf, wd_bf):
    i = pl.program_id(0)
    used = i < nb_ref[0]
    new_expert = (i == 0) | (be_ref[i] != be_ref[jnp.maximum(i - 1, 0)])

    @pl.when(used & new_expert)
    def _():
        for src, dst in ((wg_ref, wg_bf), (wu_ref, wu_bf), (wd_ref, wd_bf)):
            for r in range(0, src.shape[2], MOE_CAST_ROWS):
                dst[r:r + MOE_CAST_ROWS, :] = src[0, 0, r:r + MOE_CAST_ROWS, :].astype(BF16)

    @pl.when(used)
    def _():
        x = x_ref[...].astype(BF16)
        g = jnp.minimum(_dot(x, wg_bf[...]) + bg_ref[0], SWIGLU_LIMIT)
        up = jnp.clip(_dot(x, wu_bf[...]) + bu_ref[0], -SWIGLU_LIMIT, SWIGLU_LIMIT)
        hb = (up + 1.0) * (g / (1.0 + jnp.exp(-SWIGLU_ALPHA * g)))
        y_ref[...] = _dot(hb.astype(BF16), wd_bf[...]) + bd_ref[0]

    @pl.when(jnp.logical_not(used))
    def _():
        y_ref[...] = jnp.zeros_like(y_ref)


def _moe(xs, block_e, n_used, w):
    n_rows = xs.shape[0]
    bm = MOE_BLOCK
    layer = w["layer"]
    wspec = lambda shp: pl.BlockSpec((1, 1) + shp, lambda i, be, nb: (layer, be[i], 0, 0))
    bspec = lambda shp: pl.BlockSpec((1,) + shp, lambda i, be, nb: (be[i], 0, 0))
    grid_spec = pltpu.PrefetchScalarGridSpec(
        num_scalar_prefetch=2, grid=(n_rows // bm,),
        in_specs=[pl.BlockSpec((bm, D_MODEL), lambda i, be, nb: (i, 0)),
                  wspec((D_MODEL, D_FF)), bspec((1, D_FF)), wspec((D_MODEL, D_FF)), bspec((1, D_FF)),
                  wspec((D_FF, D_MODEL)), bspec((1, D_MODEL))],
        out_specs=pl.BlockSpec((bm, D_MODEL), lambda i, be, nb: (i, 0)),
        scratch_shapes=[pltpu.VMEM((D_MODEL, D_FF), BF16), pltpu.VMEM((D_MODEL, D_FF), BF16),
                        pltpu.VMEM((D_FF, D_MODEL), BF16)])
    return pl.pallas_call(
        _moe_kernel, grid_spec=grid_spec,
        out_shape=jax.ShapeDtypeStruct((n_rows, D_MODEL), F32),
        compiler_params=_cparams(1), name="moe")(
            block_e, n_used, xs, w["moe_w_gate"], w["moe_b_gate"], w["moe_w_up"], w["moe_b_up"],
            w["moe_w_down"], w["moe_b_down"])


def _ple_kernel(h1_ref, y0_ref, y1_ref, y2_ref, y3_ref, gate_ref, p_ref, wple_ref, gn_ref, wpg_ref, pn_ref, o_ref):
    sub = PLE_SUB
    for r0 in range(0, h1_ref.shape[0], sub):
        rows = slice(r0, r0 + sub)
        gates = gate_ref[rows, :]
        h2 = h1_ref[rows, :]
        for kk, y_ref in enumerate((y0_ref, y1_ref, y2_ref, y3_ref)):
            h2 = h2 + gates[:, kk:kk + 1] * y_ref[rows, :]
        e = _dot(p_ref[rows, :].astype(BF16), wple_ref[...])
        a = _dot(_rms(h2, gn_ref[...]).astype(BF16), wpg_ref[...])
        gate = 1.0 / (1.0 + jnp.exp(-a))
        o_ref[rows, :] = h2 + _rms(e * gate, pn_ref[...])


def _ple(h1, ys_k, gates, p, w):
    T = h1.shape[0]
    tm = TOKEN_TILE
    row = lambda n: pl.BlockSpec((tm, n), lambda i: (i, 0))
    ins = [h1, *ys_k, gates, p, w["ple_w_proj"], w["ple_gate_norm"], w["ple_w_gate"], w["ple_post_norm"]]
    return pl.pallas_call(
        _ple_kernel, grid=(T // tm,),
        in_specs=[row(D_MODEL)] * (1 + TOP_K) + [row(128), row(D_PLE)] + [_full(a.shape) for a in ins[7:]],
        out_specs=row(D_MODEL), out_shape=jax.ShapeDtypeStruct((T, D_MODEL), F32),
        compiler_params=_cparams(1), name="ple")(*ins)


def _pad_heads(wm, per_head, n_heads=MLA_HEADS):
    kdim = wm.shape[0]
    w3 = wm.reshape(kdim, n_heads, per_head)
    return jnp.pad(w3, ((0, 0), (0, 0), (0, HEAD_PAD - per_head))).reshape(kdim, n_heads * HEAD_PAD)


def _swap_rope_halves(a):
    a3 = a.reshape(a.shape[0], -1, HEAD_PAD)
    half = MLA_ROPE // 2
    x1 = a3[:, :, MLA_NOPE:MLA_NOPE + half]
    x2 = a3[:, :, MLA_NOPE + half:MLA_QK]
    out = jnp.zeros_like(a3).at[:, :, MLA_NOPE:MLA_NOPE + half].set(x2).at[:, :, MLA_NOPE + half:MLA_QK].set(x1)
    return out.reshape(a.shape)


def _layer_params(i, mix_norm, w_in, gla_w_gate, gla_b_gate, gla_out_norm, mla_q_norm, mla_w_uq, mla_kv_norm,
                  mla_w_ukv, mla_qk_q_norm, mla_qk_k_norm, pool_w, pool_scale, w_out, ffn_norm, router_w,
                  router_b, moe_w_gate, moe_b_gate, moe_w_up, moe_b_up, moe_w_down, moe_b_down,
                  ple_w_proj, ple_gate_norm, ple_w_gate, ple_post_norm):
    wi = w_in[i]
    c = np.cumsum((0, 128, 128, 256, 16, 256, 256, 128, 32, 256))
    gq, gk, gv, glow, gr, cq, ckv, krope, upool = [wi[:, c[j]:c[j + 1]] for j in range(9)]
    misc = jnp.concatenate([glow, krope, jnp.zeros((D_MODEL, 128 - 48), F32)], axis=1)
    w_in_p = jnp.concatenate([gq, gk, gv, gr, cq, upool, ckv, misc], axis=1).astype(BF16)
    wgate_p = jnp.zeros((128, GLA_K), F32).at[MISC_GLOW:MISC_GLOW + GLA_GATE_RANK].set(gla_w_gate[i]).astype(BF16)
    ukv = mla_w_ukv[i].reshape(MLA_KV_RANK, MLA_HEADS, MLA_NOPE + MLA_V)
    ukv_k = _pad_heads(ukv[:, :, :MLA_NOPE].reshape(MLA_KV_RANK, MLA_HEADS * MLA_NOPE), MLA_NOPE)
    ukv_v = _pad_heads(ukv[:, :, MLA_NOPE:].reshape(MLA_KV_RANK, MLA_W), MLA_V)
    pw = pool_w[i]
    pool_bd = jnp.zeros((POOL_W, POOL_W), F32)
    for g in range(4):
        pool_bd = pool_bd.at[g * 64:(g + 1) * 64, g * 64:(g + 1) * 64].set(pw[g])
    rw = jnp.pad(router_w[i], ((0, 0), (0, 128 - N_EXPERTS)))
    rw_hi = rw.astype(BF16)
    rw_lo = (rw - rw_hi.astype(F32)).astype(BF16)
    row = lambda a: a.reshape(1, -1)
    pad96 = lambda a: jnp.pad(a, (0, HEAD_PAD - MLA_QK)).reshape(1, HEAD_PAD)
    wuq_p = _pad_heads(mla_w_uq[i], MLA_QK)
    gq_p = pad96(mla_qk_q_norm[i] * (MLA_QK ** -0.5 * LOG2E))
    return {
        "mix_norm": row(mix_norm[i]), "w_in": w_in_p, "gla_w_gate": wgate_p, "gla_b_gate": row(gla_b_gate[i]),
        "gla_out_norm": row(jnp.tile(gla_out_norm[i], GLA_HEADS)),
        "mla_q_norm": row(mla_q_norm[i]),
        "mla_w_uq": jnp.concatenate([wuq_p, _swap_rope_halves(wuq_p)], axis=1).astype(BF16),
        "mla_kv_norm": row(mla_kv_norm[i]), "mla_w_ukv_k": ukv_k.astype(BF16), "mla_w_ukv_v": ukv_v.astype(BF16),
        "mla_gq": jnp.concatenate([gq_p, _swap_rope_halves(gq_p)], axis=0), "mla_gk": pad96(mla_qk_k_norm[i]),
        "pool_w": pool_bd.astype(BF16), "pool_scale": row(pool_scale[i]),
        "w_out": w_out[i].astype(BF16), "ffn_norm": row(ffn_norm[i]),
        "router_w_hi": rw_hi, "router_w_lo": rw_lo,
        "router_b": row(jnp.pad(router_b[i], (0, 128 - N_EXPERTS))),
        "layer": i,
        "moe_w_gate": moe_w_gate, "moe_b_gate": moe_b_gate[i].reshape(N_EXPERTS, 1, D_FF),
        "moe_w_up": moe_w_up, "moe_b_up": moe_b_up[i].reshape(N_EXPERTS, 1, D_FF),
        "moe_w_down": moe_w_down, "moe_b_down": moe_b_down[i].reshape(N_EXPERTS, 1, D_MODEL),
        "ple_w_proj": ple_w_proj[i].astype(BF16), "ple_gate_norm": row(ple_gate_norm[i]),
        "ple_w_gate": ple_w_gate[i].astype(BF16), "ple_post_norm": row(ple_post_norm[i]),
    }


def _rope_tables(positions):
    T = positions.size
    inv = ROPE_BASE ** (-jnp.arange(0, MLA_ROPE, 2, dtype=F32) / MLA_ROPE)
    ang = positions.reshape(T, 1).astype(F32) * inv
    cos, sin = jnp.cos(ang), jnp.sin(ang)
    z16 = jnp.zeros((T, 16), F32)
    tail = jnp.zeros((T, HEAD_PAD - MLA_QK), F32)
    c = jnp.concatenate([jnp.ones((T, MLA_NOPE), F32), cos, cos, tail], axis=1)
    s1 = jnp.concatenate([jnp.zeros((T, MLA_NOPE), F32), -sin, z16, tail], axis=1)
    s2 = jnp.concatenate([jnp.zeros((T, MLA_NOPE), F32), z16, sin, tail], axis=1)
    return c, s1, s2


def _route(top_idx, rank, counts, T):
    bm = MOE_BLOCK
    A = T * TOP_K
    padded = (counts + bm - 1) // bm * bm
    pad_end = jnp.cumsum(padded)
    pad_start = pad_end - padded
    experts = jnp.arange(N_EXPERTS, dtype=jnp.int32)
    dest = rank + jnp.sum(jnp.where(top_idx[:, :, None] == experts, pad_start, 0), axis=-1)
    n_blocks = (A + N_EXPERTS * (bm - 1) + bm - 1) // bm
    n_rows = n_blocks * bm
    tok = jnp.broadcast_to(jnp.arange(T, dtype=jnp.int32)[:, None], (T, TOP_K))
    row_tok = jnp.zeros((n_rows,), jnp.int32).at[dest.reshape(A)].set(
        tok.reshape(A), unique_indices=True, mode="promise_in_bounds")
    block_start = jnp.arange(n_blocks, dtype=jnp.int32) * bm
    block_e = jnp.minimum(jnp.sum((pad_end[None, :] <= block_start[:, None]).astype(jnp.int32), axis=1),
                          N_EXPERTS - 1)
    n_used = (pad_end[-1] // bm).astype(jnp.int32).reshape(1)
    return dest, row_tok, block_e, n_used


def kernel(x, p, positions, mix_norm, w_in, gla_w_gate, gla_b_gate, gla_out_norm, mla_q_norm, mla_w_uq,
           mla_kv_norm, mla_w_ukv, mla_qk_q_norm, mla_qk_k_norm, pool_w, pool_scale, w_out, ffn_norm,
           router_w, router_b, moe_w_gate, moe_b_gate, moe_w_up, moe_b_up, moe_w_down, moe_b_down,
           ple_w_proj, ple_gate_norm, ple_w_gate, ple_post_norm):
    B, S, D = x.shape
    T = B * S
    depth = p.shape[0]
    params = (mix_norm, w_in, gla_w_gate, gla_b_gate, gla_out_norm, mla_q_norm, mla_w_uq, mla_kv_norm,
              mla_w_ukv, mla_qk_q_norm, mla_qk_k_norm, pool_w, pool_scale, w_out, ffn_norm, router_w,
              router_b, moe_w_gate, moe_b_gate, moe_w_up, moe_b_up, moe_w_down, moe_b_down,
              ple_w_proj, ple_gate_norm, ple_w_gate, ple_post_norm)
    rope_c, rope_s1, rope_s2 = _rope_tables(positions)
    h = x.reshape(T, D)
    for i in range(depth):
        w = _layer_params(i, *params)
        zg, la, q, k, v, y_pool = _mix_pre(h, w, rope_c, rope_s1, rope_s2, S)
        y_gla = _gla(zg, la, w["gla_out_norm"], B, S)
        y_mla = _attn(q, k, v, B, S)
        h1, hn, route, gates, counts = _out_router(h, y_gla, y_mla, y_pool, w)
        dest, row_tok, block_e, n_used = _route(route[:, :TOP_K], route[:, TOP_K:2 * TOP_K],
                                                counts[0, :N_EXPERTS], T)
        take = lambda a, idx: a.at[idx].get(mode="promise_in_bounds")
        ys = _moe(take(hn, row_tok), block_e, n_used, w)
        h = _ple(h1, [take(ys, dest[:, kk]) for kk in range(TOP_K)], gates, p[i].reshape(T, D_PLE), w)
    return h.reshape(B, S, D)
```

```python
import functools

import jax
import jax.numpy as jnp
import numpy as np
from jax import lax
from jax.experimental import pallas as pl
from jax.experimental.pallas import tpu as pltpu

F32 = jnp.float32
BF16 = jnp.bfloat16

D_MODEL = 1024
EPS = 1e-6
D_PLE = 256

GLA_HEADS = 4
GLA_DK = 32
GLA_DV = 64
GLA_GATE_RANK = 16
GLA_TAU = 16.0
GLA_CHUNK = 64
GLA_K = GLA_HEADS * GLA_DK
GLA_W = GLA_HEADS * GLA_DV

MLA_HEADS = 8
MLA_Q_RANK = 256
MLA_KV_RANK = 128
MLA_NOPE = 64
MLA_ROPE = 32
MLA_QK = MLA_NOPE + MLA_ROPE
MLA_V = 64
MLA_W = MLA_HEADS * MLA_V
ROPE_BASE = 10000.0
HEAD_PAD = 128
MLA_QK_PAD = MLA_HEADS * HEAD_PAD

POOL_WINDOWS = (2, 4, 8, 16)
POOL_GROUP = 64
POOL_W = 256
POOL_HALO = 16

N_EXPERTS = 32
TOP_K = 4
D_FF = 1024
SWIGLU_LIMIT = 7.0
SWIGLU_ALPHA = 1.702

COL_GQ, COL_GK, COL_GV, COL_GR, COL_CQ, COL_POOL, COL_CKV, COL_MISC = 0, 128, 256, 512, 768, 1024, 1280, 1408
D_IN_PAD = 1536
MISC_GLOW = 0
MISC_ROPE = 16

LOG2E = 1.4426950408889634
TOKEN_TILE = 512
MIX_SUB, ROUTER_SUB, PLE_SUB = 512, 512, 256
GLA_TILE = 512
ATTN_TILE = 1024
ATTN_SUB = 512
MOE_BLOCK = 512
MOE_CAST_ROWS = 256
VMEM_LIMIT = 56 * 1024 * 1024
NEG_BIG = -1e30


def _cparams(n_axes, **flags):
    return pltpu.CompilerParams(dimension_semantics=("arbitrary",) * n_axes,
                                vmem_limit_bytes=VMEM_LIMIT, flags=flags or None)


def _rms(x, g):
    return x * lax.rsqrt(jnp.mean(x * x, axis=-1, keepdims=True) + EPS) * g


def _dot(a, b):
    return jnp.dot(a, b, preferred_element_type=F32)


def _dot_nt(a, b):
    return lax.dot_general(a, b, (((1,), (1,)), ((), ())), preferred_element_type=F32)


def _dot_tn(a, b):
    return lax.dot_general(a, b, (((0,), (0,)), ((), ())), preferred_element_type=F32)


def _split3(x):
    hi = x.astype(BF16)
    r = x - hi.astype(F32)
    mid = r.astype(BF16)
    lo = (r - mid.astype(F32)).astype(BF16)
    return hi, mid, lo


def _split2(x):
    hi = x.astype(BF16)
    lo = (x - hi.astype(F32)).astype(BF16)
    return hi, lo


def _full(shape):
    nd = len(shape)
    return pl.BlockSpec(shape, lambda *_: (0,) * nd)


def _rope(x, c, s1, s2):
    return x * c + pltpu.roll(x, HEAD_PAD - 16, 1) * s1 + pltpu.roll(x, 16, 1) * s2


def _mix_pre_kernel(h_ref, mixn_ref, win_ref, wgate_ref, bgate_ref, qn_ref, wuq_ref, kvn_ref,
                    wukvk_ref, wukvv_ref, gq_ref, gk_ref, rc_ref, rs1_ref, rs2_ref,
                    wpool_ref, pscale_ref,
                    zg_ref, la_ref, q_ref, k_ref, v_ref, yp_ref, carry_ref, *, tiles_per_seq):
    tm = h_ref.shape[0]
    sub = MIX_SUB
    seq_tile = pl.program_id(0) % tiles_per_seq

    @pl.when(seq_tile == 0)
    def _():
        carry_ref[...] = jnp.zeros_like(carry_ref)

    lane = lax.broadcasted_iota(jnp.int32, (sub, HEAD_PAD), 1)
    in_rope = (lane >= MLA_NOPE) & (lane < MLA_QK)
    lane_v = lax.broadcasted_iota(jnp.int32, (sub, MLA_QK_PAD), 1)
    ones_lane = lane_v % HEAD_PAD == MLA_V
    lane_p = lax.broadcasted_iota(jnp.int32, (sub, POOL_W), 1)
    row_p = lax.broadcasted_iota(jnp.int32, (sub, POOL_W), 0)
    g0, g1, g2 = lane_p < 64, lane_p < 128, lane_p < 192
    win = jnp.where(g0, 2.0, jnp.where(g1, 4.0, jnp.where(g2, 8.0, 16.0)))
    gq, gq_sw, gk = gq_ref[0:1, :], gq_ref[1:2, :], gk_ref[...]

    for r0 in range(0, tm, sub):
        rows = slice(r0, r0 + sub)
        hn = _rms(h_ref[rows, :], mixn_ref[...]).astype(BF16)
        z = _dot(hn, win_ref[...])
        zg_ref[rows, :] = z[:, COL_GQ:COL_CQ]
        zm = z[:, COL_MISC:COL_MISC + 128]

        logit = _dot(zm.astype(BF16), wgate_ref[...]) + bgate_ref[...]
        la_ref[rows, :] = (jnp.minimum(logit, 0.0) - jnp.log(1.0 + jnp.exp(-jnp.abs(logit)))) * (1.0 / GLA_TAU)

        cqn = _rms(z[:, COL_CQ:COL_CQ + MLA_Q_RANK], qn_ref[...]).astype(BF16)
        qf = _dot(cqn, wuq_ref[...])
        ckvn = _rms(z[:, COL_CKV:COL_CKV + MLA_KV_RANK], kvn_ref[...]).astype(BF16)
        kn = _dot(ckvn, wukvk_ref[...])
        v_ref[rows, :] = jnp.where(ones_lane, 1.0, _dot(ckvn, wukvv_ref[...])).astype(BF16)

        rc, rs1, rs2 = rc_ref[rows, :], rs1_ref[rows, :], rs2_ref[rows, :]
        kr = jnp.where(in_rope, pltpu.roll(zm, MLA_NOPE - MISC_ROPE, 1), 0.0)
        kr_ss = jnp.sum(kr * kr, axis=-1, keepdims=True)
        krr = _rope(kr * gk, rc, rs1, rs2)
        cq = rc * gq
        sq_tab = (rs1 + rs2) * gq_sw
        for hh in range(MLA_HEADS):
            sl = slice(hh * HEAD_PAD, (hh + 1) * HEAD_PAD)
            qh = qf[:, sl]
            qsw = qf[:, MLA_QK_PAD + hh * HEAD_PAD:MLA_QK_PAD + (hh + 1) * HEAD_PAD]
            sq = lax.rsqrt(jnp.sum(qh * qh, axis=-1, keepdims=True) * (1.0 / MLA_QK) + EPS)
            q_ref[rows, sl] = ((qh * cq + qsw * sq_tab) * sq).astype(BF16)
            kh = kn[:, sl]
            sk = lax.rsqrt((jnp.sum(kh * kh, axis=-1, keepdims=True) + kr_ss) * (1.0 / MLA_QK) + EPS)
            k_ref[rows, sl] = (sk * (kh * gk + krr)).astype(BF16)

        u = z[:, COL_POOL:COL_POOL + POOL_W]
        xe = jnp.concatenate([carry_ref[...], u], axis=0)
        carry_ref[...] = u[sub - POOL_HALO:, :]
        s2 = xe + pltpu.roll(xe, 1, 0)
        s4 = s2 + pltpu.roll(s2, 2, 0)
        s8 = s4 + pltpu.roll(s4, 4, 0)
        s16 = s8 + pltpu.roll(s8, 8, 0)
        pooled = jnp.where(g0, s2[POOL_HALO:], jnp.where(g1, s4[POOL_HALO:],
                           jnp.where(g2, s8[POOL_HALO:], s16[POOL_HALO:])))
        cnt = jnp.minimum((seq_tile * tm + r0 + row_p + 1).astype(F32), win)
        d = pooled / cnt - u
        yp_ref[rows, :] = (_dot(d.astype(BF16), wpool_ref[...]) * pscale_ref[...]).astype(BF16)


def _mix_pre(h, w, rope_c, rope_s1, rope_s2, seq_len):
    T = h.shape[0]
    tm = TOKEN_TILE
    row = lambda n: pl.BlockSpec((tm, n), lambda i: (i, 0))
    ins = [h, w["mix_norm"], w["w_in"], w["gla_w_gate"], w["gla_b_gate"], w["mla_q_norm"], w["mla_w_uq"],
           w["mla_kv_norm"], w["mla_w_ukv_k"], w["mla_w_ukv_v"], w["mla_gq"], w["mla_gk"],
           rope_c, rope_s1, rope_s2, w["pool_w"], w["pool_scale"]]
    in_specs = [row(D_MODEL)] + [_full(a.shape) for a in ins[1:12]] + [row(HEAD_PAD)] * 3 + \
               [_full(w["pool_w"].shape), _full(w["pool_scale"].shape)]
    out_shape = [jax.ShapeDtypeStruct((T, COL_CQ), F32), jax.ShapeDtypeStruct((T, GLA_K), F32),
                 jax.ShapeDtypeStruct((T, MLA_QK_PAD), BF16), jax.ShapeDtypeStruct((T, MLA_QK_PAD), BF16),
                 jax.ShapeDtypeStruct((T, MLA_QK_PAD), BF16), jax.ShapeDtypeStruct((T, POOL_W), BF16)]
    out_specs = [row(COL_CQ), row(GLA_K), row(MLA_QK_PAD), row(MLA_QK_PAD), row(MLA_QK_PAD), row(POOL_W)]
    return pl.pallas_call(
        functools.partial(_mix_pre_kernel, tiles_per_seq=seq_len // tm),
        grid=(T // tm,), in_specs=in_specs, out_specs=out_specs, out_shape=out_shape,
        scratch_shapes=[pltpu.VMEM((POOL_HALO, POOL_W), F32)],
        compiler_params=_cparams(1), name="mix_pre")(*ins)


def _gla_kernel(zg_ref, la_ref, gn_ref, y_ref, state_ref, o_ref):
    tg = zg_ref.shape[0]
    C = GLA_CHUNK

    @pl.when(pl.program_id(1) == 0)
    def _():
        state_ref[...] = jnp.zeros_like(state_ref)

    r_i = lax.broadcasted_iota(jnp.int32, (C, C), 0)
    c_i = lax.broadcasted_iota(jnp.int32, (C, C), 1)
    tri = (r_i >= c_i).astype(BF16)
    ones = jnp.ones((C, GLA_W), BF16)
    head_k = lax.broadcasted_iota(jnp.int32, (C, GLA_K), 1) // GLA_DK
    head_v = lax.broadcasted_iota(jnp.int32, (C, GLA_W), 1) // GLA_DV
    ar = lax.broadcasted_iota(jnp.int32, (GLA_HEADS * C, C), 0)
    ac = lax.broadcasted_iota(jnp.int32, (GLA_HEADS * C, C), 1)
    causal = (ar % C) >= ac
    sk = lax.broadcasted_iota(jnp.int32, (GLA_K, GLA_W), 0) // GLA_DK
    sv = lax.broadcasted_iota(jnp.int32, (GLA_K, GLA_W), 1) // GLA_DV
    blockdiag = sk == sv

    for c in range(tg // C):
        rows = slice(c * C, (c + 1) * C)
        q = zg_ref[rows, COL_GQ:COL_GQ + GLA_K] * (GLA_DK ** -0.5)
        k = zg_ref[rows, COL_GK:COL_GK + GLA_K]
        v = zg_ref[rows, COL_GV:COL_GV + GLA_W].astype(BF16)
        la3 = _split3(la_ref[rows, :])
        bc = _dot(tri, la3[0]) + _dot(tri, la3[1]) + _dot(tri, la3[2])
        b_last = bc[C - 1:C, :]
        q_dec = (q * jnp.exp(bc)).astype(BF16)
        k_dec = (k * jnp.exp(-bc)).astype(BF16)
        k_end = (k * jnp.exp(b_last - bc)).astype(BF16)
        decay = jnp.exp(_dot_tn(la3[0], ones) + _dot_tn(la3[1], ones) + _dot_tn(la3[2], ones))
        zero = jnp.zeros_like(q_dec)
        qs = jnp.concatenate([jnp.where(head_k == hh, q_dec, zero) for hh in range(GLA_HEADS)], axis=0)
        att = jnp.where(causal, _dot_nt(qs, k_dec), 0.0).astype(BF16)
        o_full = _dot(att, v)
        o = _dot(q_dec, state_ref[...].astype(BF16))
        for hh in range(GLA_HEADS):
            o = o + jnp.where(head_v == hh, o_full[hh * C:(hh + 1) * C, :], 0.0)
        upd = _dot_tn(k_end, v)
        state_ref[...] = decay * state_ref[...] + jnp.where(blockdiag, upd, 0.0)
        o_ref[rows, :] = o

    o = o_ref[...]
    gr = lax.broadcasted_iota(jnp.int32, (GLA_W, GLA_W), 0) // GLA_DV
    gc = lax.broadcasted_iota(jnp.int32, (GLA_W, GLA_W), 1) // GLA_DV
    group = (gr == gc).astype(BF16)
    oo = _split2(o * o)
    ms = (_dot(oo[0], group) + _dot(oo[1], group)) * (1.0 / GLA_DV)
    r = zg_ref[:, COL_GR:COL_GR + GLA_W]
    y = o * lax.rsqrt(ms + EPS) * gn_ref[...] * (r / (1.0 + jnp.exp(-r)))
    y_ref[...] = y.astype(BF16)


def _gla(zg, la, gn, batch, seq_len):
    T = zg.shape[0]
    tg = GLA_TILE
    nt = seq_len // tg
    return pl.pallas_call(
        _gla_kernel, grid=(batch, nt),
        in_specs=[pl.BlockSpec((tg, COL_CQ), lambda b, s: (b * nt + s, 0)),
                  pl.BlockSpec((tg, GLA_K), lambda b, s: (b * nt + s, 0)),
                  _full(gn.shape)],
        out_specs=pl.BlockSpec((tg, GLA_W), lambda b, s: (b * nt + s, 0)),
        out_shape=jax.ShapeDtypeStruct((T, GLA_W), BF16),
        scratch_shapes=[pltpu.VMEM((GLA_K, GLA_W), F32), pltpu.VMEM((tg, GLA_W), F32)],
        compiler_params=_cparams(2), name="gla")(zg, la, gn)


def _attn_kernel(q_ref, k_ref, v_ref, o_ref, m_ref, acc_ref):
    tq = q_ref.shape[0]
    ts = ATTN_SUB
    i = pl.program_id(2)
    m_ref[...] = jnp.full_like(m_ref, NEG_BIG)
    acc_ref[...] = jnp.zeros_like(acc_ref)

    def sub_block(hh, start, r0, mask_off):
        hs = slice(hh * HEAD_PAD, (hh + 1) * HEAD_PAD)
        kj = k_ref[pl.ds(start, ts), hs]
        vj = v_ref[pl.ds(start, ts), hs]
        s = _dot_nt(q_ref[r0:, hs], kj)
        if mask_off is not None:
            row = lax.broadcasted_iota(jnp.int32, s.shape, 0) + r0
            col = lax.broadcasted_iota(jnp.int32, s.shape, 1) + mask_off
            s = jnp.where(col <= row, s, NEG_BIG)
        m_old = m_ref[hh, r0:, :]
        parts = [s[:, c * 128:(c + 1) * 128] for c in range(ts // 128)]
        m_new = jnp.maximum(m_old, jnp.max(functools.reduce(jnp.maximum, parts), axis=-1, keepdims=True))
        p = jnp.concatenate([jnp.exp2(x - m_new) for x in parts], axis=1).astype(BF16)
        acc_ref[hh, r0:, :] = jnp.exp2(m_old - m_new) * acc_ref[hh, r0:, :] + _dot(p, vj)
        m_ref[hh, r0:, :] = m_new

    def body(j, carry):
        base = pl.multiple_of(j * tq, tq)
        for sb in range(tq // ts):
            for hh in range(2):
                sub_block(hh, base + sb * ts, 0, None)
        return carry

    lax.fori_loop(0, i, body, 0)
    base = pl.multiple_of(i * tq, tq)
    for sb in range(tq // ts):
        for hh in range(2):
            sub_block(hh, base + sb * ts, sb * ts, sb * ts)
    outs = []
    for hh in range(2):
        a = acc_ref[hh]
        outs.append(a / a[:, MLA_V:MLA_V + 1])
    lane = lax.broadcasted_iota(jnp.int32, (tq, HEAD_PAD), 1)
    o_ref[...] = jnp.where(lane < MLA_V, outs[0], pltpu.roll(outs[1], MLA_V, 1)).astype(BF16)


def _attn(q, k, v, batch, seq_len):
    T = q.shape[0]
    tq = ATTN_TILE
    nq = seq_len // tq
    pairs = MLA_HEADS // 2
    return pl.pallas_call(
        _attn_kernel, grid=(batch, pairs, nq),
        in_specs=[pl.BlockSpec((tq, 2 * HEAD_PAD), lambda b, p, i: (b * nq + i, p)),
                  pl.BlockSpec((seq_len, 2 * HEAD_PAD), lambda b, p, i: (b, p)),
                  pl.BlockSpec((seq_len, 2 * HEAD_PAD), lambda b, p, i: (b, p))],
        out_specs=pl.BlockSpec((tq, 2 * MLA_V), lambda b, p, i: (b * nq + i, p)),
        out_shape=jax.ShapeDtypeStruct((T, MLA_W), BF16),
        scratch_shapes=[pltpu.VMEM((2, tq, HEAD_PAD), F32), pltpu.VMEM((2, tq, HEAD_PAD), F32)],
        compiler_params=_cparams(3), name="attn")(q, k, v)


def _out_router_kernel(h_ref, yg_ref, ym_ref, yp_ref, wo_ref, fn_ref, rw_hi_ref, rw_lo_ref, rb_ref,
                       h1_ref, hn_ref, idx_ref, gate_ref, cnt_ref, carry_ref):
    tm = h_ref.shape[0]

    @pl.when(pl.program_id(0) == 0)
    def _():
        carry_ref[...] = jnp.zeros_like(carry_ref)

    sub = ROUTER_SUB
    lane = lax.broadcasted_iota(jnp.int32, (sub, 128), 1)
    r_i = lax.broadcasted_iota(jnp.int32, (sub, sub), 0)
    c_i = lax.broadcasted_iota(jnp.int32, (sub, sub), 1)
    tri = (r_i >= c_i).astype(BF16)
    for r0 in range(0, tm, sub):
        rows = slice(r0, r0 + sub)
        h1 = (h_ref[rows, :] + _dot(yg_ref[rows, :], wo_ref[0:GLA_W, :])
              + _dot(ym_ref[rows, :], wo_ref[GLA_W:GLA_W + MLA_W, :])
              + _dot(yp_ref[rows, :], wo_ref[GLA_W + MLA_W:, :]))
        h1_ref[rows, :] = h1
        hn = _rms(h1, fn_ref[...])
        hi, lo = _split2(hn)
        hn_ref[rows, :] = hi.astype(F32)
        logits = _dot(hi, rw_hi_ref[...]) + _dot(lo, rw_hi_ref[...]) + _dot(hi, rw_lo_ref[...]) + rb_ref[...]
        cur = jnp.where(lane < N_EXPERTS, logits, NEG_BIG)
        idx_out = jnp.zeros((sub, 128), jnp.int32)
        val_out = jnp.zeros((sub, 128), F32)
        chosen = jnp.zeros((sub, 128), F32)
        top0 = None
        sels = []
        for kk in range(TOP_K):
            m = jnp.max(cur, axis=-1, keepdims=True)
            sel = jnp.min(jnp.where(cur == m, lane, 128), axis=-1, keepdims=True)
            if kk == 0:
                top0 = m
            sels.append(sel)
            idx_out = jnp.where(lane == kk, sel, idx_out)
            val_out = jnp.where(lane == kk, jnp.exp(m - top0), val_out)
            chosen = jnp.where(lane == sel, 1.0, chosen)
            cur = jnp.where(lane == sel, NEG_BIG, cur)
        gate_ref[rows, :] = val_out / jnp.sum(val_out, axis=-1, keepdims=True)

        incl = _dot(tri, chosen.astype(BF16))
        before = carry_ref[0:1, :] + incl - chosen
        for kk in range(TOP_K):
            rank = jnp.sum(jnp.where(lane == sels[kk], before, 0.0), axis=-1, keepdims=True)
            idx_out = jnp.where(lane == TOP_K + kk, rank.astype(jnp.int32), idx_out)
        idx_ref[rows, :] = idx_out
        carry_ref[...] = carry_ref[...] + incl[sub - 1:sub, :]
    cnt_ref[...] = carry_ref[...].astype(jnp.int32)


def _out_router(h, yg, ym, yp, w):
    T = h.shape[0]
    tm = TOKEN_TILE
    row = lambda n: pl.BlockSpec((tm, n), lambda i: (i, 0))
    ins = [h, yg, ym, yp, w["w_out"], w["ffn_norm"], w["router_w_hi"], w["router_w_lo"], w["router_b"]]
    return pl.pallas_call(
        _out_router_kernel, grid=(T // tm,),
        in_specs=[row(D_MODEL), row(GLA_W), row(MLA_W), row(POOL_W)] + [_full(a.shape) for a in ins[4:]],
        out_specs=[row(D_MODEL), row(D_MODEL), row(128), row(128), _full((8, 128))],
        out_shape=[jax.ShapeDtypeStruct((T, D_MODEL), F32), jax.ShapeDtypeStruct((T, D_MODEL), F32),
                   jax.ShapeDtypeStruct((T, 128), jnp.int32), jax.ShapeDtypeStruct((T, 128), F32),
                   jax.ShapeDtypeStruct((8, 128), jnp.int32)],
        scratch_shapes=[pltpu.VMEM((8, 128), F32)],
        compiler_params=_cparams(1), name="out_router")(*ins)


def _moe_kernel(be_ref, nb_ref, x_ref, wg_ref, bg_ref, wu_ref, bu_ref, wd_ref, bd_ref, y_ref,
                wg_bf, wu_bf, wd_bf):
    i = pl.program_id(0)
    used = i < nb_ref[0]
    new_expert = (i == 0) | (be_ref[i] != be_ref[jnp.maximum(i - 1, 0)])

    @pl.when(used & new_expert)
    def _():
        for src, dst in ((wg_ref, wg_bf), (wu_ref, wu_bf), (wd_ref, wd_bf)):
            for r in range(0, src.shape[2], MOE_CAST_ROWS):
                dst[r:r + MOE_CAST_ROWS, :] = src[0, 0, r:r + MOE_CAST_ROWS, :].astype(BF16)

    @pl.when(used)
    def _():
        x = x_ref[...].astype(BF16)
        g = jnp.minimum(_dot(x, wg_bf[...]) + bg_ref[0], SWIGLU_LIMIT)
        up = jnp.clip(_dot(x, wu_bf[...]) + bu_ref[0], -SWIGLU_LIMIT, SWIGLU_LIMIT)
        hb = (up + 1.0) * (g / (1.0 + jnp.exp(-SWIGLU_ALPHA * g)))
        y_ref[...] = _dot(hb.astype(BF16), wd_bf[...]) + bd_ref[0]

    @pl.when(jnp.logical_not(used))
    def _():
        y_ref[...] = jnp.zeros_like(y_ref)


def _moe(xs, block_e, n_used, w):
    n_rows = xs.shape[0]
    bm = MOE_BLOCK
    layer = w["layer"]
    wspec = lambda shp: pl.BlockSpec((1, 1) + shp, lambda i, be, nb: (layer, be[i], 0, 0))
    bspec = lambda shp: pl.BlockSpec((1,) + shp, lambda i, be, nb: (be[i], 0, 0))
    grid_spec = pltpu.PrefetchScalarGridSpec(
        num_scalar_prefetch=2, grid=(n_rows // bm,),
        in_specs=[pl.BlockSpec((bm, D_MODEL), lambda i, be, nb: (i, 0)),
                  wspec((D_MODEL, D_FF)), bspec((1, D_FF)), wspec((D_MODEL, D_FF)), bspec((1, D_FF)),
                  wspec((D_FF, D_MODEL)), bspec((1, D_MODEL))],
        out_specs=pl.BlockSpec((bm, D_MODEL), lambda i, be, nb: (i, 0)),
        scratch_shapes=[pltpu.VMEM((D_MODEL, D_FF), BF16), pltpu.VMEM((D_MODEL, D_FF), BF16),
                        pltpu.VMEM((D_FF, D_MODEL), BF16)])
    return pl.pallas_call(
        _moe_kernel, grid_spec=grid_spec,
        out_shape=jax.ShapeDtypeStruct((n_rows, D_MODEL), F32),
        compiler_params=_cparams(1), name="moe")(
            block_e, n_used, xs, w["moe_w_gate"], w["moe_b_gate"], w["moe_w_up"], w["moe_b_up"],
            w["moe_w_down"], w["moe_b_down"])


def _ple_kernel(h1_ref, y0_ref, y1_ref, y2_ref, y3_ref, gate_ref, p_ref, wple_ref, gn_ref, wpg_ref, pn_ref, o_ref):
    sub = PLE_SUB
    for r0 in range(0, h1_ref.shape[0], sub):
        rows = slice(r0, r0 + sub)
        gates = gate_ref[rows, :]
        h2 = h1_ref[rows, :]
        for kk, y_ref in enumerate((y0_ref, y1_ref, y2_ref, y3_ref)):
            h2 = h2 + gates[:, kk:kk + 1] * y_ref[rows, :]
        e = _dot(p_ref[rows, :].astype(BF16), wple_ref[...])
        a = _dot(_rms(h2, gn_ref[...]).astype(BF16), wpg_ref[...])
        gate = 1.0 / (1.0 + jnp.exp(-a))
        o_ref[rows, :] = h2 + _rms(e * gate, pn_ref[...])


def _ple(h1, ys_k, gates, p, w):
    T = h1.shape[0]
    tm = TOKEN_TILE
    row = lambda n: pl.BlockSpec((tm, n), lambda i: (i, 0))
    ins = [h1, *ys_k, gates, p, w["ple_w_proj"], w["ple_gate_norm"], w["ple_w_gate"], w["ple_post_norm"]]
    return pl.pallas_call(
        _ple_kernel, grid=(T // tm,),
        in_specs=[row(D_MODEL)] * (1 + TOP_K) + [row(128), row(D_PLE)] + [_full(a.shape) for a in ins[7:]],
        out_specs=row(D_MODEL), out_shape=jax.ShapeDtypeStruct((T, D_MODEL), F32),
        compiler_params=_cparams(1), name="ple")(*ins)


def _pad_heads(wm, per_head, n_heads=MLA_HEADS):
    kdim = wm.shape[0]
    w3 = wm.reshape(kdim, n_heads, per_head)
    return jnp.pad(w3, ((0, 0), (0, 0), (0, HEAD_PAD - per_head))).reshape(kdim, n_heads * HEAD_PAD)


def _swap_rope_halves(a):
    a3 = a.reshape(a.shape[0], -1, HEAD_PAD)
    half = MLA_ROPE // 2
    x1 = a3[:, :, MLA_NOPE:MLA_NOPE + half]
    x2 = a3[:, :, MLA_NOPE + half:MLA_QK]
    out = jnp.zeros_like(a3).at[:, :, MLA_NOPE:MLA_NOPE + half].set(x2).at[:, :, MLA_NOPE + half:MLA_QK].set(x1)
    return out.reshape(a.shape)


def _layer_params(i, mix_norm, w_in, gla_w_gate, gla_b_gate, gla_out_norm, mla_q_norm, mla_w_uq, mla_kv_norm,
                  mla_w_ukv, mla_qk_q_norm, mla_qk_k_norm, pool_w, pool_scale, w_out, ffn_norm, router_w,
                  router_b, moe_w_gate, moe_b_gate, moe_w_up, moe_b_up, moe_w_down, moe_b_down,
                  ple_w_proj, ple_gate_norm, ple_w_gate, ple_post_norm):
    wi = w_in[i]
    c = np.cumsum((0, 128, 128, 256, 16, 256, 256, 128, 32, 256))
    gq, gk, gv, glow, gr, cq, ckv, krope, upool = [wi[:, c[j]:c[j + 1]] for j in range(9)]
    misc = jnp.concatenate([glow, krope, jnp.zeros((D_MODEL, 128 - 48), F32)], axis=1)
    w_in_p = jnp.concatenate([gq, gk, gv, gr, cq, upool, ckv, misc], axis=1).astype(BF16)
    wgate_p = jnp.zeros((128, GLA_K), F32).at[MISC_GLOW:MISC_GLOW + GLA_GATE_RANK].set(gla_w_gate[i]).astype(BF16)
    ukv = mla_w_ukv[i].reshape(MLA_KV_RANK, MLA_HEADS, MLA_NOPE + MLA_V)
    ukv_k = _pad_heads(ukv[:, :, :MLA_NOPE].reshape(MLA_KV_RANK, MLA_HEADS * MLA_NOPE), MLA_NOPE)
    ukv_v = _pad_heads(ukv[:, :, MLA_NOPE:].reshape(MLA_KV_RANK, MLA_W), MLA_V)
    pw = pool_w[i]
    pool_bd = jnp.zeros((POOL_W, POOL_W), F32)
    for g in range(4):
        pool_bd = pool_bd.at[g * 64:(g + 1) * 64, g * 64:(g + 1) * 64].set(pw[g])
    rw = jnp.pad(router_w[i], ((0, 0), (0, 128 - N_EXPERTS)))
    rw_hi = rw.astype(BF16)
    rw_lo = (rw - rw_hi.astype(F32)).astype(BF16)
    row = lambda a: a.reshape(1, -1)
    pad96 = lambda a: jnp.pad(a, (0, HEAD_PAD - MLA_QK)).reshape(1, HEAD_PAD)
    wuq_p = _pad_heads(mla_w_uq[i], MLA_QK)
    gq_p = pad96(mla_qk_q_norm[i] * (MLA_QK ** -0.5 * LOG2E))
    return {
        "mix_norm": row(mix_norm[i]), "w_in": w_in_p, "gla_w_gate": wgate_p, "gla_b_gate": row(gla_b_gate[i]),
        "gla_out_norm": row(jnp.tile(gla_out_norm[i], GLA_HEADS)),
        "mla_q_norm": row(mla_q_norm[i]),
        "mla_w_uq": jnp.concatenate([wuq_p, _swap_rope_halves(wuq_p)], axis=1).astype(BF16),
        "mla_kv_norm": row(mla_kv_norm[i]), "mla_w_ukv_k": ukv_k.astype(BF16), "mla_w_ukv_v": ukv_v.astype(BF16),
        "mla_gq": jnp.concatenate([gq_p, _swap_rope_halves(gq_p)], axis=0), "mla_gk": pad96(mla_qk_k_norm[i]),
        "pool_w": pool_bd.astype(BF16), "pool_scale": row(pool_scale[i]),
        "w_out": w_out[i].astype(BF16), "ffn_norm": row(ffn_norm[i]),
        "router_w_hi": rw_hi, "router_w_lo": rw_lo,
        "router_b": row(jnp.pad(router_b[i], (0, 128 - N_EXPERTS))),
        "layer": i,
        "moe_w_gate": moe_w_gate, "moe_b_gate": moe_b_gate[i].reshape(N_EXPERTS, 1, D_FF),
        "moe_w_up": moe_w_up, "moe_b_up": moe_b_up[i].reshape(N_EXPERTS, 1, D_FF),
        "moe_w_down": moe_w_down, "moe_b_down": moe_b_down[i].reshape(N_EXPERTS, 1, D_MODEL),
        "ple_w_proj": ple_w_proj[i].astype(BF16), "ple_gate_norm": row(ple_gate_norm[i]),
        "ple_w_gate": ple_w_gate[i].astype(BF16), "ple_post_norm": row(ple_post_norm[i]),
    }


def _rope_tables(positions):
    T = positions.size
    inv = ROPE_BASE ** (-jnp.arange(0, MLA_ROPE, 2, dtype=F32) / MLA_ROPE)
    ang = positions.reshape(T, 1).astype(F32) * inv
    cos, sin = jnp.cos(ang), jnp.sin(ang)
    z16 = jnp.zeros((T, 16), F32)
    tail = jnp.zeros((T, HEAD_PAD - MLA_QK), F32)
    c = jnp.concatenate([jnp.ones((T, MLA_NOPE), F32), cos, cos, tail], axis=1)
    s1 = jnp.concatenate([jnp.zeros((T, MLA_NOPE), F32), -sin, z16, tail], axis=1)
    s2 = jnp.concatenate([jnp.zeros((T, MLA_NOPE), F32), z16, sin, tail], axis=1)
    return c, s1, s2


def _route(top_idx, rank, counts, T):
    bm = MOE_BLOCK
    A = T * TOP_K
    padded = (counts + bm - 1) // bm * bm
    pad_end = jnp.cumsum(padded)
    pad_start = pad_end - padded
    experts = jnp.arange(N_EXPERTS, dtype=jnp.int32)
    dest = rank + jnp.sum(jnp.where(top_idx[:, :, None] == experts, pad_start, 0), axis=-1)
    n_blocks = (A + N_EXPERTS * (bm - 1) + bm - 1) // bm
    n_rows = n_blocks * bm
    tok = jnp.broadcast_to(jnp.arange(T, dtype=jnp.int32)[:, None], (T, TOP_K))
    slot = jnp.arange(bm, dtype=jnp.int32)[None, :]
    pad_key = jnp.where(slot < (padded - counts)[:, None], (pad_start + counts)[:, None] + slot, n_rows)
    keys = jnp.concatenate([dest.reshape(A), pad_key.reshape(N_EXPERTS * bm)])
    vals = jnp.concatenate([tok.reshape(A), jnp.zeros((N_EXPERTS * bm,), jnp.int32)])
    row_tok = lax.sort_key_val(keys, vals, is_stable=False)[1][:n_rows]
    block_start = jnp.arange(n_blocks, dtype=jnp.int32) * bm
    block_e = jnp.minimum(jnp.sum((pad_end[None, :] <= block_start[:, None]).astype(jnp.int32), axis=1),
                          N_EXPERTS - 1)
    n_used = (pad_end[-1] // bm).astype(jnp.int32).reshape(1)
    return dest, row_tok, block_e, n_used


def kernel(x, p, positions, mix_norm, w_in, gla_w_gate, gla_b_gate, gla_out_norm, mla_q_norm, mla_w_uq,
           mla_kv_norm, mla_w_ukv, mla_qk_q_norm, mla_qk_k_norm, pool_w, pool_scale, w_out, ffn_norm,
           router_w, router_b, moe_w_gate, moe_b_gate, moe_w_up, moe_b_up, moe_w_down, moe_b_down,
           ple_w_proj, ple_gate_norm, ple_w_gate, ple_post_norm):
    B, S, D = x.shape
    T = B * S
    depth = p.shape[0]
    params = (mix_norm, w_in, gla_w_gate, gla_b_gate, gla_out_norm, mla_q_norm, mla_w_uq, mla_kv_norm,
              mla_w_ukv, mla_qk_q_norm, mla_qk_k_norm, pool_w, pool_scale, w_out, ffn_norm, router_w,
              router_b, moe_w_gate, moe_b_gate, moe_w_up, moe_b_up, moe_w_down, moe_b_down,
              ple_w_proj, ple_gate_norm, ple_w_gate, ple_post_norm)
    rope_c, rope_s1, rope_s2 = _rope_tables(positions)
    h = x.reshape(T, D)
    for i in range(depth):
        w = _layer_params(i, *params)
        zg, la, q, k, v, y_pool = _mix_pre(h, w, rope_c, rope_s1, rope_s2, S)
        y_gla = _gla(zg, la, w["gla_out_norm"], B, S)
        y_mla = _attn(q, k, v, B, S)
        h1, hn, route, gates, counts = _out_router(h, y_gla, y_mla, y_pool, w)
        dest, row_tok, block_e, n_used = _route(route[:, :TOP_K], route[:, TOP_K:2 * TOP_K],
                                                counts[0, :N_EXPERTS], T)
        take = lambda a, idx: a.at[idx].get(mode="promise_in_bounds")
        ys = _moe(take(hn, row_tok), block_e, n_used, w)
        h = _ple(h1, [take(ys, dest[:, kk]) for kk in range(TOP_K)], gates, p[i].reshape(T, D_PLE), w)
    return h.reshape(B, S, D)
```

```python
import functools

import jax
import jax.numpy as jnp
import numpy as np
from jax import lax
from jax.experimental import pallas as pl
from jax.experimental.pallas import tpu as pltpu
from jax.experimental.pallas import tpu_sc as plsc

F32 = jnp.float32
BF16 = jnp.bfloat16

D_MODEL = 1024
EPS = 1e-6
D_PLE = 256

GLA_HEADS = 4
GLA_DK = 32
GLA_DV = 64
GLA_GATE_RANK = 16
GLA_TAU = 16.0
GLA_CHUNK = 64
GLA_K = GLA_HEADS * GLA_DK
GLA_W = GLA_HEADS * GLA_DV

MLA_HEADS = 8
MLA_Q_RANK = 256
MLA_KV_RANK = 128
MLA_NOPE = 64
MLA_ROPE = 32
MLA_QK = MLA_NOPE + MLA_ROPE
MLA_V = 64
MLA_W = MLA_HEADS * MLA_V
ROPE_BASE = 10000.0
HEAD_PAD = 128
MLA_QK_PAD = MLA_HEADS * HEAD_PAD

POOL_WINDOWS = (2, 4, 8, 16)
POOL_GROUP = 64
POOL_W = 256
POOL_HALO = 16

N_EXPERTS = 32
TOP_K = 4
D_FF = 1024
SWIGLU_LIMIT = 7.0
SWIGLU_ALPHA = 1.702

COL_GQ, COL_GK, COL_GV, COL_GR, COL_CQ, COL_POOL, COL_CKV, COL_MISC = 0, 128, 256, 512, 768, 1024, 1280, 1408
D_IN_PAD = 1536
MISC_GLOW = 0
MISC_ROPE = 16

LOG2E = 1.4426950408889634
TOKEN_TILE = 512
MIX_SUB, ROUTER_SUB, PLE_SUB = 512, 512, 256
GLA_TILE = 512
ATTN_TILE = 1024
ATTN_SUB = 512
MOE_BLOCK = 512
MOE_CAST_ROWS = 256
DISPATCH_ROWS = 128
DISPATCH_SLABS = 4
VMEM_LIMIT = 56 * 1024 * 1024
NEG_BIG = -1e30


def _cparams(n_axes, **flags):
    return pltpu.CompilerParams(dimension_semantics=("arbitrary",) * n_axes,
                                vmem_limit_bytes=VMEM_LIMIT, flags=flags or None)


def _rms(x, g):
    return x * lax.rsqrt(jnp.mean(x * x, axis=-1, keepdims=True) + EPS) * g


def _dot(a, b):
    return jnp.dot(a, b, preferred_element_type=F32)


def _dot_nt(a, b):
    return lax.dot_general(a, b, (((1,), (1,)), ((), ())), preferred_element_type=F32)


def _dot_tn(a, b):
    return lax.dot_general(a, b, (((0,), (0,)), ((), ())), preferred_element_type=F32)


def _split3(x):
    hi = x.astype(BF16)
    r = x - hi.astype(F32)
    mid = r.astype(BF16)
    lo = (r - mid.astype(F32)).astype(BF16)
    return hi, mid, lo


def _split2(x):
    hi = x.astype(BF16)
    lo = (x - hi.astype(F32)).astype(BF16)
    return hi, lo


def _full(shape):
    nd = len(shape)
    return pl.BlockSpec(shape, lambda *_: (0,) * nd)


def _rope(x, c, s1, s2):
    return x * c + pltpu.roll(x, HEAD_PAD - 16, 1) * s1 + pltpu.roll(x, 16, 1) * s2


def _mix_pre_kernel(h_ref, mixn_ref, win_ref, wgate_ref, bgate_ref, qn_ref, wuq_ref, kvn_ref,
                    wukvk_ref, wukvv_ref, gq_ref, gk_ref, rc_ref, rs1_ref, rs2_ref,
                    wpool_ref, pscale_ref,
                    zg_ref, la_ref, q_ref, k_ref, v_ref, yp_ref, carry_ref, *, tiles_per_seq):
    tm = h_ref.shape[0]
    sub = MIX_SUB
    seq_tile = pl.program_id(0) % tiles_per_seq

    @pl.when(seq_tile == 0)
    def _():
        carry_ref[...] = jnp.zeros_like(carry_ref)

    lane = lax.broadcasted_iota(jnp.int32, (sub, HEAD_PAD), 1)
    in_rope = (lane >= MLA_NOPE) & (lane < MLA_QK)
    lane_v = lax.broadcasted_iota(jnp.int32, (sub, MLA_QK_PAD), 1)
    ones_lane = lane_v % HEAD_PAD == MLA_V
    lane_p = lax.broadcasted_iota(jnp.int32, (sub, POOL_W), 1)
    row_p = lax.broadcasted_iota(jnp.int32, (sub, POOL_W), 0)
    g0, g1, g2 = lane_p < 64, lane_p < 128, lane_p < 192
    win = jnp.where(g0, 2.0, jnp.where(g1, 4.0, jnp.where(g2, 8.0, 16.0)))
    gq, gq_sw, gk = gq_ref[0:1, :], gq_ref[1:2, :], gk_ref[...]

    for r0 in range(0, tm, sub):
        rows = slice(r0, r0 + sub)
        hn = _rms(h_ref[rows, :], mixn_ref[...]).astype(BF16)
        z = _dot(hn, win_ref[...])
        zg_ref[rows, :] = z[:, COL_GQ:COL_CQ]
        zm = z[:, COL_MISC:COL_MISC + 128]

        logit = _dot(zm.astype(BF16), wgate_ref[...]) + bgate_ref[...]
        la_ref[rows, :] = (jnp.minimum(logit, 0.0) - jnp.log(1.0 + jnp.exp(-jnp.abs(logit)))) * (1.0 / GLA_TAU)

        cqn = _rms(z[:, COL_CQ:COL_CQ + MLA_Q_RANK], qn_ref[...]).astype(BF16)
        qf = _dot(cqn, wuq_ref[...])
        ckvn = _rms(z[:, COL_CKV:COL_CKV + MLA_KV_RANK], kvn_ref[...]).astype(BF16)
        kn = _dot(ckvn, wukvk_ref[...])
        v_ref[rows, :] = jnp.where(ones_lane, 1.0, _dot(ckvn, wukvv_ref[...])).astype(BF16)

        rc, rs1, rs2 = rc_ref[rows, :], rs1_ref[rows, :], rs2_ref[rows, :]
        kr = jnp.where(in_rope, pltpu.roll(zm, MLA_NOPE - MISC_ROPE, 1), 0.0)
        kr_ss = jnp.sum(kr * kr, axis=-1, keepdims=True)
        krr = _rope(kr * gk, rc, rs1, rs2)
        cq = rc * gq
        sq_tab = (rs1 + rs2) * gq_sw
        for hh in range(MLA_HEADS):
            sl = slice(hh * HEAD_PAD, (hh + 1) * HEAD_PAD)
            qh = qf[:, sl]
            qsw = qf[:, MLA_QK_PAD + hh * HEAD_PAD:MLA_QK_PAD + (hh + 1) * HEAD_PAD]
            sq = lax.rsqrt(jnp.sum(qh * qh, axis=-1, keepdims=True) * (1.0 / MLA_QK) + EPS)
            q_ref[rows, sl] = ((qh * cq + qsw * sq_tab) * sq).astype(BF16)
            kh = kn[:, sl]
            sk = lax.rsqrt((jnp.sum(kh * kh, axis=-1, keepdims=True) + kr_ss) * (1.0 / MLA_QK) + EPS)
            k_ref[rows, sl] = (sk * (kh * gk + krr)).astype(BF16)

        u = z[:, COL_POOL:COL_POOL + POOL_W]
        xe = jnp.concatenate([carry_ref[...], u], axis=0)
        carry_ref[...] = u[sub - POOL_HALO:, :]
        s2 = xe + pltpu.roll(xe, 1, 0)
        s4 = s2 + pltpu.roll(s2, 2, 0)
        s8 = s4 + pltpu.roll(s4, 4, 0)
        s16 = s8 + pltpu.roll(s8, 8, 0)
        pooled = jnp.where(g0, s2[POOL_HALO:], jnp.where(g1, s4[POOL_HALO:],
                           jnp.where(g2, s8[POOL_HALO:], s16[POOL_HALO:])))
        cnt = jnp.minimum((seq_tile * tm + r0 + row_p + 1).astype(F32), win)
        d = pooled / cnt - u
        yp_ref[rows, :] = (_dot(d.astype(BF16), wpool_ref[...]) * pscale_ref[...]).astype(BF16)


def _mix_pre(h, w, rope_c, rope_s1, rope_s2, seq_len):
    T = h.shape[0]
    tm = TOKEN_TILE
    row = lambda n: pl.BlockSpec((tm, n), lambda i: (i, 0))
    ins = [h, w["mix_norm"], w["w_in"], w["gla_w_gate"], w["gla_b_gate"], w["mla_q_norm"], w["mla_w_uq"],
           w["mla_kv_norm"], w["mla_w_ukv_k"], w["mla_w_ukv_v"], w["mla_gq"], w["mla_gk"],
           rope_c, rope_s1, rope_s2, w["pool_w"], w["pool_scale"]]
    in_specs = [row(D_MODEL)] + [_full(a.shape) for a in ins[1:12]] + [row(HEAD_PAD)] * 3 + \
               [_full(w["pool_w"].shape), _full(w["pool_scale"].shape)]
    out_shape = [jax.ShapeDtypeStruct((T, COL_CQ), F32), jax.ShapeDtypeStruct((T, GLA_K), F32),
                 jax.ShapeDtypeStruct((T, MLA_QK_PAD), BF16), jax.ShapeDtypeStruct((T, MLA_QK_PAD), BF16),
                 jax.ShapeDtypeStruct((T, MLA_QK_PAD), BF16), jax.ShapeDtypeStruct((T, POOL_W), BF16)]
    out_specs = [row(COL_CQ), row(GLA_K), row(MLA_QK_PAD), row(MLA_QK_PAD), row(MLA_QK_PAD), row(POOL_W)]
    return pl.pallas_call(
        functools.partial(_mix_pre_kernel, tiles_per_seq=seq_len // tm),
        grid=(T // tm,), in_specs=in_specs, out_specs=out_specs, out_shape=out_shape,
        scratch_shapes=[pltpu.VMEM((POOL_HALO, POOL_W), F32)],
        compiler_params=_cparams(1), name="mix_pre")(*ins)


def _gla_kernel(zg_ref, la_ref, gn_ref, y_ref, state_ref, o_ref):
    tg = zg_ref.shape[0]
    C = GLA_CHUNK

    @pl.when(pl.program_id(1) == 0)
    def _():
        state_ref[...] = jnp.zeros_like(state_ref)

    r_i = lax.broadcasted_iota(jnp.int32, (C, C), 0)
    c_i = lax.broadcasted_iota(jnp.int32, (C, C), 1)
    tri = (r_i >= c_i).astype(BF16)
    ones = jnp.ones((C, GLA_W), BF16)
    head_k = lax.broadcasted_iota(jnp.int32, (C, GLA_K), 1) // GLA_DK
    head_v = lax.broadcasted_iota(jnp.int32, (C, GLA_W), 1) // GLA_DV
    ar = lax.broadcasted_iota(jnp.int32, (GLA_HEADS * C, C), 0)
    ac = lax.broadcasted_iota(jnp.int32, (GLA_HEADS * C, C), 1)
    causal = (ar % C) >= ac
    sk = lax.broadcasted_iota(jnp.int32, (GLA_K, GLA_W), 0) // GLA_DK
    sv = lax.broadcasted_iota(jnp.int32, (GLA_K, GLA_W), 1) // GLA_DV
    blockdiag = sk == sv

    for c in range(tg // C):
        rows = slice(c * C, (c + 1) * C)
        q = zg_ref[rows, COL_GQ:COL_GQ + GLA_K] * (GLA_DK ** -0.5)
        k = zg_ref[rows, COL_GK:COL_GK + GLA_K]
        v = zg_ref[rows, COL_GV:COL_GV + GLA_W].astype(BF16)
        la3 = _split3(la_ref[rows, :])
        bc = _dot(tri, la3[0]) + _dot(tri, la3[1]) + _dot(tri, la3[2])
        b_last = bc[C - 1:C, :]
        q_dec = (q * jnp.exp(bc)).astype(BF16)
        k_dec = (k * jnp.exp(-bc)).astype(BF16)
        k_end = (k * jnp.exp(b_last - bc)).astype(BF16)
        decay = jnp.exp(_dot_tn(la3[0], ones) + _dot_tn(la3[1], ones) + _dot_tn(la3[2], ones))
        zero = jnp.zeros_like(q_dec)
        qs = jnp.concatenate([jnp.where(head_k == hh, q_dec, zero) for hh in range(GLA_HEADS)], axis=0)
        att = jnp.where(causal, _dot_nt(qs, k_dec), 0.0).astype(BF16)
        o_full = _dot(att, v)
        o = _dot(q_dec, state_ref[...].astype(BF16))
        for hh in range(GLA_HEADS):
            o = o + jnp.where(head_v == hh, o_full[hh * C:(hh + 1) * C, :], 0.0)
        upd = _dot_tn(k_end, v)
        state_ref[...] = decay * state_ref[...] + jnp.where(blockdiag, upd, 0.0)
        o_ref[rows, :] = o

    o = o_ref[...]
    gr = lax.broadcasted_iota(jnp.int32, (GLA_W, GLA_W), 0) // GLA_DV
    gc = lax.broadcasted_iota(jnp.int32, (GLA_W, GLA_W), 1) // GLA_DV
    group = (gr == gc).astype(BF16)
    oo = _split2(o * o)
    ms = (_dot(oo[0], group) + _dot(oo[1], group)) * (1.0 / GLA_DV)
    r = zg_ref[:, COL_GR:COL_GR + GLA_W]
    y = o * lax.rsqrt(ms + EPS) * gn_ref[...] * (r / (1.0 + jnp.exp(-r)))
    y_ref[...] = y.astype(BF16)


def _gla(zg, la, gn, batch, seq_len):
    T = zg.shape[0]
    tg = GLA_TILE
    nt = seq_len // tg
    return pl.pallas_call(
        _gla_kernel, grid=(batch, nt),
        in_specs=[pl.BlockSpec((tg, COL_CQ), lambda b, s: (b * nt + s, 0)),
                  pl.BlockSpec((tg, GLA_K), lambda b, s: (b * nt + s, 0)),
                  _full(gn.shape)],
        out_specs=pl.BlockSpec((tg, GLA_W), lambda b, s: (b * nt + s, 0)),
        out_shape=jax.ShapeDtypeStruct((T, GLA_W), BF16),
        scratch_shapes=[pltpu.VMEM((GLA_K, GLA_W), F32), pltpu.VMEM((tg, GLA_W), F32)],
        compiler_params=_cparams(2), name="gla")(zg, la, gn)


def _attn_kernel(q_ref, k_ref, v_ref, o_ref, m_ref, acc_ref):
    tq = q_ref.shape[0]
    ts = ATTN_SUB
    i = pl.program_id(2)
    m_ref[...] = jnp.full_like(m_ref, NEG_BIG)
    acc_ref[...] = jnp.zeros_like(acc_ref)

    def sub_block(hh, start, r0, mask_off):
        hs = slice(hh * HEAD_PAD, (hh + 1) * HEAD_PAD)
        kj = k_ref[pl.ds(start, ts), hs]
        vj = v_ref[pl.ds(start, ts), hs]
        s = _dot_nt(q_ref[r0:, hs], kj)
        if mask_off is not None:
            row = lax.broadcasted_iota(jnp.int32, s.shape, 0) + r0
            col = lax.broadcasted_iota(jnp.int32, s.shape, 1) + mask_off
            s = jnp.where(col <= row, s, NEG_BIG)
        m_old = m_ref[hh, r0:, :]
        parts = [s[:, c * 128:(c + 1) * 128] for c in range(ts // 128)]
        m_new = jnp.maximum(m_old, jnp.max(functools.reduce(jnp.maximum, parts), axis=-1, keepdims=True))
        p = jnp.concatenate([jnp.exp2(x - m_new) for x in parts], axis=1).astype(BF16)
        acc_ref[hh, r0:, :] = jnp.exp2(m_old - m_new) * acc_ref[hh, r0:, :] + _dot(p, vj)
        m_ref[hh, r0:, :] = m_new

    def body(j, carry):
        base = pl.multiple_of(j * tq, tq)
        for sb in range(tq // ts):
            for hh in range(2):
                sub_block(hh, base + sb * ts, 0, None)
        return carry

    lax.fori_loop(0, i, body, 0)
    base = pl.multiple_of(i * tq, tq)
    for sb in range(tq // ts):
        for hh in range(2):
            sub_block(hh, base + sb * ts, sb * ts, sb * ts)
    outs = []
    for hh in range(2):
        a = acc_ref[hh]
        outs.append(a / a[:, MLA_V:MLA_V + 1])
    lane = lax.broadcasted_iota(jnp.int32, (tq, HEAD_PAD), 1)
    o_ref[...] = jnp.where(lane < MLA_V, outs[0], pltpu.roll(outs[1], MLA_V, 1)).astype(BF16)


def _attn(q, k, v, batch, seq_len):
    T = q.shape[0]
    tq = ATTN_TILE
    nq = seq_len // tq
    pairs = MLA_HEADS // 2
    return pl.pallas_call(
        _attn_kernel, grid=(batch, pairs, nq),
        in_specs=[pl.BlockSpec((tq, 2 * HEAD_PAD), lambda b, p, i: (b * nq + i, p)),
                  pl.BlockSpec((seq_len, 2 * HEAD_PAD), lambda b, p, i: (b, p)),
                  pl.BlockSpec((seq_len, 2 * HEAD_PAD), lambda b, p, i: (b, p))],
        out_specs=pl.BlockSpec((tq, 2 * MLA_V), lambda b, p, i: (b * nq + i, p)),
        out_shape=jax.ShapeDtypeStruct((T, MLA_W), BF16),
        scratch_shapes=[pltpu.VMEM((2, tq, HEAD_PAD), F32), pltpu.VMEM((2, tq, HEAD_PAD), F32)],
        compiler_params=_cparams(3), name="attn")(q, k, v)


def _out_router_kernel(h_ref, yg_ref, ym_ref, yp_ref, wo_ref, fn_ref, rw_hi_ref, rw_lo_ref, rb_ref,
                       h1_ref, hn_ref, idx_ref, gate_ref, cnt_ref, carry_ref):
    tm = h_ref.shape[0]

    @pl.when(pl.program_id(0) == 0)
    def _():
        carry_ref[...] = jnp.zeros_like(carry_ref)

    sub = ROUTER_SUB
    lane = lax.broadcasted_iota(jnp.int32, (sub, 128), 1)
    r_i = lax.broadcasted_iota(jnp.int32, (sub, sub), 0)
    c_i = lax.broadcasted_iota(jnp.int32, (sub, sub), 1)
    tri = (r_i >= c_i).astype(BF16)
    for r0 in range(0, tm, sub):
        rows = slice(r0, r0 + sub)
        h1 = (h_ref[rows, :] + _dot(yg_ref[rows, :], wo_ref[0:GLA_W, :])
              + _dot(ym_ref[rows, :], wo_ref[GLA_W:GLA_W + MLA_W, :])
              + _dot(yp_ref[rows, :], wo_ref[GLA_W + MLA_W:, :]))
        h1_ref[rows, :] = h1
        hn = _rms(h1, fn_ref[...])
        hi, lo = _split2(hn)
        hn_ref[rows, :] = hi.astype(F32)
        logits = _dot(hi, rw_hi_ref[...]) + _dot(lo, rw_hi_ref[...]) + _dot(hi, rw_lo_ref[...]) + rb_ref[...]
        cur = jnp.where(lane < N_EXPERTS, logits, NEG_BIG)
        idx_out = jnp.zeros((sub, 128), jnp.int32)
        val_out = jnp.zeros((sub, 128), F32)
        chosen = jnp.zeros((sub, 128), F32)
        top0 = None
        sels = []
        for kk in range(TOP_K):
            m = jnp.max(cur, axis=-1, keepdims=True)
            sel = jnp.min(jnp.where(cur == m, lane, 128), axis=-1, keepdims=True)
            if kk == 0:
                top0 = m
            sels.append(sel)
            idx_out = jnp.where(lane == kk, sel, idx_out)
            val_out = jnp.where(lane == kk, jnp.exp(m - top0), val_out)
            chosen = jnp.where(lane == sel, 1.0, chosen)
            cur = jnp.where(lane == sel, NEG_BIG, cur)
        gate_ref[rows, :] = val_out / jnp.sum(val_out, axis=-1, keepdims=True)

        incl = _dot(tri, chosen.astype(BF16))
        before = carry_ref[0:1, :] + incl - chosen
        for kk in range(TOP_K):
            rank = jnp.sum(jnp.where(lane == sels[kk], before, 0.0), axis=-1, keepdims=True)
            idx_out = jnp.where(lane == TOP_K + kk, rank.astype(jnp.int32), idx_out)
        idx_ref[rows, :] = idx_out
        carry_ref[...] = carry_ref[...] + incl[sub - 1:sub, :]
    cnt_ref[...] = carry_ref[...].astype(jnp.int32)


def _out_router(h, yg, ym, yp, w):
    T = h.shape[0]
    tm = TOKEN_TILE
    row = lambda n: pl.BlockSpec((tm, n), lambda i: (i, 0))
    ins = [h, yg, ym, yp, w["w_out"], w["ffn_norm"], w["router_w_hi"], w["router_w_lo"], w["router_b"]]
    return pl.pallas_call(
        _out_router_kernel, grid=(T // tm,),
        in_specs=[row(D_MODEL), row(GLA_W), row(MLA_W), row(POOL_W)] + [_full(a.shape) for a in ins[4:]],
        out_specs=[row(D_MODEL), row(D_MODEL), row(128), row(128), _full((8, 128))],
        out_shape=[jax.ShapeDtypeStruct((T, D_MODEL), F32), jax.ShapeDtypeStruct((T, D_MODEL), F32),
                   jax.ShapeDtypeStruct((T, 128), jnp.int32), jax.ShapeDtypeStruct((T, 128), F32),
                   jax.ShapeDtypeStruct((8, 128), jnp.int32)],
        scratch_shapes=[pltpu.VMEM((8, 128), F32)],
        compiler_params=_cparams(1), name="out_router")(*ins)


def _moe_kernel(be_ref, nb_ref, x0_ref, x1_ref, x2_ref, x3_ref, wg_ref, bg_ref, wu_ref, bu_ref, wd_ref, bd_ref,
                y_ref, wg_bf, wu_bf, wd_bf):
    i = pl.program_id(0)
    used = i < nb_ref[0]
    new_expert = (i == 0) | (be_ref[i] != be_ref[jnp.maximum(i - 1, 0)])

    @pl.when(used & new_expert)
    def _():
        for src, dst in ((wg_ref, wg_bf), (wu_ref, wu_bf), (wd_ref, wd_bf)):
            for r in range(0, src.shape[2], MOE_CAST_ROWS):
                dst[r:r + MOE_CAST_ROWS, :] = src[0, 0, r:r + MOE_CAST_ROWS, :].astype(BF16)

    @pl.when(used)
    def _():
        x = jnp.concatenate([r[...].astype(BF16) for r in (x0_ref, x1_ref, x2_ref, x3_ref)], axis=1)
        g = jnp.minimum(_dot(x, wg_bf[...]) + bg_ref[0], SWIGLU_LIMIT)
        up = jnp.clip(_dot(x, wu_bf[...]) + bu_ref[0], -SWIGLU_LIMIT, SWIGLU_LIMIT)
        hb = (up + 1.0) * (g / (1.0 + jnp.exp(-SWIGLU_ALPHA * g)))
        y_ref[...] = _dot(hb.astype(BF16), wd_bf[...]) + bd_ref[0]

    @pl.when(jnp.logical_not(used))
    def _():
        y_ref[...] = jnp.zeros_like(y_ref)


def _moe(xs, block_e, n_used, w):
    n_rows = xs[0].shape[0]
    bm = MOE_BLOCK
    layer = w["layer"]
    wspec = lambda shp: pl.BlockSpec((1, 1) + shp, lambda i, be, nb: (layer, be[i], 0, 0))
    bspec = lambda shp: pl.BlockSpec((1,) + shp, lambda i, be, nb: (be[i], 0, 0))
    grid_spec = pltpu.PrefetchScalarGridSpec(
        num_scalar_prefetch=2, grid=(n_rows // bm,),
        in_specs=[pl.BlockSpec((bm, D_MODEL // DISPATCH_SLABS), lambda i, be, nb: (i, 0))] * DISPATCH_SLABS + [
                  wspec((D_MODEL, D_FF)), bspec((1, D_FF)), wspec((D_MODEL, D_FF)), bspec((1, D_FF)),
                  wspec((D_FF, D_MODEL)), bspec((1, D_MODEL))],
        out_specs=pl.BlockSpec((bm, D_MODEL), lambda i, be, nb: (i, 0)),
        scratch_shapes=[pltpu.VMEM((D_MODEL, D_FF), BF16), pltpu.VMEM((D_MODEL, D_FF), BF16),
                        pltpu.VMEM((D_FF, D_MODEL), BF16)])
    return pl.pallas_call(
        _moe_kernel, grid_spec=grid_spec,
        out_shape=jax.ShapeDtypeStruct((n_rows, D_MODEL), F32),
        compiler_params=_cparams(1), name="moe")(
            block_e, n_used, *xs, w["moe_w_gate"], w["moe_b_gate"], w["moe_w_up"], w["moe_b_up"],
            w["moe_w_down"], w["moe_b_down"])


def _ple_kernel(h1_ref, y0_ref, y1_ref, y2_ref, y3_ref, gate_ref, p_ref, wple_ref, gn_ref, wpg_ref, pn_ref, o_ref):
    sub = PLE_SUB
    for r0 in range(0, h1_ref.shape[0], sub):
        rows = slice(r0, r0 + sub)
        gates = gate_ref[rows, :]
        h2 = h1_ref[rows, :]
        for kk, y_ref in enumerate((y0_ref, y1_ref, y2_ref, y3_ref)):
            h2 = h2 + gates[:, kk:kk + 1] * y_ref[rows, :]
        e = _dot(p_ref[rows, :].astype(BF16), wple_ref[...])
        a = _dot(_rms(h2, gn_ref[...]).astype(BF16), wpg_ref[...])
        gate = 1.0 / (1.0 + jnp.exp(-a))
        o_ref[rows, :] = h2 + _rms(e * gate, pn_ref[...])


def _ple(h1, ys_k, gates, p, w):
    T = h1.shape[0]
    tm = TOKEN_TILE
    row = lambda n: pl.BlockSpec((tm, n), lambda i: (i, 0))
    ins = [h1, *ys_k, gates, p, w["ple_w_proj"], w["ple_gate_norm"], w["ple_w_gate"], w["ple_post_norm"]]
    return pl.pallas_call(
        _ple_kernel, grid=(T // tm,),
        in_specs=[row(D_MODEL)] * (1 + TOP_K) + [row(128), row(D_PLE)] + [_full(a.shape) for a in ins[7:]],
        out_specs=row(D_MODEL), out_shape=jax.ShapeDtypeStruct((T, D_MODEL), F32),
        compiler_params=_cparams(1), name="ple")(*ins)


def _pad_heads(wm, per_head, n_heads=MLA_HEADS):
    kdim = wm.shape[0]
    w3 = wm.reshape(kdim, n_heads, per_head)
    return jnp.pad(w3, ((0, 0), (0, 0), (0, HEAD_PAD - per_head))).reshape(kdim, n_heads * HEAD_PAD)


def _swap_rope_halves(a):
    a3 = a.reshape(a.shape[0], -1, HEAD_PAD)
    half = MLA_ROPE // 2
    x1 = a3[:, :, MLA_NOPE:MLA_NOPE + half]
    x2 = a3[:, :, MLA_NOPE + half:MLA_QK]
    out = jnp.zeros_like(a3).at[:, :, MLA_NOPE:MLA_NOPE + half].set(x2).at[:, :, MLA_NOPE + half:MLA_QK].set(x1)
    return out.reshape(a.shape)


def _layer_params(i, mix_norm, w_in, gla_w_gate, gla_b_gate, gla_out_norm, mla_q_norm, mla_w_uq, mla_kv_norm,
                  mla_w_ukv, mla_qk_q_norm, mla_qk_k_norm, pool_w, pool_scale, w_out, ffn_norm, router_w,
                  router_b, moe_w_gate, moe_b_gate, moe_w_up, moe_b_up, moe_w_down, moe_b_down,
                  ple_w_proj, ple_gate_norm, ple_w_gate, ple_post_norm):
    wi = w_in[i]
    c = np.cumsum((0, 128, 128, 256, 16, 256, 256, 128, 32, 256))
    gq, gk, gv, glow, gr, cq, ckv, krope, upool = [wi[:, c[j]:c[j + 1]] for j in range(9)]
    misc = jnp.concatenate([glow, krope, jnp.zeros((D_MODEL, 128 - 48), F32)], axis=1)
    w_in_p = jnp.concatenate([gq, gk, gv, gr, cq, upool, ckv, misc], axis=1).astype(BF16)
    wgate_p = jnp.zeros((128, GLA_K), F32).at[MISC_GLOW:MISC_GLOW + GLA_GATE_RANK].set(gla_w_gate[i]).astype(BF16)
    ukv = mla_w_ukv[i].reshape(MLA_KV_RANK, MLA_HEADS, MLA_NOPE + MLA_V)
    ukv_k = _pad_heads(ukv[:, :, :MLA_NOPE].reshape(MLA_KV_RANK, MLA_HEADS * MLA_NOPE), MLA_NOPE)
    ukv_v = _pad_heads(ukv[:, :, MLA_NOPE:].reshape(MLA_KV_RANK, MLA_W), MLA_V)
    pw = pool_w[i]
    pool_bd = jnp.zeros((POOL_W, POOL_W), F32)
    for g in range(4):
        pool_bd = pool_bd.at[g * 64:(g + 1) * 64, g * 64:(g + 1) * 64].set(pw[g])
    rw = jnp.pad(router_w[i], ((0, 0), (0, 128 - N_EXPERTS)))
    rw_hi = rw.astype(BF16)
    rw_lo = (rw - rw_hi.astype(F32)).astype(BF16)
    row = lambda a: a.reshape(1, -1)
    pad96 = lambda a: jnp.pad(a, (0, HEAD_PAD - MLA_QK)).reshape(1, HEAD_PAD)
    wuq_p = _pad_heads(mla_w_uq[i], MLA_QK)
    gq_p = pad96(mla_qk_q_norm[i] * (MLA_QK ** -0.5 * LOG2E))
    return {
        "mix_norm": row(mix_norm[i]), "w_in": w_in_p, "gla_w_gate": wgate_p, "gla_b_gate": row(gla_b_gate[i]),
        "gla_out_norm": row(jnp.tile(gla_out_norm[i], GLA_HEADS)),
        "mla_q_norm": row(mla_q_norm[i]),
        "mla_w_uq": jnp.concatenate([wuq_p, _swap_rope_halves(wuq_p)], axis=1).astype(BF16),
        "mla_kv_norm": row(mla_kv_norm[i]), "mla_w_ukv_k": ukv_k.astype(BF16), "mla_w_ukv_v": ukv_v.astype(BF16),
        "mla_gq": jnp.concatenate([gq_p, _swap_rope_halves(gq_p)], axis=0), "mla_gk": pad96(mla_qk_k_norm[i]),
        "pool_w": pool_bd.astype(BF16), "pool_scale": row(pool_scale[i]),
        "w_out": w_out[i].astype(BF16), "ffn_norm": row(ffn_norm[i]),
        "router_w_hi": rw_hi, "router_w_lo": rw_lo,
        "router_b": row(jnp.pad(router_b[i], (0, 128 - N_EXPERTS))),
        "layer": i,
        "moe_w_gate": moe_w_gate, "moe_b_gate": moe_b_gate[i].reshape(N_EXPERTS, 1, D_FF),
        "moe_w_up": moe_w_up, "moe_b_up": moe_b_up[i].reshape(N_EXPERTS, 1, D_FF),
        "moe_w_down": moe_w_down, "moe_b_down": moe_b_down[i].reshape(N_EXPERTS, 1, D_MODEL),
        "ple_w_proj": ple_w_proj[i].astype(BF16), "ple_gate_norm": row(ple_gate_norm[i]),
        "ple_w_gate": ple_w_gate[i].astype(BF16), "ple_post_norm": row(ple_post_norm[i]),
    }


def _rope_tables(positions):
    T = positions.size
    inv = ROPE_BASE ** (-jnp.arange(0, MLA_ROPE, 2, dtype=F32) / MLA_ROPE)
    ang = positions.reshape(T, 1).astype(F32) * inv
    cos, sin = jnp.cos(ang), jnp.sin(ang)
    z16 = jnp.zeros((T, 16), F32)
    tail = jnp.zeros((T, HEAD_PAD - MLA_QK), F32)
    c = jnp.concatenate([jnp.ones((T, MLA_NOPE), F32), cos, cos, tail], axis=1)
    s1 = jnp.concatenate([jnp.zeros((T, MLA_NOPE), F32), -sin, z16, tail], axis=1)
    s2 = jnp.concatenate([jnp.zeros((T, MLA_NOPE), F32), z16, sin, tail], axis=1)
    return c, s1, s2


def _route(top_idx, rank, counts, T):
    bm = MOE_BLOCK
    A = T * TOP_K
    padded = (counts + bm - 1) // bm * bm
    pad_end = jnp.cumsum(padded)
    pad_start = pad_end - padded
    experts = jnp.arange(N_EXPERTS, dtype=jnp.int32)
    dest = rank + jnp.sum(jnp.where(top_idx[:, :, None] == experts, pad_start, 0), axis=-1)
    n_blocks = (A + N_EXPERTS * (bm - 1) + bm - 1) // bm
    n_rows = n_blocks * bm
    block_start = jnp.arange(n_blocks, dtype=jnp.int32) * bm
    block_e = jnp.minimum(jnp.sum((pad_end[None, :] <= block_start[:, None]).astype(jnp.int32), axis=1),
                          N_EXPERTS - 1)
    n_used = (pad_end[-1] // bm).astype(jnp.int32).reshape(1)
    return dest, n_rows, block_e, n_used


def _dispatch(hn, dest, n_rows):
    T, D = hn.shape
    win = DISPATCH_ROWS
    width = D // DISPATCH_SLABS
    dest_t = dest.T
    mesh = plsc.VectorSubcoreMesh(core_axis_name="core", subcore_axis_name="subcore")

    @functools.partial(pl.kernel, out_type=jax.ShapeDtypeStruct((n_rows, width), hn.dtype), mesh=mesh,
                       scratch_types=[], name="dispatch")
    def scatter_rows(x_hbm, i_hbm, o_hbm):
        def body(x_vmem, i_vmem):
            for kk in range(TOP_K):
                pltpu.sync_copy(x_vmem, o_hbm.at[i_vmem.at[kk]])

        pltpu.emit_pipeline(
            body, grid=(T // win,),
            in_specs=[pl.BlockSpec((win, width), lambda i: (i, 0)), pl.BlockSpec((TOP_K, win), lambda i: (0, i))],
            out_specs=[], core_axis_name=("core", "subcore"),
            dimension_semantics=(pltpu.PARALLEL,))(x_hbm, i_hbm)

    return [scatter_rows(hn[:, c * width:(c + 1) * width], dest_t) for c in range(DISPATCH_SLABS)]


def kernel(x, p, positions, mix_norm, w_in, gla_w_gate, gla_b_gate, gla_out_norm, mla_q_norm, mla_w_uq,
           mla_kv_norm, mla_w_ukv, mla_qk_q_norm, mla_qk_k_norm, pool_w, pool_scale, w_out, ffn_norm,
           router_w, router_b, moe_w_gate, moe_b_gate, moe_w_up, moe_b_up, moe_w_down, moe_b_down,
           ple_w_proj, ple_gate_norm, ple_w_gate, ple_post_norm):
    B, S, D = x.shape
    T = B * S
    depth = p.shape[0]
    params = (mix_norm, w_in, gla_w_gate, gla_b_gate, gla_out_norm, mla_q_norm, mla_w_uq, mla_kv_norm,
              mla_w_ukv, mla_qk_q_norm, mla_qk_k_norm, pool_w, pool_scale, w_out, ffn_norm, router_w,
              router_b, moe_w_gate, moe_b_gate, moe_w_up, moe_b_up, moe_w_down, moe_b_down,
              ple_w_proj, ple_gate_norm, ple_w_gate, ple_post_norm)
    rope_c, rope_s1, rope_s2 = _rope_tables(positions)
    h = x.reshape(T, D)
    for i in range(depth):
        w = _layer_params(i, *params)
        zg, la, q, k, v, y_pool = _mix_pre(h, w, rope_c, rope_s1, rope_s2, S)
        y_gla = _gla(zg, la, w["gla_out_norm"], B, S)
        y_mla = _attn(q, k, v, B, S)
        h1, hn, route, gates, counts = _out_router(h, y_gla, y_mla, y_pool, w)
        dest, n_rows, block_e, n_used = _route(route[:, :TOP_K], route[:, TOP_K:2 * TOP_K],
                                               counts[0, :N_EXPERTS], T)
        take = lambda a, idx: a.at[idx].get(mode="promise_in_bounds")
        ys = _moe(_dispatch(hn, dest, n_rows), block_e, n_used, w)
        h = _ple(h1, [take(ys, dest[:, kk]) for kk in range(TOP_K)], gates, p[i].reshape(T, D_PLE), w)
    return h.reshape(B, S, D)
```

```python
import functools

import jax
import jax.numpy as jnp
import numpy as np
from jax import lax
from jax.experimental import pallas as pl
from jax.experimental.pallas import tpu as pltpu
from jax.experimental.pallas import tpu_sc as plsc

F32 = jnp.float32
BF16 = jnp.bfloat16

D_MODEL = 1024
EPS = 1e-6
D_PLE = 256

GLA_HEADS = 4
GLA_DK = 32
GLA_DV = 64
GLA_GATE_RANK = 16
GLA_TAU = 16.0
GLA_CHUNK = 64
GLA_K = GLA_HEADS * GLA_DK
GLA_W = GLA_HEADS * GLA_DV

MLA_HEADS = 8
MLA_Q_RANK = 256
MLA_KV_RANK = 128
MLA_NOPE = 64
MLA_ROPE = 32
MLA_QK = MLA_NOPE + MLA_ROPE
MLA_V = 64
MLA_W = MLA_HEADS * MLA_V
ROPE_BASE = 10000.0
HEAD_PAD = 128
MLA_QK_PAD = MLA_HEADS * HEAD_PAD

POOL_WINDOWS = (2, 4, 8, 16)
POOL_GROUP = 64
POOL_W = 256
POOL_HALO = 16

N_EXPERTS = 32
TOP_K = 4
D_FF = 1024
SWIGLU_LIMIT = 7.0
SWIGLU_ALPHA = 1.702

COL_GQ, COL_GK, COL_GV, COL_GR, COL_CQ, COL_POOL, COL_CKV, COL_MISC = 0, 128, 256, 512, 768, 1024, 1280, 1408
D_IN_PAD = 1536
MISC_GLOW = 0
MISC_ROPE = 16

LOG2E = 1.4426950408889634
TOKEN_TILE = 512
MIX_SUB, ROUTER_SUB, PLE_SUB = 512, 512, 256
GLA_TILE = 512
ATTN_TILE = 1024
ATTN_SUB = 512
MOE_BLOCK = 512
MOE_CAST_ROWS = 256
BATCH_GROUPS = 2
DISPATCH_ROWS = 128
DISPATCH_SLABS = 4
VMEM_LIMIT = 56 * 1024 * 1024
NEG_BIG = -1e30


def _cparams(n_axes, **flags):
    return pltpu.CompilerParams(dimension_semantics=("arbitrary",) * n_axes,
                                vmem_limit_bytes=VMEM_LIMIT, flags=flags or None)


def _rms(x, g):
    return x * lax.rsqrt(jnp.mean(x * x, axis=-1, keepdims=True) + EPS) * g


def _dot(a, b):
    return jnp.dot(a, b, preferred_element_type=F32)


def _dot_nt(a, b):
    return lax.dot_general(a, b, (((1,), (1,)), ((), ())), preferred_element_type=F32)


def _dot_tn(a, b):
    return lax.dot_general(a, b, (((0,), (0,)), ((), ())), preferred_element_type=F32)


def _split3(x):
    hi = x.astype(BF16)
    r = x - hi.astype(F32)
    mid = r.astype(BF16)
    lo = (r - mid.astype(F32)).astype(BF16)
    return hi, mid, lo


def _split2(x):
    hi = x.astype(BF16)
    lo = (x - hi.astype(F32)).astype(BF16)
    return hi, lo


def _full(shape):
    nd = len(shape)
    return pl.BlockSpec(shape, lambda *_: (0,) * nd)


def _rope(x, c, s1, s2):
    return x * c + pltpu.roll(x, HEAD_PAD - 16, 1) * s1 + pltpu.roll(x, 16, 1) * s2


def _mix_pre_kernel(h_ref, mixn_ref, win_ref, wgate_ref, bgate_ref, qn_ref, wuq_ref, kvn_ref,
                    wukvk_ref, wukvv_ref, gq_ref, gk_ref, rc_ref, rs1_ref, rs2_ref,
                    wpool_ref, pscale_ref,
                    zg_ref, la_ref, q_ref, k_ref, v_ref, yp_ref, carry_ref, *, tiles_per_seq):
    tm = h_ref.shape[0]
    sub = MIX_SUB
    seq_tile = pl.program_id(0) % tiles_per_seq

    @pl.when(seq_tile == 0)
    def _():
        carry_ref[...] = jnp.zeros_like(carry_ref)

    lane = lax.broadcasted_iota(jnp.int32, (sub, HEAD_PAD), 1)
    in_rope = (lane >= MLA_NOPE) & (lane < MLA_QK)
    lane_v = lax.broadcasted_iota(jnp.int32, (sub, MLA_QK_PAD), 1)
    ones_lane = lane_v % HEAD_PAD == MLA_V
    lane_p = lax.broadcasted_iota(jnp.int32, (sub, POOL_W), 1)
    row_p = lax.broadcasted_iota(jnp.int32, (sub, POOL_W), 0)
    g0, g1, g2 = lane_p < 64, lane_p < 128, lane_p < 192
    win = jnp.where(g0, 2.0, jnp.where(g1, 4.0, jnp.where(g2, 8.0, 16.0)))
    gq, gq_sw, gk = gq_ref[0:1, :], gq_ref[1:2, :], gk_ref[...]

    for r0 in range(0, tm, sub):
        rows = slice(r0, r0 + sub)
        hn = _rms(h_ref[rows, :], mixn_ref[...]).astype(BF16)
        z = _dot(hn, win_ref[...])
        zg_ref[rows, :] = z[:, COL_GQ:COL_CQ]
        zm = z[:, COL_MISC:COL_MISC + 128]

        logit = _dot(zm.astype(BF16), wgate_ref[...]) + bgate_ref[...]
        la_ref[rows, :] = (jnp.minimum(logit, 0.0) - jnp.log(1.0 + jnp.exp(-jnp.abs(logit)))) * (1.0 / GLA_TAU)

        cqn = _rms(z[:, COL_CQ:COL_CQ + MLA_Q_RANK], qn_ref[...]).astype(BF16)
        qf = _dot(cqn, wuq_ref[...])
        ckvn = _rms(z[:, COL_CKV:COL_CKV + MLA_KV_RANK], kvn_ref[...]).astype(BF16)
        kn = _dot(ckvn, wukvk_ref[...])
        v_ref[rows, :] = jnp.where(ones_lane, 1.0, _dot(ckvn, wukvv_ref[...])).astype(BF16)

        rc, rs1, rs2 = rc_ref[rows, :], rs1_ref[rows, :], rs2_ref[rows, :]
        kr = jnp.where(in_rope, pltpu.roll(zm, MLA_NOPE - MISC_ROPE, 1), 0.0)
        kr_ss = jnp.sum(kr * kr, axis=-1, keepdims=True)
        krr = _rope(kr * gk, rc, rs1, rs2)
        cq = rc * gq
        sq_tab = (rs1 + rs2) * gq_sw
        for hh in range(MLA_HEADS):
            sl = slice(hh * HEAD_PAD, (hh + 1) * HEAD_PAD)
            qh = qf[:, sl]
            qsw = qf[:, MLA_QK_PAD + hh * HEAD_PAD:MLA_QK_PAD + (hh + 1) * HEAD_PAD]
            sq = lax.rsqrt(jnp.sum(qh * qh, axis=-1, keepdims=True) * (1.0 / MLA_QK) + EPS)
            q_ref[rows, sl] = ((qh * cq + qsw * sq_tab) * sq).astype(BF16)
            kh = kn[:, sl]
            sk = lax.rsqrt((jnp.sum(kh * kh, axis=-1, keepdims=True) + kr_ss) * (1.0 / MLA_QK) + EPS)
            k_ref[rows, sl] = (sk * (kh * gk + krr)).astype(BF16)

        u = z[:, COL_POOL:COL_POOL + POOL_W]
        xe = jnp.concatenate([carry_ref[...], u], axis=0)
        carry_ref[...] = u[sub - POOL_HALO:, :]
        s2 = xe + pltpu.roll(xe, 1, 0)
        s4 = s2 + pltpu.roll(s2, 2, 0)
        s8 = s4 + pltpu.roll(s4, 4, 0)
        s16 = s8 + pltpu.roll(s8, 8, 0)
        pooled = jnp.where(g0, s2[POOL_HALO:], jnp.where(g1, s4[POOL_HALO:],
                           jnp.where(g2, s8[POOL_HALO:], s16[POOL_HALO:])))
        cnt = jnp.minimum((seq_tile * tm + r0 + row_p + 1).astype(F32), win)
        d = pooled / cnt - u
        yp_ref[rows, :] = (_dot(d.astype(BF16), wpool_ref[...]) * pscale_ref[...]).astype(BF16)


def _mix_pre(h, w, rope_c, rope_s1, rope_s2, seq_len):
    T = h.shape[0]
    tm = TOKEN_TILE
    row = lambda n: pl.BlockSpec((tm, n), lambda i: (i, 0))
    ins = [h, w["mix_norm"], w["w_in"], w["gla_w_gate"], w["gla_b_gate"], w["mla_q_norm"], w["mla_w_uq"],
           w["mla_kv_norm"], w["mla_w_ukv_k"], w["mla_w_ukv_v"], w["mla_gq"], w["mla_gk"],
           rope_c, rope_s1, rope_s2, w["pool_w"], w["pool_scale"]]
    in_specs = [row(D_MODEL)] + [_full(a.shape) for a in ins[1:12]] + [row(HEAD_PAD)] * 3 + \
               [_full(w["pool_w"].shape), _full(w["pool_scale"].shape)]
    out_shape = [jax.ShapeDtypeStruct((T, COL_CQ), F32), jax.ShapeDtypeStruct((T, GLA_K), F32),
                 jax.ShapeDtypeStruct((T, MLA_QK_PAD), BF16), jax.ShapeDtypeStruct((T, MLA_QK_PAD), BF16),
                 jax.ShapeDtypeStruct((T, MLA_QK_PAD), BF16), jax.ShapeDtypeStruct((T, POOL_W), BF16)]
    out_specs = [row(COL_CQ), row(GLA_K), row(MLA_QK_PAD), row(MLA_QK_PAD), row(MLA_QK_PAD), row(POOL_W)]
    return pl.pallas_call(
        functools.partial(_mix_pre_kernel, tiles_per_seq=seq_len // tm),
        grid=(T // tm,), in_specs=in_specs, out_specs=out_specs, out_shape=out_shape,
        scratch_shapes=[pltpu.VMEM((POOL_HALO, POOL_W), F32)],
        compiler_params=_cparams(1), name="mix_pre")(*ins)


def _gla_kernel(zg_ref, la_ref, gn_ref, y_ref, state_ref, o_ref):
    tg = zg_ref.shape[0]
    C = GLA_CHUNK

    @pl.when(pl.program_id(1) == 0)
    def _():
        state_ref[...] = jnp.zeros_like(state_ref)

    r_i = lax.broadcasted_iota(jnp.int32, (C, C), 0)
    c_i = lax.broadcasted_iota(jnp.int32, (C, C), 1)
    tri = (r_i >= c_i).astype(BF16)
    ones = jnp.ones((C, GLA_W), BF16)
    head_k = lax.broadcasted_iota(jnp.int32, (C, GLA_K), 1) // GLA_DK
    head_v = lax.broadcasted_iota(jnp.int32, (C, GLA_W), 1) // GLA_DV
    ar = lax.broadcasted_iota(jnp.int32, (GLA_HEADS * C, C), 0)
    ac = lax.broadcasted_iota(jnp.int32, (GLA_HEADS * C, C), 1)
    causal = (ar % C) >= ac
    sk = lax.broadcasted_iota(jnp.int32, (GLA_K, GLA_W), 0) // GLA_DK
    sv = lax.broadcasted_iota(jnp.int32, (GLA_K, GLA_W), 1) // GLA_DV
    blockdiag = sk == sv

    for c in range(tg // C):
        rows = slice(c * C, (c + 1) * C)
        q = zg_ref[rows, COL_GQ:COL_GQ + GLA_K] * (GLA_DK ** -0.5)
        k = zg_ref[rows, COL_GK:COL_GK + GLA_K]
        v = zg_ref[rows, COL_GV:COL_GV + GLA_W].astype(BF16)
        la3 = _split3(la_ref[rows, :])
        bc = _dot(tri, la3[0]) + _dot(tri, la3[1]) + _dot(tri, la3[2])
        b_last = bc[C - 1:C, :]
        q_dec = (q * jnp.exp(bc)).astype(BF16)
        k_dec = (k * jnp.exp(-bc)).astype(BF16)
        k_end = (k * jnp.exp(b_last - bc)).astype(BF16)
        decay = jnp.exp(_dot_tn(la3[0], ones) + _dot_tn(la3[1], ones) + _dot_tn(la3[2], ones))
        zero = jnp.zeros_like(q_dec)
        qs = jnp.concatenate([jnp.where(head_k == hh, q_dec, zero) for hh in range(GLA_HEADS)], axis=0)
        att = jnp.where(causal, _dot_nt(qs, k_dec), 0.0).astype(BF16)
        o_full = _dot(att, v)
        o = _dot(q_dec, state_ref[...].astype(BF16))
        for hh in range(GLA_HEADS):
            o = o + jnp.where(head_v == hh, o_full[hh * C:(hh + 1) * C, :], 0.0)
        upd = _dot_tn(k_end, v)
        state_ref[...] = decay * state_ref[...] + jnp.where(blockdiag, upd, 0.0)
        o_ref[rows, :] = o

    o = o_ref[...]
    gr = lax.broadcasted_iota(jnp.int32, (GLA_W, GLA_W), 0) // GLA_DV
    gc = lax.broadcasted_iota(jnp.int32, (GLA_W, GLA_W), 1) // GLA_DV
    group = (gr == gc).astype(BF16)
    oo = _split2(o * o)
    ms = (_dot(oo[0], group) + _dot(oo[1], group)) * (1.0 / GLA_DV)
    r = zg_ref[:, COL_GR:COL_GR + GLA_W]
    y = o * lax.rsqrt(ms + EPS) * gn_ref[...] * (r / (1.0 + jnp.exp(-r)))
    y_ref[...] = y.astype(BF16)


def _gla(zg, la, gn, batch, seq_len):
    T = zg.shape[0]
    tg = GLA_TILE
    nt = seq_len // tg
    return pl.pallas_call(
        _gla_kernel, grid=(batch, nt),
        in_specs=[pl.BlockSpec((tg, COL_CQ), lambda b, s: (b * nt + s, 0)),
                  pl.BlockSpec((tg, GLA_K), lambda b, s: (b * nt + s, 0)),
                  _full(gn.shape)],
        out_specs=pl.BlockSpec((tg, GLA_W), lambda b, s: (b * nt + s, 0)),
        out_shape=jax.ShapeDtypeStruct((T, GLA_W), BF16),
        scratch_shapes=[pltpu.VMEM((GLA_K, GLA_W), F32), pltpu.VMEM((tg, GLA_W), F32)],
        compiler_params=_cparams(2), name="gla")(zg, la, gn)


def _attn_kernel(q_ref, k_ref, v_ref, o_ref, m_ref, acc_ref):
    tq = q_ref.shape[0]
    ts = ATTN_SUB
    i = pl.program_id(2)
    m_ref[...] = jnp.full_like(m_ref, NEG_BIG)
    acc_ref[...] = jnp.zeros_like(acc_ref)

    def sub_block(hh, start, r0, mask_off):
        hs = slice(hh * HEAD_PAD, (hh + 1) * HEAD_PAD)
        kj = k_ref[pl.ds(start, ts), hs]
        vj = v_ref[pl.ds(start, ts), hs]
        s = _dot_nt(q_ref[r0:, hs], kj)
        if mask_off is not None:
            row = lax.broadcasted_iota(jnp.int32, s.shape, 0) + r0
            col = lax.broadcasted_iota(jnp.int32, s.shape, 1) + mask_off
            s = jnp.where(col <= row, s, NEG_BIG)
        m_old = m_ref[hh, r0:, :]
        parts = [s[:, c * 128:(c + 1) * 128] for c in range(ts // 128)]
        m_new = jnp.maximum(m_old, jnp.max(functools.reduce(jnp.maximum, parts), axis=-1, keepdims=True))
        p = jnp.concatenate([jnp.exp2(x - m_new) for x in parts], axis=1).astype(BF16)
        acc_ref[hh, r0:, :] = jnp.exp2(m_old - m_new) * acc_ref[hh, r0:, :] + _dot(p, vj)
        m_ref[hh, r0:, :] = m_new

    def body(j, carry):
        base = pl.multiple_of(j * tq, tq)
        for sb in range(tq // ts):
            for hh in range(2):
                sub_block(hh, base + sb * ts, 0, None)
        return carry

    lax.fori_loop(0, i, body, 0)
    base = pl.multiple_of(i * tq, tq)
    for sb in range(tq // ts):
        for hh in range(2):
            sub_block(hh, base + sb * ts, sb * ts, sb * ts)
    outs = []
    for hh in range(2):
        a = acc_ref[hh]
        outs.append(a / a[:, MLA_V:MLA_V + 1])
    lane = lax.broadcasted_iota(jnp.int32, (tq, HEAD_PAD), 1)
    o_ref[...] = jnp.where(lane < MLA_V, outs[0], pltpu.roll(outs[1], MLA_V, 1)).astype(BF16)


def _attn(q, k, v, batch, seq_len):
    T = q.shape[0]
    tq = ATTN_TILE
    nq = seq_len // tq
    pairs = MLA_HEADS // 2
    return pl.pallas_call(
        _attn_kernel, grid=(batch, pairs, nq),
        in_specs=[pl.BlockSpec((tq, 2 * HEAD_PAD), lambda b, p, i: (b * nq + i, p)),
                  pl.BlockSpec((seq_len, 2 * HEAD_PAD), lambda b, p, i: (b, p)),
                  pl.BlockSpec((seq_len, 2 * HEAD_PAD), lambda b, p, i: (b, p))],
        out_specs=pl.BlockSpec((tq, 2 * MLA_V), lambda b, p, i: (b * nq + i, p)),
        out_shape=jax.ShapeDtypeStruct((T, MLA_W), BF16),
        scratch_shapes=[pltpu.VMEM((2, tq, HEAD_PAD), F32), pltpu.VMEM((2, tq, HEAD_PAD), F32)],
        compiler_params=_cparams(3), name="attn")(q, k, v)


def _out_router_kernel(h_ref, yg_ref, ym_ref, yp_ref, wo_ref, fn_ref, rw_hi_ref, rw_lo_ref, rb_ref,
                       h1_ref, hn_ref, idx_ref, gate_ref, cnt_ref, carry_ref):
    tm = h_ref.shape[0]

    @pl.when(pl.program_id(0) == 0)
    def _():
        carry_ref[...] = jnp.zeros_like(carry_ref)

    sub = ROUTER_SUB
    lane = lax.broadcasted_iota(jnp.int32, (sub, 128), 1)
    r_i = lax.broadcasted_iota(jnp.int32, (sub, sub), 0)
    c_i = lax.broadcasted_iota(jnp.int32, (sub, sub), 1)
    tri = (r_i >= c_i).astype(BF16)
    for r0 in range(0, tm, sub):
        rows = slice(r0, r0 + sub)
        h1 = (h_ref[rows, :] + _dot(yg_ref[rows, :], wo_ref[0:GLA_W, :])
              + _dot(ym_ref[rows, :], wo_ref[GLA_W:GLA_W + MLA_W, :])
              + _dot(yp_ref[rows, :], wo_ref[GLA_W + MLA_W:, :]))
        h1_ref[rows, :] = h1
        hn = _rms(h1, fn_ref[...])
        hi, lo = _split2(hn)
        hn_ref[rows, :] = hi.astype(F32)
        logits = _dot(hi, rw_hi_ref[...]) + _dot(lo, rw_hi_ref[...]) + _dot(hi, rw_lo_ref[...]) + rb_ref[...]
        cur = jnp.where(lane < N_EXPERTS, logits, NEG_BIG)
        idx_out = jnp.zeros((sub, 128), jnp.int32)
        val_out = jnp.zeros((sub, 128), F32)
        chosen = jnp.zeros((sub, 128), F32)
        top0 = None
        sels = []
        for kk in range(TOP_K):
            m = jnp.max(cur, axis=-1, keepdims=True)
            sel = jnp.min(jnp.where(cur == m, lane, 128), axis=-1, keepdims=True)
            if kk == 0:
                top0 = m
            sels.append(sel)
            idx_out = jnp.where(lane == kk, sel, idx_out)
            val_out = jnp.where(lane == kk, jnp.exp(m - top0), val_out)
            chosen = jnp.where(lane == sel, 1.0, chosen)
            cur = jnp.where(lane == sel, NEG_BIG, cur)
        gate_ref[rows, :] = val_out / jnp.sum(val_out, axis=-1, keepdims=True)

        incl = _dot(tri, chosen.astype(BF16))
        before = carry_ref[0:1, :] + incl - chosen
        for kk in range(TOP_K):
            rank = jnp.sum(jnp.where(lane == sels[kk], before, 0.0), axis=-1, keepdims=True)
            idx_out = jnp.where(lane == TOP_K + kk, rank.astype(jnp.int32), idx_out)
        idx_ref[rows, :] = idx_out
        carry_ref[...] = carry_ref[...] + incl[sub - 1:sub, :]
    cnt_ref[...] = carry_ref[...].astype(jnp.int32)


def _out_router(h, yg, ym, yp, w):
    T = h.shape[0]
    tm = TOKEN_TILE
    row = lambda n: pl.BlockSpec((tm, n), lambda i: (i, 0))
    ins = [h, yg, ym, yp, w["w_out"], w["ffn_norm"], w["router_w_hi"], w["router_w_lo"], w["router_b"]]
    return pl.pallas_call(
        _out_router_kernel, grid=(T // tm,),
        in_specs=[row(D_MODEL), row(GLA_W), row(MLA_W), row(POOL_W)] + [_full(a.shape) for a in ins[4:]],
        out_specs=[row(D_MODEL), row(D_MODEL), row(128), row(128), _full((8, 128))],
        out_shape=[jax.ShapeDtypeStruct((T, D_MODEL), F32), jax.ShapeDtypeStruct((T, D_MODEL), F32),
                   jax.ShapeDtypeStruct((T, 128), jnp.int32), jax.ShapeDtypeStruct((T, 128), F32),
                   jax.ShapeDtypeStruct((8, 128), jnp.int32)],
        scratch_shapes=[pltpu.VMEM((8, 128), F32)],
        compiler_params=_cparams(1), name="out_router")(*ins)


def _moe_kernel(be_ref, nb_ref, x0_ref, x1_ref, x2_ref, x3_ref, wg_ref, bg_ref, wu_ref, bu_ref, wd_ref, bd_ref,
                y_ref, wg_bf, wu_bf, wd_bf):
    i = pl.program_id(0)
    used = i < nb_ref[0]
    new_expert = (i == 0) | (be_ref[i] != be_ref[jnp.maximum(i - 1, 0)])

    @pl.when(used & new_expert)
    def _():
        for src, dst in ((wg_ref, wg_bf), (wu_ref, wu_bf), (wd_ref, wd_bf)):
            for r in range(0, src.shape[2], MOE_CAST_ROWS):
                dst[r:r + MOE_CAST_ROWS, :] = src[0, 0, r:r + MOE_CAST_ROWS, :].astype(BF16)

    @pl.when(used)
    def _():
        x = jnp.concatenate([r[...].astype(BF16) for r in (x0_ref, x1_ref, x2_ref, x3_ref)], axis=1)
        g = jnp.minimum(_dot(x, wg_bf[...]) + bg_ref[0], SWIGLU_LIMIT)
        up = jnp.clip(_dot(x, wu_bf[...]) + bu_ref[0], -SWIGLU_LIMIT, SWIGLU_LIMIT)
        hb = (up + 1.0) * (g / (1.0 + jnp.exp(-SWIGLU_ALPHA * g)))
        y_ref[...] = _dot(hb.astype(BF16), wd_bf[...]) + bd_ref[0]

    @pl.when(jnp.logical_not(used))
    def _():
        y_ref[...] = jnp.zeros_like(y_ref)


def _moe(xs, block_e, n_used, w):
    n_rows = xs[0].shape[0]
    bm = MOE_BLOCK
    layer = w["layer"]
    wspec = lambda shp: pl.BlockSpec((1, 1) + shp, lambda i, be, nb: (layer, be[i], 0, 0))
    bspec = lambda shp: pl.BlockSpec((1,) + shp, lambda i, be, nb: (be[i], 0, 0))
    grid_spec = pltpu.PrefetchScalarGridSpec(
        num_scalar_prefetch=2, grid=(n_rows // bm,),
        in_specs=[pl.BlockSpec((bm, D_MODEL // DISPATCH_SLABS), lambda i, be, nb: (i, 0))] * DISPATCH_SLABS + [
                  wspec((D_MODEL, D_FF)), bspec((1, D_FF)), wspec((D_MODEL, D_FF)), bspec((1, D_FF)),
                  wspec((D_FF, D_MODEL)), bspec((1, D_MODEL))],
        out_specs=pl.BlockSpec((bm, D_MODEL), lambda i, be, nb: (i, 0)),
        scratch_shapes=[pltpu.VMEM((D_MODEL, D_FF), BF16), pltpu.VMEM((D_MODEL, D_FF), BF16),
                        pltpu.VMEM((D_FF, D_MODEL), BF16)])
    return pl.pallas_call(
        _moe_kernel, grid_spec=grid_spec,
        out_shape=jax.ShapeDtypeStruct((n_rows, D_MODEL), F32),
        compiler_params=_cparams(1), name="moe")(
            block_e, n_used, *xs, w["moe_w_gate"], w["moe_b_gate"], w["moe_w_up"], w["moe_b_up"],
            w["moe_w_down"], w["moe_b_down"])


def _ple_kernel(h1_ref, y0_ref, y1_ref, y2_ref, y3_ref, gate_ref, p_ref, wple_ref, gn_ref, wpg_ref, pn_ref, o_ref):
    sub = PLE_SUB
    for r0 in range(0, h1_ref.shape[0], sub):
        rows = slice(r0, r0 + sub)
        gates = gate_ref[rows, :]
        h2 = h1_ref[rows, :]
        for kk, y_ref in enumerate((y0_ref, y1_ref, y2_ref, y3_ref)):
            h2 = h2 + gates[:, kk:kk + 1] * y_ref[rows, :]
        e = _dot(p_ref[rows, :].astype(BF16), wple_ref[...])
        a = _dot(_rms(h2, gn_ref[...]).astype(BF16), wpg_ref[...])
        gate = 1.0 / (1.0 + jnp.exp(-a))
        o_ref[rows, :] = h2 + _rms(e * gate, pn_ref[...])


def _ple(h1, ys_k, gates, p, w):
    T = h1.shape[0]
    tm = TOKEN_TILE
    row = lambda n: pl.BlockSpec((tm, n), lambda i: (i, 0))
    ins = [h1, *ys_k, gates, p, w["ple_w_proj"], w["ple_gate_norm"], w["ple_w_gate"], w["ple_post_norm"]]
    return pl.pallas_call(
        _ple_kernel, grid=(T // tm,),
        in_specs=[row(D_MODEL)] * (1 + TOP_K) + [row(128), row(D_PLE)] + [_full(a.shape) for a in ins[7:]],
        out_specs=row(D_MODEL), out_shape=jax.ShapeDtypeStruct((T, D_MODEL), F32),
        compiler_params=_cparams(1), name="ple")(*ins)


def _pad_heads(wm, per_head, n_heads=MLA_HEADS):
    kdim = wm.shape[0]
    w3 = wm.reshape(kdim, n_heads, per_head)
    return jnp.pad(w3, ((0, 0), (0, 0), (0, HEAD_PAD - per_head))).reshape(kdim, n_heads * HEAD_PAD)


def _swap_rope_halves(a):
    a3 = a.reshape(a.shape[0], -1, HEAD_PAD)
    half = MLA_ROPE // 2
    x1 = a3[:, :, MLA_NOPE:MLA_NOPE + half]
    x2 = a3[:, :, MLA_NOPE + half:MLA_QK]
    out = jnp.zeros_like(a3).at[:, :, MLA_NOPE:MLA_NOPE + half].set(x2).at[:, :, MLA_NOPE + half:MLA_QK].set(x1)
    return out.reshape(a.shape)


def _layer_params(i, mix_norm, w_in, gla_w_gate, gla_b_gate, gla_out_norm, mla_q_norm, mla_w_uq, mla_kv_norm,
                  mla_w_ukv, mla_qk_q_norm, mla_qk_k_norm, pool_w, pool_scale, w_out, ffn_norm, router_w,
                  router_b, moe_w_gate, moe_b_gate, moe_w_up, moe_b_up, moe_w_down, moe_b_down,
                  ple_w_proj, ple_gate_norm, ple_w_gate, ple_post_norm):
    wi = w_in[i]
    c = np.cumsum((0, 128, 128, 256, 16, 256, 256, 128, 32, 256))
    gq, gk, gv, glow, gr, cq, ckv, krope, upool = [wi[:, c[j]:c[j + 1]] for j in range(9)]
    misc = jnp.concatenate([glow, krope, jnp.zeros((D_MODEL, 128 - 48), F32)], axis=1)
    w_in_p = jnp.concatenate([gq, gk, gv, gr, cq, upool, ckv, misc], axis=1).astype(BF16)
    wgate_p = jnp.zeros((128, GLA_K), F32).at[MISC_GLOW:MISC_GLOW + GLA_GATE_RANK].set(gla_w_gate[i]).astype(BF16)
    ukv = mla_w_ukv[i].reshape(MLA_KV_RANK, MLA_HEADS, MLA_NOPE + MLA_V)
    ukv_k = _pad_heads(ukv[:, :, :MLA_NOPE].reshape(MLA_KV_RANK, MLA_HEADS * MLA_NOPE), MLA_NOPE)
    ukv_v = _pad_heads(ukv[:, :, MLA_NOPE:].reshape(MLA_KV_RANK, MLA_W), MLA_V)
    pw = pool_w[i]
    pool_bd = jnp.zeros((POOL_W, POOL_W), F32)
    for g in range(4):
        pool_bd = pool_bd.at[g * 64:(g + 1) * 64, g * 64:(g + 1) * 64].set(pw[g])
    rw = jnp.pad(router_w[i], ((0, 0), (0, 128 - N_EXPERTS)))
    rw_hi = rw.astype(BF16)
    rw_lo = (rw - rw_hi.astype(F32)).astype(BF16)
    row = lambda a: a.reshape(1, -1)
    pad96 = lambda a: jnp.pad(a, (0, HEAD_PAD - MLA_QK)).reshape(1, HEAD_PAD)
    wuq_p = _pad_heads(mla_w_uq[i], MLA_QK)
    gq_p = pad96(mla_qk_q_norm[i] * (MLA_QK ** -0.5 * LOG2E))
    return {
        "mix_norm": row(mix_norm[i]), "w_in": w_in_p, "gla_w_gate": wgate_p, "gla_b_gate": row(gla_b_gate[i]),
        "gla_out_norm": row(jnp.tile(gla_out_norm[i], GLA_HEADS)),
        "mla_q_norm": row(mla_q_norm[i]),
        "mla_w_uq": jnp.concatenate([wuq_p, _swap_rope_halves(wuq_p)], axis=1).astype(BF16),
        "mla_kv_norm": row(mla_kv_norm[i]), "mla_w_ukv_k": ukv_k.astype(BF16), "mla_w_ukv_v": ukv_v.astype(BF16),
        "mla_gq": jnp.concatenate([gq_p, _swap_rope_halves(gq_p)], axis=0), "mla_gk": pad96(mla_qk_k_norm[i]),
        "pool_w": pool_bd.astype(BF16), "pool_scale": row(pool_scale[i]),
        "w_out": w_out[i].astype(BF16), "ffn_norm": row(ffn_norm[i]),
        "router_w_hi": rw_hi, "router_w_lo": rw_lo,
        "router_b": row(jnp.pad(router_b[i], (0, 128 - N_EXPERTS))),
        "layer": i,
        "moe_w_gate": moe_w_gate, "moe_b_gate": moe_b_gate[i].reshape(N_EXPERTS, 1, D_FF),
        "moe_w_up": moe_w_up, "moe_b_up": moe_b_up[i].reshape(N_EXPERTS, 1, D_FF),
        "moe_w_down": moe_w_down, "moe_b_down": moe_b_down[i].reshape(N_EXPERTS, 1, D_MODEL),
        "ple_w_proj": ple_w_proj[i].astype(BF16), "ple_gate_norm": row(ple_gate_norm[i]),
        "ple_w_gate": ple_w_gate[i].astype(BF16), "ple_post_norm": row(ple_post_norm[i]),
    }


def _rope_tables(positions):
    T = positions.size
    inv = ROPE_BASE ** (-jnp.arange(0, MLA_ROPE, 2, dtype=F32) / MLA_ROPE)
    ang = positions.reshape(T, 1).astype(F32) * inv
    cos, sin = jnp.cos(ang), jnp.sin(ang)
    z16 = jnp.zeros((T, 16), F32)
    tail = jnp.zeros((T, HEAD_PAD - MLA_QK), F32)
    c = jnp.concatenate([jnp.ones((T, MLA_NOPE), F32), cos, cos, tail], axis=1)
    s1 = jnp.concatenate([jnp.zeros((T, MLA_NOPE), F32), -sin, z16, tail], axis=1)
    s2 = jnp.concatenate([jnp.zeros((T, MLA_NOPE), F32), z16, sin, tail], axis=1)
    return c, s1, s2


def _route(top_idx, rank, counts, T):
    bm = MOE_BLOCK
    A = T * TOP_K
    padded = (counts + bm - 1) // bm * bm
    pad_end = jnp.cumsum(padded)
    pad_start = pad_end - padded
    experts = jnp.arange(N_EXPERTS, dtype=jnp.int32)
    dest = rank + jnp.sum(jnp.where(top_idx[:, :, None] == experts, pad_start, 0), axis=-1)
    n_blocks = (A + N_EXPERTS * (bm - 1) + bm - 1) // bm
    n_rows = n_blocks * bm
    block_start = jnp.arange(n_blocks, dtype=jnp.int32) * bm
    block_e = jnp.minimum(jnp.sum((pad_end[None, :] <= block_start[:, None]).astype(jnp.int32), axis=1),
                          N_EXPERTS - 1)
    n_used = (pad_end[-1] // bm).astype(jnp.int32).reshape(1)
    return dest, n_rows, block_e, n_used


def _dispatch(hn, dest, n_rows):
    T, D = hn.shape
    win = DISPATCH_ROWS
    width = D // DISPATCH_SLABS
    dest_t = dest.T
    mesh = plsc.VectorSubcoreMesh(core_axis_name="core", subcore_axis_name="subcore")

    @functools.partial(pl.kernel, out_type=jax.ShapeDtypeStruct((n_rows, width), hn.dtype), mesh=mesh,
                       scratch_types=[], name="dispatch")
    def scatter_rows(x_hbm, i_hbm, o_hbm):
        def body(x_vmem, i_vmem):
            for kk in range(TOP_K):
                pltpu.sync_copy(x_vmem, o_hbm.at[i_vmem.at[kk]])

        pltpu.emit_pipeline(
            body, grid=(T // win,),
            in_specs=[pl.BlockSpec((win, width), lambda i: (i, 0)), pl.BlockSpec((TOP_K, win), lambda i: (0, i))],
            out_specs=[], core_axis_name=("core", "subcore"),
            dimension_semantics=(pltpu.PARALLEL,))(x_hbm, i_hbm)

    return [scatter_rows(hn[:, c * width:(c + 1) * width], dest_t) for c in range(DISPATCH_SLABS)]


def kernel(x, p, positions, mix_norm, w_in, gla_w_gate, gla_b_gate, gla_out_norm, mla_q_norm, mla_w_uq,
           mla_kv_norm, mla_w_ukv, mla_qk_q_norm, mla_qk_k_norm, pool_w, pool_scale, w_out, ffn_norm,
           router_w, router_b, moe_w_gate, moe_b_gate, moe_w_up, moe_b_up, moe_w_down, moe_b_down,
           ple_w_proj, ple_gate_norm, ple_w_gate, ple_post_norm):
    B, S, D = x.shape
    T = B * S
    depth = p.shape[0]
    params = (mix_norm, w_in, gla_w_gate, gla_b_gate, gla_out_norm, mla_q_norm, mla_w_uq, mla_kv_norm,
              mla_w_ukv, mla_qk_q_norm, mla_qk_k_norm, pool_w, pool_scale, w_out, ffn_norm, router_w,
              router_b, moe_w_gate, moe_b_gate, moe_w_up, moe_b_up, moe_w_down, moe_b_down,
              ple_w_proj, ple_gate_norm, ple_w_gate, ple_post_norm)
    groups = BATCH_GROUPS if B % BATCH_GROUPS == 0 else 1
    bg = B // groups
    tg = bg * S
    take = lambda a, idx: a.at[idx].get(mode="promise_in_bounds")
    hs, ropes = [], []
    for g in range(groups):
        hs.append(x[g * bg:(g + 1) * bg].reshape(tg, D))
        ropes.append(_rope_tables(positions[g * bg:(g + 1) * bg]))
    for i in range(depth):
        w = _layer_params(i, *params)
        for g in range(groups):
            h = hs[g]
            zg, la, q, k, v, y_pool = _mix_pre(h, w, *ropes[g], S)
            y_gla = _gla(zg, la, w["gla_out_norm"], bg, S)
            y_mla = _attn(q, k, v, bg, S)
            h1, hn, route, gates, counts = _out_router(h, y_gla, y_mla, y_pool, w)
            dest, n_rows, block_e, n_used = _route(route[:, :TOP_K], route[:, TOP_K:2 * TOP_K],
                                                   counts[0, :N_EXPERTS], tg)
            ys = _moe(_dispatch(hn, dest, n_rows), block_e, n_used, w)
            hs[g] = _ple(h1, [take(ys, dest[:, kk]) for kk in range(TOP_K)], gates,
                         p[i, g * bg:(g + 1) * bg].reshape(tg, D_PLE), w)
    return jnp.concatenate(hs, axis=0).reshape(B, S, D)
```

```python
import functools

import jax
import jax.numpy as jnp
import numpy as np
from jax import lax
from jax.experimental import pallas as pl
from jax.experimental.pallas import tpu as pltpu
from jax.experimental.pallas import tpu_sc as plsc

F32 = jnp.float32
BF16 = jnp.bfloat16

D_MODEL = 1024
EPS = 1e-6
D_PLE = 256

GLA_HEADS = 4
GLA_DK = 32
GLA_DV = 64
GLA_GATE_RANK = 16
GLA_TAU = 16.0
GLA_CHUNK = 64
GLA_K = GLA_HEADS * GLA_DK
GLA_W = GLA_HEADS * GLA_DV

MLA_HEADS = 8
MLA_Q_RANK = 256
MLA_KV_RANK = 128
MLA_NOPE = 64
MLA_ROPE = 32
MLA_QK = MLA_NOPE + MLA_ROPE
MLA_V = 64
MLA_W = MLA_HEADS * MLA_V
ROPE_BASE = 10000.0
HEAD_PAD = 128
MLA_QK_PAD = MLA_HEADS * HEAD_PAD

POOL_WINDOWS = (2, 4, 8, 16)
POOL_GROUP = 64
POOL_W = 256
POOL_HALO = 16

N_EXPERTS = 32
TOP_K = 4
D_FF = 1024
SWIGLU_LIMIT = 7.0
SWIGLU_ALPHA = 1.702

COL_GQ, COL_GK, COL_GV, COL_GR, COL_CQ, COL_POOL, COL_CKV, COL_MISC = 0, 128, 256, 512, 768, 1024, 1280, 1408
D_IN_PAD = 1536
MISC_GLOW = 0
MISC_ROPE = 16

LOG2E = 1.4426950408889634
TOKEN_TILE = 512
MIX_SUB, ROUTER_SUB, PLE_SUB = 512, 512, 256
GLA_TILE = 512
ATTN_TILE = 1024
ATTN_SUB = 512
MOE_BLOCK = 512
MOE_CAST_ROWS = 256
BATCH_GROUPS = 1
DISPATCH_ROWS = 128
DISPATCH_SLABS = 2
VMEM_LIMIT = 56 * 1024 * 1024
NEG_BIG = -1e30


def _cparams(n_axes, **flags):
    return pltpu.CompilerParams(dimension_semantics=("arbitrary",) * n_axes,
                                vmem_limit_bytes=VMEM_LIMIT, flags=flags or None)


def _rms(x, g):
    return x * lax.rsqrt(jnp.mean(x * x, axis=-1, keepdims=True) + EPS) * g


def _dot(a, b):
    return jnp.dot(a, b, preferred_element_type=F32)


def _dot_nt(a, b):
    return lax.dot_general(a, b, (((1,), (1,)), ((), ())), preferred_element_type=F32)


def _dot_tn(a, b):
    return lax.dot_general(a, b, (((0,), (0,)), ((), ())), preferred_element_type=F32)


def _split3(x):
    hi = x.astype(BF16)
    r = x - hi.astype(F32)
    mid = r.astype(BF16)
    lo = (r - mid.astype(F32)).astype(BF16)
    return hi, mid, lo


def _split2(x):
    hi = x.astype(BF16)
    lo = (x - hi.astype(F32)).astype(BF16)
    return hi, lo


def _pack_bf16_pairs(x):
    m = x.shape[1] // 2
    bits = lax.bitcast_convert_type(x.astype(BF16).astype(F32), jnp.uint32)
    return (bits[:, :m] >> 16) | (bits[:, m:] & jnp.uint32(0xFFFF0000))


def _unpack_bf16_pairs(w):
    lo = lax.bitcast_convert_type(w << 16, F32)
    hi = lax.bitcast_convert_type(w & jnp.uint32(0xFFFF0000), F32)
    return lo, hi


def _full(shape):
    nd = len(shape)
    return pl.BlockSpec(shape, lambda *_: (0,) * nd)


def _rope(x, c, s1, s2):
    return x * c + pltpu.roll(x, HEAD_PAD - 16, 1) * s1 + pltpu.roll(x, 16, 1) * s2


def _mix_pre_kernel(h_ref, mixn_ref, win_ref, wgate_ref, bgate_ref, qn_ref, wuq_ref, kvn_ref,
                    wukvk_ref, wukvv_ref, gq_ref, gk_ref, rc_ref, rs1_ref, rs2_ref,
                    wpool_ref, pscale_ref,
                    zg_ref, la_ref, q_ref, k_ref, v_ref, yp_ref, carry_ref, *, tiles_per_seq):
    tm = h_ref.shape[0]
    sub = MIX_SUB
    seq_tile = pl.program_id(0) % tiles_per_seq

    @pl.when(seq_tile == 0)
    def _():
        carry_ref[...] = jnp.zeros_like(carry_ref)

    lane = lax.broadcasted_iota(jnp.int32, (sub, HEAD_PAD), 1)
    in_rope = (lane >= MLA_NOPE) & (lane < MLA_QK)
    lane_v = lax.broadcasted_iota(jnp.int32, (sub, MLA_QK_PAD), 1)
    ones_lane = lane_v % HEAD_PAD == MLA_V
    lane_p = lax.broadcasted_iota(jnp.int32, (sub, POOL_W), 1)
    row_p = lax.broadcasted_iota(jnp.int32, (sub, POOL_W), 0)
    g0, g1, g2 = lane_p < 64, lane_p < 128, lane_p < 192
    win = jnp.where(g0, 2.0, jnp.where(g1, 4.0, jnp.where(g2, 8.0, 16.0)))
    gq, gq_sw, gk = gq_ref[0:1, :], gq_ref[1:2, :], gk_ref[...]

    for r0 in range(0, tm, sub):
        rows = slice(r0, r0 + sub)
        hn = _rms(h_ref[rows, :], mixn_ref[...]).astype(BF16)
        z = _dot(hn, win_ref[...])
        zg_ref[rows, :] = z[:, COL_GQ:COL_CQ]
        zm = z[:, COL_MISC:COL_MISC + 128]

        logit = _dot(zm.astype(BF16), wgate_ref[...]) + bgate_ref[...]
        la_ref[rows, :] = (jnp.minimum(logit, 0.0) - jnp.log(1.0 + jnp.exp(-jnp.abs(logit)))) * (1.0 / GLA_TAU)

        cqn = _rms(z[:, COL_CQ:COL_CQ + MLA_Q_RANK], qn_ref[...]).astype(BF16)
        qf = _dot(cqn, wuq_ref[...])
        ckvn = _rms(z[:, COL_CKV:COL_CKV + MLA_KV_RANK], kvn_ref[...]).astype(BF16)
        kn = _dot(ckvn, wukvk_ref[...])
        v_ref[rows, :] = jnp.where(ones_lane, 1.0, _dot(ckvn, wukvv_ref[...])).astype(BF16)

        rc, rs1, rs2 = rc_ref[rows, :], rs1_ref[rows, :], rs2_ref[rows, :]
        kr = jnp.where(in_rope, pltpu.roll(zm, MLA_NOPE - MISC_ROPE, 1), 0.0)
        kr_ss = jnp.sum(kr * kr, axis=-1, keepdims=True)
        krr = _rope(kr * gk, rc, rs1, rs2)
        cq = rc * gq
        sq_tab = (rs1 + rs2) * gq_sw
        for hh in range(MLA_HEADS):
            sl = slice(hh * HEAD_PAD, (hh + 1) * HEAD_PAD)
            qh = qf[:, sl]
            qsw = qf[:, MLA_QK_PAD + hh * HEAD_PAD:MLA_QK_PAD + (hh + 1) * HEAD_PAD]
            sq = lax.rsqrt(jnp.sum(qh * qh, axis=-1, keepdims=True) * (1.0 / MLA_QK) + EPS)
            q_ref[rows, sl] = ((qh * cq + qsw * sq_tab) * sq).astype(BF16)
            kh = kn[:, sl]
            sk = lax.rsqrt((jnp.sum(kh * kh, axis=-1, keepdims=True) + kr_ss) * (1.0 / MLA_QK) + EPS)
            k_ref[rows, sl] = (sk * (kh * gk + krr)).astype(BF16)

        u = z[:, COL_POOL:COL_POOL + POOL_W]
        xe = jnp.concatenate([carry_ref[...], u], axis=0)
        carry_ref[...] = u[sub - POOL_HALO:, :]
        s2 = xe + pltpu.roll(xe, 1, 0)
        s4 = s2 + pltpu.roll(s2, 2, 0)
        s8 = s4 + pltpu.roll(s4, 4, 0)
        s16 = s8 + pltpu.roll(s8, 8, 0)
        pooled = jnp.where(g0, s2[POOL_HALO:], jnp.where(g1, s4[POOL_HALO:],
                           jnp.where(g2, s8[POOL_HALO:], s16[POOL_HALO:])))
        cnt = jnp.minimum((seq_tile * tm + r0 + row_p + 1).astype(F32), win)
        d = pooled / cnt - u
        yp_ref[rows, :] = (_dot(d.astype(BF16), wpool_ref[...]) * pscale_ref[...]).astype(BF16)


def _mix_pre(h, w, rope_c, rope_s1, rope_s2, seq_len):
    T = h.shape[0]
    tm = TOKEN_TILE
    row = lambda n: pl.BlockSpec((tm, n), lambda i: (i, 0))
    ins = [h, w["mix_norm"], w["w_in"], w["gla_w_gate"], w["gla_b_gate"], w["mla_q_norm"], w["mla_w_uq"],
           w["mla_kv_norm"], w["mla_w_ukv_k"], w["mla_w_ukv_v"], w["mla_gq"], w["mla_gk"],
           rope_c, rope_s1, rope_s2, w["pool_w"], w["pool_scale"]]
    in_specs = [row(D_MODEL)] + [_full(a.shape) for a in ins[1:12]] + [row(HEAD_PAD)] * 3 + \
               [_full(w["pool_w"].shape), _full(w["pool_scale"].shape)]
    out_shape = [jax.ShapeDtypeStruct((T, COL_CQ), F32), jax.ShapeDtypeStruct((T, GLA_K), F32),
                 jax.ShapeDtypeStruct((T, MLA_QK_PAD), BF16), jax.ShapeDtypeStruct((T, MLA_QK_PAD), BF16),
                 jax.ShapeDtypeStruct((T, MLA_QK_PAD), BF16), jax.ShapeDtypeStruct((T, POOL_W), BF16)]
    out_specs = [row(COL_CQ), row(GLA_K), row(MLA_QK_PAD), row(MLA_QK_PAD), row(MLA_QK_PAD), row(POOL_W)]
    return pl.pallas_call(
        functools.partial(_mix_pre_kernel, tiles_per_seq=seq_len // tm),
        grid=(T // tm,), in_specs=in_specs, out_specs=out_specs, out_shape=out_shape,
        scratch_shapes=[pltpu.VMEM((POOL_HALO, POOL_W), F32)],
        compiler_params=_cparams(1), name="mix_pre")(*ins)


def _gla_kernel(zg_ref, la_ref, gn_ref, y_ref, state_ref, o_ref):
    tg = zg_ref.shape[0]
    C = GLA_CHUNK

    @pl.when(pl.program_id(1) == 0)
    def _():
        state_ref[...] = jnp.zeros_like(state_ref)

    r_i = lax.broadcasted_iota(jnp.int32, (C, C), 0)
    c_i = lax.broadcasted_iota(jnp.int32, (C, C), 1)
    tri = (r_i >= c_i).astype(BF16)
    ones = jnp.ones((C, GLA_W), BF16)
    head_k = lax.broadcasted_iota(jnp.int32, (C, GLA_K), 1) // GLA_DK
    head_v = lax.broadcasted_iota(jnp.int32, (C, GLA_W), 1) // GLA_DV
    ar = lax.broadcasted_iota(jnp.int32, (GLA_HEADS * C, C), 0)
    ac = lax.broadcasted_iota(jnp.int32, (GLA_HEADS * C, C), 1)
    causal = (ar % C) >= ac
    sk = lax.broadcasted_iota(jnp.int32, (GLA_K, GLA_W), 0) // GLA_DK
    sv = lax.broadcasted_iota(jnp.int32, (GLA_K, GLA_W), 1) // GLA_DV
    blockdiag = sk == sv

    for c in range(tg // C):
        rows = slice(c * C, (c + 1) * C)
        q = zg_ref[rows, COL_GQ:COL_GQ + GLA_K] * (GLA_DK ** -0.5)
        k = zg_ref[rows, COL_GK:COL_GK + GLA_K]
        v = zg_ref[rows, COL_GV:COL_GV + GLA_W].astype(BF16)
        la3 = _split3(la_ref[rows, :])
        bc = _dot(tri, la3[0]) + _dot(tri, la3[1]) + _dot(tri, la3[2])
        b_last = bc[C - 1:C, :]
        q_dec = (q * jnp.exp(bc)).astype(BF16)
        k_dec = (k * jnp.exp(-bc)).astype(BF16)
        k_end = (k * jnp.exp(b_last - bc)).astype(BF16)
        decay = jnp.exp(_dot_tn(la3[0], ones) + _dot_tn(la3[1], ones) + _dot_tn(la3[2], ones))
        zero = jnp.zeros_like(q_dec)
        qs = jnp.concatenate([jnp.where(head_k == hh, q_dec, zero) for hh in range(GLA_HEADS)], axis=0)
        att = jnp.where(causal, _dot_nt(qs, k_dec), 0.0).astype(BF16)
        o_full = _dot(att, v)
        o = _dot(q_dec, state_ref[...].astype(BF16))
        for hh in range(GLA_HEADS):
            o = o + jnp.where(head_v == hh, o_full[hh * C:(hh + 1) * C, :], 0.0)
        upd = _dot_tn(k_end, v)
        state_ref[...] = decay * state_ref[...] + jnp.where(blockdiag, upd, 0.0)
        o_ref[rows, :] = o

    o = o_ref[...]
    gr = lax.broadcasted_iota(jnp.int32, (GLA_W, GLA_W), 0) // GLA_DV
    gc = lax.broadcasted_iota(jnp.int32, (GLA_W, GLA_W), 1) // GLA_DV
    group = (gr == gc).astype(BF16)
    oo = _split2(o * o)
    ms = (_dot(oo[0], group) + _dot(oo[1], group)) * (1.0 / GLA_DV)
    r = zg_ref[:, COL_GR:COL_GR + GLA_W]
    y = o * lax.rsqrt(ms + EPS) * gn_ref[...] * (r / (1.0 + jnp.exp(-r)))
    y_ref[...] = y.astype(BF16)


def _gla(zg, la, gn, batch, seq_len):
    T = zg.shape[0]
    tg = GLA_TILE
    nt = seq_len // tg
    return pl.pallas_call(
        _gla_kernel, grid=(batch, nt),
        in_specs=[pl.BlockSpec((tg, COL_CQ), lambda b, s: (b * nt + s, 0)),
                  pl.BlockSpec((tg, GLA_K), lambda b, s: (b * nt + s, 0)),
                  _full(gn.shape)],
        out_specs=pl.BlockSpec((tg, GLA_W), lambda b, s: (b * nt + s, 0)),
        out_shape=jax.ShapeDtypeStruct((T, GLA_W), BF16),
        scratch_shapes=[pltpu.VMEM((GLA_K, GLA_W), F32), pltpu.VMEM((tg, GLA_W), F32)],
        compiler_params=_cparams(2), name="gla")(zg, la, gn)


def _attn_kernel(q_ref, k_ref, v_ref, o_ref, m_ref, acc_ref):
    tq = q_ref.shape[0]
    ts = ATTN_SUB
    i = pl.program_id(2)
    m_ref[...] = jnp.full_like(m_ref, NEG_BIG)
    acc_ref[...] = jnp.zeros_like(acc_ref)

    def sub_block(hh, start, r0, mask_off):
        hs = slice(hh * HEAD_PAD, (hh + 1) * HEAD_PAD)
        kj = k_ref[pl.ds(start, ts), hs]
        vj = v_ref[pl.ds(start, ts), hs]
        s = _dot_nt(q_ref[r0:, hs], kj)
        if mask_off is not None:
            row = lax.broadcasted_iota(jnp.int32, s.shape, 0) + r0
            col = lax.broadcasted_iota(jnp.int32, s.shape, 1) + mask_off
            s = jnp.where(col <= row, s, NEG_BIG)
        m_old = m_ref[hh, r0:, :]
        parts = [s[:, c * 128:(c + 1) * 128] for c in range(ts // 128)]
        m_new = jnp.maximum(m_old, jnp.max(functools.reduce(jnp.maximum, parts), axis=-1, keepdims=True))
        p = jnp.concatenate([jnp.exp2(x - m_new) for x in parts], axis=1).astype(BF16)
        acc_ref[hh, r0:, :] = jnp.exp2(m_old - m_new) * acc_ref[hh, r0:, :] + _dot(p, vj)
        m_ref[hh, r0:, :] = m_new

    def body(j, carry):
        base = pl.multiple_of(j * tq, tq)
        for sb in range(tq // ts):
            for hh in range(2):
                sub_block(hh, base + sb * ts, 0, None)
        return carry

    lax.fori_loop(0, i, body, 0)
    base = pl.multiple_of(i * tq, tq)
    for sb in range(tq // ts):
        for hh in range(2):
            sub_block(hh, base + sb * ts, sb * ts, sb * ts)
    outs = []
    for hh in range(2):
        a = acc_ref[hh]
        outs.append(a / a[:, MLA_V:MLA_V + 1])
    lane = lax.broadcasted_iota(jnp.int32, (tq, HEAD_PAD), 1)
    o_ref[...] = jnp.where(lane < MLA_V, outs[0], pltpu.roll(outs[1], MLA_V, 1)).astype(BF16)


def _attn(q, k, v, batch, seq_len):
    T = q.shape[0]
    tq = ATTN_TILE
    nq = seq_len // tq
    pairs = MLA_HEADS // 2
    return pl.pallas_call(
        _attn_kernel, grid=(batch, pairs, nq),
        in_specs=[pl.BlockSpec((tq, 2 * HEAD_PAD), lambda b, p, i: (b * nq + i, p)),
                  pl.BlockSpec((seq_len, 2 * HEAD_PAD), lambda b, p, i: (b, p)),
                  pl.BlockSpec((seq_len, 2 * HEAD_PAD), lambda b, p, i: (b, p))],
        out_specs=pl.BlockSpec((tq, 2 * MLA_V), lambda b, p, i: (b * nq + i, p)),
        out_shape=jax.ShapeDtypeStruct((T, MLA_W), BF16),
        scratch_shapes=[pltpu.VMEM((2, tq, HEAD_PAD), F32), pltpu.VMEM((2, tq, HEAD_PAD), F32)],
        compiler_params=_cparams(3), name="attn")(q, k, v)


def _out_router_kernel(h_ref, yg_ref, ym_ref, yp_ref, wo_ref, fn_ref, rw_hi_ref, rw_lo_ref, rb_ref,
                       h1_ref, hn_ref, idx_ref, gate_ref, cnt_ref, carry_ref):
    tm = h_ref.shape[0]

    @pl.when(pl.program_id(0) == 0)
    def _():
        carry_ref[...] = jnp.zeros_like(carry_ref)

    sub = ROUTER_SUB
    lane = lax.broadcasted_iota(jnp.int32, (sub, 128), 1)
    r_i = lax.broadcasted_iota(jnp.int32, (sub, sub), 0)
    c_i = lax.broadcasted_iota(jnp.int32, (sub, sub), 1)
    tri = (r_i >= c_i).astype(BF16)
    for r0 in range(0, tm, sub):
        rows = slice(r0, r0 + sub)
        h1 = (h_ref[rows, :] + _dot(yg_ref[rows, :], wo_ref[0:GLA_W, :])
              + _dot(ym_ref[rows, :], wo_ref[GLA_W:GLA_W + MLA_W, :])
              + _dot(yp_ref[rows, :], wo_ref[GLA_W + MLA_W:, :]))
        h1_ref[rows, :] = h1
        hn = _rms(h1, fn_ref[...])
        hi, lo = _split2(hn)
        hn_ref[rows, :] = _pack_bf16_pairs(hn)
        logits = _dot(hi, rw_hi_ref[...]) + _dot(lo, rw_hi_ref[...]) + _dot(hi, rw_lo_ref[...]) + rb_ref[...]
        cur = jnp.where(lane < N_EXPERTS, logits, NEG_BIG)
        idx_out = jnp.zeros((sub, 128), jnp.int32)
        val_out = jnp.zeros((sub, 128), F32)
        chosen = jnp.zeros((sub, 128), F32)
        top0 = None
        sels = []
        for kk in range(TOP_K):
            m = jnp.max(cur, axis=-1, keepdims=True)
            sel = jnp.min(jnp.where(cur == m, lane, 128), axis=-1, keepdims=True)
            if kk == 0:
                top0 = m
            sels.append(sel)
            idx_out = jnp.where(lane == kk, sel, idx_out)
            val_out = jnp.where(lane == kk, jnp.exp(m - top0), val_out)
            chosen = jnp.where(lane == sel, 1.0, chosen)
            cur = jnp.where(lane == sel, NEG_BIG, cur)
        gate_ref[rows, :] = val_out / jnp.sum(val_out, axis=-1, keepdims=True)

        incl = _dot(tri, chosen.astype(BF16))
        before = carry_ref[0:1, :] + incl - chosen
        for kk in range(TOP_K):
            rank = jnp.sum(jnp.where(lane == sels[kk], before, 0.0), axis=-1, keepdims=True)
            idx_out = jnp.where(lane == TOP_K + kk, rank.astype(jnp.int32), idx_out)
        idx_ref[rows, :] = idx_out
        carry_ref[...] = carry_ref[...] + incl[sub - 1:sub, :]
    cnt_ref[...] = carry_ref[...].astype(jnp.int32)


def _out_router(h, yg, ym, yp, w):
    T = h.shape[0]
    tm = TOKEN_TILE
    row = lambda n: pl.BlockSpec((tm, n), lambda i: (i, 0))
    ins = [h, yg, ym, yp, w["w_out"], w["ffn_norm"], w["router_w_hi"], w["router_w_lo"], w["router_b"]]
    return pl.pallas_call(
        _out_router_kernel, grid=(T // tm,),
        in_specs=[row(D_MODEL), row(GLA_W), row(MLA_W), row(POOL_W)] + [_full(a.shape) for a in ins[4:]],
        out_specs=[row(D_MODEL), row(D_MODEL // 2), row(128), row(128), _full((8, 128))],
        out_shape=[jax.ShapeDtypeStruct((T, D_MODEL), F32), jax.ShapeDtypeStruct((T, D_MODEL // 2), jnp.uint32),
                   jax.ShapeDtypeStruct((T, 128), jnp.int32), jax.ShapeDtypeStruct((T, 128), F32),
                   jax.ShapeDtypeStruct((8, 128), jnp.int32)],
        scratch_shapes=[pltpu.VMEM((8, 128), F32)],
        compiler_params=_cparams(1), name="out_router")(*ins)


def _moe_kernel(be_ref, nb_ref, x0_ref, x1_ref, wg_ref, bg_ref, wu_ref, bu_ref, wd_ref, bd_ref,
                y_ref, wg_bf, wu_bf, wd_bf):
    i = pl.program_id(0)
    used = i < nb_ref[0]
    new_expert = (i == 0) | (be_ref[i] != be_ref[jnp.maximum(i - 1, 0)])

    @pl.when(used & new_expert)
    def _():
        for src, dst in ((wg_ref, wg_bf), (wu_ref, wu_bf), (wd_ref, wd_bf)):
            for r in range(0, src.shape[2], MOE_CAST_ROWS):
                dst[r:r + MOE_CAST_ROWS, :] = src[0, 0, r:r + MOE_CAST_ROWS, :].astype(BF16)

    @pl.when(used)
    def _():
        halves = [_unpack_bf16_pairs(r[...]) for r in (x0_ref, x1_ref)]
        x = jnp.concatenate([h[0] for h in halves] + [h[1] for h in halves], axis=1).astype(BF16)
        g = jnp.minimum(_dot(x, wg_bf[...]) + bg_ref[0], SWIGLU_LIMIT)
        up = jnp.clip(_dot(x, wu_bf[...]) + bu_ref[0], -SWIGLU_LIMIT, SWIGLU_LIMIT)
        hb = (up + 1.0) * (g / (1.0 + jnp.exp(-SWIGLU_ALPHA * g)))
        y_ref[...] = _pack_bf16_pairs(_dot(hb.astype(BF16), wd_bf[...]) + bd_ref[0])

    @pl.when(jnp.logical_not(used))
    def _():
        y_ref[...] = jnp.zeros_like(y_ref)


def _moe(xs, block_e, n_used, w):
    n_rows = xs[0].shape[0]
    bm = MOE_BLOCK
    layer = w["layer"]
    wspec = lambda shp: pl.BlockSpec((1, 1) + shp, lambda i, be, nb: (layer, be[i], 0, 0))
    bspec = lambda shp: pl.BlockSpec((1,) + shp, lambda i, be, nb: (be[i], 0, 0))
    grid_spec = pltpu.PrefetchScalarGridSpec(
        num_scalar_prefetch=2, grid=(n_rows // bm,),
        in_specs=[pl.BlockSpec((bm, D_MODEL // 2 // DISPATCH_SLABS), lambda i, be, nb: (i, 0))] * DISPATCH_SLABS + [
                  wspec((D_MODEL, D_FF)), bspec((1, D_FF)), wspec((D_MODEL, D_FF)), bspec((1, D_FF)),
                  wspec((D_FF, D_MODEL)), bspec((1, D_MODEL))],
        out_specs=pl.BlockSpec((bm, D_MODEL // 2), lambda i, be, nb: (i, 0)),
        scratch_shapes=[pltpu.VMEM((D_MODEL, D_FF), BF16), pltpu.VMEM((D_MODEL, D_FF), BF16),
                        pltpu.VMEM((D_FF, D_MODEL), BF16)])
    return pl.pallas_call(
        _moe_kernel, grid_spec=grid_spec,
        out_shape=jax.ShapeDtypeStruct((n_rows, D_MODEL // 2), jnp.uint32),
        compiler_params=_cparams(1), name="moe")(
            block_e, n_used, *xs, w["moe_w_gate"], w["moe_b_gate"], w["moe_w_up"], w["moe_b_up"],
            w["moe_w_down"], w["moe_b_down"])


def _ple_kernel(h1_ref, y0_ref, y1_ref, y2_ref, y3_ref, gate_ref, p_ref, wple_ref, gn_ref, wpg_ref, pn_ref, o_ref):
    sub = PLE_SUB
    for r0 in range(0, h1_ref.shape[0], sub):
        rows = slice(r0, r0 + sub)
        gates = gate_ref[rows, :]
        h2 = h1_ref[rows, :]
        for kk, y_ref in enumerate((y0_ref, y1_ref, y2_ref, y3_ref)):
            h2 = h2 + gates[:, kk:kk + 1] * jnp.concatenate(_unpack_bf16_pairs(y_ref[rows, :]), axis=1)
        e = _dot(p_ref[rows, :].astype(BF16), wple_ref[...])
        a = _dot(_rms(h2, gn_ref[...]).astype(BF16), wpg_ref[...])
        gate = 1.0 / (1.0 + jnp.exp(-a))
        o_ref[rows, :] = h2 + _rms(e * gate, pn_ref[...])


def _ple(h1, ys_k, gates, p, w):
    T = h1.shape[0]
    tm = TOKEN_TILE
    row = lambda n: pl.BlockSpec((tm, n), lambda i: (i, 0))
    ins = [h1, *ys_k, gates, p, w["ple_w_proj"], w["ple_gate_norm"], w["ple_w_gate"], w["ple_post_norm"]]
    return pl.pallas_call(
        _ple_kernel, grid=(T // tm,),
        in_specs=[row(D_MODEL)] + [row(D_MODEL // 2)] * TOP_K + [row(128), row(D_PLE)] + [_full(a.shape) for a in ins[7:]],
        out_specs=row(D_MODEL), out_shape=jax.ShapeDtypeStruct((T, D_MODEL), F32),
        compiler_params=_cparams(1), name="ple")(*ins)


def _pad_heads(wm, per_head, n_heads=MLA_HEADS):
    kdim = wm.shape[0]
    w3 = wm.reshape(kdim, n_heads, per_head)
    return jnp.pad(w3, ((0, 0), (0, 0), (0, HEAD_PAD - per_head))).reshape(kdim, n_heads * HEAD_PAD)


def _swap_rope_halves(a):
    a3 = a.reshape(a.shape[0], -1, HEAD_PAD)
    half = MLA_ROPE // 2
    x1 = a3[:, :, MLA_NOPE:MLA_NOPE + half]
    x2 = a3[:, :, MLA_NOPE + half:MLA_QK]
    out = jnp.zeros_like(a3).at[:, :, MLA_NOPE:MLA_NOPE + half].set(x2).at[:, :, MLA_NOPE + half:MLA_QK].set(x1)
    return out.reshape(a.shape)


def _layer_params(i, mix_norm, w_in, gla_w_gate, gla_b_gate, gla_out_norm, mla_q_norm, mla_w_uq, mla_kv_norm,
                  mla_w_ukv, mla_qk_q_norm, mla_qk_k_norm, pool_w, pool_scale, w_out, ffn_norm, router_w,
                  router_b, moe_w_gate, moe_b_gate, moe_w_up, moe_b_up, moe_w_down, moe_b_down,
                  ple_w_proj, ple_gate_norm, ple_w_gate, ple_post_norm):
    wi = w_in[i]
    c = np.cumsum((0, 128, 128, 256, 16, 256, 256, 128, 32, 256))
    gq, gk, gv, glow, gr, cq, ckv, krope, upool = [wi[:, c[j]:c[j + 1]] for j in range(9)]
    misc = jnp.concatenate([glow, krope, jnp.zeros((D_MODEL, 128 - 48), F32)], axis=1)
    w_in_p = jnp.concatenate([gq, gk, gv, gr, cq, upool, ckv, misc], axis=1).astype(BF16)
    wgate_p = jnp.zeros((128, GLA_K), F32).at[MISC_GLOW:MISC_GLOW + GLA_GATE_RANK].set(gla_w_gate[i]).astype(BF16)
    ukv = mla_w_ukv[i].reshape(MLA_KV_RANK, MLA_HEADS, MLA_NOPE + MLA_V)
    ukv_k = _pad_heads(ukv[:, :, :MLA_NOPE].reshape(MLA_KV_RANK, MLA_HEADS * MLA_NOPE), MLA_NOPE)
    ukv_v = _pad_heads(ukv[:, :, MLA_NOPE:].reshape(MLA_KV_RANK, MLA_W), MLA_V)
    pw = pool_w[i]
    pool_bd = jnp.zeros((POOL_W, POOL_W), F32)
    for g in range(4):
        pool_bd = pool_bd.at[g * 64:(g + 1) * 64, g * 64:(g + 1) * 64].set(pw[g])
    rw = jnp.pad(router_w[i], ((0, 0), (0, 128 - N_EXPERTS)))
    rw_hi = rw.astype(BF16)
    rw_lo = (rw - rw_hi.astype(F32)).astype(BF16)
    row = lambda a: a.reshape(1, -1)
    pad96 = lambda a: jnp.pad(a, (0, HEAD_PAD - MLA_QK)).reshape(1, HEAD_PAD)
    wuq_p = _pad_heads(mla_w_uq[i], MLA_QK)
    gq_p = pad96(mla_qk_q_norm[i] * (MLA_QK ** -0.5 * LOG2E))
    return {
        "mix_norm": row(mix_norm[i]), "w_in": w_in_p, "gla_w_gate": wgate_p, "gla_b_gate": row(gla_b_gate[i]),
        "gla_out_norm": row(jnp.tile(gla_out_norm[i], GLA_HEADS)),
        "mla_q_norm": row(mla_q_norm[i]),
        "mla_w_uq": jnp.concatenate([wuq_p, _swap_rope_halves(wuq_p)], axis=1).astype(BF16),
        "mla_kv_norm": row(mla_kv_norm[i]), "mla_w_ukv_k": ukv_k.astype(BF16), "mla_w_ukv_v": ukv_v.astype(BF16),
        "mla_gq": jnp.concatenate([gq_p, _swap_rope_halves(gq_p)], axis=0), "mla_gk": pad96(mla_qk_k_norm[i]),
        "pool_w": pool_bd.astype(BF16), "pool_scale": row(pool_scale[i]),
        "w_out": w_out[i].astype(BF16), "ffn_norm": row(ffn_norm[i]),
        "router_w_hi": rw_hi, "router_w_lo": rw_lo,
        "router_b": row(jnp.pad(router_b[i], (0, 128 - N_EXPERTS))),
        "layer": i,
        "moe_w_gate": moe_w_gate, "moe_b_gate": moe_b_gate[i].reshape(N_EXPERTS, 1, D_FF),
        "moe_w_up": moe_w_up, "moe_b_up": moe_b_up[i].reshape(N_EXPERTS, 1, D_FF),
        "moe_w_down": moe_w_down, "moe_b_down": moe_b_down[i].reshape(N_EXPERTS, 1, D_MODEL),
        "ple_w_proj": ple_w_proj[i].astype(BF16), "ple_gate_norm": row(ple_gate_norm[i]),
        "ple_w_gate": ple_w_gate[i].astype(BF16), "ple_post_norm": row(ple_post_norm[i]),
    }


def _rope_tables(positions):
    T = positions.size
    inv = ROPE_BASE ** (-jnp.arange(0, MLA_ROPE, 2, dtype=F32) / MLA_ROPE)
    ang = positions.reshape(T, 1).astype(F32) * inv
    cos, sin = jnp.cos(ang), jnp.sin(ang)
    z16 = jnp.zeros((T, 16), F32)
    tail = jnp.zeros((T, HEAD_PAD - MLA_QK), F32)
    c = jnp.concatenate([jnp.ones((T, MLA_NOPE), F32), cos, cos, tail], axis=1)
    s1 = jnp.concatenate([jnp.zeros((T, MLA_NOPE), F32), -sin, z16, tail], axis=1)
    s2 = jnp.concatenate([jnp.zeros((T, MLA_NOPE), F32), z16, sin, tail], axis=1)
    return c, s1, s2


def _route(top_idx, rank, counts, T):
    bm = MOE_BLOCK
    A = T * TOP_K
    padded = (counts + bm - 1) // bm * bm
    pad_end = jnp.cumsum(padded)
    pad_start = pad_end - padded
    experts = jnp.arange(N_EXPERTS, dtype=jnp.int32)
    dest = rank + jnp.sum(jnp.where(top_idx[:, :, None] == experts, pad_start, 0), axis=-1)
    n_blocks = (A + N_EXPERTS * (bm - 1) + bm - 1) // bm
    n_rows = n_blocks * bm
    block_start = jnp.arange(n_blocks, dtype=jnp.int32) * bm
    block_e = jnp.minimum(jnp.sum((pad_end[None, :] <= block_start[:, None]).astype(jnp.int32), axis=1),
                          N_EXPERTS - 1)
    n_used = (pad_end[-1] // bm).astype(jnp.int32).reshape(1)
    return dest, n_rows, block_e, n_used


def _dispatch(hn, dest, n_rows):
    T, D = hn.shape
    win = DISPATCH_ROWS
    width = D // DISPATCH_SLABS
    dest_t = dest.T
    mesh = plsc.VectorSubcoreMesh(core_axis_name="core", subcore_axis_name="subcore")

    @functools.partial(pl.kernel, out_type=jax.ShapeDtypeStruct((n_rows, width), hn.dtype), mesh=mesh,
                       scratch_types=[], name="dispatch")
    def scatter_rows(x_hbm, i_hbm, o_hbm):
        def body(x_vmem, i_vmem):
            for kk in range(TOP_K):
                pltpu.sync_copy(x_vmem, o_hbm.at[i_vmem.at[kk]])

        pltpu.emit_pipeline(
            body, grid=(T // win,),
            in_specs=[pl.BlockSpec((win, width), lambda i: (i, 0)), pl.BlockSpec((TOP_K, win), lambda i: (0, i))],
            out_specs=[], core_axis_name=("core", "subcore"),
            dimension_semantics=(pltpu.PARALLEL,))(x_hbm, i_hbm)

    return [scatter_rows(hn[:, c * width:(c + 1) * width], dest_t) for c in range(DISPATCH_SLABS)]


def kernel(x, p, positions, mix_norm, w_in, gla_w_gate, gla_b_gate, gla_out_norm, mla_q_norm, mla_w_uq,
           mla_kv_norm, mla_w_ukv, mla_qk_q_norm, mla_qk_k_norm, pool_w, pool_scale, w_out, ffn_norm,
           router_w, router_b, moe_w_gate, moe_b_gate, moe_w_up, moe_b_up, moe_w_down, moe_b_down,
           ple_w_proj, ple_gate_norm, ple_w_gate, ple_post_norm):
    B, S, D = x.shape
    T = B * S
    depth = p.shape[0]
    params = (mix_norm, w_in, gla_w_gate, gla_b_gate, gla_out_norm, mla_q_norm, mla_w_uq, mla_kv_norm,
              mla_w_ukv, mla_qk_q_norm, mla_qk_k_norm, pool_w, pool_scale, w_out, ffn_norm, router_w,
              router_b, moe_w_gate, moe_b_gate, moe_w_up, moe_b_up, moe_w_down, moe_b_down,
              ple_w_proj, ple_gate_norm, ple_w_gate, ple_post_norm)
    groups = BATCH_GROUPS if B % BATCH_GROUPS == 0 else 1
    bg = B // groups
    tg = bg * S
    take = lambda a, idx: a.at[idx].get(mode="promise_in_bounds")
    hs, ropes = [], []
    for g in range(groups):
        hs.append(x[g * bg:(g + 1) * bg].reshape(tg, D))
        ropes.append(_rope_tables(positions[g * bg:(g + 1) * bg]))
    for i in range(depth):
        w = _layer_params(i, *params)
        for g in range(groups):
            h = hs[g]
            zg, la, q, k, v, y_pool = _mix_pre(h, w, *ropes[g], S)
            y_gla = _gla(zg, la, w["gla_out_norm"], bg, S)
            y_mla = _attn(q, k, v, bg, S)
            h1, hn, route, gates, counts = _out_router(h, y_gla, y_mla, y_pool, w)
            dest, n_rows, block_e, n_used = _route(route[:, :TOP_K], route[:, TOP_K:2 * TOP_K],
                                                   counts[0, :N_EXPERTS], tg)
            ys = _moe(_dispatch(hn, dest, n_rows), block_e, n_used, w)
            hs[g] = _ple(h1, [take(ys, dest[:, kk]) for kk in range(TOP_K)], gates,
                         p[i, g * bg:(g + 1) * bg].reshape(tg, D_PLE), w)
    return jnp.concatenate(hs, axis=0).reshape(B, S, D)
```

```python
import functools

import jax
import jax.numpy as jnp
import numpy as np
from jax import lax
from jax.experimental import pallas as pl
from jax.experimental.pallas import tpu as pltpu
from jax.experimental.pallas import tpu_sc as plsc

F32 = jnp.float32
BF16 = jnp.bfloat16

D_MODEL = 1024
EPS = 1e-6
D_PLE = 256

GLA_HEADS = 4
GLA_DK = 32
GLA_DV = 64
GLA_GATE_RANK = 16
GLA_TAU = 16.0
GLA_CHUNK = 64
GLA_K = GLA_HEADS * GLA_DK
GLA_W = GLA_HEADS * GLA_DV

MLA_HEADS = 8
MLA_Q_RANK = 256
MLA_KV_RANK = 128
MLA_NOPE = 64
MLA_ROPE = 32
MLA_QK = MLA_NOPE + MLA_ROPE
MLA_V = 64
MLA_W = MLA_HEADS * MLA_V
ROPE_BASE = 10000.0
HEAD_PAD = 128
MLA_QK_PAD = MLA_HEADS * HEAD_PAD

POOL_WINDOWS = (2, 4, 8, 16)
POOL_GROUP = 64
POOL_W = 256
POOL_HALO = 16

N_EXPERTS = 32
TOP_K = 4
D_FF = 1024
SWIGLU_LIMIT = 7.0
SWIGLU_ALPHA = 1.702

COL_GQ, COL_GK, COL_GV, COL_GR, COL_CQ, COL_POOL, COL_CKV, COL_MISC = 0, 128, 256, 512, 768, 1024, 1280, 1408
D_IN_PAD = 1536
MISC_GLOW = 0
MISC_ROPE = 16

LOG2E = 1.4426950408889634
TOKEN_TILE = 512
MIX_SUB, ROUTER_SUB, PLE_SUB = 256, 256, 128
GLA_TILE = 512
ATTN_TILE = 1024
ATTN_SUB = 512
MOE_BLOCK = 512
MOE_CAST_ROWS = 256
DISPATCH_ROWS = 128
DISPATCH_SLABS = 2
VMEM_LIMIT = 56 * 1024 * 1024
NEG_BIG = -1e30


def _cparams(n_axes, **flags):
    return pltpu.CompilerParams(dimension_semantics=("arbitrary",) * n_axes,
                                vmem_limit_bytes=VMEM_LIMIT, flags=flags or None)


def _rms(x, g):
    return x * lax.rsqrt(jnp.mean(x * x, axis=-1, keepdims=True) + EPS) * g


def _dot(a, b):
    return jnp.dot(a, b, preferred_element_type=F32)


def _dot_nt(a, b):
    return lax.dot_general(a, b, (((1,), (1,)), ((), ())), preferred_element_type=F32)


def _dot_tn(a, b):
    return lax.dot_general(a, b, (((0,), (0,)), ((), ())), preferred_element_type=F32)


def _split3(x):
    hi = x.astype(BF16)
    r = x - hi.astype(F32)
    mid = r.astype(BF16)
    lo = (r - mid.astype(F32)).astype(BF16)
    return hi, mid, lo


def _split2(x):
    hi = x.astype(BF16)
    lo = (x - hi.astype(F32)).astype(BF16)
    return hi, lo


def _pack_bf16_pairs(x):
    m = x.shape[1] // 2
    bits = lax.bitcast_convert_type(x.astype(BF16).astype(F32), jnp.uint32)
    return (bits[:, :m] >> 16) | (bits[:, m:] & jnp.uint32(0xFFFF0000))


def _unpack_bf16_pairs(w):
    lo = lax.bitcast_convert_type(w << 16, F32)
    hi = lax.bitcast_convert_type(w & jnp.uint32(0xFFFF0000), F32)
    return lo, hi


def _skewed(stages, n_rows, sub):
    states = [{"rows": slice(r0, r0 + sub)} for r0 in range(0, n_rows, sub)]
    for step in range(len(states) + len(stages) - 1):
        for s, stage in enumerate(stages):
            t = step - s
            if 0 <= t < len(states):
                stage(states[t])


def _full(shape):
    nd = len(shape)
    return pl.BlockSpec(shape, lambda *_: (0,) * nd)


def _rope(x, c, s1, s2):
    return x * c + pltpu.roll(x, HEAD_PAD - 16, 1) * s1 + pltpu.roll(x, 16, 1) * s2


def _mix_pre_kernel(h_ref, mixn_ref, win_ref, wgate_ref, bgate_ref, qn_ref, wuq_ref, kvn_ref,
                    wukvk_ref, wukvv_ref, gq_ref, gk_ref, rc_ref, rs1_ref, rs2_ref,
                    wpool_ref, pscale_ref,
                    zg_ref, la_ref, q_ref, k_ref, v_ref, yp_ref, carry_ref, *, tiles_per_seq):
    tm = h_ref.shape[0]
    sub = MIX_SUB
    seq_tile = pl.program_id(0) % tiles_per_seq

    @pl.when(seq_tile == 0)
    def _():
        carry_ref[...] = jnp.zeros_like(carry_ref)

    lane = lax.broadcasted_iota(jnp.int32, (sub, HEAD_PAD), 1)
    in_rope = (lane >= MLA_NOPE) & (lane < MLA_QK)
    lane_v = lax.broadcasted_iota(jnp.int32, (sub, MLA_QK_PAD), 1)
    ones_lane = lane_v % HEAD_PAD == MLA_V
    lane_p = lax.broadcasted_iota(jnp.int32, (sub, POOL_W), 1)
    row_p = lax.broadcasted_iota(jnp.int32, (sub, POOL_W), 0)
    g0, g1, g2 = lane_p < 64, lane_p < 128, lane_p < 192
    win = jnp.where(g0, 2.0, jnp.where(g1, 4.0, jnp.where(g2, 8.0, 16.0)))
    gq, gq_sw, gk = gq_ref[0:1, :], gq_ref[1:2, :], gk_ref[...]

    def norm_in(st):
        st["hn"] = _rms(h_ref[st["rows"], :], mixn_ref[...]).astype(BF16)

    def project_in(st):
        st["z"] = _dot(st["hn"], win_ref[...])

    def norm_latents(st):
        z = st["z"]
        zg_ref[st["rows"], :] = z[:, COL_GQ:COL_CQ]
        st["cqn"] = _rms(z[:, COL_CQ:COL_CQ + MLA_Q_RANK], qn_ref[...]).astype(BF16)
        st["ckvn"] = _rms(z[:, COL_CKV:COL_CKV + MLA_KV_RANK], kvn_ref[...]).astype(BF16)

    def project_up(st):
        zm = st["z"][:, COL_MISC:COL_MISC + 128]
        st["logit"] = _dot(zm.astype(BF16), wgate_ref[...]) + bgate_ref[...]
        st["qf"] = _dot(st["cqn"], wuq_ref[...])
        st["kn"] = _dot(st["ckvn"], wukvk_ref[...])
        st["v"] = _dot(st["ckvn"], wukvv_ref[...])

    def heads_and_pool(st):
        rows, z, qf, kn, logit = st["rows"], st["z"], st["qf"], st["kn"], st["logit"]
        zm = z[:, COL_MISC:COL_MISC + 128]
        la_ref[rows, :] = (jnp.minimum(logit, 0.0) - jnp.log(1.0 + jnp.exp(-jnp.abs(logit)))) * (1.0 / GLA_TAU)
        v_ref[rows, :] = jnp.where(ones_lane, 1.0, st["v"]).astype(BF16)

        rc, rs1, rs2 = rc_ref[rows, :], rs1_ref[rows, :], rs2_ref[rows, :]
        kr = jnp.where(in_rope, pltpu.roll(zm, MLA_NOPE - MISC_ROPE, 1), 0.0)
        kr_ss = jnp.sum(kr * kr, axis=-1, keepdims=True)
        krr = _rope(kr * gk, rc, rs1, rs2)
        cq = rc * gq
        sq_tab = (rs1 + rs2) * gq_sw
        for hh in range(MLA_HEADS):
            sl = slice(hh * HEAD_PAD, (hh + 1) * HEAD_PAD)
            qh = qf[:, sl]
            qsw = qf[:, MLA_QK_PAD + hh * HEAD_PAD:MLA_QK_PAD + (hh + 1) * HEAD_PAD]
            sq = lax.rsqrt(jnp.sum(qh * qh, axis=-1, keepdims=True) * (1.0 / MLA_QK) + EPS)
            q_ref[rows, sl] = ((qh * cq + qsw * sq_tab) * sq).astype(BF16)
            kh = kn[:, sl]
            sk = lax.rsqrt((jnp.sum(kh * kh, axis=-1, keepdims=True) + kr_ss) * (1.0 / MLA_QK) + EPS)
            k_ref[rows, sl] = (sk * (kh * gk + krr)).astype(BF16)

        u = z[:, COL_POOL:COL_POOL + POOL_W]
        xe = jnp.concatenate([carry_ref[...], u], axis=0)
        carry_ref[...] = u[sub - POOL_HALO:, :]
        s2 = xe + pltpu.roll(xe, 1, 0)
        s4 = s2 + pltpu.roll(s2, 2, 0)
        s8 = s4 + pltpu.roll(s4, 4, 0)
        s16 = s8 + pltpu.roll(s8, 8, 0)
        pooled = jnp.where(g0, s2[POOL_HALO:], jnp.where(g1, s4[POOL_HALO:],
                           jnp.where(g2, s8[POOL_HALO:], s16[POOL_HALO:])))
        cnt = jnp.minimum((seq_tile * tm + rows.start + row_p + 1).astype(F32), win)
        st["d"] = (pooled / cnt - u).astype(BF16)

    def project_pool(st):
        yp_ref[st["rows"], :] = (_dot(st["d"], wpool_ref[...]) * pscale_ref[...]).astype(BF16)

    _skewed([norm_in, project_in, norm_latents, project_up, heads_and_pool, project_pool], tm, sub)


def _mix_pre(h, w, rope_c, rope_s1, rope_s2, seq_len):
    T = h.shape[0]
    tm = TOKEN_TILE
    row = lambda n: pl.BlockSpec((tm, n), lambda i: (i, 0))
    ins = [h, w["mix_norm"], w["w_in"], w["gla_w_gate"], w["gla_b_gate"], w["mla_q_norm"], w["mla_w_uq"],
           w["mla_kv_norm"], w["mla_w_ukv_k"], w["mla_w_ukv_v"], w["mla_gq"], w["mla_gk"],
           rope_c, rope_s1, rope_s2, w["pool_w"], w["pool_scale"]]
    in_specs = [row(D_MODEL)] + [_full(a.shape) for a in ins[1:12]] + [row(HEAD_PAD)] * 3 + \
               [_full(w["pool_w"].shape), _full(w["pool_scale"].shape)]
    out_shape = [jax.ShapeDtypeStruct((T, COL_CQ), F32), jax.ShapeDtypeStruct((T, GLA_K), F32),
                 jax.ShapeDtypeStruct((T, MLA_QK_PAD), BF16), jax.ShapeDtypeStruct((T, MLA_QK_PAD), BF16),
                 jax.ShapeDtypeStruct((T, MLA_QK_PAD), BF16), jax.ShapeDtypeStruct((T, POOL_W), BF16)]
    out_specs = [row(COL_CQ), row(GLA_K), row(MLA_QK_PAD), row(MLA_QK_PAD), row(MLA_QK_PAD), row(POOL_W)]
    return pl.pallas_call(
        functools.partial(_mix_pre_kernel, tiles_per_seq=seq_len // tm),
        grid=(T // tm,), in_specs=in_specs, out_specs=out_specs, out_shape=out_shape,
        scratch_shapes=[pltpu.VMEM((POOL_HALO, POOL_W), F32)],
        compiler_params=_cparams(1), name="mix_pre")(*ins)


def _gla_kernel(zg_ref, la_ref, gn_ref, y_ref, state_ref, o_ref):
    tg = zg_ref.shape[0]
    C = GLA_CHUNK

    @pl.when(pl.program_id(1) == 0)
    def _():
        state_ref[...] = jnp.zeros_like(state_ref)

    r_i = lax.broadcasted_iota(jnp.int32, (C, C), 0)
    c_i = lax.broadcasted_iota(jnp.int32, (C, C), 1)
    tri = (r_i >= c_i).astype(BF16)
    ones = jnp.ones((C, GLA_W), BF16)
    head_k = lax.broadcasted_iota(jnp.int32, (C, GLA_K), 1) // GLA_DK
    head_v = lax.broadcasted_iota(jnp.int32, (C, GLA_W), 1) // GLA_DV
    ar = lax.broadcasted_iota(jnp.int32, (GLA_HEADS * C, C), 0)
    ac = lax.broadcasted_iota(jnp.int32, (GLA_HEADS * C, C), 1)
    causal = (ar % C) >= ac
    sk = lax.broadcasted_iota(jnp.int32, (GLA_K, GLA_W), 0) // GLA_DK
    sv = lax.broadcasted_iota(jnp.int32, (GLA_K, GLA_W), 1) // GLA_DV
    blockdiag = sk == sv

    def intra(st):
        rows = st["rows"]
        q = zg_ref[rows, COL_GQ:COL_GQ + GLA_K] * (GLA_DK ** -0.5)
        k = zg_ref[rows, COL_GK:COL_GK + GLA_K]
        v = zg_ref[rows, COL_GV:COL_GV + GLA_W].astype(BF16)
        la3 = _split3(la_ref[rows, :])
        bc = _dot(tri, la3[0]) + _dot(tri, la3[1]) + _dot(tri, la3[2])
        b_last = bc[C - 1:C, :]
        q_dec = (q * jnp.exp(bc)).astype(BF16)
        k_dec = (k * jnp.exp(-bc)).astype(BF16)
        k_end = (k * jnp.exp(b_last - bc)).astype(BF16)
        decay = jnp.exp(_dot_tn(la3[0], ones) + _dot_tn(la3[1], ones) + _dot_tn(la3[2], ones))
        zero = jnp.zeros_like(q_dec)
        qs = jnp.concatenate([jnp.where(head_k == hh, q_dec, zero) for hh in range(GLA_HEADS)], axis=0)
        att = jnp.where(causal, _dot_nt(qs, k_dec), 0.0).astype(BF16)
        o_full = _dot(att, v)
        o = jnp.where(head_v == 0, o_full[0:C, :], 0.0)
        for hh in range(1, GLA_HEADS):
            o = o + jnp.where(head_v == hh, o_full[hh * C:(hh + 1) * C, :], 0.0)
        st["o_intra"], st["q_dec"], st["decay"] = o, q_dec, decay
        st["upd"] = jnp.where(blockdiag, _dot_tn(k_end, v), 0.0)

    state = [state_ref[...]]

    def recur(st):
        o_ref[st["rows"], :] = st["o_intra"] + _dot(st["q_dec"], state[0].astype(BF16))
        state[0] = st["decay"] * state[0] + st["upd"]

    _skewed([intra, recur], tg, C)
    state_ref[...] = state[0]

    o = o_ref[...]
    gr = lax.broadcasted_iota(jnp.int32, (GLA_W, GLA_W), 0) // GLA_DV
    gc = lax.broadcasted_iota(jnp.int32, (GLA_W, GLA_W), 1) // GLA_DV
    group = (gr == gc).astype(BF16)
    oo = _split2(o * o)
    ms = (_dot(oo[0], group) + _dot(oo[1], group)) * (1.0 / GLA_DV)
    r = zg_ref[:, COL_GR:COL_GR + GLA_W]
    y = o * lax.rsqrt(ms + EPS) * gn_ref[...] * (r / (1.0 + jnp.exp(-r)))
    y_ref[...] = y.astype(BF16)


def _gla(zg, la, gn, batch, seq_len):
    T = zg.shape[0]
    tg = GLA_TILE
    nt = seq_len // tg
    return pl.pallas_call(
        _gla_kernel, grid=(batch, nt),
        in_specs=[pl.BlockSpec((tg, COL_CQ), lambda b, s: (b * nt + s, 0)),
                  pl.BlockSpec((tg, GLA_K), lambda b, s: (b * nt + s, 0)),
                  _full(gn.shape)],
        out_specs=pl.BlockSpec((tg, GLA_W), lambda b, s: (b * nt + s, 0)),
        out_shape=jax.ShapeDtypeStruct((T, GLA_W), BF16),
        scratch_shapes=[pltpu.VMEM((GLA_K, GLA_W), F32), pltpu.VMEM((tg, GLA_W), F32)],
        compiler_params=_cparams(2), name="gla")(zg, la, gn)


def _attn_kernel(q_ref, k_ref, v_ref, o_ref, m_ref, acc_ref):
    tq = q_ref.shape[0]
    ts = ATTN_SUB
    i = pl.program_id(2)
    m_ref[...] = jnp.full_like(m_ref, NEG_BIG)
    acc_ref[...] = jnp.zeros_like(acc_ref)

    def sub_block(hh, start, r0, mask_off):
        hs = slice(hh * HEAD_PAD, (hh + 1) * HEAD_PAD)
        kj = k_ref[pl.ds(start, ts), hs]
        vj = v_ref[pl.ds(start, ts), hs]
        s = _dot_nt(q_ref[r0:, hs], kj)
        if mask_off is not None:
            row = lax.broadcasted_iota(jnp.int32, s.shape, 0) + r0
            col = lax.broadcasted_iota(jnp.int32, s.shape, 1) + mask_off
            s = jnp.where(col <= row, s, NEG_BIG)
        m_old = m_ref[hh, r0:, :]
        parts = [s[:, c * 128:(c + 1) * 128] for c in range(ts // 128)]
        m_new = jnp.maximum(m_old, jnp.max(functools.reduce(jnp.maximum, parts), axis=-1, keepdims=True))
        p = jnp.concatenate([jnp.exp2(x - m_new) for x in parts], axis=1).astype(BF16)
        acc_ref[hh, r0:, :] = jnp.exp2(m_old - m_new) * acc_ref[hh, r0:, :] + _dot(p, vj)
        m_ref[hh, r0:, :] = m_new

    def body(j, carry):
        base = pl.multiple_of(j * tq, tq)
        for sb in range(tq // ts):
            for hh in range(2):
                sub_block(hh, base + sb * ts, 0, None)
        return carry

    lax.fori_loop(0, i, body, 0)
    base = pl.multiple_of(i * tq, tq)
    for sb in range(tq // ts):
        for hh in range(2):
            sub_block(hh, base + sb * ts, sb * ts, sb * ts)
    outs = []
    for hh in range(2):
        a = acc_ref[hh]
        outs.append(a / a[:, MLA_V:MLA_V + 1])
    lane = lax.broadcasted_iota(jnp.int32, (tq, HEAD_PAD), 1)
    o_ref[...] = jnp.where(lane < MLA_V, outs[0], pltpu.roll(outs[1], MLA_V, 1)).astype(BF16)


def _attn(q, k, v, batch, seq_len):
    T = q.shape[0]
    tq = ATTN_TILE
    nq = seq_len // tq
    pairs = MLA_HEADS // 2
    return pl.pallas_call(
        _attn_kernel, grid=(batch, pairs, nq),
        in_specs=[pl.BlockSpec((tq, 2 * HEAD_PAD), lambda b, p, i: (b * nq + i, p)),
                  pl.BlockSpec((seq_len, 2 * HEAD_PAD), lambda b, p, i: (b, p)),
                  pl.BlockSpec((seq_len, 2 * HEAD_PAD), lambda b, p, i: (b, p))],
        out_specs=pl.BlockSpec((tq, 2 * MLA_V), lambda b, p, i: (b * nq + i, p)),
        out_shape=jax.ShapeDtypeStruct((T, MLA_W), BF16),
        scratch_shapes=[pltpu.VMEM((2, tq, HEAD_PAD), F32), pltpu.VMEM((2, tq, HEAD_PAD), F32)],
        compiler_params=_cparams(3), name="attn")(q, k, v)


def _out_router_kernel(h_ref, yg_ref, ym_ref, yp_ref, wo_ref, fn_ref, rw_ref, rb_ref,
                       h1_ref, hn0_ref, hn1_ref, idx_ref, gate_ref, cnt_ref, carry_ref):
    tm = h_ref.shape[0]

    @pl.when(pl.program_id(0) == 0)
    def _():
        carry_ref[...] = jnp.zeros_like(carry_ref)

    sub = ROUTER_SUB
    lane = lax.broadcasted_iota(jnp.int32, (sub, 128), 1)
    r_i = lax.broadcasted_iota(jnp.int32, (sub, sub), 0)
    c_i = lax.broadcasted_iota(jnp.int32, (sub, sub), 1)
    tri = (r_i >= c_i).astype(BF16)
    def project(st):
        rows = st["rows"]
        st["h1"] = (h_ref[rows, :] + _dot(yg_ref[rows, :], wo_ref[0:GLA_W, :])
                    + _dot(ym_ref[rows, :], wo_ref[GLA_W:GLA_W + MLA_W, :])
                    + _dot(yp_ref[rows, :], wo_ref[GLA_W + MLA_W:, :]))

    def normalize(st):
        rows = st["rows"]
        h1_ref[rows, :] = st["h1"]
        hn = _rms(st["h1"], fn_ref[...])
        st["hi"], st["lo"] = _split2(hn)
        packed = _pack_bf16_pairs(hn)
        slab = packed.shape[1] // DISPATCH_SLABS
        hn0_ref[rows, :] = packed[:, :slab]
        hn1_ref[rows, :] = packed[:, slab:]

    def score(st):
        r2 = _dot(st["hi"], rw_ref[...])
        st["logits"] = r2[:, :128] + r2[:, 128:] + _dot(st["lo"], rw_ref[:, 0:128]) + rb_ref[...]

    def select(st):
        rows = st["rows"]
        cur = jnp.where(lane < N_EXPERTS, st["logits"], NEG_BIG)
        idx_out = jnp.zeros((sub, 128), jnp.int32)
        val_out = jnp.zeros((sub, 128), F32)
        chosen = jnp.zeros((sub, 128), F32)
        top0 = None
        sels = []
        for kk in range(TOP_K):
            m = jnp.max(cur, axis=-1, keepdims=True)
            sel = jnp.min(jnp.where(cur == m, lane, 128), axis=-1, keepdims=True)
            if kk == 0:
                top0 = m
            sels.append(sel)
            idx_out = jnp.where(lane == kk, sel, idx_out)
            val_out = jnp.where(lane == kk, jnp.exp(m - top0), val_out)
            chosen = jnp.where(lane == sel, 1.0, chosen)
            cur = jnp.where(lane == sel, NEG_BIG, cur)
        gate_ref[rows, :] = val_out / jnp.sum(val_out, axis=-1, keepdims=True)

        incl = _dot(tri, chosen.astype(BF16))
        before = carry_ref[0:1, :] + incl - chosen
        for kk in range(TOP_K):
            rank = jnp.sum(jnp.where(lane == sels[kk], before, 0.0), axis=-1, keepdims=True)
            idx_out = jnp.where(lane == TOP_K + kk, rank.astype(jnp.int32), idx_out)
        idx_ref[rows, :] = idx_out
        carry_ref[...] = carry_ref[...] + incl[sub - 1:sub, :]

    _skewed([project, normalize, score, select], tm, sub)
    cnt_ref[...] = carry_ref[...].astype(jnp.int32)


def _out_router(h, yg, ym, yp, w):
    T = h.shape[0]
    tm = TOKEN_TILE
    row = lambda n: pl.BlockSpec((tm, n), lambda i: (i, 0))
    slab = D_MODEL // 2 // DISPATCH_SLABS
    ins = [h, yg, ym, yp, w["w_out"], w["ffn_norm"], w["router_w"], w["router_b"]]
    return pl.pallas_call(
        _out_router_kernel, grid=(T // tm,),
        in_specs=[row(D_MODEL), row(GLA_W), row(MLA_W), row(POOL_W)] + [_full(a.shape) for a in ins[4:]],
        out_specs=[row(D_MODEL), row(slab), row(slab), row(128), row(128), _full((8, 128))],
        out_shape=[jax.ShapeDtypeStruct((T, D_MODEL), F32), jax.ShapeDtypeStruct((T, slab), jnp.uint32),
                   jax.ShapeDtypeStruct((T, slab), jnp.uint32),
                   jax.ShapeDtypeStruct((T, 128), jnp.int32), jax.ShapeDtypeStruct((T, 128), F32),
                   jax.ShapeDtypeStruct((8, 128), jnp.int32)],
        scratch_shapes=[pltpu.VMEM((8, 128), F32)],
        compiler_params=_cparams(1), name="out_router")(*ins)


def _moe_kernel(be_ref, nb_ref, x0_ref, x1_ref, wg_ref, bg_ref, wu_ref, bu_ref, wd_ref, bd_ref,
                y_ref, wg_bf, wu_bf, wd_bf):
    i = pl.program_id(0)
    used = i < nb_ref[0]
    new_expert = (i == 0) | (be_ref[i] != be_ref[jnp.maximum(i - 1, 0)])

    @pl.when(used & new_expert)
    def _():
        for src, dst in ((wg_ref, wg_bf), (wu_ref, wu_bf), (wd_ref, wd_bf)):
            for r in range(0, src.shape[2], MOE_CAST_ROWS):
                dst[r:r + MOE_CAST_ROWS, :] = src[0, 0, r:r + MOE_CAST_ROWS, :].astype(BF16)

    @pl.when(used)
    def _():
        halves = [_unpack_bf16_pairs(r[...]) for r in (x0_ref, x1_ref)]
        x = jnp.concatenate([h[0] for h in halves] + [h[1] for h in halves], axis=1).astype(BF16)
        g = jnp.minimum(_dot(x, wg_bf[...]) + bg_ref[0], SWIGLU_LIMIT)
        up = jnp.clip(_dot(x, wu_bf[...]) + bu_ref[0], -SWIGLU_LIMIT, SWIGLU_LIMIT)
        hb = (up + 1.0) * (g / (1.0 + jnp.exp(-SWIGLU_ALPHA * g)))
        y_ref[...] = _pack_bf16_pairs(_dot(hb.astype(BF16), wd_bf[...]) + bd_ref[0])

    @pl.when(jnp.logical_not(used))
    def _():
        y_ref[...] = jnp.zeros_like(y_ref)


def _moe(xs, block_e, n_used, w):
    n_rows = xs[0].shape[0]
    bm = MOE_BLOCK
    layer = w["layer"]
    wspec = lambda shp: pl.BlockSpec((1, 1) + shp, lambda i, be, nb: (layer, be[i], 0, 0))
    bspec = lambda shp: pl.BlockSpec((1,) + shp, lambda i, be, nb: (be[i], 0, 0))
    grid_spec = pltpu.PrefetchScalarGridSpec(
        num_scalar_prefetch=2, grid=(n_rows // bm,),
        in_specs=[pl.BlockSpec((bm, D_MODEL // 2 // DISPATCH_SLABS), lambda i, be, nb: (i, 0))] * DISPATCH_SLABS + [
                  wspec((D_MODEL, D_FF)), bspec((1, D_FF)), wspec((D_MODEL, D_FF)), bspec((1, D_FF)),
                  wspec((D_FF, D_MODEL)), bspec((1, D_MODEL))],
        out_specs=pl.BlockSpec((bm, D_MODEL // 2), lambda i, be, nb: (i, 0)),
        scratch_shapes=[pltpu.VMEM((D_MODEL, D_FF), BF16), pltpu.VMEM((D_MODEL, D_FF), BF16),
                        pltpu.VMEM((D_FF, D_MODEL), BF16)])
    return pl.pallas_call(
        _moe_kernel, grid_spec=grid_spec,
        out_shape=jax.ShapeDtypeStruct((n_rows, D_MODEL // 2), jnp.uint32),
        compiler_params=_cparams(1), name="moe")(
            block_e, n_used, *xs, w["moe_w_gate"], w["moe_b_gate"], w["moe_w_up"], w["moe_b_up"],
            w["moe_w_down"], w["moe_b_down"])


def _ple_kernel(h1_ref, y0_ref, y1_ref, y2_ref, y3_ref, gate_ref, p_ref, wple_ref, gn_ref, wpg_ref, pn_ref, o_ref):
    def combine(st):
        rows = st["rows"]
        gates = gate_ref[rows, :]
        h2 = h1_ref[rows, :]
        for kk, y_ref in enumerate((y0_ref, y1_ref, y2_ref, y3_ref)):
            h2 = h2 + gates[:, kk:kk + 1] * jnp.concatenate(_unpack_bf16_pairs(y_ref[rows, :]), axis=1)
        st["h2"] = h2
        st["hn"] = _rms(h2, gn_ref[...]).astype(BF16)

    def project(st):
        st["e"] = _dot(p_ref[0, st["rows"], :].astype(BF16), wple_ref[...])
        st["a"] = _dot(st["hn"], wpg_ref[...])

    def finish(st):
        gate = 1.0 / (1.0 + jnp.exp(-st["a"]))
        o_ref[st["rows"], :] = st["h2"] + _rms(st["e"] * gate, pn_ref[...])

    _skewed([combine, project, finish], h1_ref.shape[0], PLE_SUB)


def _ple(h1, ys_k, gates, p, w):
    T = h1.shape[0]
    tm = TOKEN_TILE
    row = lambda n: pl.BlockSpec((tm, n), lambda i: (i, 0))
    ins = [h1, *ys_k, gates, p, w["ple_w_proj"], w["ple_gate_norm"], w["ple_w_gate"], w["ple_post_norm"]]
    layer = w["layer"]
    p_spec = pl.BlockSpec((1, tm, D_PLE), lambda i: (layer, i, 0))
    return pl.pallas_call(
        _ple_kernel, grid=(T // tm,),
        in_specs=[row(D_MODEL)] + [row(D_MODEL // 2)] * TOP_K + [row(128), p_spec] + [_full(a.shape) for a in ins[7:]],
        out_specs=row(D_MODEL), out_shape=jax.ShapeDtypeStruct((T, D_MODEL), F32),
        compiler_params=_cparams(1), name="ple")(*ins)


def _pad_heads(wm, per_head, n_heads=MLA_HEADS):
    kdim = wm.shape[0]
    w3 = wm.reshape(kdim, n_heads, per_head)
    return jnp.pad(w3, ((0, 0), (0, 0), (0, HEAD_PAD - per_head))).reshape(kdim, n_heads * HEAD_PAD)


def _swap_rope_halves(a):
    a3 = a.reshape(a.shape[0], -1, HEAD_PAD)
    half = MLA_ROPE // 2
    x1 = a3[:, :, MLA_NOPE:MLA_NOPE + half]
    x2 = a3[:, :, MLA_NOPE + half:MLA_QK]
    out = jnp.zeros_like(a3).at[:, :, MLA_NOPE:MLA_NOPE + half].set(x2).at[:, :, MLA_NOPE + half:MLA_QK].set(x1)
    return out.reshape(a.shape)


def _layer_params(i, mix_norm, w_in, gla_w_gate, gla_b_gate, gla_out_norm, mla_q_norm, mla_w_uq, mla_kv_norm,
                  mla_w_ukv, mla_qk_q_norm, mla_qk_k_norm, pool_w, pool_scale, w_out, ffn_norm, router_w,
                  router_b, moe_w_gate, moe_b_gate, moe_w_up, moe_b_up, moe_w_down, moe_b_down,
                  ple_w_proj, ple_gate_norm, ple_w_gate, ple_post_norm):
    wi = w_in[i]
    c = np.cumsum((0, 128, 128, 256, 16, 256, 256, 128, 32, 256))
    gq, gk, gv, glow, gr, cq, ckv, krope, upool = [wi[:, c[j]:c[j + 1]] for j in range(9)]
    misc = jnp.concatenate([glow, krope, jnp.zeros((D_MODEL, 128 - 48), F32)], axis=1)
    w_in_p = jnp.concatenate([gq, gk, gv, gr, cq, upool, ckv, misc], axis=1).astype(BF16)
    wgate_p = jnp.zeros((128, GLA_K), F32).at[MISC_GLOW:MISC_GLOW + GLA_GATE_RANK].set(gla_w_gate[i]).astype(BF16)
    ukv = mla_w_ukv[i].reshape(MLA_KV_RANK, MLA_HEADS, MLA_NOPE + MLA_V)
    ukv_k = _pad_heads(ukv[:, :, :MLA_NOPE].reshape(MLA_KV_RANK, MLA_HEADS * MLA_NOPE), MLA_NOPE)
    ukv_v = _pad_heads(ukv[:, :, MLA_NOPE:].reshape(MLA_KV_RANK, MLA_W), MLA_V)
    pw = pool_w[i]
    pool_bd = jnp.zeros((POOL_W, POOL_W), F32)
    for g in range(4):
        pool_bd = pool_bd.at[g * 64:(g + 1) * 64, g * 64:(g + 1) * 64].set(pw[g])
    rw = jnp.pad(router_w[i], ((0, 0), (0, 128 - N_EXPERTS)))
    rw_hi = rw.astype(BF16)
    rw_lo = (rw - rw_hi.astype(F32)).astype(BF16)
    row = lambda a: a.reshape(1, -1)
    pad96 = lambda a: jnp.pad(a, (0, HEAD_PAD - MLA_QK)).reshape(1, HEAD_PAD)
    wuq_p = _pad_heads(mla_w_uq[i], MLA_QK)
    gq_p = pad96(mla_qk_q_norm[i] * (MLA_QK ** -0.5 * LOG2E))
    return {
        "mix_norm": row(mix_norm[i]), "w_in": w_in_p, "gla_w_gate": wgate_p, "gla_b_gate": row(gla_b_gate[i]),
        "gla_out_norm": row(jnp.tile(gla_out_norm[i], GLA_HEADS)),
        "mla_q_norm": row(mla_q_norm[i]),
        "mla_w_uq": jnp.concatenate([wuq_p, _swap_rope_halves(wuq_p)], axis=1).astype(BF16),
        "mla_kv_norm": row(mla_kv_norm[i]), "mla_w_ukv_k": ukv_k.astype(BF16), "mla_w_ukv_v": ukv_v.astype(BF16),
        "mla_gq": jnp.concatenate([gq_p, _swap_rope_halves(gq_p)], axis=0), "mla_gk": pad96(mla_qk_k_norm[i]),
        "pool_w": pool_bd.astype(BF16), "pool_scale": row(pool_scale[i]),
        "w_out": w_out[i].astype(BF16), "ffn_norm": row(ffn_norm[i]),
        "router_w": jnp.concatenate([rw_hi, rw_lo], axis=1),
        "router_b": row(jnp.pad(router_b[i], (0, 128 - N_EXPERTS))),
        "layer": i,
        "moe_w_gate": moe_w_gate, "moe_b_gate": moe_b_gate[i].reshape(N_EXPERTS, 1, D_FF),
        "moe_w_up": moe_w_up, "moe_b_up": moe_b_up[i].reshape(N_EXPERTS, 1, D_FF),
        "moe_w_down": moe_w_down, "moe_b_down": moe_b_down[i].reshape(N_EXPERTS, 1, D_MODEL),
        "ple_w_proj": ple_w_proj[i].astype(BF16), "ple_gate_norm": row(ple_gate_norm[i]),
        "ple_w_gate": ple_w_gate[i].astype(BF16), "ple_post_norm": row(ple_post_norm[i]),
    }


def _rope_tables(positions):
    T = positions.size
    inv = ROPE_BASE ** (-jnp.arange(0, MLA_ROPE, 2, dtype=F32) / MLA_ROPE)
    ang = positions.reshape(T, 1).astype(F32) * inv
    cos, sin = jnp.cos(ang), jnp.sin(ang)
    z16 = jnp.zeros((T, 16), F32)
    tail = jnp.zeros((T, HEAD_PAD - MLA_QK), F32)
    c = jnp.concatenate([jnp.ones((T, MLA_NOPE), F32), cos, cos, tail], axis=1)
    s1 = jnp.concatenate([jnp.zeros((T, MLA_NOPE), F32), -sin, z16, tail], axis=1)
    s2 = jnp.concatenate([jnp.zeros((T, MLA_NOPE), F32), z16, sin, tail], axis=1)
    return c, s1, s2


def _route(top_idx, rank, counts, T):
    bm = MOE_BLOCK
    A = T * TOP_K
    padded = (counts + bm - 1) // bm * bm
    pad_end = jnp.cumsum(padded)
    pad_start = pad_end - padded
    experts = jnp.arange(N_EXPERTS, dtype=jnp.int32)
    dest = rank + jnp.sum(jnp.where(top_idx[:, :, None] == experts, pad_start, 0), axis=-1)
    n_blocks = (A + N_EXPERTS * (bm - 1) + bm - 1) // bm
    n_rows = n_blocks * bm
    block_start = jnp.arange(n_blocks, dtype=jnp.int32) * bm
    block_e = jnp.minimum(jnp.sum((pad_end[None, :] <= block_start[:, None]).astype(jnp.int32), axis=1),
                          N_EXPERTS - 1)
    n_used = (pad_end[-1] // bm).astype(jnp.int32).reshape(1)
    return dest, n_rows, block_e, n_used


def _dispatch(hn_slabs, dest, n_rows):
    T, width = hn_slabs[0].shape
    win = DISPATCH_ROWS
    dest_t = dest.T
    mesh = plsc.VectorSubcoreMesh(core_axis_name="core", subcore_axis_name="subcore")

    @functools.partial(pl.kernel, out_type=jax.ShapeDtypeStruct((n_rows, width), hn_slabs[0].dtype), mesh=mesh,
                       scratch_types=[], name="dispatch")
    def scatter_rows(x_hbm, i_hbm, o_hbm):
        def body(x_vmem, i_vmem):
            for kk in range(TOP_K):
                pltpu.sync_copy(x_vmem, o_hbm.at[i_vmem.at[kk]])

        pltpu.emit_pipeline(
            body, grid=(T // win,),
            in_specs=[pl.BlockSpec((win, width), lambda i: (i, 0)), pl.BlockSpec((TOP_K, win), lambda i: (0, i))],
            out_specs=[], core_axis_name=("core", "subcore"),
            dimension_semantics=(pltpu.PARALLEL,))(x_hbm, i_hbm)

    return [scatter_rows(slab, dest_t) for slab in hn_slabs]


def kernel(x, p, positions, mix_norm, w_in, gla_w_gate, gla_b_gate, gla_out_norm, mla_q_norm, mla_w_uq,
           mla_kv_norm, mla_w_ukv, mla_qk_q_norm, mla_qk_k_norm, pool_w, pool_scale, w_out, ffn_norm,
           router_w, router_b, moe_w_gate, moe_b_gate, moe_w_up, moe_b_up, moe_w_down, moe_b_down,
           ple_w_proj, ple_gate_norm, ple_w_gate, ple_post_norm):
    B, S, D = x.shape
    T = B * S
    depth = p.shape[0]
    params = (mix_norm, w_in, gla_w_gate, gla_b_gate, gla_out_norm, mla_q_norm, mla_w_uq, mla_kv_norm,
              mla_w_ukv, mla_qk_q_norm, mla_qk_k_norm, pool_w, pool_scale, w_out, ffn_norm, router_w,
              router_b, moe_w_gate, moe_b_gate, moe_w_up, moe_b_up, moe_w_down, moe_b_down,
              ple_w_proj, ple_gate_norm, ple_w_gate, ple_post_norm)
    take = lambda a, idx: a.at[idx].get(mode="promise_in_bounds")
    rope_c, rope_s1, rope_s2 = _rope_tables(positions)
    p_flat = p.reshape(depth, T, D_PLE)
    h = x.reshape(T, D)
    for i in range(depth):
        w = _layer_params(i, *params)
        zg, la, q, k, v, y_pool = _mix_pre(h, w, rope_c, rope_s1, rope_s2, S)
        y_gla = _gla(zg, la, w["gla_out_norm"], B, S)
        y_mla = _attn(q, k, v, B, S)
        h1, hn0, hn1, route, gates, counts = _out_router(h, y_gla, y_mla, y_pool, w)
        dest, n_rows, block_e, n_used = _route(route[:, :TOP_K], route[:, TOP_K:2 * TOP_K],
                                               counts[0, :N_EXPERTS], T)
        ys = _moe(_dispatch([hn0, hn1], dest, n_rows), block_e, n_used, w)
        h = _ple(h1, [take(ys, dest[:, kk]) for kk in range(TOP_K)], gates, p_flat, w)
    return h.reshape(B, S, D)
```

```python
import functools

import jax
import jax.numpy as jnp
import numpy as np
from jax import lax
from jax.experimental import pallas as pl
from jax.experimental.pallas import tpu as pltpu
from jax.experimental.pallas import tpu_sc as plsc

F32 = jnp.float32
BF16 = jnp.bfloat16

D_MODEL = 1024
EPS = 1e-6
D_PLE = 256

GLA_HEADS = 4
GLA_DK = 32
GLA_DV = 64
GLA_GATE_RANK = 16
GLA_TAU = 16.0
GLA_CHUNK = 64
GLA_K = GLA_HEADS * GLA_DK
GLA_W = GLA_HEADS * GLA_DV

MLA_HEADS = 8
MLA_Q_RANK = 256
MLA_KV_RANK = 128
MLA_NOPE = 64
MLA_ROPE = 32
MLA_QK = MLA_NOPE + MLA_ROPE
MLA_V = 64
MLA_W = MLA_HEADS * MLA_V
ROPE_BASE = 10000.0
HEAD_PAD = 128
MLA_QK_PAD = MLA_HEADS * HEAD_PAD

POOL_WINDOWS = (2, 4, 8, 16)
POOL_GROUP = 64
POOL_W = 256
POOL_HALO = 16

N_EXPERTS = 32
TOP_K = 4
D_FF = 1024
SWIGLU_LIMIT = 7.0
SWIGLU_ALPHA = 1.702

COL_GQ, COL_GK, COL_GV, COL_GR, COL_CQ, COL_POOL, COL_CKV, COL_MISC = 0, 128, 256, 512, 768, 1024, 1280, 1408
D_IN_PAD = 1536
MISC_GLOW = 0
MISC_ROPE = 16

LOG2E = 1.4426950408889634
TOKEN_TILE = 512
MIX_SUB, ROUTER_SUB, PLE_SUB = 256, 256, 128
GLA_TILE = 512
ATTN_TILE = 1024
ATTN_SUB = 512
MOE_BLOCK = 512
MOE_CAST_ROWS = 256
COMBINE_PARTS = 4
DISPATCH_ROWS = 128
DISPATCH_SLABS = 2
VMEM_LIMIT = 56 * 1024 * 1024
NEG_BIG = -1e30


def _cparams(n_axes, **flags):
    return pltpu.CompilerParams(dimension_semantics=("arbitrary",) * n_axes,
                                vmem_limit_bytes=VMEM_LIMIT, flags=flags or None)


def _rms(x, g):
    return x * lax.rsqrt(jnp.mean(x * x, axis=-1, keepdims=True) + EPS) * g


def _dot(a, b):
    return jnp.dot(a, b, preferred_element_type=F32)


def _dot_nt(a, b):
    return lax.dot_general(a, b, (((1,), (1,)), ((), ())), preferred_element_type=F32)


def _dot_tn(a, b):
    return lax.dot_general(a, b, (((0,), (0,)), ((), ())), preferred_element_type=F32)


def _split3(x):
    hi = x.astype(BF16)
    r = x - hi.astype(F32)
    mid = r.astype(BF16)
    lo = (r - mid.astype(F32)).astype(BF16)
    return hi, mid, lo


def _split2(x):
    hi = x.astype(BF16)
    lo = (x - hi.astype(F32)).astype(BF16)
    return hi, lo


def _pack_bf16_pairs(x):
    m = x.shape[1] // 2
    bits = lax.bitcast_convert_type(x.astype(BF16).astype(F32), jnp.uint32)
    return (bits[:, :m] >> 16) | (bits[:, m:] & jnp.uint32(0xFFFF0000))


def _unpack_bf16_pairs(w):
    lo = lax.bitcast_convert_type(w << 16, F32)
    hi = lax.bitcast_convert_type(w & jnp.uint32(0xFFFF0000), F32)
    return lo, hi


def _skewed(stages, n_rows, sub):
    states = [{"rows": slice(r0, r0 + sub)} for r0 in range(0, n_rows, sub)]
    for step in range(len(states) + len(stages) - 1):
        for s, stage in enumerate(stages):
            t = step - s
            if 0 <= t < len(states):
                stage(states[t])


def _full(shape):
    nd = len(shape)
    return pl.BlockSpec(shape, lambda *_: (0,) * nd)


def _rope(x, c, s1, s2):
    return x * c + pltpu.roll(x, HEAD_PAD - 16, 1) * s1 + pltpu.roll(x, 16, 1) * s2


def _mix_pre_kernel(h_ref, mixn_ref, win_ref, wgate_ref, bgate_ref, qn_ref, wuq_ref, kvn_ref,
                    wukvk_ref, wukvv_ref, gq_ref, gk_ref, rc_ref, rs1_ref, rs2_ref,
                    wpool_ref, pscale_ref,
                    zg_ref, la_ref, q_ref, k_ref, v_ref, yp_ref, carry_ref, *, tiles_per_seq):
    tm = h_ref.shape[0]
    sub = MIX_SUB
    seq_tile = pl.program_id(0) % tiles_per_seq

    @pl.when(seq_tile == 0)
    def _():
        carry_ref[...] = jnp.zeros_like(carry_ref)

    lane = lax.broadcasted_iota(jnp.int32, (sub, HEAD_PAD), 1)
    in_rope = (lane >= MLA_NOPE) & (lane < MLA_QK)
    lane_v = lax.broadcasted_iota(jnp.int32, (sub, MLA_QK_PAD), 1)
    ones_lane = lane_v % HEAD_PAD == MLA_V
    lane_p = lax.broadcasted_iota(jnp.int32, (sub, POOL_W), 1)
    row_p = lax.broadcasted_iota(jnp.int32, (sub, POOL_W), 0)
    g0, g1, g2 = lane_p < 64, lane_p < 128, lane_p < 192
    win = jnp.where(g0, 2.0, jnp.where(g1, 4.0, jnp.where(g2, 8.0, 16.0)))
    gq, gq_sw, gk = gq_ref[0:1, :], gq_ref[1:2, :], gk_ref[...]

    def norm_in(st):
        st["hn"] = _rms(h_ref[st["rows"], :], mixn_ref[...]).astype(BF16)

    def project_in(st):
        st["z"] = _dot(st["hn"], win_ref[...])

    def norm_latents(st):
        z = st["z"]
        zg_ref[st["rows"], :] = z[:, COL_GQ:COL_CQ]
        st["cqn"] = _rms(z[:, COL_CQ:COL_CQ + MLA_Q_RANK], qn_ref[...]).astype(BF16)
        st["ckvn"] = _rms(z[:, COL_CKV:COL_CKV + MLA_KV_RANK], kvn_ref[...]).astype(BF16)

    def project_up(st):
        zm = st["z"][:, COL_MISC:COL_MISC + 128]
        st["logit"] = _dot(zm.astype(BF16), wgate_ref[...]) + bgate_ref[...]
        st["qf"] = _dot(st["cqn"], wuq_ref[...])
        st["kn"] = _dot(st["ckvn"], wukvk_ref[...])
        st["v"] = _dot(st["ckvn"], wukvv_ref[...])

    def heads_and_pool(st):
        rows, z, qf, kn, logit = st["rows"], st["z"], st["qf"], st["kn"], st["logit"]
        zm = z[:, COL_MISC:COL_MISC + 128]
        la_ref[rows, :] = (jnp.minimum(logit, 0.0) - jnp.log(1.0 + jnp.exp(-jnp.abs(logit)))) * (1.0 / GLA_TAU)
        v_ref[rows, :] = jnp.where(ones_lane, 1.0, st["v"]).astype(BF16)

        rc, rs1, rs2 = rc_ref[rows, :], rs1_ref[rows, :], rs2_ref[rows, :]
        kr = jnp.where(in_rope, pltpu.roll(zm, MLA_NOPE - MISC_ROPE, 1), 0.0)
        kr_ss = jnp.sum(kr * kr, axis=-1, keepdims=True)
        krr = _rope(kr * gk, rc, rs1, rs2)
        cq = rc * gq
        sq_tab = (rs1 + rs2) * gq_sw
        for hh in range(MLA_HEADS):
            sl = slice(hh * HEAD_PAD, (hh + 1) * HEAD_PAD)
            qh = qf[:, sl]
            qsw = qf[:, MLA_QK_PAD + hh * HEAD_PAD:MLA_QK_PAD + (hh + 1) * HEAD_PAD]
            sq = lax.rsqrt(jnp.sum(qh * qh, axis=-1, keepdims=True) * (1.0 / MLA_QK) + EPS)
            q_ref[rows, sl] = ((qh * cq + qsw * sq_tab) * sq).astype(BF16)
            kh = kn[:, sl]
            sk = lax.rsqrt((jnp.sum(kh * kh, axis=-1, keepdims=True) + kr_ss) * (1.0 / MLA_QK) + EPS)
            k_ref[rows, sl] = (sk * (kh * gk + krr)).astype(BF16)

        u = z[:, COL_POOL:COL_POOL + POOL_W]
        xe = jnp.concatenate([carry_ref[...], u], axis=0)
        carry_ref[...] = u[sub - POOL_HALO:, :]
        s2 = xe + pltpu.roll(xe, 1, 0)
        s4 = s2 + pltpu.roll(s2, 2, 0)
        s8 = s4 + pltpu.roll(s4, 4, 0)
        s16 = s8 + pltpu.roll(s8, 8, 0)
        pooled = jnp.where(g0, s2[POOL_HALO:], jnp.where(g1, s4[POOL_HALO:],
                           jnp.where(g2, s8[POOL_HALO:], s16[POOL_HALO:])))
        cnt = jnp.minimum((seq_tile * tm + rows.start + row_p + 1).astype(F32), win)
        st["d"] = (pooled / cnt - u).astype(BF16)

    def project_pool(st):
        yp_ref[st["rows"], :] = (_dot(st["d"], wpool_ref[...]) * pscale_ref[...]).astype(BF16)

    _skewed([norm_in, project_in, norm_latents, project_up, heads_and_pool, project_pool], tm, sub)


def _mix_pre(h, w, rope_c, rope_s1, rope_s2, seq_len):
    T = h.shape[0]
    tm = TOKEN_TILE
    row = lambda n: pl.BlockSpec((tm, n), lambda i: (i, 0))
    ins = [h, w["mix_norm"], w["w_in"], w["gla_w_gate"], w["gla_b_gate"], w["mla_q_norm"], w["mla_w_uq"],
           w["mla_kv_norm"], w["mla_w_ukv_k"], w["mla_w_ukv_v"], w["mla_gq"], w["mla_gk"],
           rope_c, rope_s1, rope_s2, w["pool_w"], w["pool_scale"]]
    in_specs = [row(D_MODEL)] + [_full(a.shape) for a in ins[1:12]] + [row(HEAD_PAD)] * 3 + \
               [_full(w["pool_w"].shape), _full(w["pool_scale"].shape)]
    out_shape = [jax.ShapeDtypeStruct((T, COL_CQ), F32), jax.ShapeDtypeStruct((T, GLA_K), F32),
                 jax.ShapeDtypeStruct((T, MLA_QK_PAD), BF16), jax.ShapeDtypeStruct((T, MLA_QK_PAD), BF16),
                 jax.ShapeDtypeStruct((T, MLA_QK_PAD), BF16), jax.ShapeDtypeStruct((T, POOL_W), BF16)]
    out_specs = [row(COL_CQ), row(GLA_K), row(MLA_QK_PAD), row(MLA_QK_PAD), row(MLA_QK_PAD), row(POOL_W)]
    return pl.pallas_call(
        functools.partial(_mix_pre_kernel, tiles_per_seq=seq_len // tm),
        grid=(T // tm,), in_specs=in_specs, out_specs=out_specs, out_shape=out_shape,
        scratch_shapes=[pltpu.VMEM((POOL_HALO, POOL_W), F32)],
        compiler_params=_cparams(1), name="mix_pre")(*ins)


def _gla_kernel(zg_ref, la_ref, gn_ref, y_ref, state_ref, o_ref):
    tg = zg_ref.shape[0]
    C = GLA_CHUNK

    @pl.when(pl.program_id(1) == 0)
    def _():
        state_ref[...] = jnp.zeros_like(state_ref)

    r_i = lax.broadcasted_iota(jnp.int32, (C, C), 0)
    c_i = lax.broadcasted_iota(jnp.int32, (C, C), 1)
    tri = (r_i >= c_i).astype(BF16)
    ones = jnp.ones((C, GLA_W), BF16)
    head_k = lax.broadcasted_iota(jnp.int32, (C, GLA_K), 1) // GLA_DK
    head_v = lax.broadcasted_iota(jnp.int32, (C, GLA_W), 1) // GLA_DV
    ar = lax.broadcasted_iota(jnp.int32, (GLA_HEADS * C, C), 0)
    ac = lax.broadcasted_iota(jnp.int32, (GLA_HEADS * C, C), 1)
    causal = (ar % C) >= ac
    sk = lax.broadcasted_iota(jnp.int32, (GLA_K, GLA_W), 0) // GLA_DK
    sv = lax.broadcasted_iota(jnp.int32, (GLA_K, GLA_W), 1) // GLA_DV
    blockdiag = sk == sv

    def intra(st):
        rows = st["rows"]
        q = zg_ref[rows, COL_GQ:COL_GQ + GLA_K] * (GLA_DK ** -0.5)
        k = zg_ref[rows, COL_GK:COL_GK + GLA_K]
        v = zg_ref[rows, COL_GV:COL_GV + GLA_W].astype(BF16)
        la3 = _split3(la_ref[rows, :])
        bc = _dot(tri, la3[0]) + _dot(tri, la3[1]) + _dot(tri, la3[2])
        b_last = bc[C - 1:C, :]
        q_dec = (q * jnp.exp(bc)).astype(BF16)
        k_dec = (k * jnp.exp(-bc)).astype(BF16)
        k_end = (k * jnp.exp(b_last - bc)).astype(BF16)
        decay = jnp.exp(_dot_tn(la3[0], ones) + _dot_tn(la3[1], ones) + _dot_tn(la3[2], ones))
        zero = jnp.zeros_like(q_dec)
        qs = jnp.concatenate([jnp.where(head_k == hh, q_dec, zero) for hh in range(GLA_HEADS)], axis=0)
        att = jnp.where(causal, _dot_nt(qs, k_dec), 0.0).astype(BF16)
        o_full = _dot(att, v)
        o = jnp.where(head_v == 0, o_full[0:C, :], 0.0)
        for hh in range(1, GLA_HEADS):
            o = o + jnp.where(head_v == hh, o_full[hh * C:(hh + 1) * C, :], 0.0)
        st["o_intra"], st["q_dec"], st["decay"] = o, q_dec, decay
        st["upd"] = jnp.where(blockdiag, _dot_tn(k_end, v), 0.0)

    state = [state_ref[...]]

    def recur(st):
        o_ref[st["rows"], :] = st["o_intra"] + _dot(st["q_dec"], state[0].astype(BF16))
        state[0] = st["decay"] * state[0] + st["upd"]

    _skewed([intra, recur], tg, C)
    state_ref[...] = state[0]

    o = o_ref[...]
    gr = lax.broadcasted_iota(jnp.int32, (GLA_W, GLA_W), 0) // GLA_DV
    gc = lax.broadcasted_iota(jnp.int32, (GLA_W, GLA_W), 1) // GLA_DV
    group = (gr == gc).astype(BF16)
    oo = _split2(o * o)
    ms = (_dot(oo[0], group) + _dot(oo[1], group)) * (1.0 / GLA_DV)
    r = zg_ref[:, COL_GR:COL_GR + GLA_W]
    y = o * lax.rsqrt(ms + EPS) * gn_ref[...] * (r / (1.0 + jnp.exp(-r)))
    y_ref[...] = y.astype(BF16)


def _gla(zg, la, gn, batch, seq_len):
    T = zg.shape[0]
    tg = GLA_TILE
    nt = seq_len // tg
    return pl.pallas_call(
        _gla_kernel, grid=(batch, nt),
        in_specs=[pl.BlockSpec((tg, COL_CQ), lambda b, s: (b * nt + s, 0)),
                  pl.BlockSpec((tg, GLA_K), lambda b, s: (b * nt + s, 0)),
                  _full(gn.shape)],
        out_specs=pl.BlockSpec((tg, GLA_W), lambda b, s: (b * nt + s, 0)),
        out_shape=jax.ShapeDtypeStruct((T, GLA_W), BF16),
        scratch_shapes=[pltpu.VMEM((GLA_K, GLA_W), F32), pltpu.VMEM((tg, GLA_W), F32)],
        compiler_params=_cparams(2), name="gla")(zg, la, gn)


def _attn_kernel(q_ref, k_ref, v_ref, o_ref, m_ref, acc_ref):
    tq = q_ref.shape[0]
    ts = ATTN_SUB
    i = pl.program_id(2)
    m_ref[...] = jnp.full_like(m_ref, NEG_BIG)
    acc_ref[...] = jnp.zeros_like(acc_ref)

    def sub_block(hh, start, r0, mask_off):
        hs = slice(hh * HEAD_PAD, (hh + 1) * HEAD_PAD)
        kj = k_ref[pl.ds(start, ts), hs]
        vj = v_ref[pl.ds(start, ts), hs]
        s = _dot_nt(q_ref[r0:, hs], kj)
        if mask_off is not None:
            row = lax.broadcasted_iota(jnp.int32, s.shape, 0) + r0
            col = lax.broadcasted_iota(jnp.int32, s.shape, 1) + mask_off
            s = jnp.where(col <= row, s, NEG_BIG)
        m_old = m_ref[hh, r0:, :]
        parts = [s[:, c * 128:(c + 1) * 128] for c in range(ts // 128)]
        m_new = jnp.maximum(m_old, jnp.max(functools.reduce(jnp.maximum, parts), axis=-1, keepdims=True))
        p = jnp.concatenate([jnp.exp2((x - m_new).astype(BF16)) for x in parts], axis=1)
        acc_ref[hh, r0:, :] = jnp.exp2(m_old - m_new) * acc_ref[hh, r0:, :] + _dot(p, vj)
        m_ref[hh, r0:, :] = m_new

    def body(j, carry):
        base = pl.multiple_of(j * tq, tq)
        for sb in range(tq // ts):
            for hh in range(2):
                sub_block(hh, base + sb * ts, 0, None)
        return carry

    lax.fori_loop(0, i, body, 0)
    base = pl.multiple_of(i * tq, tq)
    for sb in range(tq // ts):
        for hh in range(2):
            sub_block(hh, base + sb * ts, sb * ts, sb * ts)
    outs = []
    for hh in range(2):
        a = acc_ref[hh]
        outs.append(a / a[:, MLA_V:MLA_V + 1])
    lane = lax.broadcasted_iota(jnp.int32, (tq, HEAD_PAD), 1)
    o_ref[...] = jnp.where(lane < MLA_V, outs[0], pltpu.roll(outs[1], MLA_V, 1)).astype(BF16)


def _attn(q, k, v, batch, seq_len):
    T = q.shape[0]
    tq = ATTN_TILE
    nq = seq_len // tq
    pairs = MLA_HEADS // 2
    return pl.pallas_call(
        _attn_kernel, grid=(batch, pairs, nq),
        in_specs=[pl.BlockSpec((tq, 2 * HEAD_PAD), lambda b, p, i: (b * nq + i, p)),
                  pl.BlockSpec((seq_len, 2 * HEAD_PAD), lambda b, p, i: (b, p)),
                  pl.BlockSpec((seq_len, 2 * HEAD_PAD), lambda b, p, i: (b, p))],
        out_specs=pl.BlockSpec((tq, 2 * MLA_V), lambda b, p, i: (b * nq + i, p)),
        out_shape=jax.ShapeDtypeStruct((T, MLA_W), BF16),
        scratch_shapes=[pltpu.VMEM((2, tq, HEAD_PAD), F32), pltpu.VMEM((2, tq, HEAD_PAD), F32)],
        compiler_params=_cparams(3), name="attn")(q, k, v)


def _out_router_kernel(h_ref, yg_ref, ym_ref, yp_ref, wo_ref, fn_ref, rw_ref, rb_ref,
                       h1_ref, hn0_ref, hn1_ref, idx_ref, gate_ref, cnt_ref, carry_ref):
    tm = h_ref.shape[0]

    @pl.when(pl.program_id(0) == 0)
    def _():
        carry_ref[...] = jnp.zeros_like(carry_ref)

    sub = ROUTER_SUB
    lane = lax.broadcasted_iota(jnp.int32, (sub, 128), 1)
    r_i = lax.broadcasted_iota(jnp.int32, (sub, sub), 0)
    c_i = lax.broadcasted_iota(jnp.int32, (sub, sub), 1)
    tri = (r_i >= c_i).astype(BF16)
    def project(st):
        rows = st["rows"]
        st["h1"] = (h_ref[rows, :] + _dot(yg_ref[rows, :], wo_ref[0:GLA_W, :])
                    + _dot(ym_ref[rows, :], wo_ref[GLA_W:GLA_W + MLA_W, :])
                    + _dot(yp_ref[rows, :], wo_ref[GLA_W + MLA_W:, :]))

    def normalize(st):
        rows = st["rows"]
        h1_ref[rows, :] = st["h1"]
        hn = _rms(st["h1"], fn_ref[...])
        st["hi"], st["lo"] = _split2(hn)
        packed = _pack_bf16_pairs(hn)
        slab = packed.shape[1] // DISPATCH_SLABS
        hn0_ref[rows, :] = packed[:, :slab]
        hn1_ref[rows, :] = packed[:, slab:]

    def score(st):
        r2 = _dot(st["hi"], rw_ref[...])
        st["logits"] = r2[:, :128] + r2[:, 128:] + _dot(st["lo"], rw_ref[:, 0:128]) + rb_ref[...]

    def select(st):
        rows = st["rows"]
        cur = jnp.where(lane < N_EXPERTS, st["logits"], NEG_BIG)
        idx_out = jnp.zeros((sub, 128), jnp.int32)
        val_out = jnp.zeros((sub, 128), F32)
        chosen = jnp.zeros((sub, 128), F32)
        top0 = None
        sels = []
        for kk in range(TOP_K):
            m = jnp.max(cur, axis=-1, keepdims=True)
            sel = jnp.min(jnp.where(cur == m, lane, 128), axis=-1, keepdims=True)
            if kk == 0:
                top0 = m
            sels.append(sel)
            idx_out = jnp.where(lane == kk, sel, idx_out)
            val_out = jnp.where(lane == kk, jnp.exp(m - top0), val_out)
            chosen = jnp.where(lane == sel, 1.0, chosen)
            cur = jnp.where(lane == sel, NEG_BIG, cur)
        gate_ref[rows, :] = val_out / jnp.sum(val_out, axis=-1, keepdims=True)

        incl = _dot(tri, chosen.astype(BF16))
        before = carry_ref[0:1, :] + incl - chosen
        for kk in range(TOP_K):
            rank = jnp.sum(jnp.where(lane == sels[kk], before, 0.0), axis=-1, keepdims=True)
            idx_out = jnp.where(lane == TOP_K + kk, rank.astype(jnp.int32), idx_out)
        idx_ref[rows, :] = idx_out
        carry_ref[...] = carry_ref[...] + incl[sub - 1:sub, :]

    _skewed([project, normalize, score, select], tm, sub)
    cnt_ref[...] = carry_ref[...].astype(jnp.int32)


def _out_router(h, yg, ym, yp, w):
    T = h.shape[0]
    tm = TOKEN_TILE
    row = lambda n: pl.BlockSpec((tm, n), lambda i: (i, 0))
    slab = D_MODEL // 2 // DISPATCH_SLABS
    ins = [h, yg, ym, yp, w["w_out"], w["ffn_norm"], w["router_w"], w["router_b"]]
    return pl.pallas_call(
        _out_router_kernel, grid=(T // tm,),
        in_specs=[row(D_MODEL), row(GLA_W), row(MLA_W), row(POOL_W)] + [_full(a.shape) for a in ins[4:]],
        out_specs=[row(D_MODEL), row(slab), row(slab), row(128), row(128), _full((8, 128))],
        out_shape=[jax.ShapeDtypeStruct((T, D_MODEL), F32), jax.ShapeDtypeStruct((T, slab), jnp.uint32),
                   jax.ShapeDtypeStruct((T, slab), jnp.uint32),
                   jax.ShapeDtypeStruct((T, 128), jnp.int32), jax.ShapeDtypeStruct((T, 128), F32),
                   jax.ShapeDtypeStruct((8, 128), jnp.int32)],
        scratch_shapes=[pltpu.VMEM((8, 128), F32)],
        compiler_params=_cparams(1), name="out_router")(*ins)


def _moe_kernel(be_ref, nb_ref, x0_ref, x1_ref, wg_ref, bg_ref, wu_ref, bu_ref, wd_ref, bd_ref,
                y_ref, wg_bf, wu_bf, wd_bf):
    i = pl.program_id(0)
    used = i < nb_ref[0]
    new_expert = (i == 0) | (be_ref[i] != be_ref[jnp.maximum(i - 1, 0)])

    @pl.when(used & new_expert)
    def _():
        for src, dst in ((wg_ref, wg_bf), (wu_ref, wu_bf), (wd_ref, wd_bf)):
            for r in range(0, src.shape[2], MOE_CAST_ROWS):
                dst[r:r + MOE_CAST_ROWS, :] = src[0, 0, r:r + MOE_CAST_ROWS, :].astype(BF16)

    @pl.when(used)
    def _():
        halves = [_unpack_bf16_pairs(r[...]) for r in (x0_ref, x1_ref)]
        x = jnp.concatenate([h[0] for h in halves] + [h[1] for h in halves], axis=1).astype(BF16)
        g = jnp.minimum(_dot(x, wg_bf[...]) + bg_ref[0], SWIGLU_LIMIT)
        up = jnp.clip(_dot(x, wu_bf[...]) + bu_ref[0], -SWIGLU_LIMIT, SWIGLU_LIMIT)
        hb = (up + 1.0) * (g / (1.0 + jnp.exp(-SWIGLU_ALPHA * g)))
        y_ref[...] = _pack_bf16_pairs(_dot(hb.astype(BF16), wd_bf[...]) + bd_ref[0])

    @pl.when(jnp.logical_not(used))
    def _():
        y_ref[...] = jnp.zeros_like(y_ref)


def _moe(xs, block_e, n_used, w):
    n_rows = xs[0].shape[0]
    bm = MOE_BLOCK
    layer = w["layer"]
    wspec = lambda shp: pl.BlockSpec((1, 1) + shp, lambda i, be, nb: (layer, be[i], 0, 0))
    bspec = lambda shp: pl.BlockSpec((1,) + shp, lambda i, be, nb: (be[i], 0, 0))
    grid_spec = pltpu.PrefetchScalarGridSpec(
        num_scalar_prefetch=2, grid=(n_rows // bm,),
        in_specs=[pl.BlockSpec((bm, D_MODEL // 2 // DISPATCH_SLABS), lambda i, be, nb: (i, 0))] * DISPATCH_SLABS + [
                  wspec((D_MODEL, D_FF)), bspec((1, D_FF)), wspec((D_MODEL, D_FF)), bspec((1, D_FF)),
                  wspec((D_FF, D_MODEL)), bspec((1, D_MODEL))],
        out_specs=pl.BlockSpec((bm, D_MODEL // 2), lambda i, be, nb: (i, 0)),
        scratch_shapes=[pltpu.VMEM((D_MODEL, D_FF), BF16), pltpu.VMEM((D_MODEL, D_FF), BF16),
                        pltpu.VMEM((D_FF, D_MODEL), BF16)])
    return pl.pallas_call(
        _moe_kernel, grid_spec=grid_spec,
        out_shape=jax.ShapeDtypeStruct((n_rows, D_MODEL // 2), jnp.uint32),
        compiler_params=_cparams(1), name="moe")(
            block_e, n_used, *xs, w["moe_w_gate"], w["moe_b_gate"], w["moe_w_up"], w["moe_b_up"],
            w["moe_w_down"], w["moe_b_down"])


def _ple_kernel(h1_ref, y0_ref, y1_ref, y2_ref, y3_ref, gate_ref, p_ref, wple_ref, gn_ref, wpg_ref, pn_ref, o_ref):
    def combine(st):
        rows = st["rows"]
        gates = gate_ref[rows, :]
        h2 = h1_ref[rows, :]
        for kk, y_ref in enumerate((y0_ref, y1_ref, y2_ref, y3_ref)):
            h2 = h2 + gates[:, kk:kk + 1] * jnp.concatenate(_unpack_bf16_pairs(y_ref[rows, :]), axis=1)
        st["h2"] = h2
        st["hn"] = _rms(h2, gn_ref[...]).astype(BF16)

    def project(st):
        st["e"] = _dot(p_ref[0, st["rows"], :].astype(BF16), wple_ref[...])
        st["a"] = _dot(st["hn"], wpg_ref[...])

    def finish(st):
        gate = 1.0 / (1.0 + jnp.exp(-st["a"]))
        o_ref[st["rows"], :] = st["h2"] + _rms(st["e"] * gate, pn_ref[...])

    _skewed([combine, project, finish], h1_ref.shape[0], PLE_SUB)


def _ple(h1, ys_k, gates, p, w, part):
    T = h1.shape[0]
    tm = TOKEN_TILE
    steps = T // COMBINE_PARTS // tm
    off = part * steps
    row = lambda n: pl.BlockSpec((tm, n), lambda i: (i + off, 0))
    local = pl.BlockSpec((tm, D_MODEL // 2), lambda i: (i, 0))
    weights = [w["ple_w_proj"], w["ple_gate_norm"], w["ple_w_gate"], w["ple_post_norm"]]
    layer = w["layer"]
    p_spec = pl.BlockSpec((1, tm, D_PLE), lambda i: (layer, i + off, 0))
    ins = [h1, *ys_k, gates, p, *weights]
    in_specs = [row(D_MODEL)] + [local] * TOP_K + [row(128), p_spec] + [_full(a.shape) for a in weights]
    return pl.pallas_call(
        _ple_kernel, grid=(steps,), in_specs=in_specs,
        out_specs=row(D_MODEL), out_shape=jax.ShapeDtypeStruct((T, D_MODEL), F32),
        input_output_aliases={0: 0}, compiler_params=_cparams(1), name="ple")(*ins)


def _pad_heads(wm, per_head, n_heads=MLA_HEADS):
    kdim = wm.shape[0]
    w3 = wm.reshape(kdim, n_heads, per_head)
    return jnp.pad(w3, ((0, 0), (0, 0), (0, HEAD_PAD - per_head))).reshape(kdim, n_heads * HEAD_PAD)


def _swap_rope_halves(a):
    a3 = a.reshape(a.shape[0], -1, HEAD_PAD)
    half = MLA_ROPE // 2
    x1 = a3[:, :, MLA_NOPE:MLA_NOPE + half]
    x2 = a3[:, :, MLA_NOPE + half:MLA_QK]
    out = jnp.zeros_like(a3).at[:, :, MLA_NOPE:MLA_NOPE + half].set(x2).at[:, :, MLA_NOPE + half:MLA_QK].set(x1)
    return out.reshape(a.shape)


def _layer_params(i, mix_norm, w_in, gla_w_gate, gla_b_gate, gla_out_norm, mla_q_norm, mla_w_uq, mla_kv_norm,
                  mla_w_ukv, mla_qk_q_norm, mla_qk_k_norm, pool_w, pool_scale, w_out, ffn_norm, router_w,
                  router_b, moe_w_gate, moe_b_gate, moe_w_up, moe_b_up, moe_w_down, moe_b_down,
                  ple_w_proj, ple_gate_norm, ple_w_gate, ple_post_norm):
    wi = w_in[i]
    c = np.cumsum((0, 128, 128, 256, 16, 256, 256, 128, 32, 256))
    gq, gk, gv, glow, gr, cq, ckv, krope, upool = [wi[:, c[j]:c[j + 1]] for j in range(9)]
    misc = jnp.concatenate([glow, krope, jnp.zeros((D_MODEL, 128 - 48), F32)], axis=1)
    w_in_p = jnp.concatenate([gq, gk, gv, gr, cq, upool, ckv, misc], axis=1).astype(BF16)
    wgate_p = jnp.zeros((128, GLA_K), F32).at[MISC_GLOW:MISC_GLOW + GLA_GATE_RANK].set(gla_w_gate[i]).astype(BF16)
    ukv = mla_w_ukv[i].reshape(MLA_KV_RANK, MLA_HEADS, MLA_NOPE + MLA_V)
    ukv_k = _pad_heads(ukv[:, :, :MLA_NOPE].reshape(MLA_KV_RANK, MLA_HEADS * MLA_NOPE), MLA_NOPE)
    ukv_v = _pad_heads(ukv[:, :, MLA_NOPE:].reshape(MLA_KV_RANK, MLA_W), MLA_V)
    pw = pool_w[i]
    pool_bd = jnp.zeros((POOL_W, POOL_W), F32)
    for g in range(4):
        pool_bd = pool_bd.at[g * 64:(g + 1) * 64, g * 64:(g + 1) * 64].set(pw[g])
    rw = jnp.pad(router_w[i], ((0, 0), (0, 128 - N_EXPERTS)))
    rw_hi = rw.astype(BF16)
    rw_lo = (rw - rw_hi.astype(F32)).astype(BF16)
    row = lambda a: a.reshape(1, -1)
    pad96 = lambda a: jnp.pad(a, (0, HEAD_PAD - MLA_QK)).reshape(1, HEAD_PAD)
    wuq_p = _pad_heads(mla_w_uq[i], MLA_QK)
    gq_p = pad96(mla_qk_q_norm[i] * (MLA_QK ** -0.5 * LOG2E))
    return {
        "mix_norm": row(mix_norm[i]), "w_in": w_in_p, "gla_w_gate": wgate_p, "gla_b_gate": row(gla_b_gate[i]),
        "gla_out_norm": row(jnp.tile(gla_out_norm[i], GLA_HEADS)),
        "mla_q_norm": row(mla_q_norm[i]),
        "mla_w_uq": jnp.concatenate([wuq_p, _swap_rope_halves(wuq_p)], axis=1).astype(BF16),
        "mla_kv_norm": row(mla_kv_norm[i]), "mla_w_ukv_k": ukv_k.astype(BF16), "mla_w_ukv_v": ukv_v.astype(BF16),
        "mla_gq": jnp.concatenate([gq_p, _swap_rope_halves(gq_p)], axis=0), "mla_gk": pad96(mla_qk_k_norm[i]),
        "pool_w": pool_bd.astype(BF16), "pool_scale": row(pool_scale[i]),
        "w_out": w_out[i].astype(BF16), "ffn_norm": row(ffn_norm[i]),
        "router_w": jnp.concatenate([rw_hi, rw_lo], axis=1),
        "router_b": row(jnp.pad(router_b[i], (0, 128 - N_EXPERTS))),
        "layer": i,
        "moe_w_gate": moe_w_gate, "moe_b_gate": moe_b_gate[i].reshape(N_EXPERTS, 1, D_FF),
        "moe_w_up": moe_w_up, "moe_b_up": moe_b_up[i].reshape(N_EXPERTS, 1, D_FF),
        "moe_w_down": moe_w_down, "moe_b_down": moe_b_down[i].reshape(N_EXPERTS, 1, D_MODEL),
        "ple_w_proj": ple_w_proj[i].astype(BF16), "ple_gate_norm": row(ple_gate_norm[i]),
        "ple_w_gate": ple_w_gate[i].astype(BF16), "ple_post_norm": row(ple_post_norm[i]),
    }


def _rope_tables(positions):
    T = positions.size
    inv = ROPE_BASE ** (-jnp.arange(0, MLA_ROPE, 2, dtype=F32) / MLA_ROPE)
    ang = positions.reshape(T, 1).astype(F32) * inv
    cos, sin = jnp.cos(ang), jnp.sin(ang)
    z16 = jnp.zeros((T, 16), F32)
    tail = jnp.zeros((T, HEAD_PAD - MLA_QK), F32)
    c = jnp.concatenate([jnp.ones((T, MLA_NOPE), F32), cos, cos, tail], axis=1)
    s1 = jnp.concatenate([jnp.zeros((T, MLA_NOPE), F32), -sin, z16, tail], axis=1)
    s2 = jnp.concatenate([jnp.zeros((T, MLA_NOPE), F32), z16, sin, tail], axis=1)
    return c, s1, s2


def _route(top_idx, rank, counts, T):
    bm = MOE_BLOCK
    A = T * TOP_K
    padded = (counts + bm - 1) // bm * bm
    pad_end = jnp.cumsum(padded)
    pad_start = pad_end - padded
    experts = jnp.arange(N_EXPERTS, dtype=jnp.int32)
    dest = rank + jnp.sum(jnp.where(top_idx[:, :, None] == experts, pad_start, 0), axis=-1)
    n_blocks = (A + N_EXPERTS * (bm - 1) + bm - 1) // bm
    n_rows = n_blocks * bm
    block_start = jnp.arange(n_blocks, dtype=jnp.int32) * bm
    block_e = jnp.minimum(jnp.sum((pad_end[None, :] <= block_start[:, None]).astype(jnp.int32), axis=1),
                          N_EXPERTS - 1)
    n_used = (pad_end[-1] // bm).astype(jnp.int32).reshape(1)
    return dest, n_rows, block_e, n_used


def _dispatch(hn_slabs, dest, n_rows):
    T, width = hn_slabs[0].shape
    win = DISPATCH_ROWS
    dest_t = dest.T
    mesh = plsc.VectorSubcoreMesh(core_axis_name="core", subcore_axis_name="subcore")

    @functools.partial(pl.kernel, out_type=jax.ShapeDtypeStruct((n_rows, width), hn_slabs[0].dtype), mesh=mesh,
                       scratch_types=[], name="dispatch")
    def scatter_rows(x_hbm, i_hbm, o_hbm):
        def body(x_vmem, i_vmem):
            for kk in range(TOP_K):
                pltpu.sync_copy(x_vmem, o_hbm.at[i_vmem.at[kk]])

        pltpu.emit_pipeline(
            body, grid=(T // win,),
            in_specs=[pl.BlockSpec((win, width), lambda i: (i, 0)), pl.BlockSpec((TOP_K, win), lambda i: (0, i))],
            out_specs=[], core_axis_name=("core", "subcore"),
            dimension_semantics=(pltpu.PARALLEL,))(x_hbm, i_hbm)

    return [scatter_rows(slab, dest_t) for slab in hn_slabs]


def kernel(x, p, positions, mix_norm, w_in, gla_w_gate, gla_b_gate, gla_out_norm, mla_q_norm, mla_w_uq,
           mla_kv_norm, mla_w_ukv, mla_qk_q_norm, mla_qk_k_norm, pool_w, pool_scale, w_out, ffn_norm,
           router_w, router_b, moe_w_gate, moe_b_gate, moe_w_up, moe_b_up, moe_w_down, moe_b_down,
           ple_w_proj, ple_gate_norm, ple_w_gate, ple_post_norm):
    B, S, D = x.shape
    T = B * S
    depth = p.shape[0]
    params = (mix_norm, w_in, gla_w_gate, gla_b_gate, gla_out_norm, mla_q_norm, mla_w_uq, mla_kv_norm,
              mla_w_ukv, mla_qk_q_norm, mla_qk_k_norm, pool_w, pool_scale, w_out, ffn_norm, router_w,
              router_b, moe_w_gate, moe_b_gate, moe_w_up, moe_b_up, moe_w_down, moe_b_down,
              ple_w_proj, ple_gate_norm, ple_w_gate, ple_post_norm)
    take = lambda a, idx: a.at[idx].get(mode="promise_in_bounds")
    rope_c, rope_s1, rope_s2 = _rope_tables(positions)
    p_flat = p.reshape(depth, T, D_PLE)
    h = x.reshape(T, D)
    for i in range(depth):
        w = _layer_params(i, *params)
        zg, la, q, k, v, y_pool = _mix_pre(h, w, rope_c, rope_s1, rope_s2, S)
        y_gla = _gla(zg, la, w["gla_out_norm"], B, S)
        y_mla = _attn(q, k, v, B, S)
        h1, hn0, hn1, route, gates, counts = _out_router(h, y_gla, y_mla, y_pool, w)
        dest, n_rows, block_e, n_used = _route(route[:, :TOP_K], route[:, TOP_K:2 * TOP_K],
                                               counts[0, :N_EXPERTS], T)
        ys = _moe(_dispatch([hn0, hn1], dest, n_rows), block_e, n_used, w)
        h = h1
        for part in range(COMBINE_PARTS):
            d = dest[part * (T // COMBINE_PARTS):(part + 1) * (T // COMBINE_PARTS)]
            h = _ple(h, [take(ys, d[:, kk]) for kk in range(TOP_K)], gates, p_flat, w, part)
    return h.reshape(B, S, D)
```

```python
import functools

import jax
import jax.numpy as jnp
import numpy as np
from jax import lax
from jax.experimental import pallas as pl
from jax.experimental.pallas import tpu as pltpu
from jax.experimental.pallas import tpu_sc as plsc

F32 = jnp.float32
BF16 = jnp.bfloat16

D_MODEL = 1024
EPS = 1e-6
D_PLE = 256

GLA_HEADS = 4
GLA_DK = 32
GLA_DV = 64
GLA_GATE_RANK = 16
GLA_TAU = 16.0
GLA_CHUNK = 64
GLA_K = GLA_HEADS * GLA_DK
GLA_W = GLA_HEADS * GLA_DV

MLA_HEADS = 8
MLA_Q_RANK = 256
MLA_KV_RANK = 128
MLA_NOPE = 64
MLA_ROPE = 32
MLA_QK = MLA_NOPE + MLA_ROPE
MLA_V = 64
MLA_W = MLA_HEADS * MLA_V
ROPE_BASE = 10000.0
HEAD_PAD = 128
MLA_QK_PAD = MLA_HEADS * HEAD_PAD

POOL_WINDOWS = (2, 4, 8, 16)
POOL_GROUP = 64
POOL_W = 256
POOL_HALO = 16

N_EXPERTS = 32
TOP_K = 4
D_FF = 1024
SWIGLU_LIMIT = 7.0
SWIGLU_ALPHA = 1.702

COL_GQ, COL_GK, COL_GV, COL_GR, COL_CQ, COL_POOL, COL_CKV, COL_MISC = 0, 128, 256, 512, 768, 1024, 1280, 1408
D_IN_PAD = 1536
MISC_GLOW = 0
MISC_ROPE = 16

LOG2E = 1.4426950408889634
TOKEN_TILE = 512
MIX_SUB, ROUTER_SUB, PLE_SUB = 256, 256, 128
GLA_TILE = 512
ATTN_TILE = 1024
ATTN_SUB = 512
MOE_BLOCK = 512
MOE_CAST_ROWS = 256
COMBINE_PARTS = 4
DISPATCH_ROWS = 128
DISPATCH_SLABS = 2
VMEM_LIMIT = 56 * 1024 * 1024
NEG_BIG = -1e30


def _cparams(n_axes, **flags):
    return pltpu.CompilerParams(dimension_semantics=("arbitrary",) * n_axes,
                                vmem_limit_bytes=VMEM_LIMIT, flags=flags or None)


def _rms(x, g):
    return x * lax.rsqrt(jnp.mean(x * x, axis=-1, keepdims=True) + EPS) * g


def _dot(a, b):
    return jnp.dot(a, b, preferred_element_type=F32)


def _dot_nt(a, b):
    return lax.dot_general(a, b, (((1,), (1,)), ((), ())), preferred_element_type=F32)


def _dot_tn(a, b):
    return lax.dot_general(a, b, (((0,), (0,)), ((), ())), preferred_element_type=F32)


def _split3(x):
    hi = x.astype(BF16)
    r = x - hi.astype(F32)
    mid = r.astype(BF16)
    lo = (r - mid.astype(F32)).astype(BF16)
    return hi, mid, lo


def _split2(x):
    hi = x.astype(BF16)
    lo = (x - hi.astype(F32)).astype(BF16)
    return hi, lo


def _pack_bf16_pairs(x):
    m = x.shape[1] // 2
    bits = lax.bitcast_convert_type(x.astype(BF16).astype(F32), jnp.uint32)
    return (bits[:, :m] >> 16) | (bits[:, m:] & jnp.uint32(0xFFFF0000))


def _unpack_bf16_pairs(w):
    lo = lax.bitcast_convert_type(w << 16, F32)
    hi = lax.bitcast_convert_type(w & jnp.uint32(0xFFFF0000), F32)
    return lo, hi


def _skewed(stages, n_rows, sub):
    states = [{"rows": slice(r0, r0 + sub)} for r0 in range(0, n_rows, sub)]
    for step in range(len(states) + len(stages) - 1):
        for s, stage in enumerate(stages):
            t = step - s
            if 0 <= t < len(states):
                stage(states[t])


def _full(shape):
    nd = len(shape)
    return pl.BlockSpec(shape, lambda *_: (0,) * nd)


def _rope(x, c, s1, s2):
    return x * c + pltpu.roll(x, HEAD_PAD - 16, 1) * s1 + pltpu.roll(x, 16, 1) * s2


def _mix_pre_kernel(h_ref, mixn_ref, win_ref, wgate_ref, bgate_ref, qn_ref, wuq_ref, kvn_ref,
                    wukvk_ref, wukvv_ref, gq_ref, gk_ref, rc_ref, rs1_ref, rs2_ref,
                    wpool_ref, pscale_ref,
                    zg_ref, la_ref, q_ref, k_ref, v_ref, yp_ref, carry_ref, *, tiles_per_seq):
    tm = h_ref.shape[0]
    sub = MIX_SUB
    seq_tile = pl.program_id(0) % tiles_per_seq

    @pl.when(seq_tile == 0)
    def _():
        carry_ref[...] = jnp.zeros_like(carry_ref)

    lane = lax.broadcasted_iota(jnp.int32, (sub, HEAD_PAD), 1)
    in_rope = (lane >= MLA_NOPE) & (lane < MLA_QK)
    lane_v = lax.broadcasted_iota(jnp.int32, (sub, MLA_QK_PAD), 1)
    ones_lane = lane_v % HEAD_PAD == MLA_V
    lane_p = lax.broadcasted_iota(jnp.int32, (sub, POOL_W), 1)
    row_p = lax.broadcasted_iota(jnp.int32, (sub, POOL_W), 0)
    g0, g1, g2 = lane_p < 64, lane_p < 128, lane_p < 192
    win = jnp.where(g0, 2.0, jnp.where(g1, 4.0, jnp.where(g2, 8.0, 16.0)))
    gq, gq_sw, gk = gq_ref[0:1, :], gq_ref[1:2, :], gk_ref[...]

    def norm_in(st):
        st["hn"] = _rms(h_ref[st["rows"], :], mixn_ref[...]).astype(BF16)

    def project_in(st):
        st["z"] = _dot(st["hn"], win_ref[...])

    def norm_latents(st):
        z = st["z"]
        zg_ref[st["rows"], :] = z[:, COL_GQ:COL_CQ]
        st["cqn"] = _rms(z[:, COL_CQ:COL_CQ + MLA_Q_RANK], qn_ref[...]).astype(BF16)
        st["ckvn"] = _rms(z[:, COL_CKV:COL_CKV + MLA_KV_RANK], kvn_ref[...]).astype(BF16)

    def project_up(st):
        zm = st["z"][:, COL_MISC:COL_MISC + 128]
        st["logit"] = _dot(zm.astype(BF16), wgate_ref[...]) + bgate_ref[...]
        st["qf"] = _dot(st["cqn"], wuq_ref[...])
        st["kn"] = _dot(st["ckvn"], wukvk_ref[...])
        st["v"] = _dot(st["ckvn"], wukvv_ref[...])

    def heads_and_pool(st):
        rows, z, qf, kn, logit = st["rows"], st["z"], st["qf"], st["kn"], st["logit"]
        zm = z[:, COL_MISC:COL_MISC + 128]
        la_ref[rows, :] = (jnp.minimum(logit, 0.0) - jnp.log(1.0 + jnp.exp(-jnp.abs(logit)))) * (1.0 / GLA_TAU)
        v_ref[rows, :] = jnp.where(ones_lane, 1.0, st["v"]).astype(BF16)

        rc, rs1, rs2 = rc_ref[rows, :], rs1_ref[rows, :], rs2_ref[rows, :]
        kr = jnp.where(in_rope, pltpu.roll(zm, MLA_NOPE - MISC_ROPE, 1), 0.0)
        kr_ss = jnp.sum(kr * kr, axis=-1, keepdims=True)
        krr = _rope(kr * gk, rc, rs1, rs2)
        cq = rc * gq
        sq_tab = (rs1 + rs2) * gq_sw
        for hh in range(MLA_HEADS):
            sl = slice(hh * HEAD_PAD, (hh + 1) * HEAD_PAD)
            qh = qf[:, sl]
            qsw = qf[:, MLA_QK_PAD + hh * HEAD_PAD:MLA_QK_PAD + (hh + 1) * HEAD_PAD]
            sq = lax.rsqrt(jnp.sum(qh * qh, axis=-1, keepdims=True) * (1.0 / MLA_QK) + EPS)
            q_ref[rows, sl] = ((qh * cq + qsw * sq_tab) * sq).astype(BF16)
            kh = kn[:, sl]
            sk = lax.rsqrt((jnp.sum(kh * kh, axis=-1, keepdims=True) + kr_ss) * (1.0 / MLA_QK) + EPS)
            k_ref[rows, sl] = (sk * (kh * gk + krr)).astype(BF16)

        u = z[:, COL_POOL:COL_POOL + POOL_W]
        xe = jnp.concatenate([carry_ref[...], u], axis=0)
        carry_ref[...] = u[sub - POOL_HALO:, :]
        s2 = xe + pltpu.roll(xe, 1, 0)
        s4 = s2 + pltpu.roll(s2, 2, 0)
        s8 = s4 + pltpu.roll(s4, 4, 0)
        s16 = s8 + pltpu.roll(s8, 8, 0)
        pooled = jnp.where(g0, s2[POOL_HALO:], jnp.where(g1, s4[POOL_HALO:],
                           jnp.where(g2, s8[POOL_HALO:], s16[POOL_HALO:])))
        cnt = jnp.minimum((seq_tile * tm + rows.start + row_p + 1).astype(F32), win)
        st["d"] = (pooled / cnt - u).astype(BF16)

    def project_pool(st):
        yp_ref[st["rows"], :] = (_dot(st["d"], wpool_ref[...]) * pscale_ref[...]).astype(BF16)

    _skewed([norm_in, project_in, norm_latents, project_up, heads_and_pool, project_pool], tm, sub)


def _mix_pre(h, w, rope_c, rope_s1, rope_s2, seq_len):
    T = h.shape[0]
    tm = TOKEN_TILE
    row = lambda n: pl.BlockSpec((tm, n), lambda i: (i, 0))
    ins = [h, w["mix_norm"], w["w_in"], w["gla_w_gate"], w["gla_b_gate"], w["mla_q_norm"], w["mla_w_uq"],
           w["mla_kv_norm"], w["mla_w_ukv_k"], w["mla_w_ukv_v"], w["mla_gq"], w["mla_gk"],
           rope_c, rope_s1, rope_s2, w["pool_w"], w["pool_scale"]]
    in_specs = [row(D_MODEL)] + [_full(a.shape) for a in ins[1:12]] + [row(HEAD_PAD)] * 3 + \
               [_full(w["pool_w"].shape), _full(w["pool_scale"].shape)]
    out_shape = [jax.ShapeDtypeStruct((T, COL_CQ), F32), jax.ShapeDtypeStruct((T, GLA_K), F32),
                 jax.ShapeDtypeStruct((T, MLA_QK_PAD), BF16), jax.ShapeDtypeStruct((T, MLA_QK_PAD), BF16),
                 jax.ShapeDtypeStruct((T, MLA_QK_PAD), BF16), jax.ShapeDtypeStruct((T, POOL_W), BF16)]
    out_specs = [row(COL_CQ), row(GLA_K), row(MLA_QK_PAD), row(MLA_QK_PAD), row(MLA_QK_PAD), row(POOL_W)]
    return pl.pallas_call(
        functools.partial(_mix_pre_kernel, tiles_per_seq=seq_len // tm),
        grid=(T // tm,), in_specs=in_specs, out_specs=out_specs, out_shape=out_shape,
        scratch_shapes=[pltpu.VMEM((POOL_HALO, POOL_W), F32)],
        compiler_params=_cparams(1), name="mix_pre")(*ins)


def _gla_kernel(zg_ref, la_ref, gn_ref, y_ref, state_ref, o_ref):
    tg = zg_ref.shape[0]
    C = GLA_CHUNK

    @pl.when(pl.program_id(1) == 0)
    def _():
        state_ref[...] = jnp.zeros_like(state_ref)

    r_i = lax.broadcasted_iota(jnp.int32, (C, C), 0)
    c_i = lax.broadcasted_iota(jnp.int32, (C, C), 1)
    tri = (r_i >= c_i).astype(BF16)
    ones = jnp.ones((C, GLA_W), BF16)
    head_k = lax.broadcasted_iota(jnp.int32, (C, GLA_K), 1) // GLA_DK
    head_v = lax.broadcasted_iota(jnp.int32, (C, GLA_W), 1) // GLA_DV
    ar = lax.broadcasted_iota(jnp.int32, (GLA_HEADS * C, C), 0)
    ac = lax.broadcasted_iota(jnp.int32, (GLA_HEADS * C, C), 1)
    causal = (ar % C) >= ac
    sk = lax.broadcasted_iota(jnp.int32, (GLA_K, GLA_W), 0) // GLA_DK
    sv = lax.broadcasted_iota(jnp.int32, (GLA_K, GLA_W), 1) // GLA_DV
    blockdiag = sk == sv

    def log_decay(st):
        la3 = _split3(la_ref[st["rows"], :])
        st["bc"] = _dot(tri, la3[0]) + _dot(tri, la3[1]) + _dot(tri, la3[2])
        st["dsum"] = _dot_tn(la3[0], ones) + _dot_tn(la3[1], ones) + _dot_tn(la3[2], ones)

    def scores(st):
        rows, bc = st["rows"], st["bc"]
        q = zg_ref[rows, COL_GQ:COL_GQ + GLA_K] * (GLA_DK ** -0.5)
        k = zg_ref[rows, COL_GK:COL_GK + GLA_K]
        b_last = bc[C - 1:C, :]
        q_dec = (q * jnp.exp(bc)).astype(BF16)
        k_dec = (k * jnp.exp(-bc)).astype(BF16)
        st["k_end"] = (k * jnp.exp(b_last - bc)).astype(BF16)
        st["decay"] = jnp.exp(st["dsum"])
        zero = jnp.zeros_like(q_dec)
        qs = jnp.concatenate([jnp.where(head_k == hh, q_dec, zero) for hh in range(GLA_HEADS)], axis=0)
        st["q_dec"] = q_dec
        st["att"] = _dot_nt(qs, k_dec)

    def values(st):
        v = zg_ref[st["rows"], COL_GV:COL_GV + GLA_W].astype(BF16)
        st["o_full"] = _dot(jnp.where(causal, st["att"], 0.0).astype(BF16), v)
        st["upd"] = jnp.where(blockdiag, _dot_tn(st["k_end"], v), 0.0)

    state = [state_ref[...]]

    def recur(st):
        o_full = st["o_full"]
        o = _dot(st["q_dec"], state[0].astype(BF16))
        for hh in range(GLA_HEADS):
            o = o + jnp.where(head_v == hh, o_full[hh * C:(hh + 1) * C, :], 0.0)
        o_ref[st["rows"], :] = o
        state[0] = st["decay"] * state[0] + st["upd"]

    _skewed([log_decay, scores, values, recur], tg, C)
    state_ref[...] = state[0]

    o = o_ref[...]
    gr = lax.broadcasted_iota(jnp.int32, (GLA_W, GLA_W), 0) // GLA_DV
    gc = lax.broadcasted_iota(jnp.int32, (GLA_W, GLA_W), 1) // GLA_DV
    group = (gr == gc).astype(BF16)
    oo = _split2(o * o)
    ms = (_dot(oo[0], group) + _dot(oo[1], group)) * (1.0 / GLA_DV)
    r = zg_ref[:, COL_GR:COL_GR + GLA_W]
    y = o * lax.rsqrt(ms + EPS) * gn_ref[...] * (r / (1.0 + jnp.exp(-r)))
    y_ref[...] = y.astype(BF16)


def _gla(zg, la, gn, batch, seq_len):
    T = zg.shape[0]
    tg = GLA_TILE
    nt = seq_len // tg
    return pl.pallas_call(
        _gla_kernel, grid=(batch, nt),
        in_specs=[pl.BlockSpec((tg, COL_CQ), lambda b, s: (b * nt + s, 0)),
                  pl.BlockSpec((tg, GLA_K), lambda b, s: (b * nt + s, 0)),
                  _full(gn.shape)],
        out_specs=pl.BlockSpec((tg, GLA_W), lambda b, s: (b * nt + s, 0)),
        out_shape=jax.ShapeDtypeStruct((T, GLA_W), BF16),
        scratch_shapes=[pltpu.VMEM((GLA_K, GLA_W), F32), pltpu.VMEM((tg, GLA_W), F32)],
        compiler_params=_cparams(2), name="gla")(zg, la, gn)


def _attn_kernel(q_ref, k_ref, v_ref, o_ref, m_ref, acc_ref):
    tq = q_ref.shape[0]
    ts = ATTN_SUB
    i = pl.program_id(2)
    m_ref[...] = jnp.full_like(m_ref, NEG_BIG)
    acc_ref[...] = jnp.zeros_like(acc_ref)

    def sub_block(hh, start, r0, mask_off):
        hs = slice(hh * HEAD_PAD, (hh + 1) * HEAD_PAD)
        kj = k_ref[pl.ds(start, ts), hs]
        vj = v_ref[pl.ds(start, ts), hs]
        s = _dot_nt(q_ref[r0:, hs], kj)
        if mask_off is not None:
            row = lax.broadcasted_iota(jnp.int32, s.shape, 0) + r0
            col = lax.broadcasted_iota(jnp.int32, s.shape, 1) + mask_off
            s = jnp.where(col <= row, s, NEG_BIG)
        m_old = m_ref[hh, r0:, :]
        parts = [s[:, c * 128:(c + 1) * 128] for c in range(ts // 128)]
        m_new = jnp.maximum(m_old, jnp.max(functools.reduce(jnp.maximum, parts), axis=-1, keepdims=True))
        p = jnp.concatenate([jnp.exp2((x - m_new).astype(BF16)) for x in parts], axis=1)
        acc_ref[hh, r0:, :] = jnp.exp2(m_old - m_new) * acc_ref[hh, r0:, :] + _dot(p, vj)
        m_ref[hh, r0:, :] = m_new

    def body(j, carry):
        base = pl.multiple_of(j * tq, tq)
        for sb in range(tq // ts):
            for hh in range(2):
                sub_block(hh, base + sb * ts, 0, None)
        return carry

    lax.fori_loop(0, i, body, 0)
    base = pl.multiple_of(i * tq, tq)
    for sb in range(tq // ts):
        for hh in range(2):
            sub_block(hh, base + sb * ts, sb * ts, sb * ts)
    outs = []
    for hh in range(2):
        a = acc_ref[hh]
        outs.append(a / a[:, MLA_V:MLA_V + 1])
    lane = lax.broadcasted_iota(jnp.int32, (tq, HEAD_PAD), 1)
    o_ref[...] = jnp.where(lane < MLA_V, outs[0], pltpu.roll(outs[1], MLA_V, 1)).astype(BF16)


def _attn(q, k, v, batch, seq_len):
    T = q.shape[0]
    tq = ATTN_TILE
    nq = seq_len // tq
    pairs = MLA_HEADS // 2
    return pl.pallas_call(
        _attn_kernel, grid=(batch, pairs, nq),
        in_specs=[pl.BlockSpec((tq, 2 * HEAD_PAD), lambda b, p, i: (b * nq + i, p)),
                  pl.BlockSpec((seq_len, 2 * HEAD_PAD), lambda b, p, i: (b, p)),
                  pl.BlockSpec((seq_len, 2 * HEAD_PAD), lambda b, p, i: (b, p))],
        out_specs=pl.BlockSpec((tq, 2 * MLA_V), lambda b, p, i: (b * nq + i, p)),
        out_shape=jax.ShapeDtypeStruct((T, MLA_W), BF16),
        scratch_shapes=[pltpu.VMEM((2, tq, HEAD_PAD), F32), pltpu.VMEM((2, tq, HEAD_PAD), F32)],
        compiler_params=_cparams(3), name="attn")(q, k, v)


def _out_router_kernel(h_ref, yg_ref, ym_ref, yp_ref, wo_ref, fn_ref, rw_ref, rb_ref,
                       h1_ref, hn0_ref, hn1_ref, idx_ref, gate_ref, cnt_ref, carry_ref):
    tm = h_ref.shape[0]

    @pl.when(pl.program_id(0) == 0)
    def _():
        carry_ref[...] = jnp.zeros_like(carry_ref)

    sub = ROUTER_SUB
    lane = lax.broadcasted_iota(jnp.int32, (sub, 128), 1)
    r_i = lax.broadcasted_iota(jnp.int32, (sub, sub), 0)
    c_i = lax.broadcasted_iota(jnp.int32, (sub, sub), 1)
    tri = (r_i >= c_i).astype(BF16)
    def project(st):
        rows = st["rows"]
        st["h1"] = (h_ref[rows, :] + _dot(yg_ref[rows, :], wo_ref[0:GLA_W, :])
                    + _dot(ym_ref[rows, :], wo_ref[GLA_W:GLA_W + MLA_W, :])
                    + _dot(yp_ref[rows, :], wo_ref[GLA_W + MLA_W:, :]))

    def normalize(st):
        rows = st["rows"]
        h1_ref[rows, :] = st["h1"]
        hn = _rms(st["h1"], fn_ref[...])
        st["hi"], st["lo"] = _split2(hn)
        packed = _pack_bf16_pairs(hn)
        slab = packed.shape[1] // DISPATCH_SLABS
        hn0_ref[rows, :] = packed[:, :slab]
        hn1_ref[rows, :] = packed[:, slab:]

    def score(st):
        r2 = _dot(st["hi"], rw_ref[...])
        st["logits"] = r2[:, :128] + r2[:, 128:] + _dot(st["lo"], rw_ref[:, 0:128]) + rb_ref[...]

    def select(st):
        rows = st["rows"]
        cur = jnp.where(lane < N_EXPERTS, st["logits"], NEG_BIG)
        idx_out = jnp.zeros((sub, 128), jnp.int32)
        val_out = jnp.zeros((sub, 128), F32)
        chosen = jnp.zeros((sub, 128), F32)
        top0 = None
        sels = []
        for kk in range(TOP_K):
            m = jnp.max(cur, axis=-1, keepdims=True)
            sel = jnp.min(jnp.where(cur == m, lane, 128), axis=-1, keepdims=True)
            if kk == 0:
                top0 = m
            sels.append(sel)
            idx_out = jnp.where(lane == kk, sel, idx_out)
            val_out = jnp.where(lane == kk, jnp.exp(m - top0), val_out)
            chosen = jnp.where(lane == sel, 1.0, chosen)
            cur = jnp.where(lane == sel, NEG_BIG, cur)
        gate_ref[rows, :] = val_out / jnp.sum(val_out, axis=-1, keepdims=True)

        incl = _dot(tri, chosen.astype(BF16))
        before = carry_ref[0:1, :] + incl - chosen
        for kk in range(TOP_K):
            rank = jnp.sum(jnp.where(lane == sels[kk], before, 0.0), axis=-1, keepdims=True)
            idx_out = jnp.where(lane == TOP_K + kk, rank.astype(jnp.int32), idx_out)
        idx_ref[rows, :] = idx_out
        carry_ref[...] = carry_ref[...] + incl[sub - 1:sub, :]

    _skewed([project, normalize, score, select], tm, sub)
    cnt_ref[...] = carry_ref[...].astype(jnp.int32)


def _out_router(h, yg, ym, yp, w):
    T = h.shape[0]
    tm = TOKEN_TILE
    row = lambda n: pl.BlockSpec((tm, n), lambda i: (i, 0))
    slab = D_MODEL // 2 // DISPATCH_SLABS
    ins = [h, yg, ym, yp, w["w_out"], w["ffn_norm"], w["router_w"], w["router_b"]]
    return pl.pallas_call(
        _out_router_kernel, grid=(T // tm,),
        in_specs=[row(D_MODEL), row(GLA_W), row(MLA_W), row(POOL_W)] + [_full(a.shape) for a in ins[4:]],
        out_specs=[row(D_MODEL), row(slab), row(slab), row(128), row(128), _full((8, 128))],
        out_shape=[jax.ShapeDtypeStruct((T, D_MODEL), F32), jax.ShapeDtypeStruct((T, slab), jnp.uint32),
                   jax.ShapeDtypeStruct((T, slab), jnp.uint32),
                   jax.ShapeDtypeStruct((T, 128), jnp.int32), jax.ShapeDtypeStruct((T, 128), F32),
                   jax.ShapeDtypeStruct((8, 128), jnp.int32)],
        scratch_shapes=[pltpu.VMEM((8, 128), F32)],
        compiler_params=_cparams(1), name="out_router")(*ins)


def _moe_kernel(be_ref, nb_ref, x0_ref, x1_ref, wg_ref, bg_ref, wu_ref, bu_ref, wd_ref, bd_ref,
                y_ref, wg_bf, wu_bf, wd_bf):
    i = pl.program_id(0)
    used = i < nb_ref[0]
    new_expert = (i == 0) | (be_ref[i] != be_ref[jnp.maximum(i - 1, 0)])

    @pl.when(used & new_expert)
    def _():
        for src, dst in ((wg_ref, wg_bf), (wu_ref, wu_bf), (wd_ref, wd_bf)):
            for r in range(0, src.shape[2], MOE_CAST_ROWS):
                dst[r:r + MOE_CAST_ROWS, :] = src[0, 0, r:r + MOE_CAST_ROWS, :].astype(BF16)

    @pl.when(used)
    def _():
        halves = [_unpack_bf16_pairs(r[...]) for r in (x0_ref, x1_ref)]
        x = jnp.concatenate([h[0] for h in halves] + [h[1] for h in halves], axis=1).astype(BF16)
        g = jnp.minimum(_dot(x, wg_bf[...]) + bg_ref[0], SWIGLU_LIMIT)
        up = jnp.clip(_dot(x, wu_bf[...]) + bu_ref[0], -SWIGLU_LIMIT, SWIGLU_LIMIT)
        hb = (up + 1.0) * (g / (1.0 + jnp.exp(-SWIGLU_ALPHA * g)))
        y_ref[...] = _pack_bf16_pairs(_dot(hb.astype(BF16), wd_bf[...]) + bd_ref[0])

    @pl.when(jnp.logical_not(used))
    def _():
        y_ref[...] = jnp.zeros_like(y_ref)


def _moe(xs, block_e, n_used, w):
    n_rows = xs[0].shape[0]
    bm = MOE_BLOCK
    layer = w["layer"]
    wspec = lambda shp: pl.BlockSpec((1, 1) + shp, lambda i, be, nb: (layer, be[i], 0, 0))
    bspec = lambda shp: pl.BlockSpec((1,) + shp, lambda i, be, nb: (be[i], 0, 0))
    grid_spec = pltpu.PrefetchScalarGridSpec(
        num_scalar_prefetch=2, grid=(n_rows // bm,),
        in_specs=[pl.BlockSpec((bm, D_MODEL // 2 // DISPATCH_SLABS), lambda i, be, nb: (i, 0))] * DISPATCH_SLABS + [
                  wspec((D_MODEL, D_FF)), bspec((1, D_FF)), wspec((D_MODEL, D_FF)), bspec((1, D_FF)),
                  wspec((D_FF, D_MODEL)), bspec((1, D_MODEL))],
        out_specs=pl.BlockSpec((bm, D_MODEL // 2), lambda i, be, nb: (i, 0)),
        scratch_shapes=[pltpu.VMEM((D_MODEL, D_FF), BF16), pltpu.VMEM((D_MODEL, D_FF), BF16),
                        pltpu.VMEM((D_FF, D_MODEL), BF16)])
    return pl.pallas_call(
        _moe_kernel, grid_spec=grid_spec,
        out_shape=jax.ShapeDtypeStruct((n_rows, D_MODEL // 2), jnp.uint32),
        compiler_params=_cparams(1), name="moe")(
            block_e, n_used, *xs, w["moe_w_gate"], w["moe_b_gate"], w["moe_w_up"], w["moe_b_up"],
            w["moe_w_down"], w["moe_b_down"])


def _ple_kernel(h1_ref, y0_ref, y1_ref, y2_ref, y3_ref, gate_ref, p_ref, wple_ref, gn_ref, wpg_ref, pn_ref, o_ref):
    def combine(st):
        rows = st["rows"]
        gates = gate_ref[rows, :]
        h2 = h1_ref[rows, :]
        for kk, y_ref in enumerate((y0_ref, y1_ref, y2_ref, y3_ref)):
            h2 = h2 + gates[:, kk:kk + 1] * jnp.concatenate(_unpack_bf16_pairs(y_ref[rows, :]), axis=1)
        st["h2"] = h2
        st["hn"] = _rms(h2, gn_ref[...]).astype(BF16)

    def project(st):
        st["e"] = _dot(p_ref[0, st["rows"], :].astype(BF16), wple_ref[...])
        st["a"] = _dot(st["hn"], wpg_ref[...])

    def finish(st):
        gate = 1.0 / (1.0 + jnp.exp(-st["a"]))
        o_ref[st["rows"], :] = st["h2"] + _rms(st["e"] * gate, pn_ref[...])

    _skewed([combine, project, finish], h1_ref.shape[0], PLE_SUB)


def _ple(h1, ys_k, gates, p, w, part):
    T = h1.shape[0]
    tm = TOKEN_TILE
    steps = T // COMBINE_PARTS // tm
    off = part * steps
    row = lambda n: pl.BlockSpec((tm, n), lambda i: (i + off, 0))
    local = pl.BlockSpec((tm, D_MODEL // 2), lambda i: (i, 0))
    weights = [w["ple_w_proj"], w["ple_gate_norm"], w["ple_w_gate"], w["ple_post_norm"]]
    layer = w["layer"]
    p_spec = pl.BlockSpec((1, tm, D_PLE), lambda i: (layer, i + off, 0))
    ins = [h1, *ys_k, gates, p, *weights]
    in_specs = [row(D_MODEL)] + [local] * TOP_K + [row(128), p_spec] + [_full(a.shape) for a in weights]
    return pl.pallas_call(
        _ple_kernel, grid=(steps,), in_specs=in_specs,
        out_specs=row(D_MODEL), out_shape=jax.ShapeDtypeStruct((T, D_MODEL), F32),
        input_output_aliases={0: 0}, compiler_params=_cparams(1), name="ple")(*ins)


def _pad_heads(wm, per_head, n_heads=MLA_HEADS):
    kdim = wm.shape[0]
    w3 = wm.reshape(kdim, n_heads, per_head)
    return jnp.pad(w3, ((0, 0), (0, 0), (0, HEAD_PAD - per_head))).reshape(kdim, n_heads * HEAD_PAD)


def _swap_rope_halves(a):
    a3 = a.reshape(a.shape[0], -1, HEAD_PAD)
    half = MLA_ROPE // 2
    x1 = a3[:, :, MLA_NOPE:MLA_NOPE + half]
    x2 = a3[:, :, MLA_NOPE + half:MLA_QK]
    out = jnp.zeros_like(a3).at[:, :, MLA_NOPE:MLA_NOPE + half].set(x2).at[:, :, MLA_NOPE + half:MLA_QK].set(x1)
    return out.reshape(a.shape)


def _layer_params(i, mix_norm, w_in, gla_w_gate, gla_b_gate, gla_out_norm, mla_q_norm, mla_w_uq, mla_kv_norm,
                  mla_w_ukv, mla_qk_q_norm, mla_qk_k_norm, pool_w, pool_scale, w_out, ffn_norm, router_w,
                  router_b, moe_w_gate, moe_b_gate, moe_w_up, moe_b_up, moe_w_down, moe_b_down,
                  ple_w_proj, ple_gate_norm, ple_w_gate, ple_post_norm):
    wi = w_in[i]
    c = np.cumsum((0, 128, 128, 256, 16, 256, 256, 128, 32, 256))
    gq, gk, gv, glow, gr, cq, ckv, krope, upool = [wi[:, c[j]:c[j + 1]] for j in range(9)]
    misc = jnp.concatenate([glow, krope, jnp.zeros((D_MODEL, 128 - 48), F32)], axis=1)
    w_in_p = jnp.concatenate([gq, gk, gv, gr, cq, upool, ckv, misc], axis=1).astype(BF16)
    wgate_p = jnp.zeros((128, GLA_K), F32).at[MISC_GLOW:MISC_GLOW + GLA_GATE_RANK].set(gla_w_gate[i]).astype(BF16)
    ukv = mla_w_ukv[i].reshape(MLA_KV_RANK, MLA_HEADS, MLA_NOPE + MLA_V)
    ukv_k = _pad_heads(ukv[:, :, :MLA_NOPE].reshape(MLA_KV_RANK, MLA_HEADS * MLA_NOPE), MLA_NOPE)
    ukv_v = _pad_heads(ukv[:, :, MLA_NOPE:].reshape(MLA_KV_RANK, MLA_W), MLA_V)
    pw = pool_w[i]
    pool_bd = jnp.zeros((POOL_W, POOL_W), F32)
    for g in range(4):
        pool_bd = pool_bd.at[g * 64:(g + 1) * 64, g * 64:(g + 1) * 64].set(pw[g])
    rw = jnp.pad(router_w[i], ((0, 0), (0, 128 - N_EXPERTS)))
    rw_hi = rw.astype(BF16)
    rw_lo = (rw - rw_hi.astype(F32)).astype(BF16)
    row = lambda a: a.reshape(1, -1)
    pad96 = lambda a: jnp.pad(a, (0, HEAD_PAD - MLA_QK)).reshape(1, HEAD_PAD)
    wuq_p = _pad_heads(mla_w_uq[i], MLA_QK)
    gq_p = pad96(mla_qk_q_norm[i] * (MLA_QK ** -0.5 * LOG2E))
    return {
        "mix_norm": row(mix_norm[i]), "w_in": w_in_p, "gla_w_gate": wgate_p, "gla_b_gate": row(gla_b_gate[i]),
        "gla_out_norm": row(jnp.tile(gla_out_norm[i], GLA_HEADS)),
        "mla_q_norm": row(mla_q_norm[i]),
        "mla_w_uq": jnp.concatenate([wuq_p, _swap_rope_halves(wuq_p)], axis=1).astype(BF16),
        "mla_kv_norm": row(mla_kv_norm[i]), "mla_w_ukv_k": ukv_k.astype(BF16), "mla_w_ukv_v": ukv_v.astype(BF16),
        "mla_gq": jnp.concatenate([gq_p, _swap_rope_halves(gq_p)], axis=0), "mla_gk": pad96(mla_qk_k_norm[i]),
        "pool_w": pool_bd.astype(BF16), "pool_scale": row(pool_scale[i]),
        "w_out": w_out[i].astype(BF16), "ffn_norm": row(ffn_norm[i]),
        "router_w": jnp.concatenate([rw_hi, rw_lo], axis=1),
        "router_b": row(jnp.pad(router_b[i], (0, 128 - N_EXPERTS))),
        "layer": i,
        "moe_w_gate": moe_w_gate, "moe_b_gate": moe_b_gate[i].reshape(N_EXPERTS, 1, D_FF),
        "moe_w_up": moe_w_up, "moe_b_up": moe_b_up[i].reshape(N_EXPERTS, 1, D_FF),
        "moe_w_down": moe_w_down, "moe_b_down": moe_b_down[i].reshape(N_EXPERTS, 1, D_MODEL),
        "ple_w_proj": ple_w_proj[i].astype(BF16), "ple_gate_norm": row(ple_gate_norm[i]),
        "ple_w_gate": ple_w_gate[i].astype(BF16), "ple_post_norm": row(ple_post_norm[i]),
    }


def _rope_tables(positions):
    T = positions.size
    inv = ROPE_BASE ** (-jnp.arange(0, MLA_ROPE, 2, dtype=F32) / MLA_ROPE)
    ang = (positions.reshape(T // 8, 8, 1).astype(F32) * inv).reshape(T // 8, 8 * (MLA_ROPE // 2))
    cos, sin = jnp.cos(ang).reshape(T, MLA_ROPE // 2), jnp.sin(ang).reshape(T, MLA_ROPE // 2)
    z16 = jnp.zeros((T, 16), F32)
    tail = jnp.zeros((T, HEAD_PAD - MLA_QK), F32)
    c = jnp.concatenate([jnp.ones((T, MLA_NOPE), F32), cos, cos, tail], axis=1)
    s1 = jnp.concatenate([jnp.zeros((T, MLA_NOPE), F32), -sin, z16, tail], axis=1)
    s2 = jnp.concatenate([jnp.zeros((T, MLA_NOPE), F32), z16, sin, tail], axis=1)
    return c, s1, s2


def _route(top_idx, rank, counts, T):
    bm = MOE_BLOCK
    A = T * TOP_K
    padded = (counts + bm - 1) // bm * bm
    pad_end = jnp.cumsum(padded)
    pad_start = pad_end - padded
    experts = jnp.arange(N_EXPERTS, dtype=jnp.int32)
    dest = rank + jnp.sum(jnp.where(top_idx[:, :, None] == experts, pad_start, 0), axis=-1)
    n_blocks = (A + N_EXPERTS * (bm - 1) + bm - 1) // bm
    n_rows = n_blocks * bm
    block_start = jnp.arange(n_blocks, dtype=jnp.int32) * bm
    block_e = jnp.minimum(jnp.sum((pad_end[None, :] <= block_start[:, None]).astype(jnp.int32), axis=1),
                          N_EXPERTS - 1)
    n_used = (pad_end[-1] // bm).astype(jnp.int32).reshape(1)
    return dest, n_rows, block_e, n_used


def _dispatch(hn_slabs, dest, n_rows):
    T, width = hn_slabs[0].shape
    win = DISPATCH_ROWS
    dest_t = dest.T
    mesh = plsc.VectorSubcoreMesh(core_axis_name="core", subcore_axis_name="subcore")

    @functools.partial(pl.kernel, out_type=jax.ShapeDtypeStruct((n_rows, width), hn_slabs[0].dtype), mesh=mesh,
                       scratch_types=[], name="dispatch")
    def scatter_rows(x_hbm, i_hbm, o_hbm):
        def body(x_vmem, i_vmem):
            for kk in range(TOP_K):
                pltpu.sync_copy(x_vmem, o_hbm.at[i_vmem.at[kk]])

        pltpu.emit_pipeline(
            body, grid=(T // win,),
            in_specs=[pl.BlockSpec((win, width), lambda i: (i, 0)), pl.BlockSpec((TOP_K, win), lambda i: (0, i))],
            out_specs=[], core_axis_name=("core", "subcore"),
            dimension_semantics=(pltpu.PARALLEL,))(x_hbm, i_hbm)

    return [scatter_rows(slab, dest_t) for slab in hn_slabs]


def kernel(x, p, positions, mix_norm, w_in, gla_w_gate, gla_b_gate, gla_out_norm, mla_q_norm, mla_w_uq,
           mla_kv_norm, mla_w_ukv, mla_qk_q_norm, mla_qk_k_norm, pool_w, pool_scale, w_out, ffn_norm,
           router_w, router_b, moe_w_gate, moe_b_gate, moe_w_up, moe_b_up, moe_w_down, moe_b_down,
           ple_w_proj, ple_gate_norm, ple_w_gate, ple_post_norm):
    B, S, D = x.shape
    T = B * S
    depth = p.shape[0]
    params = (mix_norm, w_in, gla_w_gate, gla_b_gate, gla_out_norm, mla_q_norm, mla_w_uq, mla_kv_norm,
              mla_w_ukv, mla_qk_q_norm, mla_qk_k_norm, pool_w, pool_scale, w_out, ffn_norm, router_w,
              router_b, moe_w_gate, moe_b_gate, moe_w_up, moe_b_up, moe_w_down, moe_b_down,
              ple_w_proj, ple_gate_norm, ple_w_gate, ple_post_norm)
    take = lambda a, idx: a.at[idx].get(mode="promise_in_bounds")
    rope_c, rope_s1, rope_s2 = _rope_tables(positions)
    p_flat = p.reshape(depth, T, D_PLE)
    h = x.reshape(T, D)
    for i in range(depth):
        w = _layer_params(i, *params)
        zg, la, q, k, v, y_pool = _mix_pre(h, w, rope_c, rope_s1, rope_s2, S)
        y_gla = _gla(zg, la, w["gla_out_norm"], B, S)
        y_mla = _attn(q, k, v, B, S)
        h1, hn0, hn1, route, gates, counts = _out_router(h, y_gla, y_mla, y_pool, w)
        dest, n_rows, block_e, n_used = _route(route[:, :TOP_K], route[:, TOP_K:2 * TOP_K],
                                               counts[0, :N_EXPERTS], T)
        ys = _moe(_dispatch([hn0, hn1], dest, n_rows), block_e, n_used, w)
        h = h1
        for part in range(COMBINE_PARTS):
            d = dest[part * (T // COMBINE_PARTS):(part + 1) * (T // COMBINE_PARTS)]
            h = _ple(h, [take(ys, d[:, kk]) for kk in range(TOP_K)], gates, p_flat, w, part)
    return h.reshape(B, S, D)
```

```python
import functools

import jax
import jax.numpy as jnp
import numpy as np
from jax import lax
from jax.experimental import pallas as pl
from jax.experimental.pallas import tpu as pltpu
from jax.experimental.pallas import tpu_sc as plsc

F32 = jnp.float32
BF16 = jnp.bfloat16

D_MODEL = 1024
EPS = 1e-6
D_PLE = 256

GLA_HEADS = 4
GLA_DK = 32
GLA_DV = 64
GLA_GATE_RANK = 16
GLA_TAU = 16.0
GLA_CHUNK = 64
GLA_K = GLA_HEADS * GLA_DK
GLA_W = GLA_HEADS * GLA_DV

MLA_HEADS = 8
MLA_Q_RANK = 256
MLA_KV_RANK = 128
MLA_NOPE = 64
MLA_ROPE = 32
MLA_QK = MLA_NOPE + MLA_ROPE
MLA_V = 64
MLA_W = MLA_HEADS * MLA_V
ROPE_BASE = 10000.0
HEAD_PAD = 128
MLA_QK_PAD = MLA_HEADS * HEAD_PAD

POOL_WINDOWS = (2, 4, 8, 16)
POOL_GROUP = 64
POOL_W = 256
POOL_HALO = 16

N_EXPERTS = 32
TOP_K = 4
D_FF = 1024
SWIGLU_LIMIT = 7.0
SWIGLU_ALPHA = 1.702

COL_GQ, COL_GK, COL_GV, COL_GR, COL_CQ, COL_POOL, COL_CKV, COL_MISC = 0, 128, 256, 512, 768, 1024, 1280, 1408
D_IN_PAD = 1536
MISC_GLOW = 0
MISC_ROPE = 16

LOG2E = 1.4426950408889634
TOKEN_TILE = 512
MIX_SUB, ROUTER_SUB, PLE_SUB = 256, 256, 128
GLA_TILE = 512
ATTN_TILE = 1024
ATTN_SUB = 512
MOE_BLOCK = 512
MOE_CAST_ROWS = 256
COMBINE_PARTS = 8
DISPATCH_ROWS = 128
DISPATCH_SLABS = 2
VMEM_LIMIT = 56 * 1024 * 1024
NEG_BIG = -1e30


def _cparams(n_axes, **flags):
    return pltpu.CompilerParams(dimension_semantics=("arbitrary",) * n_axes,
                                vmem_limit_bytes=VMEM_LIMIT, flags=flags or None)


def _rms(x, g):
    return x * lax.rsqrt(jnp.mean(x * x, axis=-1, keepdims=True) + EPS) * g


def _dot(a, b):
    return jnp.dot(a, b, preferred_element_type=F32)


def _dot_nt(a, b):
    return lax.dot_general(a, b, (((1,), (1,)), ((), ())), preferred_element_type=F32)


def _dot_tn(a, b):
    return lax.dot_general(a, b, (((0,), (0,)), ((), ())), preferred_element_type=F32)


def _split3(x):
    hi = x.astype(BF16)
    r = x - hi.astype(F32)
    mid = r.astype(BF16)
    lo = (r - mid.astype(F32)).astype(BF16)
    return hi, mid, lo


def _split2(x):
    hi = x.astype(BF16)
    lo = (x - hi.astype(F32)).astype(BF16)
    return hi, lo


def _pack_bf16_pairs(x):
    m = x.shape[1] // 2
    bits = lax.bitcast_convert_type(x.astype(BF16).astype(F32), jnp.uint32)
    return (bits[:, :m] >> 16) | (bits[:, m:] & jnp.uint32(0xFFFF0000))


def _unpack_bf16_pairs(w):
    lo = lax.bitcast_convert_type(w << 16, F32)
    hi = lax.bitcast_convert_type(w & jnp.uint32(0xFFFF0000), F32)
    return lo, hi


def _skewed(stages, n_rows, sub):
    states = [{"rows": slice(r0, r0 + sub)} for r0 in range(0, n_rows, sub)]
    for step in range(len(states) + len(stages) - 1):
        for s, stage in enumerate(stages):
            t = step - s
            if 0 <= t < len(states):
                stage(states[t])


def _full(shape):
    nd = len(shape)
    return pl.BlockSpec(shape, lambda *_: (0,) * nd)


def _rope(x, c, s1, s2):
    return x * c + pltpu.roll(x, HEAD_PAD - 16, 1) * s1 + pltpu.roll(x, 16, 1) * s2


def _mix_pre_kernel(h_ref, mixn_ref, win_ref, wgate_ref, bgate_ref, qn_ref, wuq_ref, kvn_ref,
                    wukvk_ref, wukvv_ref, gq_ref, gk_ref, rc_ref, rs1_ref, rs2_ref,
                    wpool_ref, pscale_ref,
                    zg_ref, la_ref, q_ref, k_ref, v_ref, yp_ref, carry_ref, *, tiles_per_seq):
    tm = h_ref.shape[0]
    sub = MIX_SUB
    seq_tile = pl.program_id(0) % tiles_per_seq

    @pl.when(seq_tile == 0)
    def _():
        carry_ref[...] = jnp.zeros_like(carry_ref)

    lane = lax.broadcasted_iota(jnp.int32, (sub, HEAD_PAD), 1)
    in_rope = (lane >= MLA_NOPE) & (lane < MLA_QK)
    lane_v = lax.broadcasted_iota(jnp.int32, (sub, MLA_QK_PAD), 1)
    ones_lane = lane_v % HEAD_PAD == MLA_V
    lane_p = lax.broadcasted_iota(jnp.int32, (sub, POOL_W), 1)
    row_p = lax.broadcasted_iota(jnp.int32, (sub, POOL_W), 0)
    g0, g1, g2 = lane_p < 64, lane_p < 128, lane_p < 192
    win = jnp.where(g0, 2.0, jnp.where(g1, 4.0, jnp.where(g2, 8.0, 16.0)))
    gq, gq_sw, gk = gq_ref[0:1, :], gq_ref[1:2, :], gk_ref[...]

    def norm_in(st):
        st["hn"] = _rms(h_ref[st["rows"], :], mixn_ref[...]).astype(BF16)

    def project_in(st):
        st["z"] = _dot(st["hn"], win_ref[...])

    def norm_latents(st):
        z = st["z"]
        zg_ref[st["rows"], :] = z[:, COL_GQ:COL_CQ]
        st["cqn"] = _rms(z[:, COL_CQ:COL_CQ + MLA_Q_RANK], qn_ref[...]).astype(BF16)
        st["ckvn"] = _rms(z[:, COL_CKV:COL_CKV + MLA_KV_RANK], kvn_ref[...]).astype(BF16)

    def project_up(st):
        zm = st["z"][:, COL_MISC:COL_MISC + 128]
        st["logit"] = _dot(zm.astype(BF16), wgate_ref[...]) + bgate_ref[...]
        st["qf"] = _dot(st["cqn"], wuq_ref[...])
        st["kn"] = _dot(st["ckvn"], wukvk_ref[...])
        st["v"] = _dot(st["ckvn"], wukvv_ref[...])

    def heads_and_pool(st):
        rows, z, qf, kn, logit = st["rows"], st["z"], st["qf"], st["kn"], st["logit"]
        zm = z[:, COL_MISC:COL_MISC + 128]
        la_ref[rows, :] = (jnp.minimum(logit, 0.0) - jnp.log(1.0 + jnp.exp(-jnp.abs(logit)))) * (1.0 / GLA_TAU)
        v_ref[rows, :] = jnp.where(ones_lane, 1.0, st["v"]).astype(BF16)

        rc, rs1, rs2 = rc_ref[rows, :], rs1_ref[rows, :], rs2_ref[rows, :]
        kr = jnp.where(in_rope, pltpu.roll(zm, MLA_NOPE - MISC_ROPE, 1), 0.0)
        kr_ss = jnp.sum(kr * kr, axis=-1, keepdims=True)
        krr = _rope(kr * gk, rc, rs1, rs2)
        cq = rc * gq
        sq_tab = (rs1 + rs2) * gq_sw
        for hh in range(MLA_HEADS):
            sl = slice(hh * HEAD_PAD, (hh + 1) * HEAD_PAD)
            qh = qf[:, sl]
            qsw = qf[:, MLA_QK_PAD + hh * HEAD_PAD:MLA_QK_PAD + (hh + 1) * HEAD_PAD]
            sq = lax.rsqrt(jnp.sum(qh * qh, axis=-1, keepdims=True) * (1.0 / MLA_QK) + EPS)
            q_ref[rows, sl] = ((qh * cq + qsw * sq_tab) * sq).astype(BF16)
            kh = kn[:, sl]
            sk = lax.rsqrt((jnp.sum(kh * kh, axis=-1, keepdims=True) + kr_ss) * (1.0 / MLA_QK) + EPS)
            k_ref[rows, sl] = (sk * (kh * gk + krr)).astype(BF16)

        u = z[:, COL_POOL:COL_POOL + POOL_W]
        xe = jnp.concatenate([carry_ref[...], u], axis=0)
        carry_ref[...] = u[sub - POOL_HALO:, :]
        s2 = xe + pltpu.roll(xe, 1, 0)
        s4 = s2 + pltpu.roll(s2, 2, 0)
        s8 = s4 + pltpu.roll(s4, 4, 0)
        s16 = s8 + pltpu.roll(s8, 8, 0)
        pooled = jnp.where(g0, s2[POOL_HALO:], jnp.where(g1, s4[POOL_HALO:],
                           jnp.where(g2, s8[POOL_HALO:], s16[POOL_HALO:])))
        cnt = jnp.minimum((seq_tile * tm + rows.start + row_p + 1).astype(F32), win)
        st["d"] = (pooled / cnt - u).astype(BF16)

    def project_pool(st):
        yp_ref[st["rows"], :] = (_dot(st["d"], wpool_ref[...]) * pscale_ref[...]).astype(BF16)

    _skewed([norm_in, project_in, norm_latents, project_up, heads_and_pool, project_pool], tm, sub)


def _mix_pre(h, w, rope_c, rope_s1, rope_s2, seq_len):
    T = h.shape[0]
    tm = TOKEN_TILE
    row = lambda n: pl.BlockSpec((tm, n), lambda i: (i, 0))
    ins = [h, w["mix_norm"], w["w_in"], w["gla_w_gate"], w["gla_b_gate"], w["mla_q_norm"], w["mla_w_uq"],
           w["mla_kv_norm"], w["mla_w_ukv_k"], w["mla_w_ukv_v"], w["mla_gq"], w["mla_gk"],
           rope_c, rope_s1, rope_s2, w["pool_w"], w["pool_scale"]]
    in_specs = [row(D_MODEL)] + [_full(a.shape) for a in ins[1:12]] + [row(HEAD_PAD)] * 3 + \
               [_full(w["pool_w"].shape), _full(w["pool_scale"].shape)]
    out_shape = [jax.ShapeDtypeStruct((T, COL_CQ), F32), jax.ShapeDtypeStruct((T, GLA_K), F32),
                 jax.ShapeDtypeStruct((T, MLA_QK_PAD), BF16), jax.ShapeDtypeStruct((T, MLA_QK_PAD), BF16),
                 jax.ShapeDtypeStruct((T, MLA_QK_PAD), BF16), jax.ShapeDtypeStruct((T, POOL_W), BF16)]
    out_specs = [row(COL_CQ), row(GLA_K), row(MLA_QK_PAD), row(MLA_QK_PAD), row(MLA_QK_PAD), row(POOL_W)]
    return pl.pallas_call(
        functools.partial(_mix_pre_kernel, tiles_per_seq=seq_len // tm),
        grid=(T // tm,), in_specs=in_specs, out_specs=out_specs, out_shape=out_shape,
        scratch_shapes=[pltpu.VMEM((POOL_HALO, POOL_W), F32)],
        compiler_params=_cparams(1), name="mix_pre")(*ins)


def _gla_kernel(zg_ref, la_ref, gn_ref, y_ref, state_ref, o_ref):
    tg = zg_ref.shape[0]
    C = GLA_CHUNK

    @pl.when(pl.program_id(1) == 0)
    def _():
        state_ref[...] = jnp.zeros_like(state_ref)

    r_i = lax.broadcasted_iota(jnp.int32, (C, C), 0)
    c_i = lax.broadcasted_iota(jnp.int32, (C, C), 1)
    tri = (r_i >= c_i).astype(BF16)
    ones = jnp.ones((C, GLA_W), BF16)
    head_k = lax.broadcasted_iota(jnp.int32, (C, GLA_K), 1) // GLA_DK
    head_v = lax.broadcasted_iota(jnp.int32, (C, GLA_W), 1) // GLA_DV
    ar = lax.broadcasted_iota(jnp.int32, (GLA_HEADS * C, C), 0)
    ac = lax.broadcasted_iota(jnp.int32, (GLA_HEADS * C, C), 1)
    causal = (ar % C) >= ac
    sk = lax.broadcasted_iota(jnp.int32, (GLA_K, GLA_W), 0) // GLA_DK
    sv = lax.broadcasted_iota(jnp.int32, (GLA_K, GLA_W), 1) // GLA_DV
    blockdiag = sk == sv

    def log_decay(st):
        la3 = _split3(la_ref[st["rows"], :])
        st["bc"] = _dot(tri, la3[0]) + _dot(tri, la3[1]) + _dot(tri, la3[2])
        st["dsum"] = _dot_tn(la3[0], ones) + _dot_tn(la3[1], ones) + _dot_tn(la3[2], ones)

    def scores(st):
        rows, bc = st["rows"], st["bc"]
        q = zg_ref[rows, COL_GQ:COL_GQ + GLA_K] * (GLA_DK ** -0.5)
        k = zg_ref[rows, COL_GK:COL_GK + GLA_K]
        b_last = bc[C - 1:C, :]
        q_dec = (q * jnp.exp(bc)).astype(BF16)
        k_dec = (k * jnp.exp(-bc)).astype(BF16)
        st["k_end"] = (k * jnp.exp(b_last - bc)).astype(BF16)
        st["decay"] = jnp.exp(st["dsum"])
        zero = jnp.zeros_like(q_dec)
        qs = jnp.concatenate([jnp.where(head_k == hh, q_dec, zero) for hh in range(GLA_HEADS)], axis=0)
        st["q_dec"] = q_dec
        st["att"] = _dot_nt(qs, k_dec)

    def values(st):
        v = zg_ref[st["rows"], COL_GV:COL_GV + GLA_W].astype(BF16)
        st["o_full"] = _dot(jnp.where(causal, st["att"], 0.0).astype(BF16), v)
        st["upd"] = jnp.where(blockdiag, _dot_tn(st["k_end"], v), 0.0)

    state = [state_ref[...]]

    def recur(st):
        o_full = st["o_full"]
        o = _dot(st["q_dec"], state[0].astype(BF16))
        for hh in range(GLA_HEADS):
            o = o + jnp.where(head_v == hh, o_full[hh * C:(hh + 1) * C, :], 0.0)
        o_ref[st["rows"], :] = o
        state[0] = st["decay"] * state[0] + st["upd"]

    _skewed([log_decay, scores, values, recur], tg, C)
    state_ref[...] = state[0]

    o = o_ref[...]
    gr = lax.broadcasted_iota(jnp.int32, (GLA_W, GLA_W), 0) // GLA_DV
    gc = lax.broadcasted_iota(jnp.int32, (GLA_W, GLA_W), 1) // GLA_DV
    group = (gr == gc).astype(BF16)
    oo = _split2(o * o)
    ms = (_dot(oo[0], group) + _dot(oo[1], group)) * (1.0 / GLA_DV)
    r = zg_ref[:, COL_GR:COL_GR + GLA_W]
    y = o * lax.rsqrt(ms + EPS) * gn_ref[...] * (r / (1.0 + jnp.exp(-r)))
    y_ref[...] = y.astype(BF16)


def _gla(zg, la, gn, batch, seq_len):
    T = zg.shape[0]
    tg = GLA_TILE
    nt = seq_len // tg
    return pl.pallas_call(
        _gla_kernel, grid=(batch, nt),
        in_specs=[pl.BlockSpec((tg, COL_CQ), lambda b, s: (b * nt + s, 0)),
                  pl.BlockSpec((tg, GLA_K), lambda b, s: (b * nt + s, 0)),
                  _full(gn.shape)],
        out_specs=pl.BlockSpec((tg, GLA_W), lambda b, s: (b * nt + s, 0)),
        out_shape=jax.ShapeDtypeStruct((T, GLA_W), BF16),
        scratch_shapes=[pltpu.VMEM((GLA_K, GLA_W), F32), pltpu.VMEM((tg, GLA_W), F32)],
        compiler_params=_cparams(2), name="gla")(zg, la, gn)


def _attn_kernel(q_ref, k_ref, v_ref, o_ref, m_ref, acc_ref):
    tq = q_ref.shape[0]
    ts = ATTN_SUB
    i = pl.program_id(2)
    m_ref[...] = jnp.full_like(m_ref, NEG_BIG)
    acc_ref[...] = jnp.zeros_like(acc_ref)

    def sub_block(hh, start, r0, mask_off):
        hs = slice(hh * HEAD_PAD, (hh + 1) * HEAD_PAD)
        kj = k_ref[pl.ds(start, ts), hs]
        vj = v_ref[pl.ds(start, ts), hs]
        s = _dot_nt(q_ref[r0:, hs], kj)
        if mask_off is not None:
            row = lax.broadcasted_iota(jnp.int32, s.shape, 0) + r0
            col = lax.broadcasted_iota(jnp.int32, s.shape, 1) + mask_off
            s = jnp.where(col <= row, s, NEG_BIG)
        m_old = m_ref[hh, r0:, :]
        parts = [s[:, c * 128:(c + 1) * 128] for c in range(ts // 128)]
        m_new = jnp.maximum(m_old, jnp.max(functools.reduce(jnp.maximum, parts), axis=-1, keepdims=True))
        p = jnp.concatenate([jnp.exp2((x - m_new).astype(BF16)) for x in parts], axis=1)
        acc_ref[hh, r0:, :] = jnp.exp2(m_old - m_new) * acc_ref[hh, r0:, :] + _dot(p, vj)
        m_ref[hh, r0:, :] = m_new

    def body(j, carry):
        base = pl.multiple_of(j * tq, tq)
        for sb in range(tq // ts):
            for hh in range(2):
                sub_block(hh, base + sb * ts, 0, None)
        return carry

    lax.fori_loop(0, i, body, 0)
    base = pl.multiple_of(i * tq, tq)
    for sb in range(tq // ts):
        for hh in range(2):
            sub_block(hh, base + sb * ts, sb * ts, sb * ts)
    outs = []
    for hh in range(2):
        a = acc_ref[hh]
        outs.append(a / a[:, MLA_V:MLA_V + 1])
    lane = lax.broadcasted_iota(jnp.int32, (tq, HEAD_PAD), 1)
    o_ref[...] = jnp.where(lane < MLA_V, outs[0], pltpu.roll(outs[1], MLA_V, 1)).astype(BF16)


def _attn(q, k, v, batch, seq_len):
    T = q.shape[0]
    tq = ATTN_TILE
    nq = seq_len // tq
    pairs = MLA_HEADS // 2
    return pl.pallas_call(
        _attn_kernel, grid=(batch, pairs, nq),
        in_specs=[pl.BlockSpec((tq, 2 * HEAD_PAD), lambda b, p, i: (b * nq + i, p)),
                  pl.BlockSpec((seq_len, 2 * HEAD_PAD), lambda b, p, i: (b, p)),
                  pl.BlockSpec((seq_len, 2 * HEAD_PAD), lambda b, p, i: (b, p))],
        out_specs=pl.BlockSpec((tq, 2 * MLA_V), lambda b, p, i: (b * nq + i, p)),
        out_shape=jax.ShapeDtypeStruct((T, MLA_W), BF16),
        scratch_shapes=[pltpu.VMEM((2, tq, HEAD_PAD), F32), pltpu.VMEM((2, tq, HEAD_PAD), F32)],
        compiler_params=_cparams(3), name="attn")(q, k, v)


def _out_router_kernel(h_ref, yg_ref, ym_ref, yp_ref, wo_ref, fn_ref, rw_ref, rb_ref,
                       h1_ref, hn0_ref, hn1_ref, idx_ref, gate_ref, cnt_ref, carry_ref):
    tm = h_ref.shape[0]

    @pl.when(pl.program_id(0) == 0)
    def _():
        carry_ref[...] = jnp.zeros_like(carry_ref)

    sub = ROUTER_SUB
    lane = lax.broadcasted_iota(jnp.int32, (sub, 128), 1)
    r_i = lax.broadcasted_iota(jnp.int32, (sub, sub), 0)
    c_i = lax.broadcasted_iota(jnp.int32, (sub, sub), 1)
    tri = (r_i >= c_i).astype(BF16)
    def project(st):
        rows = st["rows"]
        st["h1"] = (h_ref[rows, :] + _dot(yg_ref[rows, :], wo_ref[0:GLA_W, :])
                    + _dot(ym_ref[rows, :], wo_ref[GLA_W:GLA_W + MLA_W, :])
                    + _dot(yp_ref[rows, :], wo_ref[GLA_W + MLA_W:, :]))

    def normalize(st):
        rows = st["rows"]
        h1_ref[rows, :] = st["h1"]
        hn = _rms(st["h1"], fn_ref[...])
        st["hi"], st["lo"] = _split2(hn)
        packed = _pack_bf16_pairs(hn)
        slab = packed.shape[1] // DISPATCH_SLABS
        hn0_ref[rows, :] = packed[:, :slab]
        hn1_ref[rows, :] = packed[:, slab:]

    def score(st):
        r2 = _dot(st["hi"], rw_ref[...])
        st["logits"] = r2[:, :128] + r2[:, 128:] + _dot(st["lo"], rw_ref[:, 0:128]) + rb_ref[...]

    def select(st):
        rows = st["rows"]
        cur = jnp.where(lane < N_EXPERTS, st["logits"], NEG_BIG)
        idx_out = jnp.zeros((sub, 128), jnp.int32)
        val_out = jnp.zeros((sub, 128), F32)
        chosen = jnp.zeros((sub, 128), F32)
        top0 = None
        sels = []
        for kk in range(TOP_K):
            m = jnp.max(cur, axis=-1, keepdims=True)
            sel = jnp.min(jnp.where(cur == m, lane, 128), axis=-1, keepdims=True)
            if kk == 0:
                top0 = m
            sels.append(sel)
            idx_out = jnp.where(lane == kk, sel, idx_out)
            val_out = jnp.where(lane == kk, jnp.exp(m - top0), val_out)
            chosen = jnp.where(lane == sel, 1.0, chosen)
            cur = jnp.where(lane == sel, NEG_BIG, cur)
        gate_ref[rows, :] = val_out / jnp.sum(val_out, axis=-1, keepdims=True)

        incl = _dot(tri, chosen.astype(BF16))
        before = carry_ref[0:1, :] + incl - chosen
        for kk in range(TOP_K):
            rank = jnp.sum(jnp.where(lane == sels[kk], before, 0.0), axis=-1, keepdims=True)
            idx_out = jnp.where(lane == TOP_K + kk, rank.astype(jnp.int32), idx_out)
        idx_ref[rows, :] = idx_out
        carry_ref[...] = carry_ref[...] + incl[sub - 1:sub, :]

    _skewed([project, normalize, score, select], tm, sub)
    cnt_ref[...] = carry_ref[...].astype(jnp.int32)


def _out_router(h, yg, ym, yp, w):
    T = h.shape[0]
    tm = TOKEN_TILE
    row = lambda n: pl.BlockSpec((tm, n), lambda i: (i, 0))
    slab = D_MODEL // 2 // DISPATCH_SLABS
    ins = [h, yg, ym, yp, w["w_out"], w["ffn_norm"], w["router_w"], w["router_b"]]
    return pl.pallas_call(
        _out_router_kernel, grid=(T // tm,),
        in_specs=[row(D_MODEL), row(GLA_W), row(MLA_W), row(POOL_W)] + [_full(a.shape) for a in ins[4:]],
        out_specs=[row(D_MODEL), row(slab), row(slab), row(128), row(128), _full((8, 128))],
        out_shape=[jax.ShapeDtypeStruct((T, D_MODEL), F32), jax.ShapeDtypeStruct((T, slab), jnp.uint32),
                   jax.ShapeDtypeStruct((T, slab), jnp.uint32),
                   jax.ShapeDtypeStruct((T, 128), jnp.int32), jax.ShapeDtypeStruct((T, 128), F32),
                   jax.ShapeDtypeStruct((8, 128), jnp.int32)],
        scratch_shapes=[pltpu.VMEM((8, 128), F32)],
        compiler_params=_cparams(1), name="out_router")(*ins)


def _moe_kernel(be_ref, nb_ref, x0_ref, x1_ref, wg_ref, bg_ref, wu_ref, bu_ref, wd_ref, bd_ref,
                y_ref, wg_bf, wu_bf, wd_bf):
    i = pl.program_id(0)
    used = i < nb_ref[0]
    new_expert = (i == 0) | (be_ref[i] != be_ref[jnp.maximum(i - 1, 0)])

    @pl.when(used & new_expert)
    def _():
        for src, dst in ((wg_ref, wg_bf), (wu_ref, wu_bf), (wd_ref, wd_bf)):
            for r in range(0, src.shape[2], MOE_CAST_ROWS):
                dst[r:r + MOE_CAST_ROWS, :] = src[0, 0, r:r + MOE_CAST_ROWS, :].astype(BF16)

    @pl.when(used)
    def _():
        halves = [_unpack_bf16_pairs(r[...]) for r in (x0_ref, x1_ref)]
        x = jnp.concatenate([h[0] for h in halves] + [h[1] for h in halves], axis=1).astype(BF16)
        g = jnp.minimum(_dot(x, wg_bf[...]) + bg_ref[0], SWIGLU_LIMIT)
        up = jnp.clip(_dot(x, wu_bf[...]) + bu_ref[0], -SWIGLU_LIMIT, SWIGLU_LIMIT)
        hb = (up + 1.0) * (g / (1.0 + jnp.exp(-SWIGLU_ALPHA * g)))
        y_ref[...] = _pack_bf16_pairs(_dot(hb.astype(BF16), wd_bf[...]) + bd_ref[0])

    @pl.when(jnp.logical_not(used))
    def _():
        y_ref[...] = jnp.zeros_like(y_ref)


def _moe(xs, block_e, n_used, w):
    n_rows = xs[0].shape[0]
    bm = MOE_BLOCK
    layer = w["layer"]
    wspec = lambda shp: pl.BlockSpec((1, 1) + shp, lambda i, be, nb: (layer, be[i], 0, 0))
    bspec = lambda shp: pl.BlockSpec((1,) + shp, lambda i, be, nb: (be[i], 0, 0))
    grid_spec = pltpu.PrefetchScalarGridSpec(
        num_scalar_prefetch=2, grid=(n_rows // bm,),
        in_specs=[pl.BlockSpec((bm, D_MODEL // 2 // DISPATCH_SLABS), lambda i, be, nb: (i, 0))] * DISPATCH_SLABS + [
                  wspec((D_MODEL, D_FF)), bspec((1, D_FF)), wspec((D_MODEL, D_FF)), bspec((1, D_FF)),
                  wspec((D_FF, D_MODEL)), bspec((1, D_MODEL))],
        out_specs=pl.BlockSpec((bm, D_MODEL // 2), lambda i, be, nb: (i, 0)),
        scratch_shapes=[pltpu.VMEM((D_MODEL, D_FF), BF16), pltpu.VMEM((D_MODEL, D_FF), BF16),
                        pltpu.VMEM((D_FF, D_MODEL), BF16)])
    return pl.pallas_call(
        _moe_kernel, grid_spec=grid_spec,
        out_shape=jax.ShapeDtypeStruct((n_rows, D_MODEL // 2), jnp.uint32),
        compiler_params=_cparams(1), name="moe")(
            block_e, n_used, *xs, w["moe_w_gate"], w["moe_b_gate"], w["moe_w_up"], w["moe_b_up"],
            w["moe_w_down"], w["moe_b_down"])


def _ple_kernel(h1_ref, y0_ref, y1_ref, y2_ref, y3_ref, gate_ref, p_ref, wple_ref, gn_ref, wpg_ref, pn_ref, o_ref):
    def combine(st):
        rows = st["rows"]
        gates = gate_ref[rows, :]
        h2 = h1_ref[rows, :]
        for kk, y_ref in enumerate((y0_ref, y1_ref, y2_ref, y3_ref)):
            h2 = h2 + gates[:, kk:kk + 1] * jnp.concatenate(_unpack_bf16_pairs(y_ref[rows, :]), axis=1)
        st["h2"] = h2
        st["hn"] = _rms(h2, gn_ref[...]).astype(BF16)

    def project(st):
        st["e"] = _dot(p_ref[0, st["rows"], :].astype(BF16), wple_ref[...])
        st["a"] = _dot(st["hn"], wpg_ref[...])

    def finish(st):
        gate = 1.0 / (1.0 + jnp.exp(-st["a"]))
        o_ref[st["rows"], :] = st["h2"] + _rms(st["e"] * gate, pn_ref[...])

    _skewed([combine, project, finish], h1_ref.shape[0], PLE_SUB)


def _ple(h1, ys_k, gates, p, w, part):
    T = h1.shape[0]
    tm = TOKEN_TILE
    steps = T // COMBINE_PARTS // tm
    off = part * steps
    row = lambda n: pl.BlockSpec((tm, n), lambda i: (i + off, 0))
    local = pl.BlockSpec((tm, D_MODEL // 2), lambda i: (i, 0))
    weights = [w["ple_w_proj"], w["ple_gate_norm"], w["ple_w_gate"], w["ple_post_norm"]]
    layer = w["layer"]
    p_spec = pl.BlockSpec((1, tm, D_PLE), lambda i: (layer, i + off, 0))
    ins = [h1, *ys_k, gates, p, *weights]
    in_specs = [row(D_MODEL)] + [local] * TOP_K + [row(128), p_spec] + [_full(a.shape) for a in weights]
    return pl.pallas_call(
        _ple_kernel, grid=(steps,), in_specs=in_specs,
        out_specs=row(D_MODEL), out_shape=jax.ShapeDtypeStruct((T, D_MODEL), F32),
        input_output_aliases={0: 0}, compiler_params=_cparams(1), name="ple")(*ins)


def _pad_heads(wm, per_head, n_heads=MLA_HEADS):
    kdim = wm.shape[0]
    w3 = wm.reshape(kdim, n_heads, per_head)
    return jnp.pad(w3, ((0, 0), (0, 0), (0, HEAD_PAD - per_head))).reshape(kdim, n_heads * HEAD_PAD)


def _swap_rope_halves(a):
    a3 = a.reshape(a.shape[0], -1, HEAD_PAD)
    half = MLA_ROPE // 2
    x1 = a3[:, :, MLA_NOPE:MLA_NOPE + half]
    x2 = a3[:, :, MLA_NOPE + half:MLA_QK]
    out = jnp.zeros_like(a3).at[:, :, MLA_NOPE:MLA_NOPE + half].set(x2).at[:, :, MLA_NOPE + half:MLA_QK].set(x1)
    return out.reshape(a.shape)


def _layer_params(i, mix_norm, w_in, gla_w_gate, gla_b_gate, gla_out_norm, mla_q_norm, mla_w_uq, mla_kv_norm,
                  mla_w_ukv, mla_qk_q_norm, mla_qk_k_norm, pool_w, pool_scale, w_out, ffn_norm, router_w,
                  router_b, moe_w_gate, moe_b_gate, moe_w_up, moe_b_up, moe_w_down, moe_b_down,
                  ple_w_proj, ple_gate_norm, ple_w_gate, ple_post_norm):
    wi = w_in[i]
    c = np.cumsum((0, 128, 128, 256, 16, 256, 256, 128, 32, 256))
    gq, gk, gv, glow, gr, cq, ckv, krope, upool = [wi[:, c[j]:c[j + 1]] for j in range(9)]
    misc = jnp.concatenate([glow, krope, jnp.zeros((D_MODEL, 128 - 48), F32)], axis=1)
    w_in_p = jnp.concatenate([gq, gk, gv, gr, cq, upool, ckv, misc], axis=1).astype(BF16)
    wgate_p = jnp.zeros((128, GLA_K), F32).at[MISC_GLOW:MISC_GLOW + GLA_GATE_RANK].set(gla_w_gate[i]).astype(BF16)
    ukv = mla_w_ukv[i].reshape(MLA_KV_RANK, MLA_HEADS, MLA_NOPE + MLA_V)
    ukv_k = _pad_heads(ukv[:, :, :MLA_NOPE].reshape(MLA_KV_RANK, MLA_HEADS * MLA_NOPE), MLA_NOPE)
    ukv_v = _pad_heads(ukv[:, :, MLA_NOPE:].reshape(MLA_KV_RANK, MLA_W), MLA_V)
    pw = pool_w[i]
    pool_bd = jnp.zeros((POOL_W, POOL_W), F32)
    for g in range(4):
        pool_bd = pool_bd.at[g * 64:(g + 1) * 64, g * 64:(g + 1) * 64].set(pw[g])
    rw = jnp.pad(router_w[i], ((0, 0), (0, 128 - N_EXPERTS)))
    rw_hi = rw.astype(BF16)
    rw_lo = (rw - rw_hi.astype(F32)).astype(BF16)
    row = lambda a: a.reshape(1, -1)
    pad96 = lambda a: jnp.pad(a, (0, HEAD_PAD - MLA_QK)).reshape(1, HEAD_PAD)
    wuq_p = _pad_heads(mla_w_uq[i], MLA_QK)
    gq_p = pad96(mla_qk_q_norm[i] * (MLA_QK ** -0.5 * LOG2E))
    return {
        "mix_norm": row(mix_norm[i]), "w_in": w_in_p, "gla_w_gate": wgate_p, "gla_b_gate": row(gla_b_gate[i]),
        "gla_out_norm": row(jnp.tile(gla_out_norm[i], GLA_HEADS)),
        "mla_q_norm": row(mla_q_norm[i]),
        "mla_w_uq": jnp.concatenate([wuq_p, _swap_rope_halves(wuq_p)], axis=1).astype(BF16),
        "mla_kv_norm": row(mla_kv_norm[i]), "mla_w_ukv_k": ukv_k.astype(BF16), "mla_w_ukv_v": ukv_v.astype(BF16),
        "mla_gq": jnp.concatenate([gq_p, _swap_rope_halves(gq_p)], axis=0), "mla_gk": pad96(mla_qk_k_norm[i]),
        "pool_w": pool_bd.astype(BF16), "pool_scale": row(pool_scale[i]),
        "w_out": w_out[i].astype(BF16), "ffn_norm": row(ffn_norm[i]),
        "router_w": jnp.concatenate([rw_hi, rw_lo], axis=1),
        "router_b": row(jnp.pad(router_b[i], (0, 128 - N_EXPERTS))),
        "layer": i,
        "moe_w_gate": moe_w_gate, "moe_b_gate": moe_b_gate[i].reshape(N_EXPERTS, 1, D_FF),
        "moe_w_up": moe_w_up, "moe_b_up": moe_b_up[i].reshape(N_EXPERTS, 1, D_FF),
        "moe_w_down": moe_w_down, "moe_b_down": moe_b_down[i].reshape(N_EXPERTS, 1, D_MODEL),
        "ple_w_proj": ple_w_proj[i].astype(BF16), "ple_gate_norm": row(ple_gate_norm[i]),
        "ple_w_gate": ple_w_gate[i].astype(BF16), "ple_post_norm": row(ple_post_norm[i]),
    }


def _rope_tables(positions):
    T = positions.size
    inv = ROPE_BASE ** (-jnp.arange(0, MLA_ROPE, 2, dtype=F32) / MLA_ROPE)
    ang = (positions.reshape(T, 1).astype(F32) * inv).reshape(T * (MLA_ROPE // 2))
    cos, sin = jnp.cos(ang).reshape(T, MLA_ROPE // 2), jnp.sin(ang).reshape(T, MLA_ROPE // 2)
    z16 = jnp.zeros((T, 16), F32)
    tail = jnp.zeros((T, HEAD_PAD - MLA_QK), F32)
    c = jnp.concatenate([jnp.ones((T, MLA_NOPE), F32), cos, cos, tail], axis=1)
    s1 = jnp.concatenate([jnp.zeros((T, MLA_NOPE), F32), -sin, z16, tail], axis=1)
    s2 = jnp.concatenate([jnp.zeros((T, MLA_NOPE), F32), z16, sin, tail], axis=1)
    return c, s1, s2


def _route(top_idx, rank, counts, T):
    bm = MOE_BLOCK
    A = T * TOP_K
    padded = (counts + bm - 1) // bm * bm
    pad_end = jnp.cumsum(padded)
    pad_start = pad_end - padded
    experts = jnp.arange(N_EXPERTS, dtype=jnp.int32)
    dest = rank + jnp.sum(jnp.where(top_idx[:, :, None] == experts, pad_start, 0), axis=-1)
    n_blocks = (A + N_EXPERTS * (bm - 1) + bm - 1) // bm
    n_rows = n_blocks * bm
    block_start = jnp.arange(n_blocks, dtype=jnp.int32) * bm
    block_e = jnp.minimum(jnp.sum((pad_end[None, :] <= block_start[:, None]).astype(jnp.int32), axis=1),
                          N_EXPERTS - 1)
    n_used = (pad_end[-1] // bm).astype(jnp.int32).reshape(1)
    return dest, n_rows, block_e, n_used


def _dispatch(hn_slabs, dest, n_rows):
    T, width = hn_slabs[0].shape
    win = DISPATCH_ROWS
    dest_t = dest.T
    mesh = plsc.VectorSubcoreMesh(core_axis_name="core", subcore_axis_name="subcore")

    @functools.partial(pl.kernel, out_type=jax.ShapeDtypeStruct((n_rows, width), hn_slabs[0].dtype), mesh=mesh,
                       scratch_types=[], name="dispatch")
    def scatter_rows(x_hbm, i_hbm, o_hbm):
        def body(x_vmem, i_vmem):
            for kk in range(TOP_K):
                pltpu.sync_copy(x_vmem, o_hbm.at[i_vmem.at[kk]])

        pltpu.emit_pipeline(
            body, grid=(T // win,),
            in_specs=[pl.BlockSpec((win, width), lambda i: (i, 0)), pl.BlockSpec((TOP_K, win), lambda i: (0, i))],
            out_specs=[], core_axis_name=("core", "subcore"),
            dimension_semantics=(pltpu.PARALLEL,))(x_hbm, i_hbm)

    return [scatter_rows(slab, dest_t) for slab in hn_slabs]


def kernel(x, p, positions, mix_norm, w_in, gla_w_gate, gla_b_gate, gla_out_norm, mla_q_norm, mla_w_uq,
           mla_kv_norm, mla_w_ukv, mla_qk_q_norm, mla_qk_k_norm, pool_w, pool_scale, w_out, ffn_norm,
           router_w, router_b, moe_w_gate, moe_b_gate, moe_w_up, moe_b_up, moe_w_down, moe_b_down,
           ple_w_proj, ple_gate_norm, ple_w_gate, ple_post_norm):
    B, S, D = x.shape
    T = B * S
    depth = p.shape[0]
    params = (mix_norm, w_in, gla_w_gate, gla_b_gate, gla_out_norm, mla_q_norm, mla_w_uq, mla_kv_norm,
              mla_w_ukv, mla_qk_q_norm, mla_qk_k_norm, pool_w, pool_scale, w_out, ffn_norm, router_w,
              router_b, moe_w_gate, moe_b_gate, moe_w_up, moe_b_up, moe_w_down, moe_b_down,
              ple_w_proj, ple_gate_norm, ple_w_gate, ple_post_norm)
    take = lambda a, idx: a.at[idx].get(mode="promise_in_bounds")
    rope_c, rope_s1, rope_s2 = _rope_tables(positions)
    p_flat = p.reshape(depth, T, D_PLE)
    h = x.reshape(T, D)
    for i in range(depth):
        w = _layer_params(i, *params)
        zg, la, q, k, v, y_pool = _mix_pre(h, w, rope_c, rope_s1, rope_s2, S)
        y_gla = _gla(zg, la, w["gla_out_norm"], B, S)
        y_mla = _attn(q, k, v, B, S)
        h1, hn0, hn1, route, gates, counts = _out_router(h, y_gla, y_mla, y_pool, w)
        dest, n_rows, block_e, n_used = _route(route[:, :TOP_K], route[:, TOP_K:2 * TOP_K],
                                               counts[0, :N_EXPERTS], T)
        ys = _moe(_dispatch([hn0, hn1], dest, n_rows), block_e, n_used, w)
        h = h1
        for part in range(COMBINE_PARTS):
            d = dest[part * (T // COMBINE_PARTS):(part + 1) * (T // COMBINE_PARTS)]
            h = _ple(h, [take(ys, d[:, kk]) for kk in range(TOP_K)], gates, p_flat, w, part)
    return h.reshape(B, S, D)
```

```python
import functools

import jax
import jax.numpy as jnp
import numpy as np
from jax import lax
from jax.experimental import pallas as pl
from jax.experimental.pallas import tpu as pltpu
from jax.experimental.pallas import tpu_sc as plsc

F32 = jnp.float32
BF16 = jnp.bfloat16

D_MODEL = 1024
EPS = 1e-6
D_PLE = 256

GLA_HEADS = 4
GLA_DK = 32
GLA_DV = 64
GLA_GATE_RANK = 16
GLA_TAU = 16.0
GLA_CHUNK = 64
GLA_K = GLA_HEADS * GLA_DK
GLA_W = GLA_HEADS * GLA_DV

MLA_HEADS = 8
MLA_Q_RANK = 256
MLA_KV_RANK = 128
MLA_NOPE = 64
MLA_ROPE = 32
MLA_QK = MLA_NOPE + MLA_ROPE
MLA_V = 64
MLA_W = MLA_HEADS * MLA_V
ROPE_BASE = 10000.0
HEAD_PAD = 128
MLA_QK_PAD = MLA_HEADS * HEAD_PAD

POOL_WINDOWS = (2, 4, 8, 16)
POOL_GROUP = 64
POOL_W = 256
POOL_HALO = 16

N_EXPERTS = 32
TOP_K = 4
D_FF = 1024
SWIGLU_LIMIT = 7.0
SWIGLU_ALPHA = 1.702

COL_GQ, COL_GK, COL_GV, COL_GR, COL_CQ, COL_POOL, COL_CKV, COL_MISC = 0, 128, 256, 512, 768, 1024, 1280, 1408
D_IN_PAD = 1536
MISC_GLOW = 0
MISC_ROPE = 16

LOG2E = 1.4426950408889634
TOKEN_TILE = 512
MIX_SUB, ROUTER_SUB, PLE_SUB = 256, 256, 128
GLA_TILE = 512
ATTN_TILE = 2048
ATTN_SUB = 512
MOE_BLOCK = 512
MOE_CAST_ROWS = 256
COMBINE_PARTS = 4
DISPATCH_ROWS = 128
DISPATCH_SLABS = 2
VMEM_LIMIT = 56 * 1024 * 1024
NEG_BIG = -1e30


def _cparams(n_axes, **flags):
    return pltpu.CompilerParams(dimension_semantics=("arbitrary",) * n_axes,
                                vmem_limit_bytes=VMEM_LIMIT, flags=flags or None)


def _rms(x, g):
    return x * lax.rsqrt(jnp.mean(x * x, axis=-1, keepdims=True) + EPS) * g


def _dot(a, b):
    return jnp.dot(a, b, preferred_element_type=F32)


def _dot_nt(a, b):
    return lax.dot_general(a, b, (((1,), (1,)), ((), ())), preferred_element_type=F32)


def _dot_tn(a, b):
    return lax.dot_general(a, b, (((0,), (0,)), ((), ())), preferred_element_type=F32)


def _split3(x):
    hi = x.astype(BF16)
    r = x - hi.astype(F32)
    mid = r.astype(BF16)
    lo = (r - mid.astype(F32)).astype(BF16)
    return hi, mid, lo


def _split2(x):
    hi = x.astype(BF16)
    lo = (x - hi.astype(F32)).astype(BF16)
    return hi, lo


def _pack_bf16_pairs(x):
    m = x.shape[1] // 2
    bits = lax.bitcast_convert_type(x.astype(BF16).astype(F32), jnp.uint32)
    return (bits[:, :m] >> 16) | (bits[:, m:] & jnp.uint32(0xFFFF0000))


def _unpack_bf16_pairs(w):
    lo = lax.bitcast_convert_type(w << 16, F32)
    hi = lax.bitcast_convert_type(w & jnp.uint32(0xFFFF0000), F32)
    return lo, hi


def _skewed(stages, n_rows, sub):
    states = [{"rows": slice(r0, r0 + sub)} for r0 in range(0, n_rows, sub)]
    for step in range(len(states) + len(stages) - 1):
        for s, stage in enumerate(stages):
            t = step - s
            if 0 <= t < len(states):
                stage(states[t])


def _full(shape):
    nd = len(shape)
    return pl.BlockSpec(shape, lambda *_: (0,) * nd)


def _rope(x, c, s1, s2):
    return x * c + pltpu.roll(x, HEAD_PAD - 16, 1) * s1 + pltpu.roll(x, 16, 1) * s2


def _mix_pre_kernel(h_ref, mixn_ref, win_ref, wgate_ref, bgate_ref, qn_ref, wuq_ref, kvn_ref,
                    wukvk_ref, wukvv_ref, gq_ref, gk_ref, rc_ref, rs1_ref, rs2_ref,
                    wpool_ref, pscale_ref,
                    zg_ref, la_ref, q_ref, k_ref, v_ref, yp_ref, carry_ref, *, tiles_per_seq):
    tm = h_ref.shape[0]
    sub = MIX_SUB
    seq_tile = pl.program_id(0) % tiles_per_seq

    @pl.when(seq_tile == 0)
    def _():
        carry_ref[...] = jnp.zeros_like(carry_ref)

    lane = lax.broadcasted_iota(jnp.int32, (sub, HEAD_PAD), 1)
    in_rope = (lane >= MLA_NOPE) & (lane < MLA_QK)
    lane_v = lax.broadcasted_iota(jnp.int32, (sub, MLA_QK_PAD), 1)
    ones_lane = lane_v % HEAD_PAD == MLA_V
    lane_p = lax.broadcasted_iota(jnp.int32, (sub, POOL_W), 1)
    row_p = lax.broadcasted_iota(jnp.int32, (sub, POOL_W), 0)
    g0, g1, g2 = lane_p < 64, lane_p < 128, lane_p < 192
    win = jnp.where(g0, 2.0, jnp.where(g1, 4.0, jnp.where(g2, 8.0, 16.0)))
    gq, gq_sw, gk = gq_ref[0:1, :], gq_ref[1:2, :], gk_ref[...]

    def norm_in(st):
        st["hn"] = _rms(h_ref[st["rows"], :], mixn_ref[...]).astype(BF16)

    def project_in(st):
        st["z"] = _dot(st["hn"], win_ref[...])

    def norm_latents(st):
        z = st["z"]
        zg_ref[st["rows"], :] = z[:, COL_GQ:COL_CQ]
        st["cqn"] = _rms(z[:, COL_CQ:COL_CQ + MLA_Q_RANK], qn_ref[...]).astype(BF16)
        st["ckvn"] = _rms(z[:, COL_CKV:COL_CKV + MLA_KV_RANK], kvn_ref[...]).astype(BF16)

    def project_up(st):
        zm = st["z"][:, COL_MISC:COL_MISC + 128]
        st["logit"] = _dot(zm.astype(BF16), wgate_ref[...]) + bgate_ref[...]
        st["qf"] = _dot(st["cqn"], wuq_ref[...])
        st["kn"] = _dot(st["ckvn"], wukvk_ref[...])
        st["v"] = _dot(st["ckvn"], wukvv_ref[...])

    def heads_and_pool(st):
        rows, z, qf, kn, logit = st["rows"], st["z"], st["qf"], st["kn"], st["logit"]
        zm = z[:, COL_MISC:COL_MISC + 128]
        la_ref[rows, :] = (jnp.minimum(logit, 0.0) - jnp.log(1.0 + jnp.exp(-jnp.abs(logit)))) * (1.0 / GLA_TAU)
        v_ref[rows, :] = jnp.where(ones_lane, 1.0, st["v"]).astype(BF16)

        rc, rs1, rs2 = rc_ref[rows, :], rs1_ref[rows, :], rs2_ref[rows, :]
        kr = jnp.where(in_rope, pltpu.roll(zm, MLA_NOPE - MISC_ROPE, 1), 0.0)
        kr_ss = jnp.sum(kr * kr, axis=-1, keepdims=True)
        krr = _rope(kr * gk, rc, rs1, rs2)
        cq = rc * gq
        sq_tab = (rs1 + rs2) * gq_sw
        for hh in range(MLA_HEADS):
            sl = slice(hh * HEAD_PAD, (hh + 1) * HEAD_PAD)
            qh = qf[:, sl]
            qsw = qf[:, MLA_QK_PAD + hh * HEAD_PAD:MLA_QK_PAD + (hh + 1) * HEAD_PAD]
            sq = lax.rsqrt(jnp.sum(qh * qh, axis=-1, keepdims=True) * (1.0 / MLA_QK) + EPS)
            q_ref[rows, sl] = ((qh * cq + qsw * sq_tab) * sq).astype(BF16)
            kh = kn[:, sl]
            sk = lax.rsqrt((jnp.sum(kh * kh, axis=-1, keepdims=True) + kr_ss) * (1.0 / MLA_QK) + EPS)
            k_ref[rows, sl] = (sk * (kh * gk + krr)).astype(BF16)

        u = z[:, COL_POOL:COL_POOL + POOL_W]
        xe = jnp.concatenate([carry_ref[...], u], axis=0)
        carry_ref[...] = u[sub - POOL_HALO:, :]
        s2 = xe + pltpu.roll(xe, 1, 0)
        s4 = s2 + pltpu.roll(s2, 2, 0)
        s8 = s4 + pltpu.roll(s4, 4, 0)
        s16 = s8 + pltpu.roll(s8, 8, 0)
        pooled = jnp.where(g0, s2[POOL_HALO:], jnp.where(g1, s4[POOL_HALO:],
                           jnp.where(g2, s8[POOL_HALO:], s16[POOL_HALO:])))
        cnt = jnp.minimum((seq_tile * tm + rows.start + row_p + 1).astype(F32), win)
        st["d"] = (pooled / cnt - u).astype(BF16)

    def project_pool(st):
        yp_ref[st["rows"], :] = (_dot(st["d"], wpool_ref[...]) * pscale_ref[...]).astype(BF16)

    _skewed([norm_in, project_in, norm_latents, project_up, heads_and_pool, project_pool], tm, sub)


def _mix_pre(h, w, rope_c, rope_s1, rope_s2, seq_len):
    T = h.shape[0]
    tm = TOKEN_TILE
    row = lambda n: pl.BlockSpec((tm, n), lambda i: (i, 0))
    ins = [h, w["mix_norm"], w["w_in"], w["gla_w_gate"], w["gla_b_gate"], w["mla_q_norm"], w["mla_w_uq"],
           w["mla_kv_norm"], w["mla_w_ukv_k"], w["mla_w_ukv_v"], w["mla_gq"], w["mla_gk"],
           rope_c, rope_s1, rope_s2, w["pool_w"], w["pool_scale"]]
    in_specs = [row(D_MODEL)] + [_full(a.shape) for a in ins[1:12]] + [row(HEAD_PAD)] * 3 + \
               [_full(w["pool_w"].shape), _full(w["pool_scale"].shape)]
    out_shape = [jax.ShapeDtypeStruct((T, COL_CQ), F32), jax.ShapeDtypeStruct((T, GLA_K), F32),
                 jax.ShapeDtypeStruct((T, MLA_QK_PAD), BF16), jax.ShapeDtypeStruct((T, MLA_QK_PAD), BF16),
                 jax.ShapeDtypeStruct((T, MLA_QK_PAD), BF16), jax.ShapeDtypeStruct((T, POOL_W), BF16)]
    out_specs = [row(COL_CQ), row(GLA_K), row(MLA_QK_PAD), row(MLA_QK_PAD), row(MLA_QK_PAD), row(POOL_W)]
    return pl.pallas_call(
        functools.partial(_mix_pre_kernel, tiles_per_seq=seq_len // tm),
        grid=(T // tm,), in_specs=in_specs, out_specs=out_specs, out_shape=out_shape,
        scratch_shapes=[pltpu.VMEM((POOL_HALO, POOL_W), F32)],
        compiler_params=_cparams(1), name="mix_pre")(*ins)


def _gla_kernel(zg_ref, la_ref, gn_ref, y_ref, state_ref, o_ref):
    tg = zg_ref.shape[0]
    C = GLA_CHUNK

    @pl.when(pl.program_id(1) == 0)
    def _():
        state_ref[...] = jnp.zeros_like(state_ref)

    r_i = lax.broadcasted_iota(jnp.int32, (C, C), 0)
    c_i = lax.broadcasted_iota(jnp.int32, (C, C), 1)
    tri = (r_i >= c_i).astype(BF16)
    ones = jnp.ones((C, GLA_W), BF16)
    head_k = lax.broadcasted_iota(jnp.int32, (C, GLA_K), 1) // GLA_DK
    head_v = lax.broadcasted_iota(jnp.int32, (C, GLA_W), 1) // GLA_DV
    ar = lax.broadcasted_iota(jnp.int32, (GLA_HEADS * C, C), 0)
    ac = lax.broadcasted_iota(jnp.int32, (GLA_HEADS * C, C), 1)
    causal = (ar % C) >= ac
    sk = lax.broadcasted_iota(jnp.int32, (GLA_K, GLA_W), 0) // GLA_DK
    sv = lax.broadcasted_iota(jnp.int32, (GLA_K, GLA_W), 1) // GLA_DV
    blockdiag = sk == sv

    def log_decay(st):
        la3 = _split3(la_ref[st["rows"], :])
        st["bc"] = _dot(tri, la3[0]) + _dot(tri, la3[1]) + _dot(tri, la3[2])
        st["dsum"] = _dot_tn(la3[0], ones) + _dot_tn(la3[1], ones) + _dot_tn(la3[2], ones)

    def scores(st):
        rows, bc = st["rows"], st["bc"]
        q = zg_ref[rows, COL_GQ:COL_GQ + GLA_K] * (GLA_DK ** -0.5)
        k = zg_ref[rows, COL_GK:COL_GK + GLA_K]
        b_last = bc[C - 1:C, :]
        q_dec = (q * jnp.exp(bc)).astype(BF16)
        k_dec = (k * jnp.exp(-bc)).astype(BF16)
        st["k_end"] = (k * jnp.exp(b_last - bc)).astype(BF16)
        st["decay"] = jnp.exp(st["dsum"])
        zero = jnp.zeros_like(q_dec)
        qs = jnp.concatenate([jnp.where(head_k == hh, q_dec, zero) for hh in range(GLA_HEADS)], axis=0)
        st["q_dec"] = q_dec
        st["att"] = _dot_nt(qs, k_dec)

    def values(st):
        v = zg_ref[st["rows"], COL_GV:COL_GV + GLA_W].astype(BF16)
        st["o_full"] = _dot(jnp.where(causal, st["att"], 0.0).astype(BF16), v)
        st["upd"] = jnp.where(blockdiag, _dot_tn(st["k_end"], v), 0.0)

    state = [state_ref[...]]

    def recur(st):
        o_full = st["o_full"]
        o = _dot(st["q_dec"], state[0].astype(BF16))
        for hh in range(GLA_HEADS):
            o = o + jnp.where(head_v == hh, o_full[hh * C:(hh + 1) * C, :], 0.0)
        o_ref[st["rows"], :] = o
        state[0] = st["decay"] * state[0] + st["upd"]

    _skewed([log_decay, scores, values, recur], tg, C)
    state_ref[...] = state[0]

    o = o_ref[...]
    gr = lax.broadcasted_iota(jnp.int32, (GLA_W, GLA_W), 0) // GLA_DV
    gc = lax.broadcasted_iota(jnp.int32, (GLA_W, GLA_W), 1) // GLA_DV
    group = (gr == gc).astype(BF16)
    oo = _split2(o * o)
    ms = (_dot(oo[0], group) + _dot(oo[1], group)) * (1.0 / GLA_DV)
    r = zg_ref[:, COL_GR:COL_GR + GLA_W]
    y = o * lax.rsqrt(ms + EPS) * gn_ref[...] * (r / (1.0 + jnp.exp(-r)))
    y_ref[...] = y.astype(BF16)


def _gla(zg, la, gn, batch, seq_len):
    T = zg.shape[0]
    tg = GLA_TILE
    nt = seq_len // tg
    return pl.pallas_call(
        _gla_kernel, grid=(batch, nt),
        in_specs=[pl.BlockSpec((tg, COL_CQ), lambda b, s: (b * nt + s, 0)),
                  pl.BlockSpec((tg, GLA_K), lambda b, s: (b * nt + s, 0)),
                  _full(gn.shape)],
        out_specs=pl.BlockSpec((tg, GLA_W), lambda b, s: (b * nt + s, 0)),
        out_shape=jax.ShapeDtypeStruct((T, GLA_W), BF16),
        scratch_shapes=[pltpu.VMEM((GLA_K, GLA_W), F32), pltpu.VMEM((tg, GLA_W), F32)],
        compiler_params=_cparams(2), name="gla")(zg, la, gn)


def _attn_kernel(q_ref, k_ref, v_ref, o_ref, m_ref, acc_ref):
    tq = q_ref.shape[0]
    ts = ATTN_SUB
    i = pl.program_id(2)
    m_ref[...] = jnp.full_like(m_ref, NEG_BIG)
    acc_ref[...] = jnp.zeros_like(acc_ref)

    def sub_block(hh, start, r0, mask_off):
        hs = slice(hh * HEAD_PAD, (hh + 1) * HEAD_PAD)
        kj = k_ref[pl.ds(start, ts), hs]
        vj = v_ref[pl.ds(start, ts), hs]
        s = _dot_nt(q_ref[r0:, hs], kj)
        if mask_off is not None:
            row = lax.broadcasted_iota(jnp.int32, s.shape, 0) + r0
            col = lax.broadcasted_iota(jnp.int32, s.shape, 1) + mask_off
            s = jnp.where(col <= row, s, NEG_BIG)
        m_old = m_ref[hh, r0:, :]
        parts = [s[:, c * 128:(c + 1) * 128] for c in range(ts // 128)]
        m_new = jnp.maximum(m_old, jnp.max(functools.reduce(jnp.maximum, parts), axis=-1, keepdims=True))
        p = jnp.concatenate([jnp.exp2((x - m_new).astype(BF16)) for x in parts], axis=1)
        acc_ref[hh, r0:, :] = jnp.exp2(m_old - m_new) * acc_ref[hh, r0:, :] + _dot(p, vj)
        m_ref[hh, r0:, :] = m_new

    def body(j, carry):
        base = pl.multiple_of(j * tq, tq)
        for sb in range(tq // ts):
            for hh in range(2):
                sub_block(hh, base + sb * ts, 0, None)
        return carry

    lax.fori_loop(0, i, body, 0)
    base = pl.multiple_of(i * tq, tq)
    for sb in range(tq // ts):
        for hh in range(2):
            sub_block(hh, base + sb * ts, sb * ts, sb * ts)
    outs = []
    for hh in range(2):
        a = acc_ref[hh]
        outs.append(a / a[:, MLA_V:MLA_V + 1])
    lane = lax.broadcasted_iota(jnp.int32, (tq, HEAD_PAD), 1)
    o_ref[...] = jnp.where(lane < MLA_V, outs[0], pltpu.roll(outs[1], MLA_V, 1)).astype(BF16)


def _attn(q, k, v, batch, seq_len):
    T = q.shape[0]
    tq = ATTN_TILE
    nq = seq_len // tq
    pairs = MLA_HEADS // 2
    return pl.pallas_call(
        _attn_kernel, grid=(batch, pairs, nq),
        in_specs=[pl.BlockSpec((tq, 2 * HEAD_PAD), lambda b, p, i: (b * nq + i, p)),
                  pl.BlockSpec((seq_len, 2 * HEAD_PAD), lambda b, p, i: (b, p)),
                  pl.BlockSpec((seq_len, 2 * HEAD_PAD), lambda b, p, i: (b, p))],
        out_specs=pl.BlockSpec((tq, 2 * MLA_V), lambda b, p, i: (b * nq + i, p)),
        out_shape=jax.ShapeDtypeStruct((T, MLA_W), BF16),
        scratch_shapes=[pltpu.VMEM((2, tq, HEAD_PAD), F32), pltpu.VMEM((2, tq, HEAD_PAD), F32)],
        compiler_params=_cparams(3), name="attn")(q, k, v)


def _out_router_kernel(h_ref, yg_ref, ym_ref, yp_ref, wo_ref, fn_ref, rw_ref, rb_ref,
                       h1_ref, hn0_ref, hn1_ref, idx_ref, gate_ref, cnt_ref, carry_ref):
    tm = h_ref.shape[0]

    @pl.when(pl.program_id(0) == 0)
    def _():
        carry_ref[...] = jnp.zeros_like(carry_ref)

    sub = ROUTER_SUB
    lane = lax.broadcasted_iota(jnp.int32, (sub, 128), 1)
    r_i = lax.broadcasted_iota(jnp.int32, (sub, sub), 0)
    c_i = lax.broadcasted_iota(jnp.int32, (sub, sub), 1)
    tri = (r_i >= c_i).astype(BF16)
    def project(st):
        rows = st["rows"]
        st["h1"] = (h_ref[rows, :] + _dot(yg_ref[rows, :], wo_ref[0:GLA_W, :])
                    + _dot(ym_ref[rows, :], wo_ref[GLA_W:GLA_W + MLA_W, :])
                    + _dot(yp_ref[rows, :], wo_ref[GLA_W + MLA_W:, :]))

    def normalize(st):
        rows = st["rows"]
        h1_ref[rows, :] = st["h1"]
        hn = _rms(st["h1"], fn_ref[...])
        st["hi"], st["lo"] = _split2(hn)
        packed = _pack_bf16_pairs(hn)
        slab = packed.shape[1] // DISPATCH_SLABS
        hn0_ref[rows, :] = packed[:, :slab]
        hn1_ref[rows, :] = packed[:, slab:]

    def score(st):
        r2 = _dot(st["hi"], rw_ref[...])
        st["logits"] = r2[:, :128] + r2[:, 128:] + _dot(st["lo"], rw_ref[:, 0:128]) + rb_ref[...]

    def select(st):
        rows = st["rows"]
        cur = jnp.where(lane < N_EXPERTS, st["logits"], NEG_BIG)
        idx_out = jnp.zeros((sub, 128), jnp.int32)
        val_out = jnp.zeros((sub, 128), F32)
        chosen = jnp.zeros((sub, 128), F32)
        top0 = None
        sels = []
        for kk in range(TOP_K):
            m = jnp.max(cur, axis=-1, keepdims=True)
            sel = jnp.min(jnp.where(cur == m, lane, 128), axis=-1, keepdims=True)
            if kk == 0:
                top0 = m
            sels.append(sel)
            idx_out = jnp.where(lane == kk, sel, idx_out)
            val_out = jnp.where(lane == kk, jnp.exp(m - top0), val_out)
            chosen = jnp.where(lane == sel, 1.0, chosen)
            cur = jnp.where(lane == sel, NEG_BIG, cur)
        gate_ref[rows, :] = val_out / jnp.sum(val_out, axis=-1, keepdims=True)

        incl = _dot(tri, chosen.astype(BF16))
        before = carry_ref[0:1, :] + incl - chosen
        for kk in range(TOP_K):
            rank = jnp.sum(jnp.where(lane == sels[kk], before, 0.0), axis=-1, keepdims=True)
            idx_out = jnp.where(lane == TOP_K + kk, rank.astype(jnp.int32), idx_out)
        idx_ref[rows, :] = idx_out
        carry_ref[...] = carry_ref[...] + incl[sub - 1:sub, :]

    _skewed([project, normalize, score, select], tm, sub)
    cnt_ref[...] = carry_ref[...].astype(jnp.int32)


def _out_router(h, yg, ym, yp, w):
    T = h.shape[0]
    tm = TOKEN_TILE
    row = lambda n: pl.BlockSpec((tm, n), lambda i: (i, 0))
    slab = D_MODEL // 2 // DISPATCH_SLABS
    ins = [h, yg, ym, yp, w["w_out"], w["ffn_norm"], w["router_w"], w["router_b"]]
    return pl.pallas_call(
        _out_router_kernel, grid=(T // tm,),
        in_specs=[row(D_MODEL), row(GLA_W), row(MLA_W), row(POOL_W)] + [_full(a.shape) for a in ins[4:]],
        out_specs=[row(D_MODEL), row(slab), row(slab), row(128), row(128), _full((8, 128))],
        out_shape=[jax.ShapeDtypeStruct((T, D_MODEL), F32), jax.ShapeDtypeStruct((T, slab), jnp.uint32),
                   jax.ShapeDtypeStruct((T, slab), jnp.uint32),
                   jax.ShapeDtypeStruct((T, 128), jnp.int32), jax.ShapeDtypeStruct((T, 128), F32),
                   jax.ShapeDtypeStruct((8, 128), jnp.int32)],
        scratch_shapes=[pltpu.VMEM((8, 128), F32)],
        compiler_params=_cparams(1), name="out_router")(*ins)


def _moe_kernel(be_ref, nb_ref, x0_ref, x1_ref, wg_ref, bg_ref, wu_ref, bu_ref, wd_ref, bd_ref,
                y_ref, wg_bf, wu_bf, wd_bf):
    i = pl.program_id(0)
    used = i < nb_ref[0]
    new_expert = (i == 0) | (be_ref[i] != be_ref[jnp.maximum(i - 1, 0)])

    @pl.when(used & new_expert)
    def _():
        for src, dst in ((wg_ref, wg_bf), (wu_ref, wu_bf), (wd_ref, wd_bf)):
            for r in range(0, src.shape[2], MOE_CAST_ROWS):
                dst[r:r + MOE_CAST_ROWS, :] = src[0, 0, r:r + MOE_CAST_ROWS, :].astype(BF16)

    @pl.when(used)
    def _():
        halves = [_unpack_bf16_pairs(r[...]) for r in (x0_ref, x1_ref)]
        x = jnp.concatenate([h[0] for h in halves] + [h[1] for h in halves], axis=1).astype(BF16)
        g = jnp.minimum(_dot(x, wg_bf[...]) + bg_ref[0], SWIGLU_LIMIT)
        up = jnp.clip(_dot(x, wu_bf[...]) + bu_ref[0], -SWIGLU_LIMIT, SWIGLU_LIMIT)
        hb = (up + 1.0) * (g / (1.0 + jnp.exp(-SWIGLU_ALPHA * g)))
        y_ref[...] = _pack_bf16_pairs(_dot(hb.astype(BF16), wd_bf[...]) + bd_ref[0])

    @pl.when(jnp.logical_not(used))
    def _():
        y_ref[...] = jnp.zeros_like(y_ref)


def _moe(xs, block_e, n_used, w):
    n_rows = xs[0].shape[0]
    bm = MOE_BLOCK
    layer = w["layer"]
    wspec = lambda shp: pl.BlockSpec((1, 1) + shp, lambda i, be, nb: (layer, be[i], 0, 0))
    bspec = lambda shp: pl.BlockSpec((1,) + shp, lambda i, be, nb: (be[i], 0, 0))
    grid_spec = pltpu.PrefetchScalarGridSpec(
        num_scalar_prefetch=2, grid=(n_rows // bm,),
        in_specs=[pl.BlockSpec((bm, D_MODEL // 2 // DISPATCH_SLABS), lambda i, be, nb: (i, 0))] * DISPATCH_SLABS + [
                  wspec((D_MODEL, D_FF)), bspec((1, D_FF)), wspec((D_MODEL, D_FF)), bspec((1, D_FF)),
                  wspec((D_FF, D_MODEL)), bspec((1, D_MODEL))],
        out_specs=pl.BlockSpec((bm, D_MODEL // 2), lambda i, be, nb: (i, 0)),
        scratch_shapes=[pltpu.VMEM((D_MODEL, D_FF), BF16), pltpu.VMEM((D_MODEL, D_FF), BF16),
                        pltpu.VMEM((D_FF, D_MODEL), BF16)])
    return pl.pallas_call(
        _moe_kernel, grid_spec=grid_spec,
        out_shape=jax.ShapeDtypeStruct((n_rows, D_MODEL // 2), jnp.uint32),
        compiler_params=_cparams(1), name="moe")(
            block_e, n_used, *xs, w["moe_w_gate"], w["moe_b_gate"], w["moe_w_up"], w["moe_b_up"],
            w["moe_w_down"], w["moe_b_down"])


def _ple_kernel(h1_ref, y0_ref, y1_ref, y2_ref, y3_ref, gate_ref, p_ref, wple_ref, gn_ref, wpg_ref, pn_ref, o_ref):
    def combine(st):
        rows = st["rows"]
        gates = gate_ref[rows, :]
        h2 = h1_ref[rows, :]
        for kk, y_ref in enumerate((y0_ref, y1_ref, y2_ref, y3_ref)):
            h2 = h2 + gates[:, kk:kk + 1] * jnp.concatenate(_unpack_bf16_pairs(y_ref[rows, :]), axis=1)
        st["h2"] = h2
        st["hn"] = _rms(h2, gn_ref[...]).astype(BF16)

    def project(st):
        st["e"] = _dot(p_ref[0, st["rows"], :].astype(BF16), wple_ref[...])
        st["a"] = _dot(st["hn"], wpg_ref[...])

    def finish(st):
        gate = 1.0 / (1.0 + jnp.exp(-st["a"]))
        o_ref[st["rows"], :] = st["h2"] + _rms(st["e"] * gate, pn_ref[...])

    _skewed([combine, project, finish], h1_ref.shape[0], PLE_SUB)


def _ple(h1, ys_k, gates, p, w, part):
    T = h1.shape[0]
    tm = TOKEN_TILE
    steps = T // COMBINE_PARTS // tm
    off = part * steps
    row = lambda n: pl.BlockSpec((tm, n), lambda i: (i + off, 0))
    local = pl.BlockSpec((tm, D_MODEL // 2), lambda i: (i, 0))
    weights = [w["ple_w_proj"], w["ple_gate_norm"], w["ple_w_gate"], w["ple_post_norm"]]
    layer = w["layer"]
    p_spec = pl.BlockSpec((1, tm, D_PLE), lambda i: (layer, i + off, 0))
    ins = [h1, *ys_k, gates, p, *weights]
    in_specs = [row(D_MODEL)] + [local] * TOP_K + [row(128), p_spec] + [_full(a.shape) for a in weights]
    return pl.pallas_call(
        _ple_kernel, grid=(steps,), in_specs=in_specs,
        out_specs=row(D_MODEL), out_shape=jax.ShapeDtypeStruct((T, D_MODEL), F32),
        input_output_aliases={0: 0}, compiler_params=_cparams(1), name="ple")(*ins)


def _pad_heads(wm, per_head, n_heads=MLA_HEADS):
    kdim = wm.shape[0]
    w3 = wm.reshape(kdim, n_heads, per_head)
    return jnp.pad(w3, ((0, 0), (0, 0), (0, HEAD_PAD - per_head))).reshape(kdim, n_heads * HEAD_PAD)


def _swap_rope_halves(a):
    a3 = a.reshape(a.shape[0], -1, HEAD_PAD)
    half = MLA_ROPE // 2
    x1 = a3[:, :, MLA_NOPE:MLA_NOPE + half]
    x2 = a3[:, :, MLA_NOPE + half:MLA_QK]
    out = jnp.zeros_like(a3).at[:, :, MLA_NOPE:MLA_NOPE + half].set(x2).at[:, :, MLA_NOPE + half:MLA_QK].set(x1)
    return out.reshape(a.shape)


def _layer_params(i, mix_norm, w_in, gla_w_gate, gla_b_gate, gla_out_norm, mla_q_norm, mla_w_uq, mla_kv_norm,
                  mla_w_ukv, mla_qk_q_norm, mla_qk_k_norm, pool_w, pool_scale, w_out, ffn_norm, router_w,
                  router_b, moe_w_gate, moe_b_gate, moe_w_up, moe_b_up, moe_w_down, moe_b_down,
                  ple_w_proj, ple_gate_norm, ple_w_gate, ple_post_norm):
    wi = w_in[i]
    c = np.cumsum((0, 128, 128, 256, 16, 256, 256, 128, 32, 256))
    gq, gk, gv, glow, gr, cq, ckv, krope, upool = [wi[:, c[j]:c[j + 1]] for j in range(9)]
    misc = jnp.concatenate([glow, krope, jnp.zeros((D_MODEL, 128 - 48), F32)], axis=1)
    w_in_p = jnp.concatenate([gq, gk, gv, gr, cq, upool, ckv, misc], axis=1).astype(BF16)
    wgate_p = jnp.zeros((128, GLA_K), F32).at[MISC_GLOW:MISC_GLOW + GLA_GATE_RANK].set(gla_w_gate[i]).astype(BF16)
    ukv = mla_w_ukv[i].reshape(MLA_KV_RANK, MLA_HEADS, MLA_NOPE + MLA_V)
    ukv_k = _pad_heads(ukv[:, :, :MLA_NOPE].reshape(MLA_KV_RANK, MLA_HEADS * MLA_NOPE), MLA_NOPE)
    ukv_v = _pad_heads(ukv[:, :, MLA_NOPE:].reshape(MLA_KV_RANK, MLA_W), MLA_V)
    pw = pool_w[i]
    pool_bd = jnp.zeros((POOL_W, POOL_W), F32)
    for g in range(4):
        pool_bd = pool_bd.at[g * 64:(g + 1) * 64, g * 64:(g + 1) * 64].set(pw[g])
    rw = jnp.pad(router_w[i], ((0, 0), (0, 128 - N_EXPERTS)))
    rw_hi = rw.astype(BF16)
    rw_lo = (rw - rw_hi.astype(F32)).astype(BF16)
    row = lambda a: a.reshape(1, -1)
    pad96 = lambda a: jnp.pad(a, (0, HEAD_PAD - MLA_QK)).reshape(1, HEAD_PAD)
    wuq_p = _pad_heads(mla_w_uq[i], MLA_QK)
    gq_p = pad96(mla_qk_q_norm[i] * (MLA_QK ** -0.5 * LOG2E))
    return {
        "mix_norm": row(mix_norm[i]), "w_in": w_in_p, "gla_w_gate": wgate_p, "gla_b_gate": row(gla_b_gate[i]),
        "gla_out_norm": row(jnp.tile(gla_out_norm[i], GLA_HEADS)),
        "mla_q_norm": row(mla_q_norm[i]),
        "mla_w_uq": jnp.concatenate([wuq_p, _swap_rope_halves(wuq_p)], axis=1).astype(BF16),
        "mla_kv_norm": row(mla_kv_norm[i]), "mla_w_ukv_k": ukv_k.astype(BF16), "mla_w_ukv_v": ukv_v.astype(BF16),
        "mla_gq": jnp.concatenate([gq_p, _swap_rope_halves(gq_p)], axis=0), "mla_gk": pad96(mla_qk_k_norm[i]),
        "pool_w": pool_bd.astype(BF16), "pool_scale": row(pool_scale[i]),
        "w_out": w_out[i].astype(BF16), "ffn_norm": row(ffn_norm[i]),
        "router_w": jnp.concatenate([rw_hi, rw_lo], axis=1),
        "router_b": row(jnp.pad(router_b[i], (0, 128 - N_EXPERTS))),
        "layer": i,
        "moe_w_gate": moe_w_gate, "moe_b_gate": moe_b_gate[i].reshape(N_EXPERTS, 1, D_FF),
        "moe_w_up": moe_w_up, "moe_b_up": moe_b_up[i].reshape(N_EXPERTS, 1, D_FF),
        "moe_w_down": moe_w_down, "moe_b_down": moe_b_down[i].reshape(N_EXPERTS, 1, D_MODEL),
        "ple_w_proj": ple_w_proj[i].astype(BF16), "ple_gate_norm": row(ple_gate_norm[i]),
        "ple_w_gate": ple_w_gate[i].astype(BF16), "ple_post_norm": row(ple_post_norm[i]),
    }


def _rope_tables(positions):
    T = positions.size
    inv = ROPE_BASE ** (-jnp.arange(0, MLA_ROPE, 2, dtype=F32) / MLA_ROPE)
    ang = positions.reshape(T, 1).astype(F32) * inv
    cos, sin = jnp.cos(ang), jnp.sin(ang)
    z16 = jnp.zeros((T, 16), F32)
    tail = jnp.zeros((T, HEAD_PAD - MLA_QK), F32)
    c = jnp.concatenate([jnp.ones((T, MLA_NOPE), F32), cos, cos, tail], axis=1)
    s1 = jnp.concatenate([jnp.zeros((T, MLA_NOPE), F32), -sin, z16, tail], axis=1)
    s2 = jnp.concatenate([jnp.zeros((T, MLA_NOPE), F32), z16, sin, tail], axis=1)
    return c, s1, s2


def _route(top_idx, rank, counts, T):
    bm = MOE_BLOCK
    A = T * TOP_K
    padded = (counts + bm - 1) // bm * bm
    pad_end = jnp.cumsum(padded)
    pad_start = pad_end - padded
    experts = jnp.arange(N_EXPERTS, dtype=jnp.int32)
    dest = rank + jnp.sum(jnp.where(top_idx[:, :, None] == experts, pad_start, 0), axis=-1)
    n_blocks = (A + N_EXPERTS * (bm - 1) + bm - 1) // bm
    n_rows = n_blocks * bm
    block_start = jnp.arange(n_blocks, dtype=jnp.int32) * bm
    block_e = jnp.minimum(jnp.sum((pad_end[None, :] <= block_start[:, None]).astype(jnp.int32), axis=1),
                          N_EXPERTS - 1)
    n_used = (pad_end[-1] // bm).astype(jnp.int32).reshape(1)
    return dest, n_rows, block_e, n_used


def _dispatch(hn_slabs, dest, n_rows):
    T, width = hn_slabs[0].shape
    win = DISPATCH_ROWS
    dest_t = dest.T
    mesh = plsc.VectorSubcoreMesh(core_axis_name="core", subcore_axis_name="subcore")

    @functools.partial(pl.kernel, out_type=jax.ShapeDtypeStruct((n_rows, width), hn_slabs[0].dtype), mesh=mesh,
                       scratch_types=[], name="dispatch")
    def scatter_rows(x_hbm, i_hbm, o_hbm):
        def body(x_vmem, i_vmem):
            for kk in range(TOP_K):
                pltpu.sync_copy(x_vmem, o_hbm.at[i_vmem.at[kk]])

        pltpu.emit_pipeline(
            body, grid=(T // win,),
            in_specs=[pl.BlockSpec((win, width), lambda i: (i, 0)), pl.BlockSpec((TOP_K, win), lambda i: (0, i))],
            out_specs=[], core_axis_name=("core", "subcore"),
            dimension_semantics=(pltpu.PARALLEL,))(x_hbm, i_hbm)

    return [scatter_rows(slab, dest_t) for slab in hn_slabs]


def kernel(x, p, positions, mix_norm, w_in, gla_w_gate, gla_b_gate, gla_out_norm, mla_q_norm, mla_w_uq,
           mla_kv_norm, mla_w_ukv, mla_qk_q_norm, mla_qk_k_norm, pool_w, pool_scale, w_out, ffn_norm,
           router_w, router_b, moe_w_gate, moe_b_gate, moe_w_up, moe_b_up, moe_w_down, moe_b_down,
           ple_w_proj, ple_gate_norm, ple_w_gate, ple_post_norm):
    B, S, D = x.shape
    T = B * S
    depth = p.shape[0]
    params = (mix_norm, w_in, gla_w_gate, gla_b_gate, gla_out_norm, mla_q_norm, mla_w_uq, mla_kv_norm,
              mla_w_ukv, mla_qk_q_norm, mla_qk_k_norm, pool_w, pool_scale, w_out, ffn_norm, router_w,
              router_b, moe_w_gate, moe_b_gate, moe_w_up, moe_b_up, moe_w_down, moe_b_down,
              ple_w_proj, ple_gate_norm, ple_w_gate, ple_post_norm)
    take = lambda a, idx: a.at[idx].get(mode="promise_in_bounds")
    rope_c, rope_s1, rope_s2 = _rope_tables(positions)
    p_flat = p.reshape(depth, T, D_PLE)
    h = x.reshape(T, D)
    for i in range(depth):
        w = _layer_params(i, *params)
        zg, la, q, k, v, y_pool = _mix_pre(h, w, rope_c, rope_s1, rope_s2, S)
        y_gla = _gla(zg, la, w["gla_out_norm"], B, S)
        y_mla = _attn(q, k, v, B, S)
        h1, hn0, hn1, route, gates, counts = _out_router(h, y_gla, y_mla, y_pool, w)
        dest, n_rows, block_e, n_used = _route(route[:, :TOP_K], route[:, TOP_K:2 * TOP_K],
                                               counts[0, :N_EXPERTS], T)
        ys = _moe(_dispatch([hn0, hn1], dest, n_rows), block_e, n_used, w)
        h = h1
        for part in range(COMBINE_PARTS):
            d = dest[part * (T // COMBINE_PARTS):(part + 1) * (T // COMBINE_PARTS)]
            h = _ple(h, [take(ys, d[:, kk]) for kk in range(TOP_K)], gates, p_flat, w, part)
    return h.reshape(B, S, D)
```

```python
import functools

import jax
import jax.numpy as jnp
import numpy as np
from jax import lax
from jax.experimental import pallas as pl
from jax.experimental.pallas import tpu as pltpu
from jax.experimental.pallas import tpu_sc as plsc

F32 = jnp.float32
BF16 = jnp.bfloat16

D_MODEL = 1024
EPS = 1e-6
D_PLE = 256

GLA_HEADS = 4
GLA_DK = 32
GLA_DV = 64
GLA_GATE_RANK = 16
GLA_TAU = 16.0
GLA_CHUNK = 64
GLA_K = GLA_HEADS * GLA_DK
GLA_W = GLA_HEADS * GLA_DV

MLA_HEADS = 8
MLA_Q_RANK = 256
MLA_KV_RANK = 128
MLA_NOPE = 64
MLA_ROPE = 32
MLA_QK = MLA_NOPE + MLA_ROPE
MLA_V = 64
MLA_W = MLA_HEADS * MLA_V
ROPE_BASE = 10000.0
HEAD_PAD = 128
MLA_QK_PAD = MLA_HEADS * HEAD_PAD

POOL_WINDOWS = (2, 4, 8, 16)
POOL_GROUP = 64
POOL_W = 256
POOL_HALO = 16

N_EXPERTS = 32
TOP_K = 4
D_FF = 1024
SWIGLU_LIMIT = 7.0
SWIGLU_ALPHA = 1.702

COL_GQ, COL_GK, COL_GV, COL_GR, COL_CQ, COL_POOL, COL_CKV, COL_MISC = 0, 128, 256, 512, 768, 1024, 1280, 1408
D_IN_PAD = 1536
MISC_GLOW = 0
MISC_ROPE = 16

LOG2E = 1.4426950408889634
TOKEN_TILE = 512
MIX_SUB, ROUTER_SUB, PLE_SUB = 256, 256, 128
GLA_TILE = 512
ATTN_TILE = 2048
ATTN_SUB = 512
MOE_BLOCK = 512
MOE_CAST_ROWS = 256
COMBINE_PARTS = 4
DISPATCH_ROWS = 128
DISPATCH_SLABS = 2
VMEM_LIMIT = 56 * 1024 * 1024
NEG_BIG = -1e30


def _cparams(n_axes, **flags):
    return pltpu.CompilerParams(dimension_semantics=("arbitrary",) * n_axes,
                                vmem_limit_bytes=VMEM_LIMIT, flags=flags or None)


def _rms(x, g):
    return x * lax.rsqrt(jnp.mean(x * x, axis=-1, keepdims=True) + EPS) * g


def _dot(a, b):
    return jnp.dot(a, b, preferred_element_type=F32)


def _dot_nt(a, b):
    return lax.dot_general(a, b, (((1,), (1,)), ((), ())), preferred_element_type=F32)


def _dot_tn(a, b):
    return lax.dot_general(a, b, (((0,), (0,)), ((), ())), preferred_element_type=F32)


def _split3(x):
    hi = x.astype(BF16)
    r = x - hi.astype(F32)
    mid = r.astype(BF16)
    lo = (r - mid.astype(F32)).astype(BF16)
    return hi, mid, lo


def _split2(x):
    hi = x.astype(BF16)
    lo = (x - hi.astype(F32)).astype(BF16)
    return hi, lo


def _pack_bf16_pairs(x):
    m = x.shape[1] // 2
    bits = lax.bitcast_convert_type(x.astype(BF16).astype(F32), jnp.uint32)
    return (bits[:, :m] >> 16) | (bits[:, m:] & jnp.uint32(0xFFFF0000))


def _unpack_bf16_pairs(w):
    lo = lax.bitcast_convert_type(w << 16, F32)
    hi = lax.bitcast_convert_type(w & jnp.uint32(0xFFFF0000), F32)
    return lo, hi


def _skewed(stages, n_rows, sub):
    states = [{"rows": slice(r0, r0 + sub)} for r0 in range(0, n_rows, sub)]
    for step in range(len(states) + len(stages) - 1):
        for s, stage in enumerate(stages):
            t = step - s
            if 0 <= t < len(states):
                stage(states[t])


def _full(shape):
    nd = len(shape)
    return pl.BlockSpec(shape, lambda *_: (0,) * nd)


def _rope(x, c, s1, s2):
    return x * c + pltpu.roll(x, HEAD_PAD - 16, 1) * s1 + pltpu.roll(x, 16, 1) * s2


def _mix_pre_kernel(h_ref, mixn_ref, win_ref, wgate_ref, bgate_ref, qn_ref, wuq_ref, kvn_ref,
                    wukvk_ref, wukvv_ref, gq_ref, gk_ref, rc_ref, rs1_ref, rs2_ref,
                    wpool_ref, pscale_ref,
                    zg_ref, la_ref, q_ref, k_ref, v_ref, yp_ref, carry_ref, *, tiles_per_seq):
    tm = h_ref.shape[0]
    sub = MIX_SUB
    seq_tile = pl.program_id(0) % tiles_per_seq

    @pl.when(seq_tile == 0)
    def _():
        carry_ref[...] = jnp.zeros_like(carry_ref)

    lane = lax.broadcasted_iota(jnp.int32, (sub, HEAD_PAD), 1)
    in_rope = (lane >= MLA_NOPE) & (lane < MLA_QK)
    lane_v = lax.broadcasted_iota(jnp.int32, (sub, MLA_QK_PAD), 1)
    ones_lane = lane_v % HEAD_PAD == MLA_V
    lane_p = lax.broadcasted_iota(jnp.int32, (sub, POOL_W), 1)
    row_p = lax.broadcasted_iota(jnp.int32, (sub, POOL_W), 0)
    g0, g1, g2 = lane_p < 64, lane_p < 128, lane_p < 192
    win = jnp.where(g0, 2.0, jnp.where(g1, 4.0, jnp.where(g2, 8.0, 16.0)))
    gq, gq_sw, gk = gq_ref[0:1, :], gq_ref[1:2, :], gk_ref[...]

    def norm_in(st):
        st["hn"] = _rms(h_ref[st["rows"], :], mixn_ref[...]).astype(BF16)

    def project_in(st):
        st["z"] = _dot(st["hn"], win_ref[...])

    def norm_latents(st):
        z = st["z"]
        zg_ref[st["rows"], :] = z[:, COL_GQ:COL_CQ]
        st["cqn"] = _rms(z[:, COL_CQ:COL_CQ + MLA_Q_RANK], qn_ref[...]).astype(BF16)
        st["ckvn"] = _rms(z[:, COL_CKV:COL_CKV + MLA_KV_RANK], kvn_ref[...]).astype(BF16)

    def project_up(st):
        zm = st["z"][:, COL_MISC:COL_MISC + 128]
        st["logit"] = _dot(zm.astype(BF16), wgate_ref[...]) + bgate_ref[...]
        st["qf"] = _dot(st["cqn"], wuq_ref[...])
        st["kn"] = _dot(st["ckvn"], wukvk_ref[...])
        st["v"] = _dot(st["ckvn"], wukvv_ref[...])

    def heads_and_pool(st):
        rows, z, qf, kn, logit = st["rows"], st["z"], st["qf"], st["kn"], st["logit"]
        zm = z[:, COL_MISC:COL_MISC + 128]
        la_ref[rows, :] = (jnp.minimum(logit, 0.0) - jnp.log(1.0 + jnp.exp(-jnp.abs(logit)))) * (1.0 / GLA_TAU)
        v_ref[rows, :] = jnp.where(ones_lane, 1.0, st["v"]).astype(BF16)

        rc, rs1, rs2 = rc_ref[rows, :], rs1_ref[rows, :], rs2_ref[rows, :]
        kr = jnp.where(in_rope, pltpu.roll(zm, MLA_NOPE - MISC_ROPE, 1), 0.0)
        kr_ss = jnp.sum(kr * kr, axis=-1, keepdims=True)
        krr = _rope(kr * gk, rc, rs1, rs2)
        cq = rc * gq
        sq_tab = (rs1 + rs2) * gq_sw
        for hh in range(MLA_HEADS):
            sl = slice(hh * HEAD_PAD, (hh + 1) * HEAD_PAD)
            qh = qf[:, sl]
            qsw = qf[:, MLA_QK_PAD + hh * HEAD_PAD:MLA_QK_PAD + (hh + 1) * HEAD_PAD]
            sq = lax.rsqrt(jnp.sum(qh * qh, axis=-1, keepdims=True) * (1.0 / MLA_QK) + EPS)
            q_ref[rows, sl] = ((qh * cq + qsw * sq_tab) * sq).astype(BF16)
            kh = kn[:, sl]
            sk = lax.rsqrt((jnp.sum(kh * kh, axis=-1, keepdims=True) + kr_ss) * (1.0 / MLA_QK) + EPS)
            k_ref[rows, sl] = (sk * (kh * gk + krr)).astype(BF16)

        u = z[:, COL_POOL:COL_POOL + POOL_W]
        xe = jnp.concatenate([carry_ref[...], u], axis=0)
        carry_ref[...] = u[sub - POOL_HALO:, :]
        s2 = xe + pltpu.roll(xe, 1, 0)
        s4 = s2 + pltpu.roll(s2, 2, 0)
        s8 = s4 + pltpu.roll(s4, 4, 0)
        s16 = s8 + pltpu.roll(s8, 8, 0)
        pooled = jnp.where(g0, s2[POOL_HALO:], jnp.where(g1, s4[POOL_HALO:],
                           jnp.where(g2, s8[POOL_HALO:], s16[POOL_HALO:])))
        cnt = jnp.minimum((seq_tile * tm + rows.start + row_p + 1).astype(F32), win)
        st["d"] = (pooled / cnt - u).astype(BF16)

    def project_pool(st):
        yp_ref[st["rows"], :] = (_dot(st["d"], wpool_ref[...]) * pscale_ref[...]).astype(BF16)

    _skewed([norm_in, project_in, norm_latents, project_up, heads_and_pool, project_pool], tm, sub)


def _mix_pre(h, w, rope_c, rope_s1, rope_s2, seq_len):
    T = h.shape[0]
    tm = TOKEN_TILE
    row = lambda n: pl.BlockSpec((tm, n), lambda i: (i, 0))
    ins = [h, w["mix_norm"], w["w_in"], w["gla_w_gate"], w["gla_b_gate"], w["mla_q_norm"], w["mla_w_uq"],
           w["mla_kv_norm"], w["mla_w_ukv_k"], w["mla_w_ukv_v"], w["mla_gq"], w["mla_gk"],
           rope_c, rope_s1, rope_s2, w["pool_w"], w["pool_scale"]]
    in_specs = [row(D_MODEL)] + [_full(a.shape) for a in ins[1:12]] + [row(HEAD_PAD)] * 3 + \
               [_full(w["pool_w"].shape), _full(w["pool_scale"].shape)]
    out_shape = [jax.ShapeDtypeStruct((T, COL_CQ), F32), jax.ShapeDtypeStruct((T, GLA_K), F32),
                 jax.ShapeDtypeStruct((T, MLA_QK_PAD), BF16), jax.ShapeDtypeStruct((T, MLA_QK_PAD), BF16),
                 jax.ShapeDtypeStruct((T, MLA_QK_PAD), BF16), jax.ShapeDtypeStruct((T, POOL_W), BF16)]
    out_specs = [row(COL_CQ), row(GLA_K), row(MLA_QK_PAD), row(MLA_QK_PAD), row(MLA_QK_PAD), row(POOL_W)]
    return pl.pallas_call(
        functools.partial(_mix_pre_kernel, tiles_per_seq=seq_len // tm),
        grid=(T // tm,), in_specs=in_specs, out_specs=out_specs, out_shape=out_shape,
        scratch_shapes=[pltpu.VMEM((POOL_HALO, POOL_W), F32)],
        compiler_params=_cparams(1), name="mix_pre")(*ins)


def _gla_kernel(zg_ref, la_ref, gn_ref, y_ref, state_ref, o_ref):
    tg = zg_ref.shape[0]
    C = GLA_CHUNK

    @pl.when(pl.program_id(1) == 0)
    def _():
        state_ref[...] = jnp.zeros_like(state_ref)

    r_i = lax.broadcasted_iota(jnp.int32, (C, C), 0)
    c_i = lax.broadcasted_iota(jnp.int32, (C, C), 1)
    tri = (r_i >= c_i).astype(BF16)
    ones = jnp.ones((C, GLA_W), BF16)
    head_k = lax.broadcasted_iota(jnp.int32, (C, GLA_K), 1) // GLA_DK
    head_v = lax.broadcasted_iota(jnp.int32, (C, GLA_W), 1) // GLA_DV
    ar = lax.broadcasted_iota(jnp.int32, (GLA_HEADS * C, C), 0)
    ac = lax.broadcasted_iota(jnp.int32, (GLA_HEADS * C, C), 1)
    causal = (ar % C) >= ac
    sk = lax.broadcasted_iota(jnp.int32, (GLA_K, GLA_W), 0) // GLA_DK
    sv = lax.broadcasted_iota(jnp.int32, (GLA_K, GLA_W), 1) // GLA_DV
    blockdiag = sk == sv

    def log_decay(st):
        la3 = _split3(la_ref[st["rows"], :])
        st["bc"] = _dot(tri, la3[0]) + _dot(tri, la3[1]) + _dot(tri, la3[2])
        st["dsum"] = _dot_tn(la3[0], ones) + _dot_tn(la3[1], ones) + _dot_tn(la3[2], ones)

    def scores(st):
        rows, bc = st["rows"], st["bc"]
        q = zg_ref[rows, COL_GQ:COL_GQ + GLA_K] * (GLA_DK ** -0.5)
        k = zg_ref[rows, COL_GK:COL_GK + GLA_K]
        b_last = bc[C - 1:C, :]
        q_dec = (q * jnp.exp(bc)).astype(BF16)
        k_dec = (k * jnp.exp(-bc)).astype(BF16)
        st["k_end"] = (k * jnp.exp(b_last - bc)).astype(BF16)
        st["decay"] = jnp.exp(st["dsum"])
        zero = jnp.zeros_like(q_dec)
        qs = jnp.concatenate([jnp.where(head_k == hh, q_dec, zero) for hh in range(GLA_HEADS)], axis=0)
        st["q_dec"] = q_dec
        st["att"] = _dot_nt(qs, k_dec)

    def values(st):
        v = zg_ref[st["rows"], COL_GV:COL_GV + GLA_W].astype(BF16)
        st["o_full"] = _dot(jnp.where(causal, st["att"], 0.0).astype(BF16), v)
        st["upd"] = jnp.where(blockdiag, _dot_tn(st["k_end"], v), 0.0)

    state = [state_ref[...]]

    def recur(st):
        o_full = st["o_full"]
        o = _dot(st["q_dec"], state[0].astype(BF16))
        for hh in range(GLA_HEADS):
            o = o + jnp.where(head_v == hh, o_full[hh * C:(hh + 1) * C, :], 0.0)
        o_ref[st["rows"], :] = o
        state[0] = st["decay"] * state[0] + st["upd"]

    _skewed([log_decay, scores, values, recur], tg, C)
    state_ref[...] = state[0]

    o = o_ref[...]
    gr = lax.broadcasted_iota(jnp.int32, (GLA_W, GLA_W), 0) // GLA_DV
    gc = lax.broadcasted_iota(jnp.int32, (GLA_W, GLA_W), 1) // GLA_DV
    group = (gr == gc).astype(BF16)
    oo = _split2(o * o)
    ms = (_dot(oo[0], group) + _dot(oo[1], group)) * (1.0 / GLA_DV)
    r = zg_ref[:, COL_GR:COL_GR + GLA_W]
    y = o * lax.rsqrt(ms + EPS) * gn_ref[...] * (r / (1.0 + jnp.exp(-r)))
    y_ref[...] = y.astype(BF16)


def _gla(zg, la, gn, batch, seq_len):
    T = zg.shape[0]
    tg = GLA_TILE
    nt = seq_len // tg
    return pl.pallas_call(
        _gla_kernel, grid=(batch, nt),
        in_specs=[pl.BlockSpec((tg, COL_CQ), lambda b, s: (b * nt + s, 0)),
                  pl.BlockSpec((tg, GLA_K), lambda b, s: (b * nt + s, 0)),
                  _full(gn.shape)],
        out_specs=pl.BlockSpec((tg, GLA_W), lambda b, s: (b * nt + s, 0)),
        out_shape=jax.ShapeDtypeStruct((T, GLA_W), BF16),
        scratch_shapes=[pltpu.VMEM((GLA_K, GLA_W), F32), pltpu.VMEM((tg, GLA_W), F32)],
        compiler_params=_cparams(2), name="gla")(zg, la, gn)


def _attn_kernel(q_ref, k_ref, v_ref, o_ref, m_ref, acc_ref):
    tq = q_ref.shape[0]
    ts = ATTN_SUB
    i = pl.program_id(2)
    m_ref[...] = jnp.full_like(m_ref, NEG_BIG)
    acc_ref[...] = jnp.zeros_like(acc_ref)

    def sub_block(hh, start, r0, mask_off):
        hs = slice(hh * HEAD_PAD, (hh + 1) * HEAD_PAD)
        kj = k_ref[pl.ds(start, ts), hs]
        vj = v_ref[pl.ds(start, ts), hs]
        s = _dot_nt(q_ref[r0:, hs], kj)
        if mask_off is not None:
            row = lax.broadcasted_iota(jnp.int32, s.shape, 0) + r0
            col = lax.broadcasted_iota(jnp.int32, s.shape, 1) + mask_off
            s = jnp.where(col <= row, s, NEG_BIG)
        m_old = m_ref[hh, r0:, :]
        parts = [s[:, c * 128:(c + 1) * 128] for c in range(ts // 128)]
        m_new = jnp.maximum(m_old, jnp.max(functools.reduce(jnp.maximum, parts), axis=-1, keepdims=True))
        p = jnp.concatenate([jnp.exp2((x - m_new).astype(BF16)) for x in parts], axis=1)
        acc_ref[hh, r0:, :] = jnp.exp2(m_old - m_new) * acc_ref[hh, r0:, :] + _dot(p, vj)
        m_ref[hh, r0:, :] = m_new

    def body(j, carry):
        base = pl.multiple_of(j * tq, tq)
        for sb in range(tq // ts):
            for hh in range(2):
                sub_block(hh, base + sb * ts, 0, None)
        return carry

    lax.fori_loop(0, i, body, 0)
    base = pl.multiple_of(i * tq, tq)
    for sb in range(tq // ts):
        for hh in range(2):
            sub_block(hh, base + sb * ts, sb * ts, sb * ts)
    outs = []
    for hh in range(2):
        a = acc_ref[hh]
        outs.append(a / a[:, MLA_V:MLA_V + 1])
    lane = lax.broadcasted_iota(jnp.int32, (tq, HEAD_PAD), 1)
    o_ref[...] = jnp.where(lane < MLA_V, outs[0], pltpu.roll(outs[1], MLA_V, 1)).astype(BF16)


def _attn(q, k, v, batch, seq_len):
    T = q.shape[0]
    tq = ATTN_TILE
    nq = seq_len // tq
    pairs = MLA_HEADS // 2
    return pl.pallas_call(
        _attn_kernel, grid=(batch, pairs, nq),
        in_specs=[pl.BlockSpec((tq, 2 * HEAD_PAD), lambda b, p, i: (b * nq + i, p)),
                  pl.BlockSpec((seq_len, 2 * HEAD_PAD), lambda b, p, i: (b, p)),
                  pl.BlockSpec((seq_len, 2 * HEAD_PAD), lambda b, p, i: (b, p))],
        out_specs=pl.BlockSpec((tq, 2 * MLA_V), lambda b, p, i: (b * nq + i, p)),
        out_shape=jax.ShapeDtypeStruct((T, MLA_W), BF16),
        scratch_shapes=[pltpu.VMEM((2, tq, HEAD_PAD), F32), pltpu.VMEM((2, tq, HEAD_PAD), F32)],
        compiler_params=_cparams(3), name="attn")(q, k, v)


def _out_router_kernel(h_ref, yg_ref, ym_ref, yp_ref, wo_ref, fn_ref, rw_ref, rb_ref,
                       h1_ref, hn0_ref, hn1_ref, idx_ref, gate_ref, cnt_ref, carry_ref):
    tm = h_ref.shape[0]

    @pl.when(pl.program_id(0) == 0)
    def _():
        carry_ref[...] = jnp.zeros_like(carry_ref)

    sub = ROUTER_SUB
    lane = lax.broadcasted_iota(jnp.int32, (sub, 128), 1)
    r_i = lax.broadcasted_iota(jnp.int32, (sub, sub), 0)
    c_i = lax.broadcasted_iota(jnp.int32, (sub, sub), 1)
    tri = (r_i >= c_i).astype(BF16)
    def project(st):
        rows = st["rows"]
        st["h1"] = (h_ref[rows, :] + _dot(yg_ref[rows, :], wo_ref[0:GLA_W, :])
                    + _dot(ym_ref[rows, :], wo_ref[GLA_W:GLA_W + MLA_W, :])
                    + _dot(yp_ref[rows, :], wo_ref[GLA_W + MLA_W:, :]))

    def normalize(st):
        rows = st["rows"]
        h1_ref[rows, :] = st["h1"]
        hn = _rms(st["h1"], fn_ref[...])
        st["hi"], st["lo"] = _split2(hn)
        packed = _pack_bf16_pairs(hn)
        slab = packed.shape[1] // DISPATCH_SLABS
        hn0_ref[rows, :] = packed[:, :slab]
        hn1_ref[rows, :] = packed[:, slab:]

    def score(st):
        r2 = _dot(st["hi"], rw_ref[...])
        st["logits"] = r2[:, :128] + r2[:, 128:] + _dot(st["lo"], rw_ref[:, 0:128]) + rb_ref[...]

    def select(st):
        rows = st["rows"]
        cur = jnp.where(lane < N_EXPERTS, st["logits"], NEG_BIG)
        idx_out = jnp.zeros((sub, 128), jnp.int32)
        val_out = jnp.zeros((sub, 128), F32)
        chosen = jnp.zeros((sub, 128), F32)
        top0 = None
        sels = []
        for kk in range(TOP_K):
            m = jnp.max(cur, axis=-1, keepdims=True)
            sel = jnp.min(jnp.where(cur == m, lane, 128), axis=-1, keepdims=True)
            if kk == 0:
                top0 = m
            sels.append(sel)
            idx_out = jnp.where(lane == kk, sel, idx_out)
            val_out = jnp.where(lane == kk, jnp.exp(m - top0), val_out)
            chosen = jnp.where(lane == sel, 1.0, chosen)
            cur = jnp.where(lane == sel, NEG_BIG, cur)
        gate_ref[rows, :] = val_out / jnp.sum(val_out, axis=-1, keepdims=True)

        incl = _dot(tri, chosen.astype(BF16))
        before = carry_ref[0:1, :] + incl - chosen
        for kk in range(TOP_K):
            rank = jnp.sum(jnp.where(lane == sels[kk], before, 0.0), axis=-1, keepdims=True)
            idx_out = jnp.where(lane == TOP_K + kk, rank.astype(jnp.int32), idx_out)
        idx_ref[rows, :] = idx_out
        carry_ref[...] = carry_ref[...] + incl[sub - 1:sub, :]

    _skewed([project, normalize, score, select], tm, sub)
    cnt_ref[...] = carry_ref[...].astype(jnp.int32)


def _out_router(h, yg, ym, yp, w):
    T = h.shape[0]
    tm = TOKEN_TILE
    row = lambda n: pl.BlockSpec((tm, n), lambda i: (i, 0))
    slab = D_MODEL // 2 // DISPATCH_SLABS
    ins = [h, yg, ym, yp, w["w_out"], w["ffn_norm"], w["router_w"], w["router_b"]]
    return pl.pallas_call(
        _out_router_kernel, grid=(T // tm,),
        in_specs=[row(D_MODEL), row(GLA_W), row(MLA_W), row(POOL_W)] + [_full(a.shape) for a in ins[4:]],
        out_specs=[row(D_MODEL), row(slab), row(slab), row(128), row(128), _full((8, 128))],
        out_shape=[jax.ShapeDtypeStruct((T, D_MODEL), F32), jax.ShapeDtypeStruct((T, slab), jnp.uint32),
                   jax.ShapeDtypeStruct((T, slab), jnp.uint32),
                   jax.ShapeDtypeStruct((T, 128), jnp.int32), jax.ShapeDtypeStruct((T, 128), F32),
                   jax.ShapeDtypeStruct((8, 128), jnp.int32)],
        scratch_shapes=[pltpu.VMEM((8, 128), F32)],
        compiler_params=_cparams(1), name="out_router")(*ins)


def _moe_kernel(be_ref, nb_ref, x0_ref, x1_ref, wg_ref, bg_ref, wu_ref, bu_ref, wd_ref, bd_ref,
                y0_ref, y1_ref, wg_bf, wu_bf, wd_bf):
    i = pl.program_id(0)
    used = i < nb_ref[0]
    new_expert = (i == 0) | (be_ref[i] != be_ref[jnp.maximum(i - 1, 0)])

    @pl.when(used & new_expert)
    def _():
        for src, dst in ((wg_ref, wg_bf), (wu_ref, wu_bf), (wd_ref, wd_bf)):
            for r in range(0, src.shape[2], MOE_CAST_ROWS):
                dst[r:r + MOE_CAST_ROWS, :] = src[0, 0, r:r + MOE_CAST_ROWS, :].astype(BF16)

    @pl.when(used)
    def _():
        halves = [_unpack_bf16_pairs(r[...]) for r in (x0_ref, x1_ref)]
        x = jnp.concatenate([h[0] for h in halves] + [h[1] for h in halves], axis=1).astype(BF16)
        g = jnp.minimum(_dot(x, wg_bf[...]) + bg_ref[0], SWIGLU_LIMIT)
        up = jnp.clip(_dot(x, wu_bf[...]) + bu_ref[0], -SWIGLU_LIMIT, SWIGLU_LIMIT)
        hb = (up + 1.0) * (g / (1.0 + jnp.exp(-SWIGLU_ALPHA * g)))
        packed = _pack_bf16_pairs(_dot(hb.astype(BF16), wd_bf[...]) + bd_ref[0])
        slab = y0_ref.shape[1]
        y0_ref[...] = packed[:, :slab]
        y1_ref[...] = packed[:, slab:]

    @pl.when(jnp.logical_not(used))
    def _():
        y0_ref[...] = jnp.zeros_like(y0_ref)
        y1_ref[...] = jnp.zeros_like(y1_ref)


def _moe(xs, block_e, n_used, w):
    n_rows, slab = xs[0].shape
    bm = MOE_BLOCK
    layer = w["layer"]
    wspec = lambda shp: pl.BlockSpec((1, 1) + shp, lambda i, be, nb: (layer, be[i], 0, 0))
    bspec = lambda shp: pl.BlockSpec((1,) + shp, lambda i, be, nb: (be[i], 0, 0))
    grid_spec = pltpu.PrefetchScalarGridSpec(
        num_scalar_prefetch=2, grid=(n_rows // bm,),
        in_specs=[pl.BlockSpec((bm, slab), lambda i, be, nb: (i, 0))] * DISPATCH_SLABS + [
                  wspec((D_MODEL, D_FF)), bspec((1, D_FF)), wspec((D_MODEL, D_FF)), bspec((1, D_FF)),
                  wspec((D_FF, D_MODEL)), bspec((1, D_MODEL))],
        out_specs=[pl.BlockSpec((bm, slab), lambda i, be, nb: (i, 0))] * DISPATCH_SLABS,
        scratch_shapes=[pltpu.VMEM((D_MODEL, D_FF), BF16), pltpu.VMEM((D_MODEL, D_FF), BF16),
                        pltpu.VMEM((D_FF, D_MODEL), BF16)])
    return pl.pallas_call(
        _moe_kernel, grid_spec=grid_spec,
        out_shape=[jax.ShapeDtypeStruct((n_rows, slab), jnp.uint32)] * DISPATCH_SLABS,
        compiler_params=_cparams(1), name="moe")(
            block_e, n_used, *xs, w["moe_w_gate"], w["moe_b_gate"], w["moe_w_up"], w["moe_b_up"],
            w["moe_w_down"], w["moe_b_down"])


def _ple_kernel(h1_ref, ya0_ref, ya1_ref, ya2_ref, ya3_ref, yb0_ref, yb1_ref, yb2_ref, yb3_ref, gate_ref, p_ref,
                wple_ref, gn_ref, wpg_ref, pn_ref, o_ref):
    def combine(st):
        rows = st["rows"]
        gates = gate_ref[rows, :]
        h2 = h1_ref[rows, :]
        for kk, (ya_ref, yb_ref) in enumerate(((ya0_ref, yb0_ref), (ya1_ref, yb1_ref), (ya2_ref, yb2_ref),
                                               (ya3_ref, yb3_ref))):
            lo_a, hi_a = _unpack_bf16_pairs(ya_ref[rows, :])
            lo_b, hi_b = _unpack_bf16_pairs(yb_ref[rows, :])
            h2 = h2 + gates[:, kk:kk + 1] * jnp.concatenate([lo_a, lo_b, hi_a, hi_b], axis=1)
        st["h2"] = h2
        st["hn"] = _rms(h2, gn_ref[...]).astype(BF16)

    def project(st):
        st["e"] = _dot(p_ref[0, st["rows"], :].astype(BF16), wple_ref[...])
        st["a"] = _dot(st["hn"], wpg_ref[...])

    def finish(st):
        gate = 1.0 / (1.0 + jnp.exp(-st["a"]))
        o_ref[st["rows"], :] = st["h2"] + _rms(st["e"] * gate, pn_ref[...])

    _skewed([combine, project, finish], h1_ref.shape[0], PLE_SUB)


def _ple(h1, y_slabs, gates, p, w, part):
    T = h1.shape[0]
    tm = TOKEN_TILE
    steps = T // COMBINE_PARTS // tm
    off = part * steps
    row = lambda n: pl.BlockSpec((tm, n), lambda i: (i + off, 0))
    slab = y_slabs[0].shape[1]
    gathered = lambda kk: pl.BlockSpec((tm, slab), lambda i: (kk * steps + i, 0))
    weights = [w["ple_w_proj"], w["ple_gate_norm"], w["ple_w_gate"], w["ple_post_norm"]]
    layer = w["layer"]
    p_spec = pl.BlockSpec((1, tm, D_PLE), lambda i: (layer, i + off, 0))
    ins = [h1] + [y for y in y_slabs for _ in range(TOP_K)] + [gates, p, *weights]
    in_specs = ([row(D_MODEL)] + [gathered(kk) for _ in y_slabs for kk in range(TOP_K)] + [row(128), p_spec]
                + [_full(a.shape) for a in weights])
    return pl.pallas_call(
        _ple_kernel, grid=(steps,), in_specs=in_specs,
        out_specs=row(D_MODEL), out_shape=jax.ShapeDtypeStruct((T, D_MODEL), F32),
        input_output_aliases={0: 0}, compiler_params=_cparams(1), name="ple")(*ins)


def _pad_heads(wm, per_head, n_heads=MLA_HEADS):
    kdim = wm.shape[0]
    w3 = wm.reshape(kdim, n_heads, per_head)
    return jnp.pad(w3, ((0, 0), (0, 0), (0, HEAD_PAD - per_head))).reshape(kdim, n_heads * HEAD_PAD)


def _swap_rope_halves(a):
    a3 = a.reshape(a.shape[0], -1, HEAD_PAD)
    half = MLA_ROPE // 2
    x1 = a3[:, :, MLA_NOPE:MLA_NOPE + half]
    x2 = a3[:, :, MLA_NOPE + half:MLA_QK]
    out = jnp.zeros_like(a3).at[:, :, MLA_NOPE:MLA_NOPE + half].set(x2).at[:, :, MLA_NOPE + half:MLA_QK].set(x1)
    return out.reshape(a.shape)


def _layer_params(i, mix_norm, w_in, gla_w_gate, gla_b_gate, gla_out_norm, mla_q_norm, mla_w_uq, mla_kv_norm,
                  mla_w_ukv, mla_qk_q_norm, mla_qk_k_norm, pool_w, pool_scale, w_out, ffn_norm, router_w,
                  router_b, moe_w_gate, moe_b_gate, moe_w_up, moe_b_up, moe_w_down, moe_b_down,
                  ple_w_proj, ple_gate_norm, ple_w_gate, ple_post_norm):
    wi = w_in[i]
    c = np.cumsum((0, 128, 128, 256, 16, 256, 256, 128, 32, 256))
    gq, gk, gv, glow, gr, cq, ckv, krope, upool = [wi[:, c[j]:c[j + 1]] for j in range(9)]
    misc = jnp.concatenate([glow, krope, jnp.zeros((D_MODEL, 128 - 48), F32)], axis=1)
    w_in_p = jnp.concatenate([gq, gk, gv, gr, cq, upool, ckv, misc], axis=1).astype(BF16)
    wgate_p = jnp.zeros((128, GLA_K), F32).at[MISC_GLOW:MISC_GLOW + GLA_GATE_RANK].set(gla_w_gate[i]).astype(BF16)
    ukv = mla_w_ukv[i].reshape(MLA_KV_RANK, MLA_HEADS, MLA_NOPE + MLA_V)
    ukv_k = _pad_heads(ukv[:, :, :MLA_NOPE].reshape(MLA_KV_RANK, MLA_HEADS * MLA_NOPE), MLA_NOPE)
    ukv_v = _pad_heads(ukv[:, :, MLA_NOPE:].reshape(MLA_KV_RANK, MLA_W), MLA_V)
    pw = pool_w[i]
    pool_bd = jnp.zeros((POOL_W, POOL_W), F32)
    for g in range(4):
        pool_bd = pool_bd.at[g * 64:(g + 1) * 64, g * 64:(g + 1) * 64].set(pw[g])
    rw = jnp.pad(router_w[i], ((0, 0), (0, 128 - N_EXPERTS)))
    rw_hi = rw.astype(BF16)
    rw_lo = (rw - rw_hi.astype(F32)).astype(BF16)
    row = lambda a: a.reshape(1, -1)
    pad96 = lambda a: jnp.pad(a, (0, HEAD_PAD - MLA_QK)).reshape(1, HEAD_PAD)
    wuq_p = _pad_heads(mla_w_uq[i], MLA_QK)
    gq_p = pad96(mla_qk_q_norm[i] * (MLA_QK ** -0.5 * LOG2E))
    return {
        "mix_norm": row(mix_norm[i]), "w_in": w_in_p, "gla_w_gate": wgate_p, "gla_b_gate": row(gla_b_gate[i]),
        "gla_out_norm": row(jnp.tile(gla_out_norm[i], GLA_HEADS)),
        "mla_q_norm": row(mla_q_norm[i]),
        "mla_w_uq": jnp.concatenate([wuq_p, _swap_rope_halves(wuq_p)], axis=1).astype(BF16),
        "mla_kv_norm": row(mla_kv_norm[i]), "mla_w_ukv_k": ukv_k.astype(BF16), "mla_w_ukv_v": ukv_v.astype(BF16),
        "mla_gq": jnp.concatenate([gq_p, _swap_rope_halves(gq_p)], axis=0), "mla_gk": pad96(mla_qk_k_norm[i]),
        "pool_w": pool_bd.astype(BF16), "pool_scale": row(pool_scale[i]),
        "w_out": w_out[i].astype(BF16), "ffn_norm": row(ffn_norm[i]),
        "router_w": jnp.concatenate([rw_hi, rw_lo], axis=1),
        "router_b": row(jnp.pad(router_b[i], (0, 128 - N_EXPERTS))),
        "layer": i,
        "moe_w_gate": moe_w_gate, "moe_b_gate": moe_b_gate[i].reshape(N_EXPERTS, 1, D_FF),
        "moe_w_up": moe_w_up, "moe_b_up": moe_b_up[i].reshape(N_EXPERTS, 1, D_FF),
        "moe_w_down": moe_w_down, "moe_b_down": moe_b_down[i].reshape(N_EXPERTS, 1, D_MODEL),
        "ple_w_proj": ple_w_proj[i].astype(BF16), "ple_gate_norm": row(ple_gate_norm[i]),
        "ple_w_gate": ple_w_gate[i].astype(BF16), "ple_post_norm": row(ple_post_norm[i]),
    }


def _rope_tables(positions):
    T = positions.size
    inv = ROPE_BASE ** (-jnp.arange(0, MLA_ROPE, 2, dtype=F32) / MLA_ROPE)
    ang = positions.reshape(T, 1).astype(F32) * inv
    cos, sin = jnp.cos(ang), jnp.sin(ang)
    z16 = jnp.zeros((T, 16), F32)
    tail = jnp.zeros((T, HEAD_PAD - MLA_QK), F32)
    c = jnp.concatenate([jnp.ones((T, MLA_NOPE), F32), cos, cos, tail], axis=1)
    s1 = jnp.concatenate([jnp.zeros((T, MLA_NOPE), F32), -sin, z16, tail], axis=1)
    s2 = jnp.concatenate([jnp.zeros((T, MLA_NOPE), F32), z16, sin, tail], axis=1)
    return c, s1, s2


def _route(top_idx, rank, counts, T):
    bm = MOE_BLOCK
    A = T * TOP_K
    padded = (counts + bm - 1) // bm * bm
    pad_end = jnp.cumsum(padded)
    pad_start = pad_end - padded
    experts = jnp.arange(N_EXPERTS, dtype=jnp.int32)
    dest = rank + jnp.sum(jnp.where(top_idx[:, :, None] == experts, pad_start, 0), axis=-1)
    n_blocks = (A + N_EXPERTS * (bm - 1) + bm - 1) // bm
    n_rows = n_blocks * bm
    block_start = jnp.arange(n_blocks, dtype=jnp.int32) * bm
    block_e = jnp.minimum(jnp.sum((pad_end[None, :] <= block_start[:, None]).astype(jnp.int32), axis=1),
                          N_EXPERTS - 1)
    n_used = (pad_end[-1] // bm).astype(jnp.int32).reshape(1)
    return dest, n_rows, block_e, n_used


def _dispatch(hn_slabs, dest, n_rows):
    T, width = hn_slabs[0].shape
    win = DISPATCH_ROWS
    dest_t = dest.T
    mesh = plsc.VectorSubcoreMesh(core_axis_name="core", subcore_axis_name="subcore")

    @functools.partial(pl.kernel, out_type=jax.ShapeDtypeStruct((n_rows, width), hn_slabs[0].dtype), mesh=mesh,
                       scratch_types=[], name="dispatch")
    def scatter_rows(x_hbm, i_hbm, o_hbm):
        def body(x_vmem, i_vmem):
            for kk in range(TOP_K):
                pltpu.sync_copy(x_vmem, o_hbm.at[i_vmem.at[kk]])

        pltpu.emit_pipeline(
            body, grid=(T // win,),
            in_specs=[pl.BlockSpec((win, width), lambda i: (i, 0)), pl.BlockSpec((TOP_K, win), lambda i: (0, i))],
            out_specs=[], core_axis_name=("core", "subcore"),
            dimension_semantics=(pltpu.PARALLEL,))(x_hbm, i_hbm)

    return [scatter_rows(slab, dest_t) for slab in hn_slabs]


def _combine_gather(y_slabs, dest_t):
    n_k, n_tok = dest_t.shape
    win = DISPATCH_ROWS
    width = y_slabs[0].shape[1]
    steps = n_tok // win
    mesh = plsc.VectorSubcoreMesh(core_axis_name="core", subcore_axis_name="subcore")

    @functools.partial(pl.kernel, out_type=jax.ShapeDtypeStruct((n_k * n_tok, width), y_slabs[0].dtype),
                       mesh=mesh, scratch_types=[], name="combine")
    def gather_rows(y_hbm, i_hbm, o_hbm):
        def body(i_vmem, o_vmem):
            pltpu.sync_copy(y_hbm.at[i_vmem.at[0]], o_vmem)

        pltpu.emit_pipeline(
            body, grid=(n_k, steps),
            in_specs=[pl.BlockSpec((1, win), lambda k, i: (k, i))],
            out_specs=[pl.BlockSpec((win, width), lambda k, i: (k * steps + i, 0))],
            core_axis_name=("core", "subcore"),
            dimension_semantics=(pltpu.PARALLEL, pltpu.PARALLEL))(i_hbm, o_hbm)

    return [gather_rows(y, dest_t) for y in y_slabs]


def kernel(x, p, positions, mix_norm, w_in, gla_w_gate, gla_b_gate, gla_out_norm, mla_q_norm, mla_w_uq,
           mla_kv_norm, mla_w_ukv, mla_qk_q_norm, mla_qk_k_norm, pool_w, pool_scale, w_out, ffn_norm,
           router_w, router_b, moe_w_gate, moe_b_gate, moe_w_up, moe_b_up, moe_w_down, moe_b_down,
           ple_w_proj, ple_gate_norm, ple_w_gate, ple_post_norm):
    B, S, D = x.shape
    T = B * S
    depth = p.shape[0]
    params = (mix_norm, w_in, gla_w_gate, gla_b_gate, gla_out_norm, mla_q_norm, mla_w_uq, mla_kv_norm,
              mla_w_ukv, mla_qk_q_norm, mla_qk_k_norm, pool_w, pool_scale, w_out, ffn_norm, router_w,
              router_b, moe_w_gate, moe_b_gate, moe_w_up, moe_b_up, moe_w_down, moe_b_down,
              ple_w_proj, ple_gate_norm, ple_w_gate, ple_post_norm)
    rope_c, rope_s1, rope_s2 = _rope_tables(positions)
    p_flat = p.reshape(depth, T, D_PLE)
    h = x.reshape(T, D)
    for i in range(depth):
        w = _layer_params(i, *params)
        zg, la, q, k, v, y_pool = _mix_pre(h, w, rope_c, rope_s1, rope_s2, S)
        y_gla = _gla(zg, la, w["gla_out_norm"], B, S)
        y_mla = _attn(q, k, v, B, S)
        h1, hn0, hn1, route, gates, counts = _out_router(h, y_gla, y_mla, y_pool, w)
        dest, n_rows, block_e, n_used = _route(route[:, :TOP_K], route[:, TOP_K:2 * TOP_K],
                                               counts[0, :N_EXPERTS], T)
        ys = _moe(_dispatch([hn0, hn1], dest, n_rows), block_e, n_used, w)
        h = h1
        for part in range(COMBINE_PARTS):
            d = dest[part * (T // COMBINE_PARTS):(part + 1) * (T // COMBINE_PARTS)]
            h = _ple(h, _combine_gather(ys, d.T), gates, p_flat, w, part)
    return h.reshape(B, S, D)
```

```python
import functools

import jax
import jax.numpy as jnp
import numpy as np
from jax import lax
from jax.experimental import pallas as pl
from jax.experimental.pallas import tpu as pltpu
from jax.experimental.pallas import tpu_sc as plsc

F32 = jnp.float32
BF16 = jnp.bfloat16

D_MODEL = 1024
EPS = 1e-6
D_PLE = 256

GLA_HEADS = 4
GLA_DK = 32
GLA_DV = 64
GLA_GATE_RANK = 16
GLA_TAU = 16.0
GLA_CHUNK = 64
GLA_K = GLA_HEADS * GLA_DK
GLA_W = GLA_HEADS * GLA_DV

MLA_HEADS = 8
MLA_Q_RANK = 256
MLA_KV_RANK = 128
MLA_NOPE = 64
MLA_ROPE = 32
MLA_QK = MLA_NOPE + MLA_ROPE
MLA_V = 64
MLA_W = MLA_HEADS * MLA_V
ROPE_BASE = 10000.0
HEAD_PAD = 128
MLA_QK_PAD = MLA_HEADS * HEAD_PAD

POOL_WINDOWS = (2, 4, 8, 16)
POOL_GROUP = 64
POOL_W = 256
POOL_HALO = 16

N_EXPERTS = 32
TOP_K = 4
D_FF = 1024
SWIGLU_LIMIT = 7.0
SWIGLU_ALPHA = 1.702

COL_GQ, COL_GK, COL_GV, COL_GR, COL_CQ, COL_POOL, COL_CKV, COL_MISC = 0, 128, 256, 512, 768, 1024, 1280, 1408
D_IN_PAD = 1536
MISC_GLOW = 0
MISC_ROPE = 16

LOG2E = 1.4426950408889634
TOKEN_TILE = 512
MIX_SUB, ROUTER_SUB, PLE_SUB = 256, 256, 128
GLA_TILE = 512
ATTN_TILE = 2048
ATTN_SUB = 512
MOE_BLOCK = 512
MOE_CAST_ROWS = 256
COMBINE_MAX_RANGES = 5
DISPATCH_ROWS = 128
DISPATCH_SLABS = 2
VMEM_LIMIT = 56 * 1024 * 1024
NEG_BIG = -1e30


def _cparams(n_axes, **flags):
    return pltpu.CompilerParams(dimension_semantics=("arbitrary",) * n_axes,
                                vmem_limit_bytes=VMEM_LIMIT, flags=flags or None)


def _rms(x, g):
    return x * lax.rsqrt(jnp.mean(x * x, axis=-1, keepdims=True) + EPS) * g


def _dot(a, b):
    return jnp.dot(a, b, preferred_element_type=F32)


def _dot_nt(a, b):
    return lax.dot_general(a, b, (((1,), (1,)), ((), ())), preferred_element_type=F32)


def _dot_tn(a, b):
    return lax.dot_general(a, b, (((0,), (0,)), ((), ())), preferred_element_type=F32)


def _split3(x):
    hi = x.astype(BF16)
    r = x - hi.astype(F32)
    mid = r.astype(BF16)
    lo = (r - mid.astype(F32)).astype(BF16)
    return hi, mid, lo


def _split2(x):
    hi = x.astype(BF16)
    lo = (x - hi.astype(F32)).astype(BF16)
    return hi, lo


def _pack_bf16_pairs(x):
    m = x.shape[1] // 2
    bits = lax.bitcast_convert_type(x.astype(BF16).astype(F32), jnp.uint32)
    return (bits[:, :m] >> 16) | (bits[:, m:] & jnp.uint32(0xFFFF0000))


def _unpack_bf16_pairs(w):
    lo = lax.bitcast_convert_type(w << 16, F32)
    hi = lax.bitcast_convert_type(w & jnp.uint32(0xFFFF0000), F32)
    return lo, hi


def _skewed(stages, n_rows, sub):
    states = [{"rows": slice(r0, r0 + sub)} for r0 in range(0, n_rows, sub)]
    for step in range(len(states) + len(stages) - 1):
        for s, stage in enumerate(stages):
            t = step - s
            if 0 <= t < len(states):
                stage(states[t])


def _full(shape):
    nd = len(shape)
    return pl.BlockSpec(shape, lambda *_: (0,) * nd)


def _rope(x, c, s1, s2):
    return x * c + pltpu.roll(x, HEAD_PAD - 16, 1) * s1 + pltpu.roll(x, 16, 1) * s2


def _mix_pre_kernel(h_ref, mixn_ref, win_ref, wgate_ref, bgate_ref, qn_ref, wuq_ref, kvn_ref,
                    wukvk_ref, wukvv_ref, gq_ref, gk_ref, rc_ref, rs1_ref, rs2_ref,
                    wpool_ref, pscale_ref,
                    zg_ref, la_ref, q_ref, k_ref, v_ref, yp_ref, carry_ref, *, tiles_per_seq):
    tm = h_ref.shape[0]
    sub = MIX_SUB
    seq_tile = pl.program_id(0) % tiles_per_seq

    @pl.when(seq_tile == 0)
    def _():
        carry_ref[...] = jnp.zeros_like(carry_ref)

    lane = lax.broadcasted_iota(jnp.int32, (sub, HEAD_PAD), 1)
    in_rope = (lane >= MLA_NOPE) & (lane < MLA_QK)
    lane_v = lax.broadcasted_iota(jnp.int32, (sub, MLA_QK_PAD), 1)
    ones_lane = lane_v % HEAD_PAD == MLA_V
    lane_p = lax.broadcasted_iota(jnp.int32, (sub, POOL_W), 1)
    row_p = lax.broadcasted_iota(jnp.int32, (sub, POOL_W), 0)
    g0, g1, g2 = lane_p < 64, lane_p < 128, lane_p < 192
    win = jnp.where(g0, 2.0, jnp.where(g1, 4.0, jnp.where(g2, 8.0, 16.0)))
    gq, gq_sw, gk = gq_ref[0:1, :], gq_ref[1:2, :], gk_ref[...]

    def norm_in(st):
        st["hn"] = _rms(h_ref[st["rows"], :], mixn_ref[...]).astype(BF16)

    def project_in(st):
        st["z"] = _dot(st["hn"], win_ref[...])

    def norm_latents(st):
        z = st["z"]
        zg_ref[st["rows"], :] = z[:, COL_GQ:COL_CQ]
        st["cqn"] = _rms(z[:, COL_CQ:COL_CQ + MLA_Q_RANK], qn_ref[...]).astype(BF16)
        st["ckvn"] = _rms(z[:, COL_CKV:COL_CKV + MLA_KV_RANK], kvn_ref[...]).astype(BF16)

    def project_up(st):
        zm = st["z"][:, COL_MISC:COL_MISC + 128]
        st["logit"] = _dot(zm.astype(BF16), wgate_ref[...]) + bgate_ref[...]
        st["qf"] = _dot(st["cqn"], wuq_ref[...])
        st["kn"] = _dot(st["ckvn"], wukvk_ref[...])
        st["v"] = _dot(st["ckvn"], wukvv_ref[...])

    def heads_and_pool(st):
        rows, z, qf, kn, logit = st["rows"], st["z"], st["qf"], st["kn"], st["logit"]
        zm = z[:, COL_MISC:COL_MISC + 128]
        la_ref[rows, :] = (jnp.minimum(logit, 0.0) - jnp.log(1.0 + jnp.exp(-jnp.abs(logit)))) * (1.0 / GLA_TAU)
        v_ref[rows, :] = jnp.where(ones_lane, 1.0, st["v"]).astype(BF16)

        rc, rs1, rs2 = rc_ref[rows, :], rs1_ref[rows, :], rs2_ref[rows, :]
        kr = jnp.where(in_rope, pltpu.roll(zm, MLA_NOPE - MISC_ROPE, 1), 0.0)
        kr_ss = jnp.sum(kr * kr, axis=-1, keepdims=True)
        krr = _rope(kr * gk, rc, rs1, rs2)
        cq = rc * gq
        sq_tab = (rs1 + rs2) * gq_sw
        for hh in range(MLA_HEADS):
            sl = slice(hh * HEAD_PAD, (hh + 1) * HEAD_PAD)
            qh = qf[:, sl]
            qsw = qf[:, MLA_QK_PAD + hh * HEAD_PAD:MLA_QK_PAD + (hh + 1) * HEAD_PAD]
            sq = lax.rsqrt(jnp.sum(qh * qh, axis=-1, keepdims=True) * (1.0 / MLA_QK) + EPS)
            q_ref[rows, sl] = ((qh * cq + qsw * sq_tab) * sq).astype(BF16)
            kh = kn[:, sl]
            sk = lax.rsqrt((jnp.sum(kh * kh, axis=-1, keepdims=True) + kr_ss) * (1.0 / MLA_QK) + EPS)
            k_ref[rows, sl] = (sk * (kh * gk + krr)).astype(BF16)

        u = z[:, COL_POOL:COL_POOL + POOL_W]
        xe = jnp.concatenate([carry_ref[...], u], axis=0)
        carry_ref[...] = u[sub - POOL_HALO:, :]
        s2 = xe + pltpu.roll(xe, 1, 0)
        s4 = s2 + pltpu.roll(s2, 2, 0)
        s8 = s4 + pltpu.roll(s4, 4, 0)
        s16 = s8 + pltpu.roll(s8, 8, 0)
        pooled = jnp.where(g0, s2[POOL_HALO:], jnp.where(g1, s4[POOL_HALO:],
                           jnp.where(g2, s8[POOL_HALO:], s16[POOL_HALO:])))
        cnt = jnp.minimum((seq_tile * tm + rows.start + row_p + 1).astype(F32), win)
        st["d"] = (pooled / cnt - u).astype(BF16)

    def project_pool(st):
        yp_ref[st["rows"], :] = (_dot(st["d"], wpool_ref[...]) * pscale_ref[...]).astype(BF16)

    _skewed([norm_in, project_in, norm_latents, project_up, heads_and_pool, project_pool], tm, sub)


def _mix_pre(h, w, rope_c, rope_s1, rope_s2, seq_len):
    T = h.shape[0]
    tm = TOKEN_TILE
    row = lambda n: pl.BlockSpec((tm, n), lambda i: (i, 0))
    ins = [h, w["mix_norm"], w["w_in"], w["gla_w_gate"], w["gla_b_gate"], w["mla_q_norm"], w["mla_w_uq"],
           w["mla_kv_norm"], w["mla_w_ukv_k"], w["mla_w_ukv_v"], w["mla_gq"], w["mla_gk"],
           rope_c, rope_s1, rope_s2, w["pool_w"], w["pool_scale"]]
    in_specs = [row(D_MODEL)] + [_full(a.shape) for a in ins[1:12]] + [row(HEAD_PAD)] * 3 + \
               [_full(w["pool_w"].shape), _full(w["pool_scale"].shape)]
    out_shape = [jax.ShapeDtypeStruct((T, COL_CQ), F32), jax.ShapeDtypeStruct((T, GLA_K), F32),
                 jax.ShapeDtypeStruct((T, MLA_QK_PAD), BF16), jax.ShapeDtypeStruct((T, MLA_QK_PAD), BF16),
                 jax.ShapeDtypeStruct((T, MLA_QK_PAD), BF16), jax.ShapeDtypeStruct((T, POOL_W), BF16)]
    out_specs = [row(COL_CQ), row(GLA_K), row(MLA_QK_PAD), row(MLA_QK_PAD), row(MLA_QK_PAD), row(POOL_W)]
    return pl.pallas_call(
        functools.partial(_mix_pre_kernel, tiles_per_seq=seq_len // tm),
        grid=(T // tm,), in_specs=in_specs, out_specs=out_specs, out_shape=out_shape,
        scratch_shapes=[pltpu.VMEM((POOL_HALO, POOL_W), F32)],
        compiler_params=_cparams(1), name="mix_pre")(*ins)


def _gla_kernel(zg_ref, la_ref, gn_ref, y_ref, state_ref, o_ref):
    tg = zg_ref.shape[0]
    C = GLA_CHUNK

    @pl.when(pl.program_id(1) == 0)
    def _():
        state_ref[...] = jnp.zeros_like(state_ref)

    r_i = lax.broadcasted_iota(jnp.int32, (C, C), 0)
    c_i = lax.broadcasted_iota(jnp.int32, (C, C), 1)
    tri = (r_i >= c_i).astype(BF16)
    ones = jnp.ones((C, GLA_W), BF16)
    head_k = lax.broadcasted_iota(jnp.int32, (C, GLA_K), 1) // GLA_DK
    head_v = lax.broadcasted_iota(jnp.int32, (C, GLA_W), 1) // GLA_DV
    ar = lax.broadcasted_iota(jnp.int32, (GLA_HEADS * C, C), 0)
    ac = lax.broadcasted_iota(jnp.int32, (GLA_HEADS * C, C), 1)
    causal = (ar % C) >= ac
    sk = lax.broadcasted_iota(jnp.int32, (GLA_K, GLA_W), 0) // GLA_DK
    sv = lax.broadcasted_iota(jnp.int32, (GLA_K, GLA_W), 1) // GLA_DV
    blockdiag = sk == sv

    def log_decay(st):
        la3 = _split3(la_ref[st["rows"], :])
        st["bc"] = _dot(tri, la3[0]) + _dot(tri, la3[1]) + _dot(tri, la3[2])
        st["dsum"] = _dot_tn(la3[0], ones) + _dot_tn(la3[1], ones) + _dot_tn(la3[2], ones)

    def scores(st):
        rows, bc = st["rows"], st["bc"]
        q = zg_ref[rows, COL_GQ:COL_GQ + GLA_K] * (GLA_DK ** -0.5)
        k = zg_ref[rows, COL_GK:COL_GK + GLA_K]
        b_last = bc[C - 1:C, :]
        q_dec = (q * jnp.exp(bc)).astype(BF16)
        k_dec = (k * jnp.exp(-bc)).astype(BF16)
        st["k_end"] = (k * jnp.exp(b_last - bc)).astype(BF16)
        st["decay"] = jnp.exp(st["dsum"])
        zero = jnp.zeros_like(q_dec)
        qs = jnp.concatenate([jnp.where(head_k == hh, q_dec, zero) for hh in range(GLA_HEADS)], axis=0)
        st["q_dec"] = q_dec
        st["att"] = _dot_nt(qs, k_dec)

    def values(st):
        v = zg_ref[st["rows"], COL_GV:COL_GV + GLA_W].astype(BF16)
        st["o_full"] = _dot(jnp.where(causal, st["att"], 0.0).astype(BF16), v)
        st["upd"] = jnp.where(blockdiag, _dot_tn(st["k_end"], v), 0.0)

    state = [state_ref[...]]

    def recur(st):
        o_full = st["o_full"]
        o = _dot(st["q_dec"], state[0].astype(BF16))
        for hh in range(GLA_HEADS):
            o = o + jnp.where(head_v == hh, o_full[hh * C:(hh + 1) * C, :], 0.0)
        o_ref[st["rows"], :] = o
        state[0] = st["decay"] * state[0] + st["upd"]

    _skewed([log_decay, scores, values, recur], tg, C)
    state_ref[...] = state[0]

    o = o_ref[...]
    gr = lax.broadcasted_iota(jnp.int32, (GLA_W, GLA_W), 0) // GLA_DV
    gc = lax.broadcasted_iota(jnp.int32, (GLA_W, GLA_W), 1) // GLA_DV
    group = (gr == gc).astype(BF16)
    oo = _split2(o * o)
    ms = (_dot(oo[0], group) + _dot(oo[1], group)) * (1.0 / GLA_DV)
    r = zg_ref[:, COL_GR:COL_GR + GLA_W]
    y = o * lax.rsqrt(ms + EPS) * gn_ref[...] * (r / (1.0 + jnp.exp(-r)))
    y_ref[...] = y.astype(BF16)


def _gla(zg, la, gn, batch, seq_len):
    T = zg.shape[0]
    tg = GLA_TILE
    nt = seq_len // tg
    return pl.pallas_call(
        _gla_kernel, grid=(batch, nt),
        in_specs=[pl.BlockSpec((tg, COL_CQ), lambda b, s: (b * nt + s, 0)),
                  pl.BlockSpec((tg, GLA_K), lambda b, s: (b * nt + s, 0)),
                  _full(gn.shape)],
        out_specs=pl.BlockSpec((tg, GLA_W), lambda b, s: (b * nt + s, 0)),
        out_shape=jax.ShapeDtypeStruct((T, GLA_W), BF16),
        scratch_shapes=[pltpu.VMEM((GLA_K, GLA_W), F32), pltpu.VMEM((tg, GLA_W), F32)],
        compiler_params=_cparams(2), name="gla")(zg, la, gn)


def _attn_kernel(q_ref, k_ref, v_ref, o_ref, m_ref, acc_ref):
    tq = q_ref.shape[0]
    ts = ATTN_SUB
    i = pl.program_id(2)
    m_ref[...] = jnp.full_like(m_ref, NEG_BIG)
    acc_ref[...] = jnp.zeros_like(acc_ref)

    def sub_block(hh, start, r0, mask_off):
        hs = slice(hh * HEAD_PAD, (hh + 1) * HEAD_PAD)
        kj = k_ref[pl.ds(start, ts), hs]
        vj = v_ref[pl.ds(start, ts), hs]
        s = _dot_nt(q_ref[r0:, hs], kj)
        if mask_off is not None:
            row = lax.broadcasted_iota(jnp.int32, s.shape, 0) + r0
            col = lax.broadcasted_iota(jnp.int32, s.shape, 1) + mask_off
            s = jnp.where(col <= row, s, NEG_BIG)
        m_old = m_ref[hh, r0:, :]
        parts = [s[:, c * 128:(c + 1) * 128] for c in range(ts // 128)]
        m_new = jnp.maximum(m_old, jnp.max(functools.reduce(jnp.maximum, parts), axis=-1, keepdims=True))
        p = jnp.concatenate([jnp.exp2((x - m_new).astype(BF16)) for x in parts], axis=1)
        acc_ref[hh, r0:, :] = jnp.exp2(m_old - m_new) * acc_ref[hh, r0:, :] + _dot(p, vj)
        m_ref[hh, r0:, :] = m_new

    def body(j, carry):
        base = pl.multiple_of(j * tq, tq)
        for sb in range(tq // ts):
            for hh in range(2):
                sub_block(hh, base + sb * ts, 0, None)
        return carry

    lax.fori_loop(0, i, body, 0)
    base = pl.multiple_of(i * tq, tq)
    for sb in range(tq // ts):
        for hh in range(2):
            sub_block(hh, base + sb * ts, sb * ts, sb * ts)
    outs = []
    for hh in range(2):
        a = acc_ref[hh]
        outs.append(a / a[:, MLA_V:MLA_V + 1])
    lane = lax.broadcasted_iota(jnp.int32, (tq, HEAD_PAD), 1)
    o_ref[...] = jnp.where(lane < MLA_V, outs[0], pltpu.roll(outs[1], MLA_V, 1)).astype(BF16)


def _attn(q, k, v, batch, seq_len):
    T = q.shape[0]
    tq = ATTN_TILE
    nq = seq_len // tq
    pairs = MLA_HEADS // 2
    return pl.pallas_call(
        _attn_kernel, grid=(batch, pairs, nq),
        in_specs=[pl.BlockSpec((tq, 2 * HEAD_PAD), lambda b, p, i: (b * nq + i, p)),
                  pl.BlockSpec((seq_len, 2 * HEAD_PAD), lambda b, p, i: (b, p)),
                  pl.BlockSpec((seq_len, 2 * HEAD_PAD), lambda b, p, i: (b, p))],
        out_specs=pl.BlockSpec((tq, 2 * MLA_V), lambda b, p, i: (b * nq + i, p)),
        out_shape=jax.ShapeDtypeStruct((T, MLA_W), BF16),
        scratch_shapes=[pltpu.VMEM((2, tq, HEAD_PAD), F32), pltpu.VMEM((2, tq, HEAD_PAD), F32)],
        compiler_params=_cparams(3), name="attn")(q, k, v)


def _out_router_kernel(h_ref, yg_ref, ym_ref, yp_ref, wo_ref, fn_ref, rw_ref, rb_ref,
                       h1_ref, hn0_ref, hn1_ref, idx_ref, gate_ref, cnt_ref, carry_ref):
    tm = h_ref.shape[0]

    @pl.when(pl.program_id(0) == 0)
    def _():
        carry_ref[...] = jnp.zeros_like(carry_ref)

    sub = ROUTER_SUB
    lane = lax.broadcasted_iota(jnp.int32, (sub, 128), 1)
    r_i = lax.broadcasted_iota(jnp.int32, (sub, sub), 0)
    c_i = lax.broadcasted_iota(jnp.int32, (sub, sub), 1)
    tri = (r_i >= c_i).astype(BF16)
    def project(st):
        rows = st["rows"]
        st["h1"] = (h_ref[rows, :] + _dot(yg_ref[rows, :], wo_ref[0:GLA_W, :])
                    + _dot(ym_ref[rows, :], wo_ref[GLA_W:GLA_W + MLA_W, :])
                    + _dot(yp_ref[rows, :], wo_ref[GLA_W + MLA_W:, :]))

    def normalize(st):
        rows = st["rows"]
        h1_ref[rows, :] = st["h1"]
        hn = _rms(st["h1"], fn_ref[...])
        st["hi"], st["lo"] = _split2(hn)
        packed = _pack_bf16_pairs(hn)
        slab = packed.shape[1] // DISPATCH_SLABS
        hn0_ref[rows, :] = packed[:, :slab]
        hn1_ref[rows, :] = packed[:, slab:]

    def score(st):
        r2 = _dot(st["hi"], rw_ref[...])
        st["logits"] = r2[:, :128] + r2[:, 128:] + _dot(st["lo"], rw_ref[:, 0:128]) + rb_ref[...]

    def select(st):
        rows = st["rows"]
        cur = jnp.where(lane < N_EXPERTS, st["logits"], NEG_BIG)
        idx_out = jnp.zeros((sub, 128), jnp.int32)
        val_out = jnp.zeros((sub, 128), F32)
        chosen = jnp.zeros((sub, 128), F32)
        top0 = None
        sels = []
        for kk in range(TOP_K):
            m = jnp.max(cur, axis=-1, keepdims=True)
            sel = jnp.min(jnp.where(cur == m, lane, 128), axis=-1, keepdims=True)
            if kk == 0:
                top0 = m
            sels.append(sel)
            idx_out = jnp.where(lane == kk, sel, idx_out)
            val_out = jnp.where(lane == kk, jnp.exp(m - top0), val_out)
            chosen = jnp.where(lane == sel, 1.0, chosen)
            cur = jnp.where(lane == sel, NEG_BIG, cur)
        gate_ref[rows, :] = val_out / jnp.sum(val_out, axis=-1, keepdims=True)

        incl = _dot(tri, chosen.astype(BF16))
        before = carry_ref[0:1, :] + incl - chosen
        for kk in range(TOP_K):
            rank = jnp.sum(jnp.where(lane == sels[kk], before, 0.0), axis=-1, keepdims=True)
            idx_out = jnp.where(lane == TOP_K + kk, rank.astype(jnp.int32), idx_out)
        idx_ref[rows, :] = idx_out
        carry_ref[...] = carry_ref[...] + incl[sub - 1:sub, :]

    _skewed([project, normalize, score, select], tm, sub)
    cnt_ref[...] = carry_ref[...].astype(jnp.int32)


def _out_router(h, yg, ym, yp, w):
    T = h.shape[0]
    tm = TOKEN_TILE
    row = lambda n: pl.BlockSpec((tm, n), lambda i: (i, 0))
    slab = D_MODEL // 2 // DISPATCH_SLABS
    ins = [h, yg, ym, yp, w["w_out"], w["ffn_norm"], w["router_w"], w["router_b"]]
    return pl.pallas_call(
        _out_router_kernel, grid=(T // tm,),
        in_specs=[row(D_MODEL), row(GLA_W), row(MLA_W), row(POOL_W)] + [_full(a.shape) for a in ins[4:]],
        out_specs=[row(D_MODEL), row(slab), row(slab), row(128), row(128), _full((8, 128))],
        out_shape=[jax.ShapeDtypeStruct((T, D_MODEL), F32), jax.ShapeDtypeStruct((T, slab), jnp.uint32),
                   jax.ShapeDtypeStruct((T, slab), jnp.uint32),
                   jax.ShapeDtypeStruct((T, 128), jnp.int32), jax.ShapeDtypeStruct((T, 128), F32),
                   jax.ShapeDtypeStruct((8, 128), jnp.int32)],
        scratch_shapes=[pltpu.VMEM((8, 128), F32)],
        compiler_params=_cparams(1), name="out_router")(*ins)


def _moe_kernel(be_ref, nb_ref, x0_ref, x1_ref, wg_ref, bg_ref, wu_ref, bu_ref, wd_ref, bd_ref,
                y0_ref, y1_ref, wg_bf, wu_bf, wd_bf):
    i = pl.program_id(0)
    used = i < nb_ref[0]
    new_expert = (i == 0) | (be_ref[i] != be_ref[jnp.maximum(i - 1, 0)])

    @pl.when(used & new_expert)
    def _():
        for src, dst in ((wg_ref, wg_bf), (wu_ref, wu_bf), (wd_ref, wd_bf)):
            for r in range(0, src.shape[2], MOE_CAST_ROWS):
                dst[r:r + MOE_CAST_ROWS, :] = src[0, 0, r:r + MOE_CAST_ROWS, :].astype(BF16)

    @pl.when(used)
    def _():
        halves = [_unpack_bf16_pairs(r[...]) for r in (x0_ref, x1_ref)]
        x = jnp.concatenate([h[0] for h in halves] + [h[1] for h in halves], axis=1).astype(BF16)
        g = jnp.minimum(_dot(x, wg_bf[...]) + bg_ref[0], SWIGLU_LIMIT)
        up = jnp.clip(_dot(x, wu_bf[...]) + bu_ref[0], -SWIGLU_LIMIT, SWIGLU_LIMIT)
        hb = (up + 1.0) * (g / (1.0 + jnp.exp(-SWIGLU_ALPHA * g)))
        packed = _pack_bf16_pairs(_dot(hb.astype(BF16), wd_bf[...]) + bd_ref[0])
        slab = y0_ref.shape[1]
        y0_ref[...] = packed[:, :slab]
        y1_ref[...] = packed[:, slab:]

    @pl.when(jnp.logical_not(used))
    def _():
        y0_ref[...] = jnp.zeros_like(y0_ref)
        y1_ref[...] = jnp.zeros_like(y1_ref)


def _moe(xs, block_e, n_used, w):
    n_rows, slab = xs[0].shape
    bm = MOE_BLOCK
    layer = w["layer"]
    wspec = lambda shp: pl.BlockSpec((1, 1) + shp, lambda i, be, nb: (layer, be[i], 0, 0))
    bspec = lambda shp: pl.BlockSpec((1,) + shp, lambda i, be, nb: (be[i], 0, 0))
    grid_spec = pltpu.PrefetchScalarGridSpec(
        num_scalar_prefetch=2, grid=(n_rows // bm,),
        in_specs=[pl.BlockSpec((bm, slab), lambda i, be, nb: (i, 0))] * DISPATCH_SLABS + [
                  wspec((D_MODEL, D_FF)), bspec((1, D_FF)), wspec((D_MODEL, D_FF)), bspec((1, D_FF)),
                  wspec((D_FF, D_MODEL)), bspec((1, D_MODEL))],
        out_specs=[pl.BlockSpec((bm, slab), lambda i, be, nb: (i, 0))] * DISPATCH_SLABS,
        scratch_shapes=[pltpu.VMEM((D_MODEL, D_FF), BF16), pltpu.VMEM((D_MODEL, D_FF), BF16),
                        pltpu.VMEM((D_FF, D_MODEL), BF16)])
    return pl.pallas_call(
        _moe_kernel, grid_spec=grid_spec,
        out_shape=[jax.ShapeDtypeStruct((n_rows, slab), jnp.uint32)] * DISPATCH_SLABS,
        compiler_params=_cparams(1), name="moe")(
            block_e, n_used, *xs, w["moe_w_gate"], w["moe_b_gate"], w["moe_w_up"], w["moe_b_up"],
            w["moe_w_down"], w["moe_b_down"])


def _ple_kernel(h1_ref, ya0_ref, ya1_ref, ya2_ref, ya3_ref, yb0_ref, yb1_ref, yb2_ref, yb3_ref, gate_ref, p_ref,
                wple_ref, gn_ref, wpg_ref, pn_ref, o_ref):
    def combine(st):
        rows = st["rows"]
        gates = gate_ref[rows, :]
        h2 = h1_ref[rows, :]
        for kk, (ya_ref, yb_ref) in enumerate(((ya0_ref, yb0_ref), (ya1_ref, yb1_ref), (ya2_ref, yb2_ref),
                                               (ya3_ref, yb3_ref))):
            lo_a, hi_a = _unpack_bf16_pairs(ya_ref[rows, :])
            lo_b, hi_b = _unpack_bf16_pairs(yb_ref[rows, :])
            h2 = h2 + gates[:, kk:kk + 1] * jnp.concatenate([lo_a, lo_b, hi_a, hi_b], axis=1)
        st["h2"] = h2
        st["hn"] = _rms(h2, gn_ref[...]).astype(BF16)

    def project(st):
        st["e"] = _dot(p_ref[0, st["rows"], :].astype(BF16), wple_ref[...])
        st["a"] = _dot(st["hn"], wpg_ref[...])

    def finish(st):
        gate = 1.0 / (1.0 + jnp.exp(-st["a"]))
        o_ref[st["rows"], :] = st["h2"] + _rms(st["e"] * gate, pn_ref[...])

    _skewed([combine, project, finish], h1_ref.shape[0], PLE_SUB)


def _combine_ranges(n_tiles):
    sizes, left = [], n_tiles
    while left > 1 and len(sizes) < COMBINE_MAX_RANGES - 1:
        sizes.append(left - left // 2)
        left //= 2
    sizes.append(left)
    sizes = [s for s in reversed(sizes) if s]
    return [(sum(sizes[:j]), s) for j, s in enumerate(sizes)]


def _ple(h1, y_slabs, gates, p, w, off, steps):
    T = h1.shape[0]
    tm = TOKEN_TILE
    row = lambda n: pl.BlockSpec((tm, n), lambda i: (i + off, 0))
    slab = y_slabs[0].shape[1]
    gathered = lambda kk: pl.BlockSpec((tm, slab), lambda i: (kk * steps + i, 0))
    weights = [w["ple_w_proj"], w["ple_gate_norm"], w["ple_w_gate"], w["ple_post_norm"]]
    layer = w["layer"]
    p_spec = pl.BlockSpec((1, tm, D_PLE), lambda i: (layer, i + off, 0))
    ins = [h1] + [y for y in y_slabs for _ in range(TOP_K)] + [gates, p, *weights]
    in_specs = ([row(D_MODEL)] + [gathered(kk) for _ in y_slabs for kk in range(TOP_K)] + [row(128), p_spec]
                + [_full(a.shape) for a in weights])
    return pl.pallas_call(
        _ple_kernel, grid=(steps,), in_specs=in_specs,
        out_specs=row(D_MODEL), out_shape=jax.ShapeDtypeStruct((T, D_MODEL), F32),
        input_output_aliases={0: 0}, compiler_params=_cparams(1), name="ple")(*ins)


def _pad_heads(wm, per_head, n_heads=MLA_HEADS):
    kdim = wm.shape[0]
    w3 = wm.reshape(kdim, n_heads, per_head)
    return jnp.pad(w3, ((0, 0), (0, 0), (0, HEAD_PAD - per_head))).reshape(kdim, n_heads * HEAD_PAD)


def _swap_rope_halves(a):
    a3 = a.reshape(a.shape[0], -1, HEAD_PAD)
    half = MLA_ROPE // 2
    x1 = a3[:, :, MLA_NOPE:MLA_NOPE + half]
    x2 = a3[:, :, MLA_NOPE + half:MLA_QK]
    out = jnp.zeros_like(a3).at[:, :, MLA_NOPE:MLA_NOPE + half].set(x2).at[:, :, MLA_NOPE + half:MLA_QK].set(x1)
    return out.reshape(a.shape)


def _layer_params(i, mix_norm, w_in, gla_w_gate, gla_b_gate, gla_out_norm, mla_q_norm, mla_w_uq, mla_kv_norm,
                  mla_w_ukv, mla_qk_q_norm, mla_qk_k_norm, pool_w, pool_scale, w_out, ffn_norm, router_w,
                  router_b, moe_w_gate, moe_b_gate, moe_w_up, moe_b_up, moe_w_down, moe_b_down,
                  ple_w_proj, ple_gate_norm, ple_w_gate, ple_post_norm):
    wi = w_in[i]
    c = np.cumsum((0, 128, 128, 256, 16, 256, 256, 128, 32, 256))
    gq, gk, gv, glow, gr, cq, ckv, krope, upool = [wi[:, c[j]:c[j + 1]] for j in range(9)]
    misc = jnp.concatenate([glow, krope, jnp.zeros((D_MODEL, 128 - 48), F32)], axis=1)
    w_in_p = jnp.concatenate([gq, gk, gv, gr, cq, upool, ckv, misc], axis=1).astype(BF16)
    wgate_p = jnp.zeros((128, GLA_K), F32).at[MISC_GLOW:MISC_GLOW + GLA_GATE_RANK].set(gla_w_gate[i]).astype(BF16)
    ukv = mla_w_ukv[i].reshape(MLA_KV_RANK, MLA_HEADS, MLA_NOPE + MLA_V)
    ukv_k = _pad_heads(ukv[:, :, :MLA_NOPE].reshape(MLA_KV_RANK, MLA_HEADS * MLA_NOPE), MLA_NOPE)
    ukv_v = _pad_heads(ukv[:, :, MLA_NOPE:].reshape(MLA_KV_RANK, MLA_W), MLA_V)
    pw = pool_w[i]
    pool_bd = jnp.zeros((POOL_W, POOL_W), F32)
    for g in range(4):
        pool_bd = pool_bd.at[g * 64:(g + 1) * 64, g * 64:(g + 1) * 64].set(pw[g])
    rw = jnp.pad(router_w[i], ((0, 0), (0, 128 - N_EXPERTS)))
    rw_hi = rw.astype(BF16)
    rw_lo = (rw - rw_hi.astype(F32)).astype(BF16)
    row = lambda a: a.reshape(1, -1)
    pad96 = lambda a: jnp.pad(a, (0, HEAD_PAD - MLA_QK)).reshape(1, HEAD_PAD)
    wuq_p = _pad_heads(mla_w_uq[i], MLA_QK)
    gq_p = pad96(mla_qk_q_norm[i] * (MLA_QK ** -0.5 * LOG2E))
    return {
        "mix_norm": row(mix_norm[i]), "w_in": w_in_p, "gla_w_gate": wgate_p, "gla_b_gate": row(gla_b_gate[i]),
        "gla_out_norm": row(jnp.tile(gla_out_norm[i], GLA_HEADS)),
        "mla_q_norm": row(mla_q_norm[i]),
        "mla_w_uq": jnp.concatenate([wuq_p, _swap_rope_halves(wuq_p)], axis=1).astype(BF16),
        "mla_kv_norm": row(mla_kv_norm[i]), "mla_w_ukv_k": ukv_k.astype(BF16), "mla_w_ukv_v": ukv_v.astype(BF16),
        "mla_gq": jnp.concatenate([gq_p, _swap_rope_halves(gq_p)], axis=0), "mla_gk": pad96(mla_qk_k_norm[i]),
        "pool_w": pool_bd.astype(BF16), "pool_scale": row(pool_scale[i]),
        "w_out": w_out[i].astype(BF16), "ffn_norm": row(ffn_norm[i]),
        "router_w": jnp.concatenate([rw_hi, rw_lo], axis=1),
        "router_b": row(jnp.pad(router_b[i], (0, 128 - N_EXPERTS))),
        "layer": i,
        "moe_w_gate": moe_w_gate, "moe_b_gate": moe_b_gate[i].reshape(N_EXPERTS, 1, D_FF),
        "moe_w_up": moe_w_up, "moe_b_up": moe_b_up[i].reshape(N_EXPERTS, 1, D_FF),
        "moe_w_down": moe_w_down, "moe_b_down": moe_b_down[i].reshape(N_EXPERTS, 1, D_MODEL),
        "ple_w_proj": ple_w_proj[i].astype(BF16), "ple_gate_norm": row(ple_gate_norm[i]),
        "ple_w_gate": ple_w_gate[i].astype(BF16), "ple_post_norm": row(ple_post_norm[i]),
    }


def _rope_tables(positions):
    T = positions.size
    inv = ROPE_BASE ** (-jnp.arange(0, MLA_ROPE, 2, dtype=F32) / MLA_ROPE)
    ang = positions.reshape(T, 1).astype(F32) * inv
    cos, sin = jnp.cos(ang), jnp.sin(ang)
    z16 = jnp.zeros((T, 16), F32)
    tail = jnp.zeros((T, HEAD_PAD - MLA_QK), F32)
    c = jnp.concatenate([jnp.ones((T, MLA_NOPE), F32), cos, cos, tail], axis=1)
    s1 = jnp.concatenate([jnp.zeros((T, MLA_NOPE), F32), -sin, z16, tail], axis=1)
    s2 = jnp.concatenate([jnp.zeros((T, MLA_NOPE), F32), z16, sin, tail], axis=1)
    return c, s1, s2


def _route(top_idx, rank, counts, T):
    bm = MOE_BLOCK
    A = T * TOP_K
    padded = (counts + bm - 1) // bm * bm
    pad_end = jnp.cumsum(padded)
    pad_start = pad_end - padded
    experts = jnp.arange(N_EXPERTS, dtype=jnp.int32)
    dest = rank + jnp.sum(jnp.where(top_idx[:, :, None] == experts, pad_start, 0), axis=-1)
    n_blocks = (A + N_EXPERTS * (bm - 1) + bm - 1) // bm
    n_rows = n_blocks * bm
    block_start = jnp.arange(n_blocks, dtype=jnp.int32) * bm
    block_e = jnp.minimum(jnp.sum((pad_end[None, :] <= block_start[:, None]).astype(jnp.int32), axis=1),
                          N_EXPERTS - 1)
    n_used = (pad_end[-1] // bm).astype(jnp.int32).reshape(1)
    return dest, n_rows, block_e, n_used


def _dispatch(hn_slabs, dest, n_rows):
    T, width = hn_slabs[0].shape
    win = DISPATCH_ROWS
    dest_t = dest.T
    mesh = plsc.VectorSubcoreMesh(core_axis_name="core", subcore_axis_name="subcore")

    @functools.partial(pl.kernel, out_type=jax.ShapeDtypeStruct((n_rows, width), hn_slabs[0].dtype), mesh=mesh,
                       scratch_types=[], name="dispatch")
    def scatter_rows(x_hbm, i_hbm, o_hbm):
        def body(x_vmem, i_vmem):
            for kk in range(TOP_K):
                pltpu.sync_copy(x_vmem, o_hbm.at[i_vmem.at[kk]])

        pltpu.emit_pipeline(
            body, grid=(T // win,),
            in_specs=[pl.BlockSpec((win, width), lambda i: (i, 0)), pl.BlockSpec((TOP_K, win), lambda i: (0, i))],
            out_specs=[], core_axis_name=("core", "subcore"),
            dimension_semantics=(pltpu.PARALLEL,))(x_hbm, i_hbm)

    return [scatter_rows(slab, dest_t) for slab in hn_slabs]


def _combine_gather(y_slabs, dest_t):
    n_k, n_tok = dest_t.shape
    win = DISPATCH_ROWS
    width = y_slabs[0].shape[1]
    steps = n_tok // win
    mesh = plsc.VectorSubcoreMesh(core_axis_name="core", subcore_axis_name="subcore")

    @functools.partial(pl.kernel, out_type=jax.ShapeDtypeStruct((n_k * n_tok, width), y_slabs[0].dtype),
                       mesh=mesh, scratch_types=[], name="combine")
    def gather_rows(y_hbm, i_hbm, o_hbm):
        def body(i_vmem, o_vmem):
            pltpu.sync_copy(y_hbm.at[i_vmem.at[0]], o_vmem)

        pltpu.emit_pipeline(
            body, grid=(n_k, steps),
            in_specs=[pl.BlockSpec((1, win), lambda k, i: (k, i))],
            out_specs=[pl.BlockSpec((win, width), lambda k, i: (k * steps + i, 0))],
            core_axis_name=("core", "subcore"),
            dimension_semantics=(pltpu.PARALLEL, pltpu.PARALLEL))(i_hbm, o_hbm)

    return [gather_rows(y, dest_t) for y in y_slabs]


def kernel(x, p, positions, mix_norm, w_in, gla_w_gate, gla_b_gate, gla_out_norm, mla_q_norm, mla_w_uq,
           mla_kv_norm, mla_w_ukv, mla_qk_q_norm, mla_qk_k_norm, pool_w, pool_scale, w_out, ffn_norm,
           router_w, router_b, moe_w_gate, moe_b_gate, moe_w_up, moe_b_up, moe_w_down, moe_b_down,
           ple_w_proj, ple_gate_norm, ple_w_gate, ple_post_norm):
    B, S, D = x.shape
    T = B * S
    depth = p.shape[0]
    params = (mix_norm, w_in, gla_w_gate, gla_b_gate, gla_out_norm, mla_q_norm, mla_w_uq, mla_kv_norm,
              mla_w_ukv, mla_qk_q_norm, mla_qk_k_norm, pool_w, pool_scale, w_out, ffn_norm, router_w,
              router_b, moe_w_gate, moe_b_gate, moe_w_up, moe_b_up, moe_w_down, moe_b_down,
              ple_w_proj, ple_gate_norm, ple_w_gate, ple_post_norm)
    rope_c, rope_s1, rope_s2 = _rope_tables(positions)
    p_flat = p.reshape(depth, T, D_PLE)
    h = x.reshape(T, D)
    for i in range(depth):
        w = _layer_params(i, *params)
        zg, la, q, k, v, y_pool = _mix_pre(h, w, rope_c, rope_s1, rope_s2, S)
        y_gla = _gla(zg, la, w["gla_out_norm"], B, S)
        y_mla = _attn(q, k, v, B, S)
        h1, hn0, hn1, route, gates, counts = _out_router(h, y_gla, y_mla, y_pool, w)
        dest, n_rows, block_e, n_used = _route(route[:, :TOP_K], route[:, TOP_K:2 * TOP_K],
                                               counts[0, :N_EXPERTS], T)
        ys = _moe(_dispatch([hn0, hn1], dest, n_rows), block_e, n_used, w)
        h = h1
        for off, steps in _combine_ranges(T // TOKEN_TILE):
            d = dest[off * TOKEN_TILE:(off + steps) * TOKEN_TILE]
            h = _ple(h, _combine_gather(ys, d.T), gates, p_flat, w, off, steps)
    return h.reshape(B, S, D)
```

```python
import functools

import jax
import jax.numpy as jnp
import numpy as np
from jax import lax
from jax.experimental import pallas as pl
from jax.experimental.pallas import tpu as pltpu
from jax.experimental.pallas import tpu_sc as plsc

F32 = jnp.float32
BF16 = jnp.bfloat16

D_MODEL = 1024
EPS = 1e-6
D_PLE = 256

GLA_HEADS = 4
GLA_DK = 32
GLA_DV = 64
GLA_GATE_RANK = 16
GLA_TAU = 16.0
GLA_CHUNK = 64
GLA_K = GLA_HEADS * GLA_DK
GLA_W = GLA_HEADS * GLA_DV

MLA_HEADS = 8
MLA_Q_RANK = 256
MLA_KV_RANK = 128
MLA_NOPE = 64
MLA_ROPE = 32
MLA_QK = MLA_NOPE + MLA_ROPE
MLA_V = 64
MLA_W = MLA_HEADS * MLA_V
ROPE_BASE = 10000.0
HEAD_PAD = 128
MLA_QK_PAD = MLA_HEADS * HEAD_PAD

POOL_WINDOWS = (2, 4, 8, 16)
POOL_GROUP = 64
POOL_W = 256
POOL_HALO = 16

N_EXPERTS = 32
TOP_K = 4
D_FF = 1024
SWIGLU_LIMIT = 7.0
SWIGLU_ALPHA = 1.702

COL_GQ, COL_GK, COL_GV, COL_GR, COL_CQ, COL_POOL, COL_CKV, COL_MISC = 0, 128, 256, 512, 768, 1024, 1280, 1408
D_IN_PAD = 1536
MISC_GLOW = 0
MISC_ROPE = 16

LOG2E = 1.4426950408889634
TOKEN_TILE = 1024
MIX_SUB, ROUTER_SUB, PLE_SUB = 256, 256, 128
GLA_TILE = 1024
ATTN_TILE = 2048
ATTN_SUB = 512
MOE_BLOCK = 512
MOE_CAST_ROWS = 256
COMBINE_PARTS = 4
DISPATCH_ROWS = 128
DISPATCH_SLABS = 2
VMEM_LIMIT = 56 * 1024 * 1024
NEG_BIG = -1e30


def _cparams(n_axes, **flags):
    return pltpu.CompilerParams(dimension_semantics=("arbitrary",) * n_axes,
                                vmem_limit_bytes=VMEM_LIMIT, flags=flags or None)


def _rms(x, g):
    return x * lax.rsqrt(jnp.mean(x * x, axis=-1, keepdims=True) + EPS) * g


def _dot(a, b):
    return jnp.dot(a, b, preferred_element_type=F32)


def _dot_nt(a, b):
    return lax.dot_general(a, b, (((1,), (1,)), ((), ())), preferred_element_type=F32)


def _dot_tn(a, b):
    return lax.dot_general(a, b, (((0,), (0,)), ((), ())), preferred_element_type=F32)


def _split3(x):
    hi = x.astype(BF16)
    r = x - hi.astype(F32)
    mid = r.astype(BF16)
    lo = (r - mid.astype(F32)).astype(BF16)
    return hi, mid, lo


def _split2(x):
    hi = x.astype(BF16)
    lo = (x - hi.astype(F32)).astype(BF16)
    return hi, lo


def _pack_bf16_pairs(x):
    m = x.shape[1] // 2
    bits = lax.bitcast_convert_type(x.astype(BF16).astype(F32), jnp.uint32)
    return (bits[:, :m] >> 16) | (bits[:, m:] & jnp.uint32(0xFFFF0000))


def _unpack_bf16_pairs(w):
    lo = lax.bitcast_convert_type(w << 16, F32)
    hi = lax.bitcast_convert_type(w & jnp.uint32(0xFFFF0000), F32)
    return lo, hi


def _skewed(stages, n_rows, sub):
    states = [{"rows": slice(r0, r0 + sub)} for r0 in range(0, n_rows, sub)]
    for step in range(len(states) + len(stages) - 1):
        for s, stage in enumerate(stages):
            t = step - s
            if 0 <= t < len(states):
                stage(states[t])


def _full(shape):
    nd = len(shape)
    return pl.BlockSpec(shape, lambda *_: (0,) * nd)


def _rope(x, c, s1, s2):
    return x * c + pltpu.roll(x, HEAD_PAD - 16, 1) * s1 + pltpu.roll(x, 16, 1) * s2


def _mix_pre_kernel(h_ref, mixn_ref, win_ref, wgate_ref, bgate_ref, qn_ref, wuq_ref, kvn_ref,
                    wukvk_ref, wukvv_ref, gq_ref, gk_ref, rc_ref, rs1_ref, rs2_ref,
                    wpool_ref, pscale_ref,
                    zg_ref, la_ref, q_ref, k_ref, v_ref, yp_ref, carry_ref, *, tiles_per_seq):
    tm = h_ref.shape[0]
    sub = MIX_SUB
    seq_tile = pl.program_id(0) % tiles_per_seq

    @pl.when(seq_tile == 0)
    def _():
        carry_ref[...] = jnp.zeros_like(carry_ref)

    lane = lax.broadcasted_iota(jnp.int32, (sub, HEAD_PAD), 1)
    in_rope = (lane >= MLA_NOPE) & (lane < MLA_QK)
    lane_v = lax.broadcasted_iota(jnp.int32, (sub, MLA_QK_PAD), 1)
    ones_lane = lane_v % HEAD_PAD == MLA_V
    lane_p = lax.broadcasted_iota(jnp.int32, (sub, POOL_W), 1)
    row_p = lax.broadcasted_iota(jnp.int32, (sub, POOL_W), 0)
    g0, g1, g2 = lane_p < 64, lane_p < 128, lane_p < 192
    win = jnp.where(g0, 2.0, jnp.where(g1, 4.0, jnp.where(g2, 8.0, 16.0)))
    gq, gq_sw, gk = gq_ref[0:1, :], gq_ref[1:2, :], gk_ref[...]

    def norm_in(st):
        st["hn"] = _rms(h_ref[st["rows"], :], mixn_ref[...]).astype(BF16)

    def project_in(st):
        st["z"] = _dot(st["hn"], win_ref[...])

    def norm_latents(st):
        z = st["z"]
        zg_ref[st["rows"], :] = z[:, COL_GQ:COL_CQ]
        st["cqn"] = _rms(z[:, COL_CQ:COL_CQ + MLA_Q_RANK], qn_ref[...]).astype(BF16)
        st["ckvn"] = _rms(z[:, COL_CKV:COL_CKV + MLA_KV_RANK], kvn_ref[...]).astype(BF16)

    def project_up(st):
        zm = st["z"][:, COL_MISC:COL_MISC + 128]
        st["logit"] = _dot(zm.astype(BF16), wgate_ref[...]) + bgate_ref[...]
        st["qf"] = _dot(st["cqn"], wuq_ref[...])
        st["kn"] = _dot(st["ckvn"], wukvk_ref[...])
        st["v"] = _dot(st["ckvn"], wukvv_ref[...])

    def heads_and_pool(st):
        rows, z, qf, kn, logit = st["rows"], st["z"], st["qf"], st["kn"], st["logit"]
        zm = z[:, COL_MISC:COL_MISC + 128]
        la_ref[rows, :] = (jnp.minimum(logit, 0.0) - jnp.log(1.0 + jnp.exp(-jnp.abs(logit)))) * (1.0 / GLA_TAU)
        v_ref[rows, :] = jnp.where(ones_lane, 1.0, st["v"]).astype(BF16)

        rc, rs1, rs2 = rc_ref[rows, :], rs1_ref[rows, :], rs2_ref[rows, :]
        kr = jnp.where(in_rope, pltpu.roll(zm, MLA_NOPE - MISC_ROPE, 1), 0.0)
        kr_ss = jnp.sum(kr * kr, axis=-1, keepdims=True)
        krr = _rope(kr * gk, rc, rs1, rs2)
        cq = rc * gq
        sq_tab = (rs1 + rs2) * gq_sw
        for hh in range(MLA_HEADS):
            sl = slice(hh * HEAD_PAD, (hh + 1) * HEAD_PAD)
            qh = qf[:, sl]
            qsw = qf[:, MLA_QK_PAD + hh * HEAD_PAD:MLA_QK_PAD + (hh + 1) * HEAD_PAD]
            sq = lax.rsqrt(jnp.sum(qh * qh, axis=-1, keepdims=True) * (1.0 / MLA_QK) + EPS)
            q_ref[rows, sl] = ((qh * cq + qsw * sq_tab) * sq).astype(BF16)
            kh = kn[:, sl]
            sk = lax.rsqrt((jnp.sum(kh * kh, axis=-1, keepdims=True) + kr_ss) * (1.0 / MLA_QK) + EPS)
            k_ref[rows, sl] = (sk * (kh * gk + krr)).astype(BF16)

        u = z[:, COL_POOL:COL_POOL + POOL_W]
        xe = jnp.concatenate([carry_ref[...], u], axis=0)
        carry_ref[...] = u[sub - POOL_HALO:, :]
        s2 = xe + pltpu.roll(xe, 1, 0)
        s4 = s2 + pltpu.roll(s2, 2, 0)
        s8 = s4 + pltpu.roll(s4, 4, 0)
        s16 = s8 + pltpu.roll(s8, 8, 0)
        pooled = jnp.where(g0, s2[POOL_HALO:], jnp.where(g1, s4[POOL_HALO:],
                           jnp.where(g2, s8[POOL_HALO:], s16[POOL_HALO:])))
        cnt = jnp.minimum((seq_tile * tm + rows.start + row_p + 1).astype(F32), win)
        st["d"] = (pooled / cnt - u).astype(BF16)

    def project_pool(st):
        yp_ref[st["rows"], :] = (_dot(st["d"], wpool_ref[...]) * pscale_ref[...]).astype(BF16)

    _skewed([norm_in, project_in, norm_latents, project_up, heads_and_pool, project_pool], tm, sub)


def _mix_pre(h, w, rope_c, rope_s1, rope_s2, seq_len):
    T = h.shape[0]
    tm = TOKEN_TILE
    row = lambda n: pl.BlockSpec((tm, n), lambda i: (i, 0))
    ins = [h, w["mix_norm"], w["w_in"], w["gla_w_gate"], w["gla_b_gate"], w["mla_q_norm"], w["mla_w_uq"],
           w["mla_kv_norm"], w["mla_w_ukv_k"], w["mla_w_ukv_v"], w["mla_gq"], w["mla_gk"],
           rope_c, rope_s1, rope_s2, w["pool_w"], w["pool_scale"]]
    in_specs = [row(D_MODEL)] + [_full(a.shape) for a in ins[1:12]] + [row(HEAD_PAD)] * 3 + \
               [_full(w["pool_w"].shape), _full(w["pool_scale"].shape)]
    out_shape = [jax.ShapeDtypeStruct((T, COL_CQ), F32), jax.ShapeDtypeStruct((T, GLA_K), F32),
                 jax.ShapeDtypeStruct((T, MLA_QK_PAD), BF16), jax.ShapeDtypeStruct((T, MLA_QK_PAD), BF16),
                 jax.ShapeDtypeStruct((T, MLA_QK_PAD), BF16), jax.ShapeDtypeStruct((T, POOL_W), BF16)]
    out_specs = [row(COL_CQ), row(GLA_K), row(MLA_QK_PAD), row(MLA_QK_PAD), row(MLA_QK_PAD), row(POOL_W)]
    return pl.pallas_call(
        functools.partial(_mix_pre_kernel, tiles_per_seq=seq_len // tm),
        grid=(T // tm,), in_specs=in_specs, out_specs=out_specs, out_shape=out_shape,
        scratch_shapes=[pltpu.VMEM((POOL_HALO, POOL_W), F32)],
        compiler_params=_cparams(1), name="mix_pre")(*ins)


def _gla_kernel(zg_ref, la_ref, gn_ref, y_ref, state_ref, o_ref):
    tg = zg_ref.shape[0]
    C = GLA_CHUNK

    @pl.when(pl.program_id(1) == 0)
    def _():
        state_ref[...] = jnp.zeros_like(state_ref)

    r_i = lax.broadcasted_iota(jnp.int32, (C, C), 0)
    c_i = lax.broadcasted_iota(jnp.int32, (C, C), 1)
    tri = (r_i >= c_i).astype(BF16)
    ones = jnp.ones((C, GLA_W), BF16)
    head_k = lax.broadcasted_iota(jnp.int32, (C, GLA_K), 1) // GLA_DK
    head_v = lax.broadcasted_iota(jnp.int32, (C, GLA_W), 1) // GLA_DV
    ar = lax.broadcasted_iota(jnp.int32, (GLA_HEADS * C, C), 0)
    ac = lax.broadcasted_iota(jnp.int32, (GLA_HEADS * C, C), 1)
    causal = (ar % C) >= ac
    sk = lax.broadcasted_iota(jnp.int32, (GLA_K, GLA_W), 0) // GLA_DK
    sv = lax.broadcasted_iota(jnp.int32, (GLA_K, GLA_W), 1) // GLA_DV
    blockdiag = sk == sv

    def log_decay(st):
        la3 = _split3(la_ref[st["rows"], :])
        st["bc"] = _dot(tri, la3[0]) + _dot(tri, la3[1]) + _dot(tri, la3[2])
        st["dsum"] = _dot_tn(la3[0], ones) + _dot_tn(la3[1], ones) + _dot_tn(la3[2], ones)

    def scores(st):
        rows, bc = st["rows"], st["bc"]
        q = zg_ref[rows, COL_GQ:COL_GQ + GLA_K] * (GLA_DK ** -0.5)
        k = zg_ref[rows, COL_GK:COL_GK + GLA_K]
        b_last = bc[C - 1:C, :]
        q_dec = (q * jnp.exp(bc)).astype(BF16)
        k_dec = (k * jnp.exp(-bc)).astype(BF16)
        st["k_end"] = (k * jnp.exp(b_last - bc)).astype(BF16)
        st["decay"] = jnp.exp(st["dsum"])
        zero = jnp.zeros_like(q_dec)
        qs = jnp.concatenate([jnp.where(head_k == hh, q_dec, zero) for hh in range(GLA_HEADS)], axis=0)
        st["q_dec"] = q_dec
        st["att"] = _dot_nt(qs, k_dec)

    def values(st):
        v = zg_ref[st["rows"], COL_GV:COL_GV + GLA_W].astype(BF16)
        st["o_full"] = _dot(jnp.where(causal, st["att"], 0.0).astype(BF16), v)
        st["upd"] = jnp.where(blockdiag, _dot_tn(st["k_end"], v), 0.0)

    state = [state_ref[...]]

    def recur(st):
        o_full = st["o_full"]
        o = _dot(st["q_dec"], state[0].astype(BF16))
        for hh in range(GLA_HEADS):
            o = o + jnp.where(head_v == hh, o_full[hh * C:(hh + 1) * C, :], 0.0)
        o_ref[st["rows"], :] = o
        state[0] = st["decay"] * state[0] + st["upd"]

    _skewed([log_decay, scores, values, recur], tg, C)
    state_ref[...] = state[0]

    o = o_ref[...]
    gr = lax.broadcasted_iota(jnp.int32, (GLA_W, GLA_W), 0) // GLA_DV
    gc = lax.broadcasted_iota(jnp.int32, (GLA_W, GLA_W), 1) // GLA_DV
    group = (gr == gc).astype(BF16)
    oo = _split2(o * o)
    ms = (_dot(oo[0], group) + _dot(oo[1], group)) * (1.0 / GLA_DV)
    r = zg_ref[:, COL_GR:COL_GR + GLA_W]
    y = o * lax.rsqrt(ms + EPS) * gn_ref[...] * (r / (1.0 + jnp.exp(-r)))
    y_ref[...] = y.astype(BF16)


def _gla(zg, la, gn, batch, seq_len):
    T = zg.shape[0]
    tg = GLA_TILE
    nt = seq_len // tg
    return pl.pallas_call(
        _gla_kernel, grid=(batch, nt),
        in_specs=[pl.BlockSpec((tg, COL_CQ), lambda b, s: (b * nt + s, 0)),
                  pl.BlockSpec((tg, GLA_K), lambda b, s: (b * nt + s, 0)),
                  _full(gn.shape)],
        out_specs=pl.BlockSpec((tg, GLA_W), lambda b, s: (b * nt + s, 0)),
        out_shape=jax.ShapeDtypeStruct((T, GLA_W), BF16),
        scratch_shapes=[pltpu.VMEM((GLA_K, GLA_W), F32), pltpu.VMEM((tg, GLA_W), F32)],
        compiler_params=_cparams(2), name="gla")(zg, la, gn)


def _attn_kernel(q_ref, k_ref, v_ref, o_ref, m_ref, acc_ref):
    tq = q_ref.shape[0]
    ts = ATTN_SUB
    i = pl.program_id(2)
    m_ref[...] = jnp.full_like(m_ref, NEG_BIG)
    acc_ref[...] = jnp.zeros_like(acc_ref)

    def sub_block(hh, start, r0, mask_off):
        hs = slice(hh * HEAD_PAD, (hh + 1) * HEAD_PAD)
        kj = k_ref[pl.ds(start, ts), hs]
        vj = v_ref[pl.ds(start, ts), hs]
        s = _dot_nt(q_ref[r0:, hs], kj)
        if mask_off is not None:
            row = lax.broadcasted_iota(jnp.int32, s.shape, 0) + r0
            col = lax.broadcasted_iota(jnp.int32, s.shape, 1) + mask_off
            s = jnp.where(col <= row, s, NEG_BIG)
        m_old = m_ref[hh, r0:, :]
        parts = [s[:, c * 128:(c + 1) * 128] for c in range(ts // 128)]
        m_new = jnp.maximum(m_old, jnp.max(functools.reduce(jnp.maximum, parts), axis=-1, keepdims=True))
        p = jnp.concatenate([jnp.exp2((x - m_new).astype(BF16)) for x in parts], axis=1)
        acc_ref[hh, r0:, :] = jnp.exp2(m_old - m_new) * acc_ref[hh, r0:, :] + _dot(p, vj)
        m_ref[hh, r0:, :] = m_new

    def body(j, carry):
        base = pl.multiple_of(j * tq, tq)
        for sb in range(tq // ts):
            for hh in range(2):
                sub_block(hh, base + sb * ts, 0, None)
        return carry

    lax.fori_loop(0, i, body, 0)
    base = pl.multiple_of(i * tq, tq)
    for sb in range(tq // ts):
        for hh in range(2):
            sub_block(hh, base + sb * ts, sb * ts, sb * ts)
    outs = []
    for hh in range(2):
        a = acc_ref[hh]
        outs.append(a / a[:, MLA_V:MLA_V + 1])
    lane = lax.broadcasted_iota(jnp.int32, (tq, HEAD_PAD), 1)
    o_ref[...] = jnp.where(lane < MLA_V, outs[0], pltpu.roll(outs[1], MLA_V, 1)).astype(BF16)


def _attn(q, k, v, batch, seq_len):
    T = q.shape[0]
    tq = ATTN_TILE
    nq = seq_len // tq
    pairs = MLA_HEADS // 2
    return pl.pallas_call(
        _attn_kernel, grid=(batch, pairs, nq),
        in_specs=[pl.BlockSpec((tq, 2 * HEAD_PAD), lambda b, p, i: (b * nq + i, p)),
                  pl.BlockSpec((seq_len, 2 * HEAD_PAD), lambda b, p, i: (b, p)),
                  pl.BlockSpec((seq_len, 2 * HEAD_PAD), lambda b, p, i: (b, p))],
        out_specs=pl.BlockSpec((tq, 2 * MLA_V), lambda b, p, i: (b * nq + i, p)),
        out_shape=jax.ShapeDtypeStruct((T, MLA_W), BF16),
        scratch_shapes=[pltpu.VMEM((2, tq, HEAD_PAD), F32), pltpu.VMEM((2, tq, HEAD_PAD), F32)],
        compiler_params=_cparams(3), name="attn")(q, k, v)


def _out_router_kernel(h_ref, yg_ref, ym_ref, yp_ref, wo_ref, fn_ref, rw_ref, rb_ref,
                       h1_ref, hn0_ref, hn1_ref, idx_ref, gate_ref, cnt_ref, carry_ref):
    tm = h_ref.shape[0]

    @pl.when(pl.program_id(0) == 0)
    def _():
        carry_ref[...] = jnp.zeros_like(carry_ref)

    sub = ROUTER_SUB
    lane = lax.broadcasted_iota(jnp.int32, (sub, 128), 1)
    r_i = lax.broadcasted_iota(jnp.int32, (sub, sub), 0)
    c_i = lax.broadcasted_iota(jnp.int32, (sub, sub), 1)
    tri = (r_i >= c_i).astype(BF16)
    def project(st):
        rows = st["rows"]
        st["h1"] = (h_ref[rows, :] + _dot(yg_ref[rows, :], wo_ref[0:GLA_W, :])
                    + _dot(ym_ref[rows, :], wo_ref[GLA_W:GLA_W + MLA_W, :])
                    + _dot(yp_ref[rows, :], wo_ref[GLA_W + MLA_W:, :]))

    def normalize(st):
        rows = st["rows"]
        h1_ref[rows, :] = st["h1"]
        hn = _rms(st["h1"], fn_ref[...])
        st["hi"], st["lo"] = _split2(hn)
        packed = _pack_bf16_pairs(hn)
        slab = packed.shape[1] // DISPATCH_SLABS
        hn0_ref[rows, :] = packed[:, :slab]
        hn1_ref[rows, :] = packed[:, slab:]

    def score(st):
        r2 = _dot(st["hi"], rw_ref[...])
        st["logits"] = r2[:, :128] + r2[:, 128:] + _dot(st["lo"], rw_ref[:, 0:128]) + rb_ref[...]

    def select(st):
        rows = st["rows"]
        cur = jnp.where(lane < N_EXPERTS, st["logits"], NEG_BIG)
        idx_out = jnp.zeros((sub, 128), jnp.int32)
        val_out = jnp.zeros((sub, 128), F32)
        chosen = jnp.zeros((sub, 128), F32)
        top0 = None
        sels = []
        for kk in range(TOP_K):
            m = jnp.max(cur, axis=-1, keepdims=True)
            sel = jnp.min(jnp.where(cur == m, lane, 128), axis=-1, keepdims=True)
            if kk == 0:
                top0 = m
            sels.append(sel)
            idx_out = jnp.where(lane == kk, sel, idx_out)
            val_out = jnp.where(lane == kk, jnp.exp(m - top0), val_out)
            chosen = jnp.where(lane == sel, 1.0, chosen)
            cur = jnp.where(lane == sel, NEG_BIG, cur)
        gate_ref[rows, :] = val_out / jnp.sum(val_out, axis=-1, keepdims=True)

        incl = _dot(tri, chosen.astype(BF16))
        before = carry_ref[0:1, :] + incl - chosen
        for kk in range(TOP_K):
            rank = jnp.sum(jnp.where(lane == sels[kk], before, 0.0), axis=-1, keepdims=True)
            idx_out = jnp.where(lane == TOP_K + kk, rank.astype(jnp.int32), idx_out)
        idx_ref[rows, :] = idx_out
        carry_ref[...] = carry_ref[...] + incl[sub - 1:sub, :]

    _skewed([project, normalize, score, select], tm, sub)
    cnt_ref[...] = carry_ref[...].astype(jnp.int32)


def _out_router(h, yg, ym, yp, w):
    T = h.shape[0]
    tm = TOKEN_TILE
    row = lambda n: pl.BlockSpec((tm, n), lambda i: (i, 0))
    slab = D_MODEL // 2 // DISPATCH_SLABS
    ins = [h, yg, ym, yp, w["w_out"], w["ffn_norm"], w["router_w"], w["router_b"]]
    return pl.pallas_call(
        _out_router_kernel, grid=(T // tm,),
        in_specs=[row(D_MODEL), row(GLA_W), row(MLA_W), row(POOL_W)] + [_full(a.shape) for a in ins[4:]],
        out_specs=[row(D_MODEL), row(slab), row(slab), row(128), row(128), _full((8, 128))],
        out_shape=[jax.ShapeDtypeStruct((T, D_MODEL), F32), jax.ShapeDtypeStruct((T, slab), jnp.uint32),
                   jax.ShapeDtypeStruct((T, slab), jnp.uint32),
                   jax.ShapeDtypeStruct((T, 128), jnp.int32), jax.ShapeDtypeStruct((T, 128), F32),
                   jax.ShapeDtypeStruct((8, 128), jnp.int32)],
        scratch_shapes=[pltpu.VMEM((8, 128), F32)],
        compiler_params=_cparams(1), name="out_router")(*ins)


def _moe_kernel(be_ref, nb_ref, x0_ref, x1_ref, wg_ref, bg_ref, wu_ref, bu_ref, wd_ref, bd_ref,
                y0_ref, y1_ref, wg_bf, wu_bf, wd_bf):
    i = pl.program_id(0)
    used = i < nb_ref[0]
    new_expert = (i == 0) | (be_ref[i] != be_ref[jnp.maximum(i - 1, 0)])

    @pl.when(used & new_expert)
    def _():
        for src, dst in ((wg_ref, wg_bf), (wu_ref, wu_bf), (wd_ref, wd_bf)):
            for r in range(0, src.shape[2], MOE_CAST_ROWS):
                dst[r:r + MOE_CAST_ROWS, :] = src[0, 0, r:r + MOE_CAST_ROWS, :].astype(BF16)

    @pl.when(used)
    def _():
        halves = [_unpack_bf16_pairs(r[...]) for r in (x0_ref, x1_ref)]
        x = jnp.concatenate([h[0] for h in halves] + [h[1] for h in halves], axis=1).astype(BF16)
        g = jnp.minimum(_dot(x, wg_bf[...]) + bg_ref[0], SWIGLU_LIMIT)
        up = jnp.clip(_dot(x, wu_bf[...]) + bu_ref[0], -SWIGLU_LIMIT, SWIGLU_LIMIT)
        hb = (up + 1.0) * (g / (1.0 + jnp.exp(-SWIGLU_ALPHA * g)))
        packed = _pack_bf16_pairs(_dot(hb.astype(BF16), wd_bf[...]) + bd_ref[0])
        slab = y0_ref.shape[1]
        y0_ref[...] = packed[:, :slab]
        y1_ref[...] = packed[:, slab:]

    @pl.when(jnp.logical_not(used))
    def _():
        y0_ref[...] = jnp.zeros_like(y0_ref)
        y1_ref[...] = jnp.zeros_like(y1_ref)


def _moe(xs, block_e, n_used, w):
    n_rows, slab = xs[0].shape
    bm = MOE_BLOCK
    layer = w["layer"]
    wspec = lambda shp: pl.BlockSpec((1, 1) + shp, lambda i, be, nb: (layer, be[i], 0, 0))
    bspec = lambda shp: pl.BlockSpec((1,) + shp, lambda i, be, nb: (be[i], 0, 0))
    grid_spec = pltpu.PrefetchScalarGridSpec(
        num_scalar_prefetch=2, grid=(n_rows // bm,),
        in_specs=[pl.BlockSpec((bm, slab), lambda i, be, nb: (i, 0))] * DISPATCH_SLABS + [
                  wspec((D_MODEL, D_FF)), bspec((1, D_FF)), wspec((D_MODEL, D_FF)), bspec((1, D_FF)),
                  wspec((D_FF, D_MODEL)), bspec((1, D_MODEL))],
        out_specs=[pl.BlockSpec((bm, slab), lambda i, be, nb: (i, 0))] * DISPATCH_SLABS,
        scratch_shapes=[pltpu.VMEM((D_MODEL, D_FF), BF16), pltpu.VMEM((D_MODEL, D_FF), BF16),
                        pltpu.VMEM((D_FF, D_MODEL), BF16)])
    return pl.pallas_call(
        _moe_kernel, grid_spec=grid_spec,
        out_shape=[jax.ShapeDtypeStruct((n_rows, slab), jnp.uint32)] * DISPATCH_SLABS,
        compiler_params=_cparams(1), name="moe")(
            block_e, n_used, *xs, w["moe_w_gate"], w["moe_b_gate"], w["moe_w_up"], w["moe_b_up"],
            w["moe_w_down"], w["moe_b_down"])


def _ple_kernel(h1_ref, ya0_ref, ya1_ref, ya2_ref, ya3_ref, yb0_ref, yb1_ref, yb2_ref, yb3_ref, gate_ref, p_ref,
                wple_ref, gn_ref, wpg_ref, pn_ref, o_ref):
    def combine(st):
        rows = st["rows"]
        gates = gate_ref[rows, :]
        h2 = h1_ref[rows, :]
        for kk, (ya_ref, yb_ref) in enumerate(((ya0_ref, yb0_ref), (ya1_ref, yb1_ref), (ya2_ref, yb2_ref),
                                               (ya3_ref, yb3_ref))):
            lo_a, hi_a = _unpack_bf16_pairs(ya_ref[rows, :])
            lo_b, hi_b = _unpack_bf16_pairs(yb_ref[rows, :])
            h2 = h2 + gates[:, kk:kk + 1] * jnp.concatenate([lo_a, lo_b, hi_a, hi_b], axis=1)
        st["h2"] = h2
        st["hn"] = _rms(h2, gn_ref[...]).astype(BF16)

    def project(st):
        st["e"] = _dot(p_ref[0, st["rows"], :].astype(BF16), wple_ref[...])
        st["a"] = _dot(st["hn"], wpg_ref[...])

    def finish(st):
        gate = 1.0 / (1.0 + jnp.exp(-st["a"]))
        o_ref[st["rows"], :] = st["h2"] + _rms(st["e"] * gate, pn_ref[...])

    _skewed([combine, project, finish], h1_ref.shape[0], PLE_SUB)


def _ple(h1, y_slabs, gates, p, w, part):
    T = h1.shape[0]
    tm = TOKEN_TILE
    steps = T // COMBINE_PARTS // tm
    off = part * steps
    row = lambda n: pl.BlockSpec((tm, n), lambda i: (i + off, 0))
    slab = y_slabs[0].shape[1]
    gathered = lambda kk: pl.BlockSpec((tm, slab), lambda i: (kk * steps + i, 0))
    weights = [w["ple_w_proj"], w["ple_gate_norm"], w["ple_w_gate"], w["ple_post_norm"]]
    layer = w["layer"]
    p_spec = pl.BlockSpec((1, tm, D_PLE), lambda i: (layer, i + off, 0))
    ins = [h1] + [y for y in y_slabs for _ in range(TOP_K)] + [gates, p, *weights]
    in_specs = ([row(D_MODEL)] + [gathered(kk) for _ in y_slabs for kk in range(TOP_K)] + [row(128), p_spec]
                + [_full(a.shape) for a in weights])
    return pl.pallas_call(
        _ple_kernel, grid=(steps,), in_specs=in_specs,
        out_specs=row(D_MODEL), out_shape=jax.ShapeDtypeStruct((T, D_MODEL), F32),
        input_output_aliases={0: 0}, compiler_params=_cparams(1), name="ple")(*ins)


def _pad_heads(wm, per_head, n_heads=MLA_HEADS):
    kdim = wm.shape[0]
    w3 = wm.reshape(kdim, n_heads, per_head)
    return jnp.pad(w3, ((0, 0), (0, 0), (0, HEAD_PAD - per_head))).reshape(kdim, n_heads * HEAD_PAD)


def _swap_rope_halves(a):
    a3 = a.reshape(a.shape[0], -1, HEAD_PAD)
    half = MLA_ROPE // 2
    x1 = a3[:, :, MLA_NOPE:MLA_NOPE + half]
    x2 = a3[:, :, MLA_NOPE + half:MLA_QK]
    out = jnp.zeros_like(a3).at[:, :, MLA_NOPE:MLA_NOPE + half].set(x2).at[:, :, MLA_NOPE + half:MLA_QK].set(x1)
    return out.reshape(a.shape)


def _layer_params(i, mix_norm, w_in, gla_w_gate, gla_b_gate, gla_out_norm, mla_q_norm, mla_w_uq, mla_kv_norm,
                  mla_w_ukv, mla_qk_q_norm, mla_qk_k_norm, pool_w, pool_scale, w_out, ffn_norm, router_w,
                  router_b, moe_w_gate, moe_b_gate, moe_w_up, moe_b_up, moe_w_down, moe_b_down,
                  ple_w_proj, ple_gate_norm, ple_w_gate, ple_post_norm):
    wi = w_in[i]
    c = np.cumsum((0, 128, 128, 256, 16, 256, 256, 128, 32, 256))
    gq, gk, gv, glow, gr, cq, ckv, krope, upool = [wi[:, c[j]:c[j + 1]] for j in range(9)]
    misc = jnp.concatenate([glow, krope, jnp.zeros((D_MODEL, 128 - 48), F32)], axis=1)
    w_in_p = jnp.concatenate([gq, gk, gv, gr, cq, upool, ckv, misc], axis=1).astype(BF16)
    wgate_p = jnp.zeros((128, GLA_K), F32).at[MISC_GLOW:MISC_GLOW + GLA_GATE_RANK].set(gla_w_gate[i]).astype(BF16)
    ukv = mla_w_ukv[i].reshape(MLA_KV_RANK, MLA_HEADS, MLA_NOPE + MLA_V)
    ukv_k = _pad_heads(ukv[:, :, :MLA_NOPE].reshape(MLA_KV_RANK, MLA_HEADS * MLA_NOPE), MLA_NOPE)
    ukv_v = _pad_heads(ukv[:, :, MLA_NOPE:].reshape(MLA_KV_RANK, MLA_W), MLA_V)
    pw = pool_w[i]
    pool_bd = jnp.zeros((POOL_W, POOL_W), F32)
    for g in range(4):
        pool_bd = pool_bd.at[g * 64:(g + 1) * 64, g * 64:(g + 1) * 64].set(pw[g])
    rw = jnp.pad(router_w[i], ((0, 0), (0, 128 - N_EXPERTS)))
    rw_hi = rw.astype(BF16)
    rw_lo = (rw - rw_hi.astype(F32)).astype(BF16)
    row = lambda a: a.reshape(1, -1)
    pad96 = lambda a: jnp.pad(a, (0, HEAD_PAD - MLA_QK)).reshape(1, HEAD_PAD)
    wuq_p = _pad_heads(mla_w_uq[i], MLA_QK)
    gq_p = pad96(mla_qk_q_norm[i] * (MLA_QK ** -0.5 * LOG2E))
    return {
        "mix_norm": row(mix_norm[i]), "w_in": w_in_p, "gla_w_gate": wgate_p, "gla_b_gate": row(gla_b_gate[i]),
        "gla_out_norm": row(jnp.tile(gla_out_norm[i], GLA_HEADS)),
        "mla_q_norm": row(mla_q_norm[i]),
        "mla_w_uq": jnp.concatenate([wuq_p, _swap_rope_halves(wuq_p)], axis=1).astype(BF16),
        "mla_kv_norm": row(mla_kv_norm[i]), "mla_w_ukv_k": ukv_k.astype(BF16), "mla_w_ukv_v": ukv_v.astype(BF16),
        "mla_gq": jnp.concatenate([gq_p, _swap_rope_halves(gq_p)], axis=0), "mla_gk": pad96(mla_qk_k_norm[i]),
        "pool_w": pool_bd.astype(BF16), "pool_scale": row(pool_scale[i]),
        "w_out": w_out[i].astype(BF16), "ffn_norm": row(ffn_norm[i]),
        "router_w": jnp.concatenate([rw_hi, rw_lo], axis=1),
        "router_b": row(jnp.pad(router_b[i], (0, 128 - N_EXPERTS))),
        "layer": i,
        "moe_w_gate": moe_w_gate, "moe_b_gate": moe_b_gate[i].reshape(N_EXPERTS, 1, D_FF),
        "moe_w_up": moe_w_up, "moe_b_up": moe_b_up[i].reshape(N_EXPERTS, 1, D_FF),
        "moe_w_down": moe_w_down, "moe_b_down": moe_b_down[i].reshape(N_EXPERTS, 1, D_MODEL),
        "ple_w_proj": ple_w_proj[i].astype(BF16), "ple_gate_norm": row(ple_gate_norm[i]),
        "ple_w_gate": ple_w_gate[i].astype(BF16), "ple_post_norm": row(ple_post_norm[i]),
    }


def _rope_tables(positions):
    T = positions.size
    inv = ROPE_BASE ** (-jnp.arange(0, MLA_ROPE, 2, dtype=F32) / MLA_ROPE)
    ang = positions.reshape(T, 1).astype(F32) * inv
    cos, sin = jnp.cos(ang), jnp.sin(ang)
    z16 = jnp.zeros((T, 16), F32)
    tail = jnp.zeros((T, HEAD_PAD - MLA_QK), F32)
    c = jnp.concatenate([jnp.ones((T, MLA_NOPE), F32), cos, cos, tail], axis=1)
    s1 = jnp.concatenate([jnp.zeros((T, MLA_NOPE), F32), -sin, z16, tail], axis=1)
    s2 = jnp.concatenate([jnp.zeros((T, MLA_NOPE), F32), z16, sin, tail], axis=1)
    return c, s1, s2


def _route(top_idx, rank, counts, T):
    bm = MOE_BLOCK
    A = T * TOP_K
    padded = (counts + bm - 1) // bm * bm
    pad_end = jnp.cumsum(padded)
    pad_start = pad_end - padded
    experts = jnp.arange(N_EXPERTS, dtype=jnp.int32)
    dest = rank + jnp.sum(jnp.where(top_idx[:, :, None] == experts, pad_start, 0), axis=-1)
    n_blocks = (A + N_EXPERTS * (bm - 1) + bm - 1) // bm
    n_rows = n_blocks * bm
    block_start = jnp.arange(n_blocks, dtype=jnp.int32) * bm
    block_e = jnp.minimum(jnp.sum((pad_end[None, :] <= block_start[:, None]).astype(jnp.int32), axis=1),
                          N_EXPERTS - 1)
    n_used = (pad_end[-1] // bm).astype(jnp.int32).reshape(1)
    return dest, n_rows, block_e, n_used


def _dispatch(hn_slabs, dest, n_rows):
    T, width = hn_slabs[0].shape
    win = DISPATCH_ROWS
    dest_t = dest.T
    mesh = plsc.VectorSubcoreMesh(core_axis_name="core", subcore_axis_name="subcore")

    @functools.partial(pl.kernel, out_type=jax.ShapeDtypeStruct((n_rows, width), hn_slabs[0].dtype), mesh=mesh,
                       scratch_types=[], name="dispatch")
    def scatter_rows(x_hbm, i_hbm, o_hbm):
        def body(x_vmem, i_vmem):
            for kk in range(TOP_K):
                pltpu.sync_copy(x_vmem, o_hbm.at[i_vmem.at[kk]])

        pltpu.emit_pipeline(
            body, grid=(T // win,),
            in_specs=[pl.BlockSpec((win, width), lambda i: (i, 0)), pl.BlockSpec((TOP_K, win), lambda i: (0, i))],
            out_specs=[], core_axis_name=("core", "subcore"),
            dimension_semantics=(pltpu.PARALLEL,))(x_hbm, i_hbm)

    return [scatter_rows(slab, dest_t) for slab in hn_slabs]


def _combine_gather(y_slabs, dest_t):
    n_k, n_tok = dest_t.shape
    win = DISPATCH_ROWS
    width = y_slabs[0].shape[1]
    steps = n_tok // win
    mesh = plsc.VectorSubcoreMesh(core_axis_name="core", subcore_axis_name="subcore")

    @functools.partial(pl.kernel, out_type=jax.ShapeDtypeStruct((n_k * n_tok, width), y_slabs[0].dtype),
                       mesh=mesh, scratch_types=[], name="combine")
    def gather_rows(y_hbm, i_hbm, o_hbm):
        def body(i_vmem, o_vmem):
            pltpu.sync_copy(y_hbm.at[i_vmem.at[0]], o_vmem)

        pltpu.emit_pipeline(
            body, grid=(n_k, steps),
            in_specs=[pl.BlockSpec((1, win), lambda k, i: (k, i))],
            out_specs=[pl.BlockSpec((win, width), lambda k, i: (k * steps + i, 0))],
            core_axis_name=("core", "subcore"),
            dimension_semantics=(pltpu.PARALLEL, pltpu.PARALLEL))(i_hbm, o_hbm)

    return [gather_rows(y, dest_t) for y in y_slabs]


def kernel(x, p, positions, mix_norm, w_in, gla_w_gate, gla_b_gate, gla_out_norm, mla_q_norm, mla_w_uq,
           mla_kv_norm, mla_w_ukv, mla_qk_q_norm, mla_qk_k_norm, pool_w, pool_scale, w_out, ffn_norm,
           router_w, router_b, moe_w_gate, moe_b_gate, moe_w_up, moe_b_up, moe_w_down, moe_b_down,
           ple_w_proj, ple_gate_norm, ple_w_gate, ple_post_norm):
    B, S, D = x.shape
    T = B * S
    depth = p.shape[0]
    params = (mix_norm, w_in, gla_w_gate, gla_b_gate, gla_out_norm, mla_q_norm, mla_w_uq, mla_kv_norm,
              mla_w_ukv, mla_qk_q_norm, mla_qk_k_norm, pool_w, pool_scale, w_out, ffn_norm, router_w,
              router_b, moe_w_gate, moe_b_gate, moe_w_up, moe_b_up, moe_w_down, moe_b_down,
              ple_w_proj, ple_gate_norm, ple_w_gate, ple_post_norm)
    rope_c, rope_s1, rope_s2 = _rope_tables(positions)
    p_flat = p.reshape(depth, T, D_PLE)
    h = x.reshape(T, D)
    for i in range(depth):
        w = _layer_params(i, *params)
        zg, la, q, k, v, y_pool = _mix_pre(h, w, rope_c, rope_s1, rope_s2, S)
        y_gla = _gla(zg, la, w["gla_out_norm"], B, S)
        y_mla = _attn(q, k, v, B, S)
        h1, hn0, hn1, route, gates, counts = _out_router(h, y_gla, y_mla, y_pool, w)
        dest, n_rows, block_e, n_used = _route(route[:, :TOP_K], route[:, TOP_K:2 * TOP_K],
                                               counts[0, :N_EXPERTS], T)
        ys = _moe(_dispatch([hn0, hn1], dest, n_rows), block_e, n_used, w)
        h = h1
        for part in range(COMBINE_PARTS):
            d = dest[part * (T // COMBINE_PARTS):(part + 1) * (T // COMBINE_PARTS)]
            h = _ple(h, _combine_gather(ys, d.T), gates, p_flat, w, part)
    return h.reshape(B, S, D)
```

```python
import functools

import jax
import jax.numpy as jnp
import numpy as np
from jax import lax
from jax.experimental import pallas as pl
from jax.experimental.pallas import tpu as pltpu
from jax.experimental.pallas import tpu_sc as plsc

F32 = jnp.float32
BF16 = jnp.bfloat16

D_MODEL = 1024
EPS = 1e-6
D_PLE = 256

GLA_HEADS = 4
GLA_DK = 32
GLA_DV = 64
GLA_GATE_RANK = 16
GLA_TAU = 16.0
GLA_CHUNK = 64
GLA_K = GLA_HEADS * GLA_DK
GLA_W = GLA_HEADS * GLA_DV

MLA_HEADS = 8
MLA_Q_RANK = 256
MLA_KV_RANK = 128
MLA_NOPE = 64
MLA_ROPE = 32
MLA_QK = MLA_NOPE + MLA_ROPE
MLA_V = 64
MLA_W = MLA_HEADS * MLA_V
ROPE_BASE = 10000.0
HEAD_PAD = 128
MLA_QK_PAD = MLA_HEADS * HEAD_PAD

POOL_WINDOWS = (2, 4, 8, 16)
POOL_GROUP = 64
POOL_W = 256
POOL_HALO = 16

N_EXPERTS = 32
TOP_K = 4
D_FF = 1024
SWIGLU_LIMIT = 7.0
SWIGLU_ALPHA = 1.702

COL_GQ, COL_GK, COL_GV, COL_GR, COL_CQ, COL_POOL, COL_CKV, COL_MISC = 0, 128, 256, 512, 768, 1024, 1280, 1408
D_IN_PAD = 1536
MISC_GLOW = 0
MISC_ROPE = 16

LOG2E = 1.4426950408889634
TOKEN_TILE = 1024
MIX_SUB, ROUTER_SUB, PLE_SUB = 256, 256, 128
ATTN_TILE = 2048
ATTN_SUB = 512
MOE_BLOCK = 512
MOE_CAST_ROWS = 256
COMBINE_PARTS = 4
DISPATCH_ROWS = 128
DISPATCH_SLABS = 2
VMEM_LIMIT = 56 * 1024 * 1024
NEG_BIG = -1e30


def _cparams(n_axes, **flags):
    return pltpu.CompilerParams(dimension_semantics=("arbitrary",) * n_axes,
                                vmem_limit_bytes=VMEM_LIMIT, flags=flags or None)


def _rms(x, g):
    return x * lax.rsqrt(jnp.mean(x * x, axis=-1, keepdims=True) + EPS) * g


def _dot(a, b):
    return jnp.dot(a, b, preferred_element_type=F32)


def _dot_nt(a, b):
    return lax.dot_general(a, b, (((1,), (1,)), ((), ())), preferred_element_type=F32)


def _dot_tn(a, b):
    return lax.dot_general(a, b, (((0,), (0,)), ((), ())), preferred_element_type=F32)


def _split3(x):
    hi = x.astype(BF16)
    r = x - hi.astype(F32)
    mid = r.astype(BF16)
    lo = (r - mid.astype(F32)).astype(BF16)
    return hi, mid, lo


def _split2(x):
    hi = x.astype(BF16)
    lo = (x - hi.astype(F32)).astype(BF16)
    return hi, lo


def _pack_bf16_pairs(x):
    m = x.shape[1] // 2
    bits = lax.bitcast_convert_type(x.astype(BF16).astype(F32), jnp.uint32)
    return (bits[:, :m] >> 16) | (bits[:, m:] & jnp.uint32(0xFFFF0000))


def _unpack_bf16_pairs(w):
    lo = lax.bitcast_convert_type(w << 16, F32)
    hi = lax.bitcast_convert_type(w & jnp.uint32(0xFFFF0000), F32)
    return lo, hi


def _skewed(stages, n_rows, sub):
    states = [{"rows": slice(r0, r0 + sub)} for r0 in range(0, n_rows, sub)]
    for step in range(len(states) + len(stages) - 1):
        for s, stage in enumerate(stages):
            t = step - s
            if 0 <= t < len(states):
                stage(states[t])


def _full(shape):
    nd = len(shape)
    return pl.BlockSpec(shape, lambda *_: (0,) * nd)


def _rope(x, c, s1, s2):
    return x * c + pltpu.roll(x, HEAD_PAD - 16, 1) * s1 + pltpu.roll(x, 16, 1) * s2


def _mix_pre_kernel(h_ref, mixn_ref, win_ref, wgate_ref, bgate_ref, qn_ref, wuq_ref, kvn_ref,
                    wukvk_ref, wukvv_ref, gq_ref, gk_ref, rc_ref, rs1_ref, rs2_ref,
                    wpool_ref, pscale_ref, gn_ref,
                    yg_ref, q_ref, k_ref, v_ref, yp_ref, carry_ref, gla_state_ref, *, tiles_per_seq):
    tm = h_ref.shape[0]
    sub = MIX_SUB
    seq_tile = pl.program_id(0) % tiles_per_seq

    @pl.when(seq_tile == 0)
    def _():
        carry_ref[...] = jnp.zeros_like(carry_ref)
        gla_state_ref[...] = jnp.zeros_like(gla_state_ref)

    gla_consts = _gla_consts()
    gla_state = [gla_state_ref[...]]

    lane = lax.broadcasted_iota(jnp.int32, (sub, HEAD_PAD), 1)
    in_rope = (lane >= MLA_NOPE) & (lane < MLA_QK)
    lane_v = lax.broadcasted_iota(jnp.int32, (sub, MLA_QK_PAD), 1)
    ones_lane = lane_v % HEAD_PAD == MLA_V
    lane_p = lax.broadcasted_iota(jnp.int32, (sub, POOL_W), 1)
    row_p = lax.broadcasted_iota(jnp.int32, (sub, POOL_W), 0)
    g0, g1, g2 = lane_p < 64, lane_p < 128, lane_p < 192
    win = jnp.where(g0, 2.0, jnp.where(g1, 4.0, jnp.where(g2, 8.0, 16.0)))
    gq, gq_sw, gk = gq_ref[0:1, :], gq_ref[1:2, :], gk_ref[...]

    def norm_in(st):
        st["hn"] = _rms(h_ref[st["rows"], :], mixn_ref[...]).astype(BF16)

    def project_in(st):
        st["z"] = _dot(st["hn"], win_ref[...])

    def norm_latents(st):
        z = st["z"]
        st["cqn"] = _rms(z[:, COL_CQ:COL_CQ + MLA_Q_RANK], qn_ref[...]).astype(BF16)
        st["ckvn"] = _rms(z[:, COL_CKV:COL_CKV + MLA_KV_RANK], kvn_ref[...]).astype(BF16)

    def project_up(st):
        zm = st["z"][:, COL_MISC:COL_MISC + 128]
        st["logit"] = _dot(zm.astype(BF16), wgate_ref[...]) + bgate_ref[...]
        st["qf"] = _dot(st["cqn"], wuq_ref[...])
        st["kn"] = _dot(st["ckvn"], wukvk_ref[...])
        st["v"] = _dot(st["ckvn"], wukvv_ref[...])

    def heads_and_pool(st):
        rows, z, qf, kn, logit = st["rows"], st["z"], st["qf"], st["kn"], st["logit"]
        zm = z[:, COL_MISC:COL_MISC + 128]
        st["la"] = (jnp.minimum(logit, 0.0) - jnp.log(1.0 + jnp.exp(-jnp.abs(logit)))) * (1.0 / GLA_TAU)
        v_ref[rows, :] = jnp.where(ones_lane, 1.0, st["v"]).astype(BF16)

        rc, rs1, rs2 = rc_ref[rows, :], rs1_ref[rows, :], rs2_ref[rows, :]
        kr = jnp.where(in_rope, pltpu.roll(zm, MLA_NOPE - MISC_ROPE, 1), 0.0)
        kr_ss = jnp.sum(kr * kr, axis=-1, keepdims=True)
        krr = _rope(kr * gk, rc, rs1, rs2)
        cq = rc * gq
        sq_tab = (rs1 + rs2) * gq_sw
        for hh in range(MLA_HEADS):
            sl = slice(hh * HEAD_PAD, (hh + 1) * HEAD_PAD)
            qh = qf[:, sl]
            qsw = qf[:, MLA_QK_PAD + hh * HEAD_PAD:MLA_QK_PAD + (hh + 1) * HEAD_PAD]
            sq = lax.rsqrt(jnp.sum(qh * qh, axis=-1, keepdims=True) * (1.0 / MLA_QK) + EPS)
            q_ref[rows, sl] = ((qh * cq + qsw * sq_tab) * sq).astype(BF16)
            kh = kn[:, sl]
            sk = lax.rsqrt((jnp.sum(kh * kh, axis=-1, keepdims=True) + kr_ss) * (1.0 / MLA_QK) + EPS)
            k_ref[rows, sl] = (sk * (kh * gk + krr)).astype(BF16)

        u = z[:, COL_POOL:COL_POOL + POOL_W]
        xe = jnp.concatenate([carry_ref[...], u], axis=0)
        carry_ref[...] = u[sub - POOL_HALO:, :]
        s2 = xe + pltpu.roll(xe, 1, 0)
        s4 = s2 + pltpu.roll(s2, 2, 0)
        s8 = s4 + pltpu.roll(s4, 4, 0)
        s16 = s8 + pltpu.roll(s8, 8, 0)
        pooled = jnp.where(g0, s2[POOL_HALO:], jnp.where(g1, s4[POOL_HALO:],
                           jnp.where(g2, s8[POOL_HALO:], s16[POOL_HALO:])))
        cnt = jnp.minimum((seq_tile * tm + rows.start + row_p + 1).astype(F32), win)
        st["d"] = (pooled / cnt - u).astype(BF16)

    def project_pool(st):
        yp_ref[st["rows"], :] = (_dot(st["d"], wpool_ref[...]) * pscale_ref[...]).astype(BF16)

    def gla(st):
        yg_ref[st["rows"], :] = _gla_rows(st["z"], st["la"], gn_ref[...], gla_consts, gla_state)

    _skewed([norm_in, project_in, norm_latents, project_up, heads_and_pool, project_pool, gla], tm, sub)
    gla_state_ref[...] = gla_state[0]


def _mix_pre(h, w, rope_c, rope_s1, rope_s2, seq_len):
    T = h.shape[0]
    tm = TOKEN_TILE
    row = lambda n: pl.BlockSpec((tm, n), lambda i: (i, 0))
    ins = [h, w["mix_norm"], w["w_in"], w["gla_w_gate"], w["gla_b_gate"], w["mla_q_norm"], w["mla_w_uq"],
           w["mla_kv_norm"], w["mla_w_ukv_k"], w["mla_w_ukv_v"], w["mla_gq"], w["mla_gk"],
           rope_c, rope_s1, rope_s2, w["pool_w"], w["pool_scale"], w["gla_out_norm"]]
    in_specs = [row(D_MODEL)] + [_full(a.shape) for a in ins[1:12]] + [row(HEAD_PAD)] * 3 + \
               [_full(a.shape) for a in ins[15:]]
    out_shape = [jax.ShapeDtypeStruct((T, GLA_W), BF16),
                 jax.ShapeDtypeStruct((T, MLA_QK_PAD), BF16), jax.ShapeDtypeStruct((T, MLA_QK_PAD), BF16),
                 jax.ShapeDtypeStruct((T, MLA_QK_PAD), BF16), jax.ShapeDtypeStruct((T, POOL_W), BF16)]
    out_specs = [row(GLA_W), row(MLA_QK_PAD), row(MLA_QK_PAD), row(MLA_QK_PAD), row(POOL_W)]
    return pl.pallas_call(
        functools.partial(_mix_pre_kernel, tiles_per_seq=seq_len // tm),
        grid=(T // tm,), in_specs=in_specs, out_specs=out_specs, out_shape=out_shape,
        scratch_shapes=[pltpu.VMEM((POOL_HALO, POOL_W), F32), pltpu.VMEM((GLA_K, GLA_W), F32)],
        compiler_params=_cparams(1), name="mix_pre")(*ins)


def _gla_consts():
    C = GLA_CHUNK
    r_i = lax.broadcasted_iota(jnp.int32, (C, C), 0)
    c_i = lax.broadcasted_iota(jnp.int32, (C, C), 1)
    ar = lax.broadcasted_iota(jnp.int32, (GLA_HEADS * C, C), 0)
    ac = lax.broadcasted_iota(jnp.int32, (GLA_HEADS * C, C), 1)
    sk = lax.broadcasted_iota(jnp.int32, (GLA_K, GLA_W), 0) // GLA_DK
    sv = lax.broadcasted_iota(jnp.int32, (GLA_K, GLA_W), 1) // GLA_DV
    gr = lax.broadcasted_iota(jnp.int32, (GLA_W, GLA_W), 0) // GLA_DV
    gc = lax.broadcasted_iota(jnp.int32, (GLA_W, GLA_W), 1) // GLA_DV
    return {
        "tri": (r_i >= c_i).astype(BF16),
        "ones": jnp.ones((C, GLA_W), BF16),
        "head_k": lax.broadcasted_iota(jnp.int32, (C, GLA_K), 1) // GLA_DK,
        "head_v": lax.broadcasted_iota(jnp.int32, (C, GLA_W), 1) // GLA_DV,
        "causal": (ar % C) >= ac,
        "blockdiag": sk == sv,
        "group": (gr == gc).astype(BF16),
    }


def _gla_rows(z, la, gn, consts, state):
    C = GLA_CHUNK
    tri, ones, head_k, head_v = consts["tri"], consts["ones"], consts["head_k"], consts["head_v"]
    causal, blockdiag = consts["causal"], consts["blockdiag"]
    outs = []

    def log_decay(st):
        la3 = _split3(la[st["rows"], :])
        st["bc"] = _dot(tri, la3[0]) + _dot(tri, la3[1]) + _dot(tri, la3[2])
        st["dsum"] = _dot_tn(la3[0], ones) + _dot_tn(la3[1], ones) + _dot_tn(la3[2], ones)

    def scores(st):
        rows, bc = st["rows"], st["bc"]
        q = z[rows, COL_GQ:COL_GQ + GLA_K] * (GLA_DK ** -0.5)
        k = z[rows, COL_GK:COL_GK + GLA_K]
        b_last = bc[C - 1:C, :]
        q_dec = (q * jnp.exp(bc)).astype(BF16)
        k_dec = (k * jnp.exp(-bc)).astype(BF16)
        st["k_end"] = (k * jnp.exp(b_last - bc)).astype(BF16)
        st["decay"] = jnp.exp(st["dsum"])
        zero = jnp.zeros_like(q_dec)
        qs = jnp.concatenate([jnp.where(head_k == hh, q_dec, zero) for hh in range(GLA_HEADS)], axis=0)
        st["q_dec"] = q_dec
        st["att"] = _dot_nt(qs, k_dec)

    def values(st):
        v = z[st["rows"], COL_GV:COL_GV + GLA_W].astype(BF16)
        st["o_full"] = _dot(jnp.where(causal, st["att"], 0.0).astype(BF16), v)
        st["upd"] = jnp.where(blockdiag, _dot_tn(st["k_end"], v), 0.0)

    def recur(st):
        o_full = st["o_full"]
        o = _dot(st["q_dec"], state[0].astype(BF16))
        for hh in range(GLA_HEADS):
            o = o + jnp.where(head_v == hh, o_full[hh * C:(hh + 1) * C, :], 0.0)
        outs.append(o)
        state[0] = st["decay"] * state[0] + st["upd"]

    _skewed([log_decay, scores, values, recur], z.shape[0], C)

    o = jnp.concatenate(outs, axis=0)
    oo = _split2(o * o)
    ms = (_dot(oo[0], consts["group"]) + _dot(oo[1], consts["group"])) * (1.0 / GLA_DV)
    r = z[:, COL_GR:COL_GR + GLA_W]
    return (o * lax.rsqrt(ms + EPS) * gn * (r / (1.0 + jnp.exp(-r)))).astype(BF16)


def _attn_kernel(q_ref, k_ref, v_ref, o_ref, m_ref, acc_ref):
    tq = q_ref.shape[0]
    ts = ATTN_SUB
    i = pl.program_id(2)
    m_ref[...] = jnp.full_like(m_ref, NEG_BIG)
    acc_ref[...] = jnp.zeros_like(acc_ref)

    def sub_block(hh, start, r0, mask_off):
        hs = slice(hh * HEAD_PAD, (hh + 1) * HEAD_PAD)
        kj = k_ref[pl.ds(start, ts), hs]
        vj = v_ref[pl.ds(start, ts), hs]
        s = _dot_nt(q_ref[r0:, hs], kj)
        if mask_off is not None:
            row = lax.broadcasted_iota(jnp.int32, s.shape, 0) + r0
            col = lax.broadcasted_iota(jnp.int32, s.shape, 1) + mask_off
            s = jnp.where(col <= row, s, NEG_BIG)
        m_old = m_ref[hh, r0:, :]
        parts = [s[:, c * 128:(c + 1) * 128] for c in range(ts // 128)]
        m_new = jnp.maximum(m_old, jnp.max(functools.reduce(jnp.maximum, parts), axis=-1, keepdims=True))
        p = jnp.concatenate([jnp.exp2((x - m_new).astype(BF16)) for x in parts], axis=1)
        acc_ref[hh, r0:, :] = jnp.exp2(m_old - m_new) * acc_ref[hh, r0:, :] + _dot(p, vj)
        m_ref[hh, r0:, :] = m_new

    def body(j, carry):
        base = pl.multiple_of(j * tq, tq)
        for sb in range(tq // ts):
            for hh in range(2):
                sub_block(hh, base + sb * ts, 0, None)
        return carry

    lax.fori_loop(0, i, body, 0)
    base = pl.multiple_of(i * tq, tq)
    for sb in range(tq // ts):
        for hh in range(2):
            sub_block(hh, base + sb * ts, sb * ts, sb * ts)
    outs = []
    for hh in range(2):
        a = acc_ref[hh]
        outs.append(a / a[:, MLA_V:MLA_V + 1])
    lane = lax.broadcasted_iota(jnp.int32, (tq, HEAD_PAD), 1)
    o_ref[...] = jnp.where(lane < MLA_V, outs[0], pltpu.roll(outs[1], MLA_V, 1)).astype(BF16)


def _attn(q, k, v, batch, seq_len):
    T = q.shape[0]
    tq = ATTN_TILE
    nq = seq_len // tq
    pairs = MLA_HEADS // 2
    return pl.pallas_call(
        _attn_kernel, grid=(batch, pairs, nq),
        in_specs=[pl.BlockSpec((tq, 2 * HEAD_PAD), lambda b, p, i: (b * nq + i, p)),
                  pl.BlockSpec((seq_len, 2 * HEAD_PAD), lambda b, p, i: (b, p)),
                  pl.BlockSpec((seq_len, 2 * HEAD_PAD), lambda b, p, i: (b, p))],
        out_specs=pl.BlockSpec((tq, 2 * MLA_V), lambda b, p, i: (b * nq + i, p)),
        out_shape=jax.ShapeDtypeStruct((T, MLA_W), BF16),
        scratch_shapes=[pltpu.VMEM((2, tq, HEAD_PAD), F32), pltpu.VMEM((2, tq, HEAD_PAD), F32)],
        compiler_params=_cparams(3), name="attn")(q, k, v)


def _out_router_kernel(h_ref, yg_ref, ym_ref, yp_ref, wo_ref, fn_ref, rw_ref, rb_ref,
                       h1_ref, hn0_ref, hn1_ref, idx_ref, gate_ref, cnt_ref, carry_ref):
    tm = h_ref.shape[0]

    @pl.when(pl.program_id(0) == 0)
    def _():
        carry_ref[...] = jnp.zeros_like(carry_ref)

    sub = ROUTER_SUB
    lane = lax.broadcasted_iota(jnp.int32, (sub, 128), 1)
    r_i = lax.broadcasted_iota(jnp.int32, (sub, sub), 0)
    c_i = lax.broadcasted_iota(jnp.int32, (sub, sub), 1)
    tri = (r_i >= c_i).astype(BF16)
    def project(st):
        rows = st["rows"]
        st["h1"] = (h_ref[rows, :] + _dot(yg_ref[rows, :], wo_ref[0:GLA_W, :])
                    + _dot(ym_ref[rows, :], wo_ref[GLA_W:GLA_W + MLA_W, :])
                    + _dot(yp_ref[rows, :], wo_ref[GLA_W + MLA_W:, :]))

    def normalize(st):
        rows = st["rows"]
        h1_ref[rows, :] = st["h1"]
        hn = _rms(st["h1"], fn_ref[...])
        st["hi"], st["lo"] = _split2(hn)
        packed = _pack_bf16_pairs(hn)
        slab = packed.shape[1] // DISPATCH_SLABS
        hn0_ref[rows, :] = packed[:, :slab]
        hn1_ref[rows, :] = packed[:, slab:]

    def score(st):
        r2 = _dot(st["hi"], rw_ref[...])
        st["logits"] = r2[:, :128] + r2[:, 128:] + _dot(st["lo"], rw_ref[:, 0:128]) + rb_ref[...]

    def select(st):
        rows = st["rows"]
        cur = jnp.where(lane < N_EXPERTS, st["logits"], NEG_BIG)
        idx_out = jnp.zeros((sub, 128), jnp.int32)
        val_out = jnp.zeros((sub, 128), F32)
        chosen = jnp.zeros((sub, 128), F32)
        top0 = None
        sels = []
        for kk in range(TOP_K):
            m = jnp.max(cur, axis=-1, keepdims=True)
            sel = jnp.min(jnp.where(cur == m, lane, 128), axis=-1, keepdims=True)
            if kk == 0:
                top0 = m
            sels.append(sel)
            idx_out = jnp.where(lane == kk, sel, idx_out)
            val_out = jnp.where(lane == kk, jnp.exp(m - top0), val_out)
            chosen = jnp.where(lane == sel, 1.0, chosen)
            cur = jnp.where(lane == sel, NEG_BIG, cur)
        gate_ref[rows, :] = val_out / jnp.sum(val_out, axis=-1, keepdims=True)

        incl = _dot(tri, chosen.astype(BF16))
        before = carry_ref[0:1, :] + incl - chosen
        for kk in range(TOP_K):
            rank = jnp.sum(jnp.where(lane == sels[kk], before, 0.0), axis=-1, keepdims=True)
            idx_out = jnp.where(lane == TOP_K + kk, rank.astype(jnp.int32), idx_out)
        idx_ref[rows, :] = idx_out
        carry_ref[...] = carry_ref[...] + incl[sub - 1:sub, :]

    _skewed([project, normalize, score, select], tm, sub)
    cnt_ref[...] = carry_ref[...].astype(jnp.int32)


def _out_router(h, yg, ym, yp, w):
    T = h.shape[0]
    tm = TOKEN_TILE
    row = lambda n: pl.BlockSpec((tm, n), lambda i: (i, 0))
    slab = D_MODEL // 2 // DISPATCH_SLABS
    ins = [h, yg, ym, yp, w["w_out"], w["ffn_norm"], w["router_w"], w["router_b"]]
    return pl.pallas_call(
        _out_router_kernel, grid=(T // tm,),
        in_specs=[row(D_MODEL), row(GLA_W), row(MLA_W), row(POOL_W)] + [_full(a.shape) for a in ins[4:]],
        out_specs=[row(D_MODEL), row(slab), row(slab), row(128), row(128), _full((8, 128))],
        out_shape=[jax.ShapeDtypeStruct((T, D_MODEL), F32), jax.ShapeDtypeStruct((T, slab), jnp.uint32),
                   jax.ShapeDtypeStruct((T, slab), jnp.uint32),
                   jax.ShapeDtypeStruct((T, 128), jnp.int32), jax.ShapeDtypeStruct((T, 128), F32),
                   jax.ShapeDtypeStruct((8, 128), jnp.int32)],
        scratch_shapes=[pltpu.VMEM((8, 128), F32)],
        compiler_params=_cparams(1), name="out_router")(*ins)


def _moe_kernel(be_ref, nb_ref, x0_ref, x1_ref, wg_ref, bg_ref, wu_ref, bu_ref, wd_ref, bd_ref,
                y0_ref, y1_ref, wg_bf, wu_bf, wd_bf):
    i = pl.program_id(0)
    used = i < nb_ref[0]
    new_expert = (i == 0) | (be_ref[i] != be_ref[jnp.maximum(i - 1, 0)])

    @pl.when(used & new_expert)
    def _():
        for src, dst in ((wg_ref, wg_bf), (wu_ref, wu_bf), (wd_ref, wd_bf)):
            for r in range(0, src.shape[2], MOE_CAST_ROWS):
                dst[r:r + MOE_CAST_ROWS, :] = src[0, 0, r:r + MOE_CAST_ROWS, :].astype(BF16)

    @pl.when(used)
    def _():
        halves = [_unpack_bf16_pairs(r[...]) for r in (x0_ref, x1_ref)]
        x = jnp.concatenate([h[0] for h in halves] + [h[1] for h in halves], axis=1).astype(BF16)
        g = jnp.minimum(_dot(x, wg_bf[...]) + bg_ref[0], SWIGLU_LIMIT)
        up = jnp.clip(_dot(x, wu_bf[...]) + bu_ref[0], -SWIGLU_LIMIT, SWIGLU_LIMIT)
        hb = (up + 1.0) * (g / (1.0 + jnp.exp(-SWIGLU_ALPHA * g)))
        packed = _pack_bf16_pairs(_dot(hb.astype(BF16), wd_bf[...]) + bd_ref[0])
        slab = y0_ref.shape[1]
        y0_ref[...] = packed[:, :slab]
        y1_ref[...] = packed[:, slab:]

    @pl.when(jnp.logical_not(used))
    def _():
        y0_ref[...] = jnp.zeros_like(y0_ref)
        y1_ref[...] = jnp.zeros_like(y1_ref)


def _moe(xs, block_e, n_used, w):
    n_rows, slab = xs[0].shape
    bm = MOE_BLOCK
    layer = w["layer"]
    wspec = lambda shp: pl.BlockSpec((1, 1) + shp, lambda i, be, nb: (layer, be[i], 0, 0))
    bspec = lambda shp: pl.BlockSpec((1,) + shp, lambda i, be, nb: (be[i], 0, 0))
    grid_spec = pltpu.PrefetchScalarGridSpec(
        num_scalar_prefetch=2, grid=(n_rows // bm,),
        in_specs=[pl.BlockSpec((bm, slab), lambda i, be, nb: (i, 0))] * DISPATCH_SLABS + [
                  wspec((D_MODEL, D_FF)), bspec((1, D_FF)), wspec((D_MODEL, D_FF)), bspec((1, D_FF)),
                  wspec((D_FF, D_MODEL)), bspec((1, D_MODEL))],
        out_specs=[pl.BlockSpec((bm, slab), lambda i, be, nb: (i, 0))] * DISPATCH_SLABS,
        scratch_shapes=[pltpu.VMEM((D_MODEL, D_FF), BF16), pltpu.VMEM((D_MODEL, D_FF), BF16),
                        pltpu.VMEM((D_FF, D_MODEL), BF16)])
    return pl.pallas_call(
        _moe_kernel, grid_spec=grid_spec,
        out_shape=[jax.ShapeDtypeStruct((n_rows, slab), jnp.uint32)] * DISPATCH_SLABS,
        compiler_params=_cparams(1), name="moe")(
            block_e, n_used, *xs, w["moe_w_gate"], w["moe_b_gate"], w["moe_w_up"], w["moe_b_up"],
            w["moe_w_down"], w["moe_b_down"])


def _ple_kernel(h1_ref, ya0_ref, ya1_ref, ya2_ref, ya3_ref, yb0_ref, yb1_ref, yb2_ref, yb3_ref, gate_ref, p_ref,
                wple_ref, gn_ref, wpg_ref, pn_ref, o_ref):
    def combine(st):
        rows = st["rows"]
        gates = gate_ref[rows, :]
        h2 = h1_ref[rows, :]
        for kk, (ya_ref, yb_ref) in enumerate(((ya0_ref, yb0_ref), (ya1_ref, yb1_ref), (ya2_ref, yb2_ref),
                                               (ya3_ref, yb3_ref))):
            lo_a, hi_a = _unpack_bf16_pairs(ya_ref[rows, :])
            lo_b, hi_b = _unpack_bf16_pairs(yb_ref[rows, :])
            h2 = h2 + gates[:, kk:kk + 1] * jnp.concatenate([lo_a, lo_b, hi_a, hi_b], axis=1)
        st["h2"] = h2
        st["hn"] = _rms(h2, gn_ref[...]).astype(BF16)

    def project(st):
        st["e"] = _dot(p_ref[0, st["rows"], :].astype(BF16), wple_ref[...])
        st["a"] = _dot(st["hn"], wpg_ref[...])

    def finish(st):
        gate = 1.0 / (1.0 + jnp.exp(-st["a"]))
        o_ref[st["rows"], :] = st["h2"] + _rms(st["e"] * gate, pn_ref[...])

    _skewed([combine, project, finish], h1_ref.shape[0], PLE_SUB)


def _ple(h1, y_slabs, gates, p, w, part):
    T = h1.shape[0]
    tm = TOKEN_TILE
    steps = T // COMBINE_PARTS // tm
    off = part * steps
    row = lambda n: pl.BlockSpec((tm, n), lambda i: (i + off, 0))
    slab = y_slabs[0].shape[1]
    gathered = lambda kk: pl.BlockSpec((tm, slab), lambda i: (kk * steps + i, 0))
    weights = [w["ple_w_proj"], w["ple_gate_norm"], w["ple_w_gate"], w["ple_post_norm"]]
    layer = w["layer"]
    p_spec = pl.BlockSpec((1, tm, D_PLE), lambda i: (layer, i + off, 0))
    ins = [h1] + [y for y in y_slabs for _ in range(TOP_K)] + [gates, p, *weights]
    in_specs = ([row(D_MODEL)] + [gathered(kk) for _ in y_slabs for kk in range(TOP_K)] + [row(128), p_spec]
                + [_full(a.shape) for a in weights])
    return pl.pallas_call(
        _ple_kernel, grid=(steps,), in_specs=in_specs,
        out_specs=row(D_MODEL), out_shape=jax.ShapeDtypeStruct((T, D_MODEL), F32),
        input_output_aliases={0: 0}, compiler_params=_cparams(1), name="ple")(*ins)


def _pad_heads(wm, per_head, n_heads=MLA_HEADS):
    kdim = wm.shape[0]
    w3 = wm.reshape(kdim, n_heads, per_head)
    return jnp.pad(w3, ((0, 0), (0, 0), (0, HEAD_PAD - per_head))).reshape(kdim, n_heads * HEAD_PAD)


def _swap_rope_halves(a):
    a3 = a.reshape(a.shape[0], -1, HEAD_PAD)
    half = MLA_ROPE // 2
    x1 = a3[:, :, MLA_NOPE:MLA_NOPE + half]
    x2 = a3[:, :, MLA_NOPE + half:MLA_QK]
    out = jnp.zeros_like(a3).at[:, :, MLA_NOPE:MLA_NOPE + half].set(x2).at[:, :, MLA_NOPE + half:MLA_QK].set(x1)
    return out.reshape(a.shape)


def _layer_params(i, mix_norm, w_in, gla_w_gate, gla_b_gate, gla_out_norm, mla_q_norm, mla_w_uq, mla_kv_norm,
                  mla_w_ukv, mla_qk_q_norm, mla_qk_k_norm, pool_w, pool_scale, w_out, ffn_norm, router_w,
                  router_b, moe_w_gate, moe_b_gate, moe_w_up, moe_b_up, moe_w_down, moe_b_down,
                  ple_w_proj, ple_gate_norm, ple_w_gate, ple_post_norm):
    wi = w_in[i]
    c = np.cumsum((0, 128, 128, 256, 16, 256, 256, 128, 32, 256))
    gq, gk, gv, glow, gr, cq, ckv, krope, upool = [wi[:, c[j]:c[j + 1]] for j in range(9)]
    misc = jnp.concatenate([glow, krope, jnp.zeros((D_MODEL, 128 - 48), F32)], axis=1)
    w_in_p = jnp.concatenate([gq, gk, gv, gr, cq, upool, ckv, misc], axis=1).astype(BF16)
    wgate_p = jnp.zeros((128, GLA_K), F32).at[MISC_GLOW:MISC_GLOW + GLA_GATE_RANK].set(gla_w_gate[i]).astype(BF16)
    ukv = mla_w_ukv[i].reshape(MLA_KV_RANK, MLA_HEADS, MLA_NOPE + MLA_V)
    ukv_k = _pad_heads(ukv[:, :, :MLA_NOPE].reshape(MLA_KV_RANK, MLA_HEADS * MLA_NOPE), MLA_NOPE)
    ukv_v = _pad_heads(ukv[:, :, MLA_NOPE:].reshape(MLA_KV_RANK, MLA_W), MLA_V)
    pw = pool_w[i]
    pool_bd = jnp.zeros((POOL_W, POOL_W), F32)
    for g in range(4):
        pool_bd = pool_bd.at[g * 64:(g + 1) * 64, g * 64:(g + 1) * 64].set(pw[g])
    rw = jnp.pad(router_w[i], ((0, 0), (0, 128 - N_EXPERTS)))
    rw_hi = rw.astype(BF16)
    rw_lo = (rw - rw_hi.astype(F32)).astype(BF16)
    row = lambda a: a.reshape(1, -1)
    pad96 = lambda a: jnp.pad(a, (0, HEAD_PAD - MLA_QK)).reshape(1, HEAD_PAD)
    wuq_p = _pad_heads(mla_w_uq[i], MLA_QK)
    gq_p = pad96(mla_qk_q_norm[i] * (MLA_QK ** -0.5 * LOG2E))
    return {
        "mix_norm": row(mix_norm[i]), "w_in": w_in_p, "gla_w_gate": wgate_p, "gla_b_gate": row(gla_b_gate[i]),
        "gla_out_norm": row(jnp.tile(gla_out_norm[i], GLA_HEADS)),
        "mla_q_norm": row(mla_q_norm[i]),
        "mla_w_uq": jnp.concatenate([wuq_p, _swap_rope_halves(wuq_p)], axis=1).astype(BF16),
        "mla_kv_norm": row(mla_kv_norm[i]), "mla_w_ukv_k": ukv_k.astype(BF16), "mla_w_ukv_v": ukv_v.astype(BF16),
        "mla_gq": jnp.concatenate([gq_p, _swap_rope_halves(gq_p)], axis=0), "mla_gk": pad96(mla_qk_k_norm[i]),
        "pool_w": pool_bd.astype(BF16), "pool_scale": row(pool_scale[i]),
        "w_out": w_out[i].astype(BF16), "ffn_norm": row(ffn_norm[i]),
        "router_w": jnp.concatenate([rw_hi, rw_lo], axis=1),
        "router_b": row(jnp.pad(router_b[i], (0, 128 - N_EXPERTS))),
        "layer": i,
        "moe_w_gate": moe_w_gate, "moe_b_gate": moe_b_gate[i].reshape(N_EXPERTS, 1, D_FF),
        "moe_w_up": moe_w_up, "moe_b_up": moe_b_up[i].reshape(N_EXPERTS, 1, D_FF),
        "moe_w_down": moe_w_down, "moe_b_down": moe_b_down[i].reshape(N_EXPERTS, 1, D_MODEL),
        "ple_w_proj": ple_w_proj[i].astype(BF16), "ple_gate_norm": row(ple_gate_norm[i]),
        "ple_w_gate": ple_w_gate[i].astype(BF16), "ple_post_norm": row(ple_post_norm[i]),
    }


def _rope_tables(positions):
    T = positions.size
    inv = ROPE_BASE ** (-jnp.arange(0, MLA_ROPE, 2, dtype=F32) / MLA_ROPE)
    ang = positions.reshape(T, 1).astype(F32) * inv
    cos, sin = jnp.cos(ang), jnp.sin(ang)
    z16 = jnp.zeros((T, 16), F32)
    tail = jnp.zeros((T, HEAD_PAD - MLA_QK), F32)
    c = jnp.concatenate([jnp.ones((T, MLA_NOPE), F32), cos, cos, tail], axis=1)
    s1 = jnp.concatenate([jnp.zeros((T, MLA_NOPE), F32), -sin, z16, tail], axis=1)
    s2 = jnp.concatenate([jnp.zeros((T, MLA_NOPE), F32), z16, sin, tail], axis=1)
    return c, s1, s2


def _route(top_idx, rank, counts, T):
    bm = MOE_BLOCK
    A = T * TOP_K
    padded = (counts + bm - 1) // bm * bm
    pad_end = jnp.cumsum(padded)
    pad_start = pad_end - padded
    experts = jnp.arange(N_EXPERTS, dtype=jnp.int32)
    dest = rank + jnp.sum(jnp.where(top_idx[:, :, None] == experts, pad_start, 0), axis=-1)
    n_blocks = (A + N_EXPERTS * (bm - 1) + bm - 1) // bm
    n_rows = n_blocks * bm
    block_start = jnp.arange(n_blocks, dtype=jnp.int32) * bm
    block_e = jnp.minimum(jnp.sum((pad_end[None, :] <= block_start[:, None]).astype(jnp.int32), axis=1),
                          N_EXPERTS - 1)
    n_used = (pad_end[-1] // bm).astype(jnp.int32).reshape(1)
    return dest, n_rows, block_e, n_used


def _dispatch(hn_slabs, dest, n_rows):
    T, width = hn_slabs[0].shape
    win = DISPATCH_ROWS
    dest_t = dest.T
    mesh = plsc.VectorSubcoreMesh(core_axis_name="core", subcore_axis_name="subcore")

    @functools.partial(pl.kernel, out_type=jax.ShapeDtypeStruct((n_rows, width), hn_slabs[0].dtype), mesh=mesh,
                       scratch_types=[], name="dispatch")
    def scatter_rows(x_hbm, i_hbm, o_hbm):
        def body(x_vmem, i_vmem):
            for kk in range(TOP_K):
                pltpu.sync_copy(x_vmem, o_hbm.at[i_vmem.at[kk]])

        pltpu.emit_pipeline(
            body, grid=(T // win,),
            in_specs=[pl.BlockSpec((win, width), lambda i: (i, 0)), pl.BlockSpec((TOP_K, win), lambda i: (0, i))],
            out_specs=[], core_axis_name=("core", "subcore"),
            dimension_semantics=(pltpu.PARALLEL,))(x_hbm, i_hbm)

    return [scatter_rows(slab, dest_t) for slab in hn_slabs]


def _combine_gather(y_slabs, dest_t):
    n_k, n_tok = dest_t.shape
    win = DISPATCH_ROWS
    width = y_slabs[0].shape[1]
    steps = n_tok // win
    mesh = plsc.VectorSubcoreMesh(core_axis_name="core", subcore_axis_name="subcore")

    @functools.partial(pl.kernel, out_type=jax.ShapeDtypeStruct((n_k * n_tok, width), y_slabs[0].dtype),
                       mesh=mesh, scratch_types=[], name="combine")
    def gather_rows(y_hbm, i_hbm, o_hbm):
        def body(i_vmem, o_vmem):
            pltpu.sync_copy(y_hbm.at[i_vmem.at[0]], o_vmem)

        pltpu.emit_pipeline(
            body, grid=(n_k, steps),
            in_specs=[pl.BlockSpec((1, win), lambda k, i: (k, i))],
            out_specs=[pl.BlockSpec((win, width), lambda k, i: (k * steps + i, 0))],
            core_axis_name=("core", "subcore"),
            dimension_semantics=(pltpu.PARALLEL, pltpu.PARALLEL))(i_hbm, o_hbm)

    return [gather_rows(y, dest_t) for y in y_slabs]


def kernel(x, p, positions, mix_norm, w_in, gla_w_gate, gla_b_gate, gla_out_norm, mla_q_norm, mla_w_uq,
           mla_kv_norm, mla_w_ukv, mla_qk_q_norm, mla_qk_k_norm, pool_w, pool_scale, w_out, ffn_norm,
           router_w, router_b, moe_w_gate, moe_b_gate, moe_w_up, moe_b_up, moe_w_down, moe_b_down,
           ple_w_proj, ple_gate_norm, ple_w_gate, ple_post_norm):
    B, S, D = x.shape
    T = B * S
    depth = p.shape[0]
    params = (mix_norm, w_in, gla_w_gate, gla_b_gate, gla_out_norm, mla_q_norm, mla_w_uq, mla_kv_norm,
              mla_w_ukv, mla_qk_q_norm, mla_qk_k_norm, pool_w, pool_scale, w_out, ffn_norm, router_w,
              router_b, moe_w_gate, moe_b_gate, moe_w_up, moe_b_up, moe_w_down, moe_b_down,
              ple_w_proj, ple_gate_norm, ple_w_gate, ple_post_norm)
    rope_c, rope_s1, rope_s2 = _rope_tables(positions)
    p_flat = p.reshape(depth, T, D_PLE)
    h = x.reshape(T, D)
    for i in range(depth):
        w = _layer_params(i, *params)
        y_gla, q, k, v, y_pool = _mix_pre(h, w, rope_c, rope_s1, rope_s2, S)
        y_mla = _attn(q, k, v, B, S)
        h1, hn0, hn1, route, gates, counts = _out_router(h, y_gla, y_mla, y_pool, w)
        dest, n_rows, block_e, n_used = _route(route[:, :TOP_K], route[:, TOP_K:2 * TOP_K],
                                               counts[0, :N_EXPERTS], T)
        ys = _moe(_dispatch([hn0, hn1], dest, n_rows), block_e, n_used, w)
        h = h1
        for part in range(COMBINE_PARTS):
            d = dest[part * (T // COMBINE_PARTS):(part + 1) * (T // COMBINE_PARTS)]
            h = _ple(h, _combine_gather(ys, d.T), gates, p_flat, w, part)
    return h.reshape(B, S, D)
```

```python
import functools

import jax
import jax.numpy as jnp
import numpy as np
from jax import lax
from jax.experimental import pallas as pl
from jax.experimental.pallas import tpu as pltpu
from jax.experimental.pallas import tpu_sc as plsc

F32 = jnp.float32
BF16 = jnp.bfloat16

D_MODEL = 1024
EPS = 1e-6
D_PLE = 256

GLA_HEADS = 4
GLA_DK = 32
GLA_DV = 64
GLA_GATE_RANK = 16
GLA_TAU = 16.0
GLA_CHUNK = 64
GLA_K = GLA_HEADS * GLA_DK
GLA_W = GLA_HEADS * GLA_DV

MLA_HEADS = 8
MLA_Q_RANK = 256
MLA_KV_RANK = 128
MLA_NOPE = 64
MLA_ROPE = 32
MLA_QK = MLA_NOPE + MLA_ROPE
MLA_V = 64
MLA_W = MLA_HEADS * MLA_V
ROPE_BASE = 10000.0
HEAD_PAD = 128
MLA_QK_PAD = MLA_HEADS * HEAD_PAD

POOL_WINDOWS = (2, 4, 8, 16)
POOL_GROUP = 64
POOL_W = 256
POOL_HALO = 16

N_EXPERTS = 32
TOP_K = 4
D_FF = 1024
SWIGLU_LIMIT = 7.0
SWIGLU_ALPHA = 1.702

COL_GQ, COL_GK, COL_GV, COL_GR, COL_CQ, COL_POOL, COL_CKV, COL_MISC = 0, 128, 256, 512, 768, 1024, 1280, 1408
D_IN_PAD = 1536
MISC_GLOW = 0
MISC_ROPE = 16

LOG2E = 1.4426950408889634
TOKEN_TILE = 1024
MIX_SUB, ROUTER_SUB, PLE_SUB = 512, 512, 256
ATTN_TILE = 2048
ATTN_SUB = 512
MOE_BLOCK = 1024
MOE_CAST_ROWS = 256
COMBINE_PARTS = 4
DISPATCH_ROWS = 128
DISPATCH_SLABS = 2
VMEM_LIMIT = 56 * 1024 * 1024
NEG_BIG = -1e30


def _cparams(n_axes, **flags):
    return pltpu.CompilerParams(dimension_semantics=("arbitrary",) * n_axes,
                                vmem_limit_bytes=VMEM_LIMIT, flags=flags or None)


def _rms(x, g):
    return x * lax.rsqrt(jnp.mean(x * x, axis=-1, keepdims=True) + EPS) * g


def _dot(a, b):
    return jnp.dot(a, b, preferred_element_type=F32)


def _dot_nt(a, b):
    return lax.dot_general(a, b, (((1,), (1,)), ((), ())), preferred_element_type=F32)


def _dot_tn(a, b):
    return lax.dot_general(a, b, (((0,), (0,)), ((), ())), preferred_element_type=F32)


def _split3(x):
    hi = x.astype(BF16)
    r = x - hi.astype(F32)
    mid = r.astype(BF16)
    lo = (r - mid.astype(F32)).astype(BF16)
    return hi, mid, lo


def _split2(x):
    hi = x.astype(BF16)
    lo = (x - hi.astype(F32)).astype(BF16)
    return hi, lo


def _pack_bf16_pairs(x):
    m = x.shape[1] // 2
    bits = lax.bitcast_convert_type(x.astype(BF16).astype(F32), jnp.uint32)
    return (bits[:, :m] >> 16) | (bits[:, m:] & jnp.uint32(0xFFFF0000))


def _unpack_bf16_pairs(w):
    lo = lax.bitcast_convert_type(w << 16, F32)
    hi = lax.bitcast_convert_type(w & jnp.uint32(0xFFFF0000), F32)
    return lo, hi


def _skewed(stages, n_rows, sub):
    states = [{"rows": slice(r0, r0 + sub)} for r0 in range(0, n_rows, sub)]
    for step in range(len(states) + len(stages) - 1):
        for s, stage in enumerate(stages):
            t = step - s
            if 0 <= t < len(states):
                stage(states[t])


def _full(shape):
    nd = len(shape)
    return pl.BlockSpec(shape, lambda *_: (0,) * nd)


def _rope(x, c, s1, s2):
    return x * c + pltpu.roll(x, HEAD_PAD - 16, 1) * s1 + pltpu.roll(x, 16, 1) * s2


def _mix_pre_kernel(h_ref, mixn_ref, win_ref, wgate_ref, bgate_ref, qn_ref, wuq_ref, kvn_ref,
                    wukvk_ref, wukvv_ref, gq_ref, gk_ref, rc_ref, rs1_ref, rs2_ref,
                    wpool_ref, pscale_ref, gn_ref,
                    yg_ref, q_ref, k_ref, v_ref, yp_ref, carry_ref, gla_state_ref, *, tiles_per_seq):
    tm = h_ref.shape[0]
    sub = MIX_SUB
    seq_tile = pl.program_id(0) % tiles_per_seq

    @pl.when(seq_tile == 0)
    def _():
        carry_ref[...] = jnp.zeros_like(carry_ref)
        gla_state_ref[...] = jnp.zeros_like(gla_state_ref)

    gla_consts = _gla_consts()
    gla_state = [gla_state_ref[...]]

    lane = lax.broadcasted_iota(jnp.int32, (sub, HEAD_PAD), 1)
    in_rope = (lane >= MLA_NOPE) & (lane < MLA_QK)
    lane_v = lax.broadcasted_iota(jnp.int32, (sub, MLA_QK_PAD), 1)
    ones_lane = lane_v % HEAD_PAD == MLA_V
    lane_p = lax.broadcasted_iota(jnp.int32, (sub, POOL_W), 1)
    row_p = lax.broadcasted_iota(jnp.int32, (sub, POOL_W), 0)
    g0, g1, g2 = lane_p < 64, lane_p < 128, lane_p < 192
    win = jnp.where(g0, 2.0, jnp.where(g1, 4.0, jnp.where(g2, 8.0, 16.0)))
    gq, gq_sw, gk = gq_ref[0:1, :], gq_ref[1:2, :], gk_ref[...]

    def norm_in(st):
        st["hn"] = _rms(h_ref[st["rows"], :], mixn_ref[...]).astype(BF16)

    def project_in(st):
        st["z"] = _dot(st["hn"], win_ref[...])

    def norm_latents(st):
        z = st["z"]
        st["cqn"] = _rms(z[:, COL_CQ:COL_CQ + MLA_Q_RANK], qn_ref[...]).astype(BF16)
        st["ckvn"] = _rms(z[:, COL_CKV:COL_CKV + MLA_KV_RANK], kvn_ref[...]).astype(BF16)

    def project_up(st):
        zm = st["z"][:, COL_MISC:COL_MISC + 128]
        st["logit"] = _dot(zm.astype(BF16), wgate_ref[...]) + bgate_ref[...]
        st["qf"] = _dot(st["cqn"], wuq_ref[...])
        st["kn"] = _dot(st["ckvn"], wukvk_ref[...])
        st["v"] = _dot(st["ckvn"], wukvv_ref[...])

    def heads_and_pool(st):
        rows, z, qf, kn, logit = st["rows"], st["z"], st["qf"], st["kn"], st["logit"]
        zm = z[:, COL_MISC:COL_MISC + 128]
        st["la"] = (jnp.minimum(logit, 0.0) - jnp.log(1.0 + jnp.exp(-jnp.abs(logit)))) * (1.0 / GLA_TAU)
        v_ref[rows, :] = jnp.where(ones_lane, 1.0, st["v"]).astype(BF16)

        rc, rs1, rs2 = rc_ref[rows, :], rs1_ref[rows, :], rs2_ref[rows, :]
        kr = jnp.where(in_rope, pltpu.roll(zm, MLA_NOPE - MISC_ROPE, 1), 0.0)
        kr_ss = jnp.sum(kr * kr, axis=-1, keepdims=True)
        krr = _rope(kr * gk, rc, rs1, rs2)
        cq = rc * gq
        sq_tab = (rs1 + rs2) * gq_sw
        for hh in range(MLA_HEADS):
            sl = slice(hh * HEAD_PAD, (hh + 1) * HEAD_PAD)
            qh = qf[:, sl]
            qsw = qf[:, MLA_QK_PAD + hh * HEAD_PAD:MLA_QK_PAD + (hh + 1) * HEAD_PAD]
            sq = lax.rsqrt(jnp.sum(qh * qh, axis=-1, keepdims=True) * (1.0 / MLA_QK) + EPS)
            q_ref[rows, sl] = ((qh * cq + qsw * sq_tab) * sq).astype(BF16)
            kh = kn[:, sl]
            sk = lax.rsqrt((jnp.sum(kh * kh, axis=-1, keepdims=True) + kr_ss) * (1.0 / MLA_QK) + EPS)
            k_ref[rows, sl] = (sk * (kh * gk + krr)).astype(BF16)

        u = z[:, COL_POOL:COL_POOL + POOL_W]
        xe = jnp.concatenate([carry_ref[...], u], axis=0)
        carry_ref[...] = u[sub - POOL_HALO:, :]
        s2 = xe + pltpu.roll(xe, 1, 0)
        s4 = s2 + pltpu.roll(s2, 2, 0)
        s8 = s4 + pltpu.roll(s4, 4, 0)
        s16 = s8 + pltpu.roll(s8, 8, 0)
        pooled = jnp.where(g0, s2[POOL_HALO:], jnp.where(g1, s4[POOL_HALO:],
                           jnp.where(g2, s8[POOL_HALO:], s16[POOL_HALO:])))
        cnt = jnp.minimum((seq_tile * tm + rows.start + row_p + 1).astype(F32), win)
        st["d"] = (pooled / cnt - u).astype(BF16)

    def project_pool(st):
        yp_ref[st["rows"], :] = (_dot(st["d"], wpool_ref[...]) * pscale_ref[...]).astype(BF16)

    def gla(st):
        yg_ref[st["rows"], :] = _gla_rows(st["z"], st["la"], gn_ref[...], gla_consts, gla_state)

    _skewed([norm_in, project_in, norm_latents, project_up, heads_and_pool, project_pool, gla], tm, sub)
    gla_state_ref[...] = gla_state[0]


def _mix_pre(h, w, rope_c, rope_s1, rope_s2, seq_len):
    T = h.shape[0]
    tm = TOKEN_TILE
    row = lambda n: pl.BlockSpec((tm, n), lambda i: (i, 0))
    ins = [h, w["mix_norm"], w["w_in"], w["gla_w_gate"], w["gla_b_gate"], w["mla_q_norm"], w["mla_w_uq"],
           w["mla_kv_norm"], w["mla_w_ukv_k"], w["mla_w_ukv_v"], w["mla_gq"], w["mla_gk"],
           rope_c, rope_s1, rope_s2, w["pool_w"], w["pool_scale"], w["gla_out_norm"]]
    in_specs = [row(D_MODEL)] + [_full(a.shape) for a in ins[1:12]] + [row(HEAD_PAD)] * 3 + \
               [_full(a.shape) for a in ins[15:]]
    out_shape = [jax.ShapeDtypeStruct((T, GLA_W), BF16),
                 jax.ShapeDtypeStruct((T, MLA_QK_PAD), BF16), jax.ShapeDtypeStruct((T, MLA_QK_PAD), BF16),
                 jax.ShapeDtypeStruct((T, MLA_QK_PAD), BF16), jax.ShapeDtypeStruct((T, POOL_W), BF16)]
    out_specs = [row(GLA_W), row(MLA_QK_PAD), row(MLA_QK_PAD), row(MLA_QK_PAD), row(POOL_W)]
    return pl.pallas_call(
        functools.partial(_mix_pre_kernel, tiles_per_seq=seq_len // tm),
        grid=(T // tm,), in_specs=in_specs, out_specs=out_specs, out_shape=out_shape,
        scratch_shapes=[pltpu.VMEM((POOL_HALO, POOL_W), F32), pltpu.VMEM((GLA_K, GLA_W), F32)],
        compiler_params=_cparams(1), name="mix_pre")(*ins)


def _gla_consts():
    C = GLA_CHUNK
    r_i = lax.broadcasted_iota(jnp.int32, (C, C), 0)
    c_i = lax.broadcasted_iota(jnp.int32, (C, C), 1)
    ar = lax.broadcasted_iota(jnp.int32, (GLA_HEADS * C, C), 0)
    ac = lax.broadcasted_iota(jnp.int32, (GLA_HEADS * C, C), 1)
    sk = lax.broadcasted_iota(jnp.int32, (GLA_K, GLA_W), 0) // GLA_DK
    sv = lax.broadcasted_iota(jnp.int32, (GLA_K, GLA_W), 1) // GLA_DV
    gr = lax.broadcasted_iota(jnp.int32, (GLA_W, GLA_W), 0) // GLA_DV
    gc = lax.broadcasted_iota(jnp.int32, (GLA_W, GLA_W), 1) // GLA_DV
    return {
        "tri": (r_i >= c_i).astype(BF16),
        "ones": jnp.ones((C, GLA_W), BF16),
        "head_k": lax.broadcasted_iota(jnp.int32, (C, GLA_K), 1) // GLA_DK,
        "head_v": lax.broadcasted_iota(jnp.int32, (C, GLA_W), 1) // GLA_DV,
        "causal": (ar % C) >= ac,
        "blockdiag": sk == sv,
        "group": (gr == gc).astype(BF16),
    }


def _gla_rows(z, la, gn, consts, state):
    C = GLA_CHUNK
    tri, ones, head_k, head_v = consts["tri"], consts["ones"], consts["head_k"], consts["head_v"]
    causal, blockdiag = consts["causal"], consts["blockdiag"]
    outs = []

    def log_decay(st):
        la3 = _split3(la[st["rows"], :])
        st["bc"] = _dot(tri, la3[0]) + _dot(tri, la3[1]) + _dot(tri, la3[2])
        st["dsum"] = _dot_tn(la3[0], ones) + _dot_tn(la3[1], ones) + _dot_tn(la3[2], ones)

    def scores(st):
        rows, bc = st["rows"], st["bc"]
        q = z[rows, COL_GQ:COL_GQ + GLA_K] * (GLA_DK ** -0.5)
        k = z[rows, COL_GK:COL_GK + GLA_K]
        b_last = bc[C - 1:C, :]
        q_dec = (q * jnp.exp(bc)).astype(BF16)
        k_dec = (k * jnp.exp(-bc)).astype(BF16)
        st["k_end"] = (k * jnp.exp(b_last - bc)).astype(BF16)
        st["decay"] = jnp.exp(st["dsum"])
        zero = jnp.zeros_like(q_dec)
        qs = jnp.concatenate([jnp.where(head_k == hh, q_dec, zero) for hh in range(GLA_HEADS)], axis=0)
        st["q_dec"] = q_dec
        st["att"] = _dot_nt(qs, k_dec)

    def values(st):
        v = z[st["rows"], COL_GV:COL_GV + GLA_W].astype(BF16)
        st["o_full"] = _dot(jnp.where(causal, st["att"], 0.0).astype(BF16), v)
        st["upd"] = jnp.where(blockdiag, _dot_tn(st["k_end"], v), 0.0)

    def recur(st):
        o_full = st["o_full"]
        o = _dot(st["q_dec"], state[0].astype(BF16))
        for hh in range(GLA_HEADS):
            o = o + jnp.where(head_v == hh, o_full[hh * C:(hh + 1) * C, :], 0.0)
        outs.append(o)
        state[0] = st["decay"] * state[0] + st["upd"]

    _skewed([log_decay, scores, values, recur], z.shape[0], C)

    o = jnp.concatenate(outs, axis=0)
    oo = _split2(o * o)
    ms = (_dot(oo[0], consts["group"]) + _dot(oo[1], consts["group"])) * (1.0 / GLA_DV)
    r = z[:, COL_GR:COL_GR + GLA_W]
    return (o * lax.rsqrt(ms + EPS) * gn * (r / (1.0 + jnp.exp(-r)))).astype(BF16)


def _attn_kernel(q_ref, k_ref, v_ref, o_ref, m_ref, acc_ref):
    tq = q_ref.shape[0]
    ts = ATTN_SUB
    i = pl.program_id(2)
    m_ref[...] = jnp.full_like(m_ref, NEG_BIG)
    acc_ref[...] = jnp.zeros_like(acc_ref)

    def sub_block(hh, start, r0, mask_off):
        hs = slice(hh * HEAD_PAD, (hh + 1) * HEAD_PAD)
        kj = k_ref[pl.ds(start, ts), hs]
        vj = v_ref[pl.ds(start, ts), hs]
        s = _dot_nt(q_ref[r0:, hs], kj)
        if mask_off is not None:
            row = lax.broadcasted_iota(jnp.int32, s.shape, 0) + r0
            col = lax.broadcasted_iota(jnp.int32, s.shape, 1) + mask_off
            s = jnp.where(col <= row, s, NEG_BIG)
        m_old = m_ref[hh, r0:, :]
        parts = [s[:, c * 128:(c + 1) * 128] for c in range(ts // 128)]
        m_new = jnp.maximum(m_old, jnp.max(functools.reduce(jnp.maximum, parts), axis=-1, keepdims=True))
        p = jnp.concatenate([jnp.exp2((x - m_new).astype(BF16)) for x in parts], axis=1)
        acc_ref[hh, r0:, :] = jnp.exp2(m_old - m_new) * acc_ref[hh, r0:, :] + _dot(p, vj)
        m_ref[hh, r0:, :] = m_new

    def body(j, carry):
        base = pl.multiple_of(j * tq, tq)
        for sb in range(tq // ts):
            for hh in range(2):
                sub_block(hh, base + sb * ts, 0, None)
        return carry

    lax.fori_loop(0, i, body, 0)
    base = pl.multiple_of(i * tq, tq)
    for sb in range(tq // ts):
        for hh in range(2):
            sub_block(hh, base + sb * ts, sb * ts, sb * ts)
    outs = []
    for hh in range(2):
        a = acc_ref[hh]
        outs.append(a / a[:, MLA_V:MLA_V + 1])
    lane = lax.broadcasted_iota(jnp.int32, (tq, HEAD_PAD), 1)
    o_ref[...] = jnp.where(lane < MLA_V, outs[0], pltpu.roll(outs[1], MLA_V, 1)).astype(BF16)


def _attn(q, k, v, batch, seq_len):
    T = q.shape[0]
    tq = ATTN_TILE
    nq = seq_len // tq
    pairs = MLA_HEADS // 2
    return pl.pallas_call(
        _attn_kernel, grid=(batch, pairs, nq),
        in_specs=[pl.BlockSpec((tq, 2 * HEAD_PAD), lambda b, p, i: (b * nq + i, p)),
                  pl.BlockSpec((seq_len, 2 * HEAD_PAD), lambda b, p, i: (b, p)),
                  pl.BlockSpec((seq_len, 2 * HEAD_PAD), lambda b, p, i: (b, p))],
        out_specs=pl.BlockSpec((tq, 2 * MLA_V), lambda b, p, i: (b * nq + i, p)),
        out_shape=jax.ShapeDtypeStruct((T, MLA_W), BF16),
        scratch_shapes=[pltpu.VMEM((2, tq, HEAD_PAD), F32), pltpu.VMEM((2, tq, HEAD_PAD), F32)],
        compiler_params=_cparams(3), name="attn")(q, k, v)


def _out_router_kernel(h_ref, yg_ref, ym_ref, yp_ref, wo_ref, fn_ref, rw_ref, rb_ref,
                       h1_ref, hn0_ref, hn1_ref, idx_ref, gate_ref, cnt_ref, carry_ref):
    tm = h_ref.shape[0]

    @pl.when(pl.program_id(0) == 0)
    def _():
        carry_ref[...] = jnp.zeros_like(carry_ref)

    sub = ROUTER_SUB
    lane = lax.broadcasted_iota(jnp.int32, (sub, 128), 1)
    r_i = lax.broadcasted_iota(jnp.int32, (sub, sub), 0)
    c_i = lax.broadcasted_iota(jnp.int32, (sub, sub), 1)
    tri = (r_i >= c_i).astype(BF16)
    def project(st):
        rows = st["rows"]
        st["h1"] = (h_ref[rows, :] + _dot(yg_ref[rows, :], wo_ref[0:GLA_W, :])
                    + _dot(ym_ref[rows, :], wo_ref[GLA_W:GLA_W + MLA_W, :])
                    + _dot(yp_ref[rows, :], wo_ref[GLA_W + MLA_W:, :]))

    def normalize(st):
        rows = st["rows"]
        h1_ref[rows, :] = st["h1"]
        hn = _rms(st["h1"], fn_ref[...])
        st["hi"], st["lo"] = _split2(hn)
        packed = _pack_bf16_pairs(hn)
        slab = packed.shape[1] // DISPATCH_SLABS
        hn0_ref[rows, :] = packed[:, :slab]
        hn1_ref[rows, :] = packed[:, slab:]

    def score(st):
        r2 = _dot(st["hi"], rw_ref[...])
        st["logits"] = r2[:, :128] + r2[:, 128:] + _dot(st["lo"], rw_ref[:, 0:128]) + rb_ref[...]

    def select(st):
        rows = st["rows"]
        cur = jnp.where(lane < N_EXPERTS, st["logits"], NEG_BIG)
        idx_out = jnp.zeros((sub, 128), jnp.int32)
        val_out = jnp.zeros((sub, 128), F32)
        chosen = jnp.zeros((sub, 128), F32)
        top0 = None
        sels = []
        for kk in range(TOP_K):
            m = jnp.max(cur, axis=-1, keepdims=True)
            sel = jnp.min(jnp.where(cur == m, lane, 128), axis=-1, keepdims=True)
            if kk == 0:
                top0 = m
            sels.append(sel)
            idx_out = jnp.where(lane == kk, sel, idx_out)
            val_out = jnp.where(lane == kk, jnp.exp(m - top0), val_out)
            chosen = jnp.where(lane == sel, 1.0, chosen)
            cur = jnp.where(lane == sel, NEG_BIG, cur)
        gate_ref[rows, :] = val_out / jnp.sum(val_out, axis=-1, keepdims=True)

        incl = _dot(tri, chosen.astype(BF16))
        before = carry_ref[0:1, :] + incl - chosen
        for kk in range(TOP_K):
            rank = jnp.sum(jnp.where(lane == sels[kk], before, 0.0), axis=-1, keepdims=True)
            idx_out = jnp.where(lane == TOP_K + kk, rank.astype(jnp.int32), idx_out)
        idx_ref[rows, :] = idx_out
        carry_ref[...] = carry_ref[...] + incl[sub - 1:sub, :]

    _skewed([project, normalize, score, select], tm, sub)
    cnt_ref[...] = carry_ref[...].astype(jnp.int32)


def _out_router(h, yg, ym, yp, w):
    T = h.shape[0]
    tm = TOKEN_TILE
    row = lambda n: pl.BlockSpec((tm, n), lambda i: (i, 0))
    slab = D_MODEL // 2 // DISPATCH_SLABS
    ins = [h, yg, ym, yp, w["w_out"], w["ffn_norm"], w["router_w"], w["router_b"]]
    return pl.pallas_call(
        _out_router_kernel, grid=(T // tm,),
        in_specs=[row(D_MODEL), row(GLA_W), row(MLA_W), row(POOL_W)] + [_full(a.shape) for a in ins[4:]],
        out_specs=[row(D_MODEL), row(slab), row(slab), row(128), row(128), _full((8, 128))],
        out_shape=[jax.ShapeDtypeStruct((T, D_MODEL), F32), jax.ShapeDtypeStruct((T, slab), jnp.uint32),
                   jax.ShapeDtypeStruct((T, slab), jnp.uint32),
                   jax.ShapeDtypeStruct((T, 128), jnp.int32), jax.ShapeDtypeStruct((T, 128), F32),
                   jax.ShapeDtypeStruct((8, 128), jnp.int32)],
        scratch_shapes=[pltpu.VMEM((8, 128), F32)],
        compiler_params=_cparams(1), name="out_router")(*ins)


def _moe_kernel(be_ref, nb_ref, x0_ref, x1_ref, wg_ref, bg_ref, wu_ref, bu_ref, wd_ref, bd_ref,
                y0_ref, y1_ref, wg_bf, wu_bf, wd_bf):
    i = pl.program_id(0)
    used = i < nb_ref[0]
    new_expert = (i == 0) | (be_ref[i] != be_ref[jnp.maximum(i - 1, 0)])

    @pl.when(used & new_expert)
    def _():
        for src, dst in ((wg_ref, wg_bf), (wu_ref, wu_bf), (wd_ref, wd_bf)):
            for r in range(0, src.shape[2], MOE_CAST_ROWS):
                dst[r:r + MOE_CAST_ROWS, :] = src[0, 0, r:r + MOE_CAST_ROWS, :].astype(BF16)

    @pl.when(used)
    def _():
        halves = [_unpack_bf16_pairs(r[...]) for r in (x0_ref, x1_ref)]
        x = jnp.concatenate([h[0] for h in halves] + [h[1] for h in halves], axis=1).astype(BF16)
        g = jnp.minimum(_dot(x, wg_bf[...]) + bg_ref[0], SWIGLU_LIMIT)
        up = jnp.clip(_dot(x, wu_bf[...]) + bu_ref[0], -SWIGLU_LIMIT, SWIGLU_LIMIT)
        hb = (up + 1.0) * (g / (1.0 + jnp.exp(-SWIGLU_ALPHA * g)))
        packed = _pack_bf16_pairs(_dot(hb.astype(BF16), wd_bf[...]) + bd_ref[0])
        slab = y0_ref.shape[1]
        y0_ref[...] = packed[:, :slab]
        y1_ref[...] = packed[:, slab:]

    @pl.when(jnp.logical_not(used))
    def _():
        y0_ref[...] = jnp.zeros_like(y0_ref)
        y1_ref[...] = jnp.zeros_like(y1_ref)


def _moe(xs, block_e, n_used, w):
    n_rows, slab = xs[0].shape
    bm = MOE_BLOCK
    layer = w["layer"]
    wspec = lambda shp: pl.BlockSpec((1, 1) + shp, lambda i, be, nb: (layer, be[i], 0, 0))
    bspec = lambda shp: pl.BlockSpec((1,) + shp, lambda i, be, nb: (be[i], 0, 0))
    grid_spec = pltpu.PrefetchScalarGridSpec(
        num_scalar_prefetch=2, grid=(n_rows // bm,),
        in_specs=[pl.BlockSpec((bm, slab), lambda i, be, nb: (i, 0))] * DISPATCH_SLABS + [
                  wspec((D_MODEL, D_FF)), bspec((1, D_FF)), wspec((D_MODEL, D_FF)), bspec((1, D_FF)),
                  wspec((D_FF, D_MODEL)), bspec((1, D_MODEL))],
        out_specs=[pl.BlockSpec((bm, slab), lambda i, be, nb: (i, 0))] * DISPATCH_SLABS,
        scratch_shapes=[pltpu.VMEM((D_MODEL, D_FF), BF16), pltpu.VMEM((D_MODEL, D_FF), BF16),
                        pltpu.VMEM((D_FF, D_MODEL), BF16)])
    return pl.pallas_call(
        _moe_kernel, grid_spec=grid_spec,
        out_shape=[jax.ShapeDtypeStruct((n_rows, slab), jnp.uint32)] * DISPATCH_SLABS,
        compiler_params=_cparams(1), name="moe")(
            block_e, n_used, *xs, w["moe_w_gate"], w["moe_b_gate"], w["moe_w_up"], w["moe_b_up"],
            w["moe_w_down"], w["moe_b_down"])


def _ple_kernel(h1_ref, ya0_ref, ya1_ref, ya2_ref, ya3_ref, yb0_ref, yb1_ref, yb2_ref, yb3_ref, gate_ref, p_ref,
                wple_ref, gn_ref, wpg_ref, pn_ref, o_ref):
    def combine(st):
        rows = st["rows"]
        gates = gate_ref[rows, :]
        h2 = h1_ref[rows, :]
        for kk, (ya_ref, yb_ref) in enumerate(((ya0_ref, yb0_ref), (ya1_ref, yb1_ref), (ya2_ref, yb2_ref),
                                               (ya3_ref, yb3_ref))):
            lo_a, hi_a = _unpack_bf16_pairs(ya_ref[rows, :])
            lo_b, hi_b = _unpack_bf16_pairs(yb_ref[rows, :])
            h2 = h2 + gates[:, kk:kk + 1] * jnp.concatenate([lo_a, lo_b, hi_a, hi_b], axis=1)
        st["h2"] = h2
        st["hn"] = _rms(h2, gn_ref[...]).astype(BF16)

    def project(st):
        st["e"] = _dot(p_ref[0, st["rows"], :].astype(BF16), wple_ref[...])
        st["a"] = _dot(st["hn"], wpg_ref[...])

    def finish(st):
        gate = 1.0 / (1.0 + jnp.exp(-st["a"]))
        o_ref[st["rows"], :] = st["h2"] + _rms(st["e"] * gate, pn_ref[...])

    _skewed([combine, project, finish], h1_ref.shape[0], PLE_SUB)


def _ple(h1, y_slabs, gates, p, w, part):
    T = h1.shape[0]
    tm = TOKEN_TILE
    steps = T // COMBINE_PARTS // tm
    off = part * steps
    row = lambda n: pl.BlockSpec((tm, n), lambda i: (i + off, 0))
    slab = y_slabs[0].shape[1]
    gathered = lambda kk: pl.BlockSpec((tm, slab), lambda i: (kk * steps + i, 0))
    weights = [w["ple_w_proj"], w["ple_gate_norm"], w["ple_w_gate"], w["ple_post_norm"]]
    layer = w["layer"]
    p_spec = pl.BlockSpec((1, tm, D_PLE), lambda i: (layer, i + off, 0))
    ins = [h1] + [y for y in y_slabs for _ in range(TOP_K)] + [gates, p, *weights]
    in_specs = ([row(D_MODEL)] + [gathered(kk) for _ in y_slabs for kk in range(TOP_K)] + [row(128), p_spec]
                + [_full(a.shape) for a in weights])
    return pl.pallas_call(
        _ple_kernel, grid=(steps,), in_specs=in_specs,
        out_specs=row(D_MODEL), out_shape=jax.ShapeDtypeStruct((T, D_MODEL), F32),
        input_output_aliases={0: 0}, compiler_params=_cparams(1), name="ple")(*ins)


def _pad_heads(wm, per_head, n_heads=MLA_HEADS):
    kdim = wm.shape[0]
    w3 = wm.reshape(kdim, n_heads, per_head)
    return jnp.pad(w3, ((0, 0), (0, 0), (0, HEAD_PAD - per_head))).reshape(kdim, n_heads * HEAD_PAD)


def _swap_rope_halves(a):
    a3 = a.reshape(a.shape[0], -1, HEAD_PAD)
    half = MLA_ROPE // 2
    x1 = a3[:, :, MLA_NOPE:MLA_NOPE + half]
    x2 = a3[:, :, MLA_NOPE + half:MLA_QK]
    out = jnp.zeros_like(a3).at[:, :, MLA_NOPE:MLA_NOPE + half].set(x2).at[:, :, MLA_NOPE + half:MLA_QK].set(x1)
    return out.reshape(a.shape)


def _layer_params(i, mix_norm, w_in, gla_w_gate, gla_b_gate, gla_out_norm, mla_q_norm, mla_w_uq, mla_kv_norm,
                  mla_w_ukv, mla_qk_q_norm, mla_qk_k_norm, pool_w, pool_scale, w_out, ffn_norm, router_w,
                  router_b, moe_w_gate, moe_b_gate, moe_w_up, moe_b_up, moe_w_down, moe_b_down,
                  ple_w_proj, ple_gate_norm, ple_w_gate, ple_post_norm):
    wi = w_in[i]
    c = np.cumsum((0, 128, 128, 256, 16, 256, 256, 128, 32, 256))
    gq, gk, gv, glow, gr, cq, ckv, krope, upool = [wi[:, c[j]:c[j + 1]] for j in range(9)]
    misc = jnp.concatenate([glow, krope, jnp.zeros((D_MODEL, 128 - 48), F32)], axis=1)
    w_in_p = jnp.concatenate([gq, gk, gv, gr, cq, upool, ckv, misc], axis=1).astype(BF16)
    wgate_p = jnp.zeros((128, GLA_K), F32).at[MISC_GLOW:MISC_GLOW + GLA_GATE_RANK].set(gla_w_gate[i]).astype(BF16)
    ukv = mla_w_ukv[i].reshape(MLA_KV_RANK, MLA_HEADS, MLA_NOPE + MLA_V)
    ukv_k = _pad_heads(ukv[:, :, :MLA_NOPE].reshape(MLA_KV_RANK, MLA_HEADS * MLA_NOPE), MLA_NOPE)
    ukv_v = _pad_heads(ukv[:, :, MLA_NOPE:].reshape(MLA_KV_RANK, MLA_W), MLA_V)
    pw = pool_w[i]
    pool_bd = jnp.zeros((POOL_W, POOL_W), F32)
    for g in range(4):
        pool_bd = pool_bd.at[g * 64:(g + 1) * 64, g * 64:(g + 1) * 64].set(pw[g])
    rw = jnp.pad(router_w[i], ((0, 0), (0, 128 - N_EXPERTS)))
    rw_hi = rw.astype(BF16)
    rw_lo = (rw - rw_hi.astype(F32)).astype(BF16)
    row = lambda a: a.reshape(1, -1)
    pad96 = lambda a: jnp.pad(a, (0, HEAD_PAD - MLA_QK)).reshape(1, HEAD_PAD)
    wuq_p = _pad_heads(mla_w_uq[i], MLA_QK)
    gq_p = pad96(mla_qk_q_norm[i] * (MLA_QK ** -0.5 * LOG2E))
    return {
        "mix_norm": row(mix_norm[i]), "w_in": w_in_p, "gla_w_gate": wgate_p, "gla_b_gate": row(gla_b_gate[i]),
        "gla_out_norm": row(jnp.tile(gla_out_norm[i], GLA_HEADS)),
        "mla_q_norm": row(mla_q_norm[i]),
        "mla_w_uq": jnp.concatenate([wuq_p, _swap_rope_halves(wuq_p)], axis=1).astype(BF16),
        "mla_kv_norm": row(mla_kv_norm[i]), "mla_w_ukv_k": ukv_k.astype(BF16), "mla_w_ukv_v": ukv_v.astype(BF16),
        "mla_gq": jnp.concatenate([gq_p, _swap_rope_halves(gq_p)], axis=0), "mla_gk": pad96(mla_qk_k_norm[i]),
        "pool_w": pool_bd.astype(BF16), "pool_scale": row(pool_scale[i]),
        "w_out": w_out[i].astype(BF16), "ffn_norm": row(ffn_norm[i]),
        "router_w": jnp.concatenate([rw_hi, rw_lo], axis=1),
        "router_b": row(jnp.pad(router_b[i], (0, 128 - N_EXPERTS))),
        "layer": i,
        "moe_w_gate": moe_w_gate, "moe_b_gate": moe_b_gate[i].reshape(N_EXPERTS, 1, D_FF),
        "moe_w_up": moe_w_up, "moe_b_up": moe_b_up[i].reshape(N_EXPERTS, 1, D_FF),
        "moe_w_down": moe_w_down, "moe_b_down": moe_b_down[i].reshape(N_EXPERTS, 1, D_MODEL),
        "ple_w_proj": ple_w_proj[i].astype(BF16), "ple_gate_norm": row(ple_gate_norm[i]),
        "ple_w_gate": ple_w_gate[i].astype(BF16), "ple_post_norm": row(ple_post_norm[i]),
    }


def _rope_tables(positions):
    T = positions.size
    inv = ROPE_BASE ** (-jnp.arange(0, MLA_ROPE, 2, dtype=F32) / MLA_ROPE)
    ang = positions.reshape(T, 1).astype(F32) * inv
    cos, sin = jnp.cos(ang), jnp.sin(ang)
    z16 = jnp.zeros((T, 16), F32)
    tail = jnp.zeros((T, HEAD_PAD - MLA_QK), F32)
    c = jnp.concatenate([jnp.ones((T, MLA_NOPE), F32), cos, cos, tail], axis=1)
    s1 = jnp.concatenate([jnp.zeros((T, MLA_NOPE), F32), -sin, z16, tail], axis=1)
    s2 = jnp.concatenate([jnp.zeros((T, MLA_NOPE), F32), z16, sin, tail], axis=1)
    return c, s1, s2


def _route(top_idx, rank, counts, T):
    bm = MOE_BLOCK
    A = T * TOP_K
    padded = (counts + bm - 1) // bm * bm
    pad_end = jnp.cumsum(padded)
    pad_start = pad_end - padded
    experts = jnp.arange(N_EXPERTS, dtype=jnp.int32)
    dest = rank + jnp.sum(jnp.where(top_idx[:, :, None] == experts, pad_start, 0), axis=-1)
    n_blocks = (A + N_EXPERTS * (bm - 1) + bm - 1) // bm
    n_rows = n_blocks * bm
    block_start = jnp.arange(n_blocks, dtype=jnp.int32) * bm
    block_e = jnp.minimum(jnp.sum((pad_end[None, :] <= block_start[:, None]).astype(jnp.int32), axis=1),
                          N_EXPERTS - 1)
    n_used = (pad_end[-1] // bm).astype(jnp.int32).reshape(1)
    return dest, n_rows, block_e, n_used


def _dispatch(hn_slabs, dest, n_rows):
    T, width = hn_slabs[0].shape
    win = DISPATCH_ROWS
    dest_t = dest.T
    mesh = plsc.VectorSubcoreMesh(core_axis_name="core", subcore_axis_name="subcore")

    @functools.partial(pl.kernel, out_type=jax.ShapeDtypeStruct((n_rows, width), hn_slabs[0].dtype), mesh=mesh,
                       scratch_types=[], name="dispatch")
    def scatter_rows(x_hbm, i_hbm, o_hbm):
        def body(x_vmem, i_vmem):
            for kk in range(TOP_K):
                pltpu.sync_copy(x_vmem, o_hbm.at[i_vmem.at[kk]])

        pltpu.emit_pipeline(
            body, grid=(T // win,),
            in_specs=[pl.BlockSpec((win, width), lambda i: (i, 0)), pl.BlockSpec((TOP_K, win), lambda i: (0, i))],
            out_specs=[], core_axis_name=("core", "subcore"),
            dimension_semantics=(pltpu.PARALLEL,))(x_hbm, i_hbm)

    return [scatter_rows(slab, dest_t) for slab in hn_slabs]


def _combine_gather(y_slabs, dest_t):
    n_k, n_tok = dest_t.shape
    win = DISPATCH_ROWS
    width = y_slabs[0].shape[1]
    steps = n_tok // win
    mesh = plsc.VectorSubcoreMesh(core_axis_name="core", subcore_axis_name="subcore")

    @functools.partial(pl.kernel, out_type=jax.ShapeDtypeStruct((n_k * n_tok, width), y_slabs[0].dtype),
                       mesh=mesh, scratch_types=[], name="combine")
    def gather_rows(y_hbm, i_hbm, o_hbm):
        def body(i_vmem, o_vmem):
            pltpu.sync_copy(y_hbm.at[i_vmem.at[0]], o_vmem)

        pltpu.emit_pipeline(
            body, grid=(n_k, steps),
            in_specs=[pl.BlockSpec((1, win), lambda k, i: (k, i))],
            out_specs=[pl.BlockSpec((win, width), lambda k, i: (k * steps + i, 0))],
            core_axis_name=("core", "subcore"),
            dimension_semantics=(pltpu.PARALLEL, pltpu.PARALLEL))(i_hbm, o_hbm)

    return [gather_rows(y, dest_t) for y in y_slabs]


def kernel(x, p, positions, mix_norm, w_in, gla_w_gate, gla_b_gate, gla_out_norm, mla_q_norm, mla_w_uq,
           mla_kv_norm, mla_w_ukv, mla_qk_q_norm, mla_qk_k_norm, pool_w, pool_scale, w_out, ffn_norm,
           router_w, router_b, moe_w_gate, moe_b_gate, moe_w_up, moe_b_up, moe_w_down, moe_b_down,
           ple_w_proj, ple_gate_norm, ple_w_gate, ple_post_norm):
    B, S, D = x.shape
    T = B * S
    depth = p.shape[0]
    params = (mix_norm, w_in, gla_w_gate, gla_b_gate, gla_out_norm, mla_q_norm, mla_w_uq, mla_kv_norm,
              mla_w_ukv, mla_qk_q_norm, mla_qk_k_norm, pool_w, pool_scale, w_out, ffn_norm, router_w,
              router_b, moe_w_gate, moe_b_gate, moe_w_up, moe_b_up, moe_w_down, moe_b_down,
              ple_w_proj, ple_gate_norm, ple_w_gate, ple_post_norm)
    rope_c, rope_s1, rope_s2 = _rope_tables(positions)
    p_flat = p.reshape(depth, T, D_PLE)
    h = x.reshape(T, D)
    for i in range(depth):
        w = _layer_params(i, *params)
        y_gla, q, k, v, y_pool = _mix_pre(h, w, rope_c, rope_s1, rope_s2, S)
        y_mla = _attn(q, k, v, B, S)
        h1, hn0, hn1, route, gates, counts = _out_router(h, y_gla, y_mla, y_pool, w)
        dest, n_rows, block_e, n_used = _route(route[:, :TOP_K], route[:, TOP_K:2 * TOP_K],
                                               counts[0, :N_EXPERTS], T)
        ys = _moe(_dispatch([hn0, hn1], dest, n_rows), block_e, n_used, w)
        h = h1
        for part in range(COMBINE_PARTS):
            d = dest[part * (T // COMBINE_PARTS):(part + 1) * (T // COMBINE_PARTS)]
            h = _ple(h, _combine_gather(ys, d.T), gates, p_flat, w, part)
    return h.reshape(B, S, D)
```

```python
import functools

import jax
import jax.numpy as jnp
import numpy as np
from jax import lax
from jax.experimental import pallas as pl
from jax.experimental.pallas import tpu as pltpu
from jax.experimental.pallas import tpu_sc as plsc

F32 = jnp.float32
BF16 = jnp.bfloat16

D_MODEL = 1024
EPS = 1e-6
D_PLE = 256

GLA_HEADS = 4
GLA_DK = 32
GLA_DV = 64
GLA_GATE_RANK = 16
GLA_TAU = 16.0
GLA_CHUNK = 64
GLA_K = GLA_HEADS * GLA_DK
GLA_W = GLA_HEADS * GLA_DV

MLA_HEADS = 8
MLA_Q_RANK = 256
MLA_KV_RANK = 128
MLA_NOPE = 64
MLA_ROPE = 32
MLA_QK = MLA_NOPE + MLA_ROPE
MLA_V = 64
MLA_W = MLA_HEADS * MLA_V
ROPE_BASE = 10000.0
HEAD_PAD = 128
MLA_QK_PAD = MLA_HEADS * HEAD_PAD

POOL_WINDOWS = (2, 4, 8, 16)
POOL_GROUP = 64
POOL_W = 256
POOL_HALO = 16

N_EXPERTS = 32
TOP_K = 4
D_FF = 1024
SWIGLU_LIMIT = 7.0
SWIGLU_ALPHA = 1.702

COL_GQ, COL_GK, COL_GV, COL_GR, COL_CQ, COL_POOL, COL_CKV, COL_MISC = 0, 128, 256, 512, 768, 1024, 1280, 1408
D_IN_PAD = 1536
MISC_GLOW = 0
MISC_ROPE = 16

LOG2E = 1.4426950408889634
TOKEN_TILE = 1024
MIX_SUB, ROUTER_SUB, PLE_SUB = 512, 512, 256
ATTN_TILE = 2048
ATTN_SUB = 512
MOE_BLOCK = 1024
MOE_PART = 512
MOE_CAST_ROWS = 256
COMBINE_PARTS = 4
DISPATCH_ROWS = 128
DISPATCH_SLABS = 2
VMEM_LIMIT = 56 * 1024 * 1024
NEG_BIG = -1e30


def _cparams(n_axes, **flags):
    return pltpu.CompilerParams(dimension_semantics=("arbitrary",) * n_axes,
                                vmem_limit_bytes=VMEM_LIMIT, flags=flags or None)


def _rms(x, g):
    return x * lax.rsqrt(jnp.mean(x * x, axis=-1, keepdims=True) + EPS) * g


def _dot(a, b):
    return jnp.dot(a, b, preferred_element_type=F32)


def _dot_nt(a, b):
    return lax.dot_general(a, b, (((1,), (1,)), ((), ())), preferred_element_type=F32)


def _dot_tn(a, b):
    return lax.dot_general(a, b, (((0,), (0,)), ((), ())), preferred_element_type=F32)


def _split3(x):
    hi = x.astype(BF16)
    r = x - hi.astype(F32)
    mid = r.astype(BF16)
    lo = (r - mid.astype(F32)).astype(BF16)
    return hi, mid, lo


def _split2(x):
    hi = x.astype(BF16)
    lo = (x - hi.astype(F32)).astype(BF16)
    return hi, lo


def _pack_bf16_pairs(x):
    m = x.shape[1] // 2
    bits = lax.bitcast_convert_type(x.astype(BF16).astype(F32), jnp.uint32)
    return (bits[:, :m] >> 16) | (bits[:, m:] & jnp.uint32(0xFFFF0000))


def _unpack_bf16_pairs(w):
    lo = lax.bitcast_convert_type(w << 16, F32)
    hi = lax.bitcast_convert_type(w & jnp.uint32(0xFFFF0000), F32)
    return lo, hi


def _skewed(stages, n_rows, sub):
    states = [{"rows": slice(r0, r0 + sub)} for r0 in range(0, n_rows, sub)]
    for step in range(len(states) + len(stages) - 1):
        for s, stage in enumerate(stages):
            t = step - s
            if 0 <= t < len(states):
                stage(states[t])


def _full(shape):
    nd = len(shape)
    return pl.BlockSpec(shape, lambda *_: (0,) * nd)


def _rope(x, c, s1, s2):
    return x * c + pltpu.roll(x, HEAD_PAD - 16, 1) * s1 + pltpu.roll(x, 16, 1) * s2


def _mix_pre_kernel(h_ref, mixn_ref, win_ref, wgate_ref, bgate_ref, qn_ref, wuq_ref, kvn_ref,
                    wukvk_ref, wukvv_ref, gq_ref, gk_ref, rc_ref, rs1_ref, rs2_ref,
                    wpool_ref, pscale_ref, gn_ref,
                    yg_ref, q_ref, k_ref, v_ref, yp_ref, carry_ref, gla_state_ref, *, tiles_per_seq):
    tm = h_ref.shape[0]
    sub = MIX_SUB
    seq_tile = pl.program_id(0) % tiles_per_seq

    @pl.when(seq_tile == 0)
    def _():
        carry_ref[...] = jnp.zeros_like(carry_ref)
        gla_state_ref[...] = jnp.zeros_like(gla_state_ref)

    gla_consts = _gla_consts()
    gla_state = [gla_state_ref[...]]

    lane = lax.broadcasted_iota(jnp.int32, (sub, HEAD_PAD), 1)
    in_rope = (lane >= MLA_NOPE) & (lane < MLA_QK)
    lane_p = lax.broadcasted_iota(jnp.int32, (sub, POOL_W), 1)
    row_p = lax.broadcasted_iota(jnp.int32, (sub, POOL_W), 0)
    g0, g1, g2 = lane_p < 64, lane_p < 128, lane_p < 192
    win = jnp.where(g0, 2.0, jnp.where(g1, 4.0, jnp.where(g2, 8.0, 16.0)))
    gq, gq_sw, gk = gq_ref[0:1, :], gq_ref[1:2, :], gk_ref[...]

    def norm_in(st):
        st["hn"] = _rms(h_ref[st["rows"], :], mixn_ref[...]).astype(BF16)

    def project_in(st):
        st["z"] = _dot(st["hn"], win_ref[...])

    def norm_latents(st):
        z = st["z"]
        st["cqn"] = _rms(z[:, COL_CQ:COL_CQ + MLA_Q_RANK], qn_ref[...]).astype(BF16)
        st["ckvn"] = _rms(z[:, COL_CKV:COL_CKV + MLA_KV_RANK], kvn_ref[...]).astype(BF16)

    def project_up(st):
        zm = st["z"][:, COL_MISC:COL_MISC + 128]
        st["logit"] = _dot(zm.astype(BF16), wgate_ref[...]) + bgate_ref[...]
        st["qf"] = _dot(st["cqn"], wuq_ref[...])
        st["kn"] = _dot(st["ckvn"], wukvk_ref[...])
        st["v"] = _dot(st["ckvn"], wukvv_ref[...])

    def heads_and_pool(st):
        rows, z, qf, kn, logit = st["rows"], st["z"], st["qf"], st["kn"], st["logit"]
        zm = z[:, COL_MISC:COL_MISC + 128]
        st["la"] = (jnp.minimum(logit, 0.0) - jnp.log(1.0 + jnp.exp(-jnp.abs(logit)))) * (1.0 / GLA_TAU)
        v_ref[rows, :] = st["v"].astype(BF16)

        rc, rs1, rs2 = rc_ref[rows, :], rs1_ref[rows, :], rs2_ref[rows, :]
        kr = jnp.where(in_rope, pltpu.roll(zm, MLA_NOPE - MISC_ROPE, 1), 0.0)
        kr_ss = jnp.sum(kr * kr, axis=-1, keepdims=True)
        krr = _rope(kr * gk, rc, rs1, rs2)
        cq = rc * gq
        sq_tab = (rs1 + rs2) * gq_sw
        for hh in range(MLA_HEADS):
            sl = slice(hh * HEAD_PAD, (hh + 1) * HEAD_PAD)
            qh = qf[:, sl]
            qsw = qf[:, MLA_QK_PAD + hh * HEAD_PAD:MLA_QK_PAD + (hh + 1) * HEAD_PAD]
            sq = lax.rsqrt(jnp.sum(qh * qh, axis=-1, keepdims=True) * (1.0 / MLA_QK) + EPS)
            q_ref[rows, sl] = ((qh * cq + qsw * sq_tab) * sq).astype(BF16)
            kh = kn[:, sl]
            sk = lax.rsqrt((jnp.sum(kh * kh, axis=-1, keepdims=True) + kr_ss) * (1.0 / MLA_QK) + EPS)
            k_ref[rows, sl] = (sk * (kh * gk + krr)).astype(BF16)

        u = z[:, COL_POOL:COL_POOL + POOL_W]
        xe = jnp.concatenate([carry_ref[...], u], axis=0)
        carry_ref[...] = u[sub - POOL_HALO:, :]
        s2 = xe + pltpu.roll(xe, 1, 0)
        s4 = s2 + pltpu.roll(s2, 2, 0)
        s8 = s4 + pltpu.roll(s4, 4, 0)
        s16 = s8 + pltpu.roll(s8, 8, 0)
        pooled = jnp.where(g0, s2[POOL_HALO:], jnp.where(g1, s4[POOL_HALO:],
                           jnp.where(g2, s8[POOL_HALO:], s16[POOL_HALO:])))
        cnt = jnp.minimum((seq_tile * tm + rows.start + row_p + 1).astype(F32), win)
        st["d"] = (pooled / cnt - u).astype(BF16)

    def project_pool(st):
        yp_ref[st["rows"], :] = (_dot(st["d"], wpool_ref[...]) * pscale_ref[...]).astype(BF16)

    def gla(st):
        yg_ref[st["rows"], :] = _gla_rows(st["z"], st["la"], gn_ref[...], gla_consts, gla_state)

    _skewed([norm_in, project_in, norm_latents, project_up, heads_and_pool, project_pool, gla], tm, sub)
    gla_state_ref[...] = gla_state[0]


def _mix_pre(h, w, rope_c, rope_s1, rope_s2, seq_len):
    T = h.shape[0]
    tm = TOKEN_TILE
    row = lambda n: pl.BlockSpec((tm, n), lambda i: (i, 0))
    ins = [h, w["mix_norm"], w["w_in"], w["gla_w_gate"], w["gla_b_gate"], w["mla_q_norm"], w["mla_w_uq"],
           w["mla_kv_norm"], w["mla_w_ukv_k"], w["mla_w_ukv_v"], w["mla_gq"], w["mla_gk"],
           rope_c, rope_s1, rope_s2, w["pool_w"], w["pool_scale"], w["gla_out_norm"]]
    in_specs = [row(D_MODEL)] + [_full(a.shape) for a in ins[1:12]] + [row(HEAD_PAD)] * 3 + \
               [_full(a.shape) for a in ins[15:]]
    out_shape = [jax.ShapeDtypeStruct((T, GLA_W), BF16),
                 jax.ShapeDtypeStruct((T, MLA_QK_PAD), BF16), jax.ShapeDtypeStruct((T, MLA_QK_PAD), BF16),
                 jax.ShapeDtypeStruct((T, MLA_W), BF16), jax.ShapeDtypeStruct((T, POOL_W), BF16)]
    out_specs = [row(GLA_W), row(MLA_QK_PAD), row(MLA_QK_PAD), row(MLA_W), row(POOL_W)]
    return pl.pallas_call(
        functools.partial(_mix_pre_kernel, tiles_per_seq=seq_len // tm),
        grid=(T // tm,), in_specs=in_specs, out_specs=out_specs, out_shape=out_shape,
        scratch_shapes=[pltpu.VMEM((POOL_HALO, POOL_W), F32), pltpu.VMEM((GLA_K, GLA_W), F32)],
        compiler_params=_cparams(1), name="mix_pre")(*ins)


def _gla_consts():
    C = GLA_CHUNK
    r_i = lax.broadcasted_iota(jnp.int32, (C, C), 0)
    c_i = lax.broadcasted_iota(jnp.int32, (C, C), 1)
    ar = lax.broadcasted_iota(jnp.int32, (GLA_HEADS * C, C), 0)
    ac = lax.broadcasted_iota(jnp.int32, (GLA_HEADS * C, C), 1)
    sk = lax.broadcasted_iota(jnp.int32, (GLA_K, GLA_W), 0) // GLA_DK
    sv = lax.broadcasted_iota(jnp.int32, (GLA_K, GLA_W), 1) // GLA_DV
    gr = lax.broadcasted_iota(jnp.int32, (GLA_W, GLA_W), 0) // GLA_DV
    gc = lax.broadcasted_iota(jnp.int32, (GLA_W, GLA_W), 1) // GLA_DV
    return {
        "tri": (r_i >= c_i).astype(BF16),
        "ones": jnp.ones((C, GLA_W), BF16),
        "head_k": lax.broadcasted_iota(jnp.int32, (C, GLA_K), 1) // GLA_DK,
        "head_v": lax.broadcasted_iota(jnp.int32, (C, GLA_W), 1) // GLA_DV,
        "causal": (ar % C) >= ac,
        "blockdiag": sk == sv,
        "group": (gr == gc).astype(BF16),
    }


def _gla_rows(z, la, gn, consts, state):
    C = GLA_CHUNK
    tri, ones, head_k, head_v = consts["tri"], consts["ones"], consts["head_k"], consts["head_v"]
    causal, blockdiag = consts["causal"], consts["blockdiag"]
    outs = []

    def log_decay(st):
        la3 = _split3(la[st["rows"], :])
        st["bc"] = _dot(tri, la3[0]) + _dot(tri, la3[1]) + _dot(tri, la3[2])
        st["dsum"] = _dot_tn(la3[0], ones) + _dot_tn(la3[1], ones) + _dot_tn(la3[2], ones)

    def scores(st):
        rows, bc = st["rows"], st["bc"]
        q = z[rows, COL_GQ:COL_GQ + GLA_K] * (GLA_DK ** -0.5)
        k = z[rows, COL_GK:COL_GK + GLA_K]
        b_last = bc[C - 1:C, :]
        q_dec = (q * jnp.exp(bc)).astype(BF16)
        k_dec = (k * jnp.exp(-bc)).astype(BF16)
        st["k_end"] = (k * jnp.exp(b_last - bc)).astype(BF16)
        st["decay"] = jnp.exp(st["dsum"])
        zero = jnp.zeros_like(q_dec)
        qs = jnp.concatenate([jnp.where(head_k == hh, q_dec, zero) for hh in range(GLA_HEADS)], axis=0)
        st["q_dec"] = q_dec
        st["att"] = _dot_nt(qs, k_dec)

    def values(st):
        v = z[st["rows"], COL_GV:COL_GV + GLA_W].astype(BF16)
        st["o_full"] = _dot(jnp.where(causal, st["att"], 0.0).astype(BF16), v)
        st["upd"] = jnp.where(blockdiag, _dot_tn(st["k_end"], v), 0.0)

    def recur(st):
        o_full = st["o_full"]
        o = _dot(st["q_dec"], state[0].astype(BF16))
        for hh in range(GLA_HEADS):
            o = o + jnp.where(head_v == hh, o_full[hh * C:(hh + 1) * C, :], 0.0)
        outs.append(o)
        state[0] = st["decay"] * state[0] + st["upd"]

    _skewed([log_decay, scores, values, recur], z.shape[0], C)

    o = jnp.concatenate(outs, axis=0)
    oo = _split2(o * o)
    ms = (_dot(oo[0], consts["group"]) + _dot(oo[1], consts["group"])) * (1.0 / GLA_DV)
    r = z[:, COL_GR:COL_GR + GLA_W]
    return (o * lax.rsqrt(ms + EPS) * gn * (r / (1.0 + jnp.exp(-r)))).astype(BF16)


def _attn_kernel(q_ref, k_ref, v_ref, o_ref, m_ref, acc_ref):
    tq = q_ref.shape[0]
    ts = ATTN_SUB
    i = pl.program_id(2)
    m_ref[...] = jnp.full_like(m_ref, NEG_BIG)
    acc_ref[...] = jnp.zeros_like(acc_ref)

    def sub_block(hh, start, r0, mask_off):
        hs = slice(hh * HEAD_PAD, (hh + 1) * HEAD_PAD)
        kj = k_ref[pl.ds(start, ts), hs]
        vp = v_ref[pl.ds(start, ts), :]
        if hh == 1:
            vp = jnp.concatenate([vp[:, MLA_V:], vp[:, :MLA_V]], axis=1)
        lane_v = lax.broadcasted_iota(jnp.int32, vp.shape, 1)
        vj = jnp.where(lane_v < MLA_V, vp, jnp.where(lane_v == MLA_V, 1.0, 0.0).astype(BF16))
        s = _dot_nt(q_ref[r0:, hs], kj)
        if mask_off is not None:
            row = lax.broadcasted_iota(jnp.int32, s.shape, 0) + r0
            col = lax.broadcasted_iota(jnp.int32, s.shape, 1) + mask_off
            s = jnp.where(col <= row, s, NEG_BIG)
        m_old = m_ref[hh, r0:, :]
        parts = [s[:, c * 128:(c + 1) * 128] for c in range(ts // 128)]
        m_new = jnp.maximum(m_old, jnp.max(functools.reduce(jnp.maximum, parts), axis=-1, keepdims=True))
        p = jnp.concatenate([jnp.exp2((x - m_new).astype(BF16)) for x in parts], axis=1)
        acc_ref[hh, r0:, :] = jnp.exp2(m_old - m_new) * acc_ref[hh, r0:, :] + _dot(p, vj)
        m_ref[hh, r0:, :] = m_new

    def body(j, carry):
        base = pl.multiple_of(j * tq, tq)
        for sb in range(tq // ts):
            for hh in range(2):
                sub_block(hh, base + sb * ts, 0, None)
        return carry

    lax.fori_loop(0, i, body, 0)
    base = pl.multiple_of(i * tq, tq)
    for sb in range(tq // ts):
        for hh in range(2):
            sub_block(hh, base + sb * ts, sb * ts, sb * ts)
    outs = []
    for hh in range(2):
        a = acc_ref[hh]
        outs.append(a / a[:, MLA_V:MLA_V + 1])
    lane = lax.broadcasted_iota(jnp.int32, (tq, HEAD_PAD), 1)
    o_ref[...] = jnp.where(lane < MLA_V, outs[0], pltpu.roll(outs[1], MLA_V, 1)).astype(BF16)


def _attn(q, k, v, batch, seq_len):
    T = q.shape[0]
    tq = ATTN_TILE
    nq = seq_len // tq
    pairs = MLA_HEADS // 2
    return pl.pallas_call(
        _attn_kernel, grid=(batch, pairs, nq),
        in_specs=[pl.BlockSpec((tq, 2 * HEAD_PAD), lambda b, p, i: (b * nq + i, p)),
                  pl.BlockSpec((seq_len, 2 * HEAD_PAD), lambda b, p, i: (b, p)),
                  pl.BlockSpec((seq_len, 2 * MLA_V), lambda b, p, i: (b, p))],
        out_specs=pl.BlockSpec((tq, 2 * MLA_V), lambda b, p, i: (b * nq + i, p)),
        out_shape=jax.ShapeDtypeStruct((T, MLA_W), BF16),
        scratch_shapes=[pltpu.VMEM((2, tq, HEAD_PAD), F32), pltpu.VMEM((2, tq, HEAD_PAD), F32)],
        compiler_params=_cparams(3), name="attn")(q, k, v)


def _out_router_kernel(h_ref, yg_ref, ym_ref, yp_ref, wo_ref, fn_ref, rw_ref, rb_ref,
                       h1_ref, hn0_ref, hn1_ref, idx_ref, gate_ref, cnt_ref, carry_ref):
    tm = h_ref.shape[0]

    @pl.when(pl.program_id(0) == 0)
    def _():
        carry_ref[...] = jnp.zeros_like(carry_ref)

    sub = ROUTER_SUB
    lane = lax.broadcasted_iota(jnp.int32, (sub, 128), 1)
    r_i = lax.broadcasted_iota(jnp.int32, (sub, sub), 0)
    c_i = lax.broadcasted_iota(jnp.int32, (sub, sub), 1)
    tri = (r_i >= c_i).astype(BF16)
    def project(st):
        rows = st["rows"]
        st["h1"] = (h_ref[rows, :] + _dot(yg_ref[rows, :], wo_ref[0:GLA_W, :])
                    + _dot(ym_ref[rows, :], wo_ref[GLA_W:GLA_W + MLA_W, :])
                    + _dot(yp_ref[rows, :], wo_ref[GLA_W + MLA_W:, :]))

    def normalize(st):
        rows = st["rows"]
        h1_ref[rows, :] = st["h1"]
        hn = _rms(st["h1"], fn_ref[...])
        st["hi"], st["lo"] = _split2(hn)
        packed = _pack_bf16_pairs(hn)
        slab = packed.shape[1] // DISPATCH_SLABS
        hn0_ref[rows, :] = packed[:, :slab]
        hn1_ref[rows, :] = packed[:, slab:]

    def score(st):
        r2 = _dot(st["hi"], rw_ref[...])
        st["logits"] = r2[:, :128] + r2[:, 128:] + _dot(st["lo"], rw_ref[:, 0:128]) + rb_ref[...]

    def select(st):
        rows = st["rows"]
        cur = jnp.where(lane < N_EXPERTS, st["logits"], NEG_BIG)
        idx_out = jnp.zeros((sub, 128), jnp.int32)
        val_out = jnp.zeros((sub, 128), F32)
        chosen = jnp.zeros((sub, 128), F32)
        top0 = None
        sels = []
        for kk in range(TOP_K):
            m = jnp.max(cur, axis=-1, keepdims=True)
            sel = jnp.min(jnp.where(cur == m, lane, 128), axis=-1, keepdims=True)
            if kk == 0:
                top0 = m
            sels.append(sel)
            idx_out = jnp.where(lane == kk, sel, idx_out)
            val_out = jnp.where(lane == kk, jnp.exp(m - top0), val_out)
            chosen = jnp.where(lane == sel, 1.0, chosen)
            cur = jnp.where(lane == sel, NEG_BIG, cur)
        gate_ref[rows, :] = val_out / jnp.sum(val_out, axis=-1, keepdims=True)

        incl = _dot(tri, chosen.astype(BF16))
        before = carry_ref[0:1, :] + incl - chosen
        for kk in range(TOP_K):
            rank = jnp.sum(jnp.where(lane == sels[kk], before, 0.0), axis=-1, keepdims=True)
            idx_out = jnp.where(lane == TOP_K + kk, rank.astype(jnp.int32), idx_out)
        idx_ref[rows, :] = idx_out
        carry_ref[...] = carry_ref[...] + incl[sub - 1:sub, :]

    _skewed([project, normalize, score, select], tm, sub)
    cnt_ref[...] = carry_ref[...].astype(jnp.int32)


def _out_router(h, yg, ym, yp, w):
    T = h.shape[0]
    tm = TOKEN_TILE
    row = lambda n: pl.BlockSpec((tm, n), lambda i: (i, 0))
    slab = D_MODEL // 2 // DISPATCH_SLABS
    ins = [h, yg, ym, yp, w["w_out"], w["ffn_norm"], w["router_w"], w["router_b"]]
    return pl.pallas_call(
        _out_router_kernel, grid=(T // tm,),
        in_specs=[row(D_MODEL), row(GLA_W), row(MLA_W), row(POOL_W)] + [_full(a.shape) for a in ins[4:]],
        out_specs=[row(D_MODEL), row(slab), row(slab), row(128), row(128), _full((8, 128))],
        out_shape=[jax.ShapeDtypeStruct((T, D_MODEL), F32), jax.ShapeDtypeStruct((T, slab), jnp.uint32),
                   jax.ShapeDtypeStruct((T, slab), jnp.uint32),
                   jax.ShapeDtypeStruct((T, 128), jnp.int32), jax.ShapeDtypeStruct((T, 128), F32),
                   jax.ShapeDtypeStruct((8, 128), jnp.int32)],
        scratch_shapes=[pltpu.VMEM((8, 128), F32)],
        compiler_params=_cparams(1), name="out_router")(*ins)


def _moe_kernel(be_ref, nb_ref, x0_ref, x1_ref, wg_ref, bg_ref, wu_ref, bu_ref, wd_ref, bd_ref,
                y0_ref, y1_ref, wg_bf, wu_bf, wd_bf):
    i = pl.program_id(0)
    used = nb_ref[i] > 0
    new_expert = (i == 0) | (be_ref[i] != be_ref[jnp.maximum(i - 1, 0)])
    slab = y0_ref.shape[1]

    @pl.when(used & new_expert)
    def _():
        for src, dst in ((wg_ref, wg_bf), (wu_ref, wu_bf), (wd_ref, wd_bf)):
            for r in range(0, src.shape[2], MOE_CAST_ROWS):
                dst[r:r + MOE_CAST_ROWS, :] = src[0, 0, r:r + MOE_CAST_ROWS, :].astype(BF16)

    for r0 in range(0, x0_ref.shape[0], MOE_PART):
        rows = slice(r0, r0 + MOE_PART)

        @pl.when(nb_ref[i] > r0)
        def _(rows=rows):
            halves = [_unpack_bf16_pairs(r[rows, :]) for r in (x0_ref, x1_ref)]
            x = jnp.concatenate([h[0] for h in halves] + [h[1] for h in halves], axis=1).astype(BF16)
            g = jnp.minimum(_dot(x, wg_bf[...]) + bg_ref[0], SWIGLU_LIMIT)
            up = jnp.clip(_dot(x, wu_bf[...]) + bu_ref[0], -SWIGLU_LIMIT, SWIGLU_LIMIT)
            hb = (up + 1.0) * (g / (1.0 + jnp.exp(-SWIGLU_ALPHA * g)))
            packed = _pack_bf16_pairs(_dot(hb.astype(BF16), wd_bf[...]) + bd_ref[0])
            y0_ref[rows, :] = packed[:, :slab]
            y1_ref[rows, :] = packed[:, slab:]

        @pl.when(nb_ref[i] <= r0)
        def _(rows=rows):
            y0_ref[rows, :] = jnp.zeros((MOE_PART, slab), y0_ref.dtype)
            y1_ref[rows, :] = jnp.zeros((MOE_PART, slab), y1_ref.dtype)


def _moe(xs, block_e, n_used, w):
    n_rows, slab = xs[0].shape
    bm = MOE_BLOCK
    layer = w["layer"]
    wspec = lambda shp: pl.BlockSpec((1, 1) + shp, lambda i, be, nb: (layer, be[i], 0, 0))
    bspec = lambda shp: pl.BlockSpec((1,) + shp, lambda i, be, nb: (be[i], 0, 0))
    grid_spec = pltpu.PrefetchScalarGridSpec(
        num_scalar_prefetch=2, grid=(n_rows // bm,),
        in_specs=[pl.BlockSpec((bm, slab), lambda i, be, nb: (i, 0))] * DISPATCH_SLABS + [
                  wspec((D_MODEL, D_FF)), bspec((1, D_FF)), wspec((D_MODEL, D_FF)), bspec((1, D_FF)),
                  wspec((D_FF, D_MODEL)), bspec((1, D_MODEL))],
        out_specs=[pl.BlockSpec((bm, slab), lambda i, be, nb: (i, 0))] * DISPATCH_SLABS,
        scratch_shapes=[pltpu.VMEM((D_MODEL, D_FF), BF16), pltpu.VMEM((D_MODEL, D_FF), BF16),
                        pltpu.VMEM((D_FF, D_MODEL), BF16)])
    return pl.pallas_call(
        _moe_kernel, grid_spec=grid_spec,
        out_shape=[jax.ShapeDtypeStruct((n_rows, slab), jnp.uint32)] * DISPATCH_SLABS,
        compiler_params=_cparams(1), name="moe")(
            block_e, n_used, *xs, w["moe_w_gate"], w["moe_b_gate"], w["moe_w_up"], w["moe_b_up"],
            w["moe_w_down"], w["moe_b_down"])


def _ple_kernel(h1_ref, ya0_ref, ya1_ref, ya2_ref, ya3_ref, yb0_ref, yb1_ref, yb2_ref, yb3_ref, gate_ref, p_ref,
                wple_ref, gn_ref, wpg_ref, pn_ref, o_ref):
    def combine(st):
        rows = st["rows"]
        gates = gate_ref[rows, :]
        h2 = h1_ref[rows, :]
        for kk, (ya_ref, yb_ref) in enumerate(((ya0_ref, yb0_ref), (ya1_ref, yb1_ref), (ya2_ref, yb2_ref),
                                               (ya3_ref, yb3_ref))):
            lo_a, hi_a = _unpack_bf16_pairs(ya_ref[rows, :])
            lo_b, hi_b = _unpack_bf16_pairs(yb_ref[rows, :])
            h2 = h2 + gates[:, kk:kk + 1] * jnp.concatenate([lo_a, lo_b, hi_a, hi_b], axis=1)
        st["h2"] = h2
        st["hn"] = _rms(h2, gn_ref[...]).astype(BF16)

    def project(st):
        st["e"] = _dot(p_ref[0, st["rows"], :].astype(BF16), wple_ref[...])
        st["a"] = _dot(st["hn"], wpg_ref[...])

    def finish(st):
        gate = 1.0 / (1.0 + jnp.exp(-st["a"]))
        o_ref[st["rows"], :] = st["h2"] + _rms(st["e"] * gate, pn_ref[...])

    _skewed([combine, project, finish], h1_ref.shape[0], PLE_SUB)


def _ple(h1, y_slabs, gates, p, w, part):
    T = h1.shape[0]
    tm = TOKEN_TILE
    steps = T // COMBINE_PARTS // tm
    off = part * steps
    row = lambda n: pl.BlockSpec((tm, n), lambda i: (i + off, 0))
    slab = y_slabs[0].shape[1]
    gathered = lambda kk: pl.BlockSpec((tm, slab), lambda i: (kk * steps + i, 0))
    weights = [w["ple_w_proj"], w["ple_gate_norm"], w["ple_w_gate"], w["ple_post_norm"]]
    layer = w["layer"]
    p_spec = pl.BlockSpec((1, tm, D_PLE), lambda i: (layer, i + off, 0))
    ins = [h1] + [y for y in y_slabs for _ in range(TOP_K)] + [gates, p, *weights]
    in_specs = ([row(D_MODEL)] + [gathered(kk) for _ in y_slabs for kk in range(TOP_K)] + [row(128), p_spec]
                + [_full(a.shape) for a in weights])
    return pl.pallas_call(
        _ple_kernel, grid=(steps,), in_specs=in_specs,
        out_specs=row(D_MODEL), out_shape=jax.ShapeDtypeStruct((T, D_MODEL), F32),
        input_output_aliases={0: 0}, compiler_params=_cparams(1), name="ple")(*ins)


def _pad_heads(wm, per_head, n_heads=MLA_HEADS):
    kdim = wm.shape[0]
    w3 = wm.reshape(kdim, n_heads, per_head)
    return jnp.pad(w3, ((0, 0), (0, 0), (0, HEAD_PAD - per_head))).reshape(kdim, n_heads * HEAD_PAD)


def _swap_rope_halves(a):
    a3 = a.reshape(a.shape[0], -1, HEAD_PAD)
    half = MLA_ROPE // 2
    x1 = a3[:, :, MLA_NOPE:MLA_NOPE + half]
    x2 = a3[:, :, MLA_NOPE + half:MLA_QK]
    out = jnp.zeros_like(a3).at[:, :, MLA_NOPE:MLA_NOPE + half].set(x2).at[:, :, MLA_NOPE + half:MLA_QK].set(x1)
    return out.reshape(a.shape)


def _layer_params(i, mix_norm, w_in, gla_w_gate, gla_b_gate, gla_out_norm, mla_q_norm, mla_w_uq, mla_kv_norm,
                  mla_w_ukv, mla_qk_q_norm, mla_qk_k_norm, pool_w, pool_scale, w_out, ffn_norm, router_w,
                  router_b, moe_w_gate, moe_b_gate, moe_w_up, moe_b_up, moe_w_down, moe_b_down,
                  ple_w_proj, ple_gate_norm, ple_w_gate, ple_post_norm):
    wi = w_in[i]
    c = np.cumsum((0, 128, 128, 256, 16, 256, 256, 128, 32, 256))
    gq, gk, gv, glow, gr, cq, ckv, krope, upool = [wi[:, c[j]:c[j + 1]] for j in range(9)]
    misc = jnp.concatenate([glow, krope, jnp.zeros((D_MODEL, 128 - 48), F32)], axis=1)
    w_in_p = jnp.concatenate([gq, gk, gv, gr, cq, upool, ckv, misc], axis=1).astype(BF16)
    wgate_p = jnp.zeros((128, GLA_K), F32).at[MISC_GLOW:MISC_GLOW + GLA_GATE_RANK].set(gla_w_gate[i]).astype(BF16)
    ukv = mla_w_ukv[i].reshape(MLA_KV_RANK, MLA_HEADS, MLA_NOPE + MLA_V)
    ukv_k = _pad_heads(ukv[:, :, :MLA_NOPE].reshape(MLA_KV_RANK, MLA_HEADS * MLA_NOPE), MLA_NOPE)
    ukv_v = ukv[:, :, MLA_NOPE:].reshape(MLA_KV_RANK, MLA_W)
    pw = pool_w[i]
    pool_bd = jnp.zeros((POOL_W, POOL_W), F32)
    for g in range(4):
        pool_bd = pool_bd.at[g * 64:(g + 1) * 64, g * 64:(g + 1) * 64].set(pw[g])
    rw = jnp.pad(router_w[i], ((0, 0), (0, 128 - N_EXPERTS)))
    rw_hi = rw.astype(BF16)
    rw_lo = (rw - rw_hi.astype(F32)).astype(BF16)
    row = lambda a: a.reshape(1, -1)
    pad96 = lambda a: jnp.pad(a, (0, HEAD_PAD - MLA_QK)).reshape(1, HEAD_PAD)
    wuq_p = _pad_heads(mla_w_uq[i], MLA_QK)
    gq_p = pad96(mla_qk_q_norm[i] * (MLA_QK ** -0.5 * LOG2E))
    return {
        "mix_norm": row(mix_norm[i]), "w_in": w_in_p, "gla_w_gate": wgate_p, "gla_b_gate": row(gla_b_gate[i]),
        "gla_out_norm": row(jnp.tile(gla_out_norm[i], GLA_HEADS)),
        "mla_q_norm": row(mla_q_norm[i]),
        "mla_w_uq": jnp.concatenate([wuq_p, _swap_rope_halves(wuq_p)], axis=1).astype(BF16),
        "mla_kv_norm": row(mla_kv_norm[i]), "mla_w_ukv_k": ukv_k.astype(BF16), "mla_w_ukv_v": ukv_v.astype(BF16),
        "mla_gq": jnp.concatenate([gq_p, _swap_rope_halves(gq_p)], axis=0), "mla_gk": pad96(mla_qk_k_norm[i]),
        "pool_w": pool_bd.astype(BF16), "pool_scale": row(pool_scale[i]),
        "w_out": w_out[i].astype(BF16), "ffn_norm": row(ffn_norm[i]),
        "router_w": jnp.concatenate([rw_hi, rw_lo], axis=1),
        "router_b": row(jnp.pad(router_b[i], (0, 128 - N_EXPERTS))),
        "layer": i,
        "moe_w_gate": moe_w_gate, "moe_b_gate": moe_b_gate[i].reshape(N_EXPERTS, 1, D_FF),
        "moe_w_up": moe_w_up, "moe_b_up": moe_b_up[i].reshape(N_EXPERTS, 1, D_FF),
        "moe_w_down": moe_w_down, "moe_b_down": moe_b_down[i].reshape(N_EXPERTS, 1, D_MODEL),
        "ple_w_proj": ple_w_proj[i].astype(BF16), "ple_gate_norm": row(ple_gate_norm[i]),
        "ple_w_gate": ple_w_gate[i].astype(BF16), "ple_post_norm": row(ple_post_norm[i]),
    }


def _rope_tables(positions):
    T = positions.size
    inv = ROPE_BASE ** (-jnp.arange(0, MLA_ROPE, 2, dtype=F32) / MLA_ROPE)
    ang = positions.reshape(T, 1).astype(F32) * inv
    cos, sin = jnp.cos(ang), jnp.sin(ang)
    z16 = jnp.zeros((T, 16), F32)
    tail = jnp.zeros((T, HEAD_PAD - MLA_QK), F32)
    c = jnp.concatenate([jnp.ones((T, MLA_NOPE), F32), cos, cos, tail], axis=1)
    s1 = jnp.concatenate([jnp.zeros((T, MLA_NOPE), F32), -sin, z16, tail], axis=1)
    s2 = jnp.concatenate([jnp.zeros((T, MLA_NOPE), F32), z16, sin, tail], axis=1)
    return c, s1, s2


def _route(top_idx, rank, counts, T):
    bm = MOE_BLOCK
    A = T * TOP_K
    padded = (counts + bm - 1) // bm * bm
    pad_end = jnp.cumsum(padded)
    pad_start = pad_end - padded
    experts = jnp.arange(N_EXPERTS, dtype=jnp.int32)
    dest = rank + jnp.sum(jnp.where(top_idx[:, :, None] == experts, pad_start, 0), axis=-1)
    n_blocks = (A + N_EXPERTS * (bm - 1) + bm - 1) // bm
    n_rows = n_blocks * bm
    block_start = jnp.arange(n_blocks, dtype=jnp.int32) * bm
    block_e = jnp.minimum(jnp.sum((pad_end[None, :] <= block_start[:, None]).astype(jnp.int32), axis=1),
                          N_EXPERTS - 1)
    block_rows = jnp.clip((pad_start + counts)[block_e] - block_start, 0, bm).astype(jnp.int32)
    return dest, n_rows, block_e, block_rows


def _dispatch(hn_slabs, dest, n_rows):
    T, width = hn_slabs[0].shape
    win = DISPATCH_ROWS
    dest_t = dest.T
    mesh = plsc.VectorSubcoreMesh(core_axis_name="core", subcore_axis_name="subcore")

    @functools.partial(pl.kernel, out_type=jax.ShapeDtypeStruct((n_rows, width), hn_slabs[0].dtype), mesh=mesh,
                       scratch_types=[], name="dispatch")
    def scatter_rows(x_hbm, i_hbm, o_hbm):
        def body(x_vmem, i_vmem):
            for kk in range(TOP_K):
                pltpu.sync_copy(x_vmem, o_hbm.at[i_vmem.at[kk]])

        pltpu.emit_pipeline(
            body, grid=(T // win,),
            in_specs=[pl.BlockSpec((win, width), lambda i: (i, 0)), pl.BlockSpec((TOP_K, win), lambda i: (0, i))],
            out_specs=[], core_axis_name=("core", "subcore"),
            dimension_semantics=(pltpu.PARALLEL,))(x_hbm, i_hbm)

    return [scatter_rows(slab, dest_t) for slab in hn_slabs]


def _combine_gather(y_slabs, dest_t):
    n_k, n_tok = dest_t.shape
    win = DISPATCH_ROWS
    width = y_slabs[0].shape[1]
    steps = n_tok // win
    mesh = plsc.VectorSubcoreMesh(core_axis_name="core", subcore_axis_name="subcore")

    @functools.partial(pl.kernel, out_type=jax.ShapeDtypeStruct((n_k * n_tok, width), y_slabs[0].dtype),
                       mesh=mesh, scratch_types=[], name="combine")
    def gather_rows(y_hbm, i_hbm, o_hbm):
        def body(i_vmem, o_vmem):
            pltpu.sync_copy(y_hbm.at[i_vmem.at[0]], o_vmem)

        pltpu.emit_pipeline(
            body, grid=(n_k, steps),
            in_specs=[pl.BlockSpec((1, win), lambda k, i: (k, i))],
            out_specs=[pl.BlockSpec((win, width), lambda k, i: (k * steps + i, 0))],
            core_axis_name=("core", "subcore"),
            dimension_semantics=(pltpu.PARALLEL, pltpu.PARALLEL))(i_hbm, o_hbm)

    return [gather_rows(y, dest_t) for y in y_slabs]


def kernel(x, p, positions, mix_norm, w_in, gla_w_gate, gla_b_gate, gla_out_norm, mla_q_norm, mla_w_uq,
           mla_kv_norm, mla_w_ukv, mla_qk_q_norm, mla_qk_k_norm, pool_w, pool_scale, w_out, ffn_norm,
           router_w, router_b, moe_w_gate, moe_b_gate, moe_w_up, moe_b_up, moe_w_down, moe_b_down,
           ple_w_proj, ple_gate_norm, ple_w_gate, ple_post_norm):
    B, S, D = x.shape
    T = B * S
    depth = p.shape[0]
    params = (mix_norm, w_in, gla_w_gate, gla_b_gate, gla_out_norm, mla_q_norm, mla_w_uq, mla_kv_norm,
              mla_w_ukv, mla_qk_q_norm, mla_qk_k_norm, pool_w, pool_scale, w_out, ffn_norm, router_w,
              router_b, moe_w_gate, moe_b_gate, moe_w_up, moe_b_up, moe_w_down, moe_b_down,
              ple_w_proj, ple_gate_norm, ple_w_gate, ple_post_norm)
    rope_c, rope_s1, rope_s2 = _rope_tables(positions)
    p_flat = p.reshape(depth, T, D_PLE)
    h = x.reshape(T, D)
    for i in range(depth):
        w = _layer_params(i, *params)
        y_gla, q, k, v, y_pool = _mix_pre(h, w, rope_c, rope_s1, rope_s2, S)
        y_mla = _attn(q, k, v, B, S)
        h1, hn0, hn1, route, gates, counts = _out_router(h, y_gla, y_mla, y_pool, w)
        dest, n_rows, block_e, n_used = _route(route[:, :TOP_K], route[:, TOP_K:2 * TOP_K],
                                               counts[0, :N_EXPERTS], T)
        ys = _moe(_dispatch([hn0, hn1], dest, n_rows), block_e, n_used, w)
        h = h1
        for part in range(COMBINE_PARTS):
            d = dest[part * (T // COMBINE_PARTS):(part + 1) * (T // COMBINE_PARTS)]
            h = _ple(h, _combine_gather(ys, d.T), gates, p_flat, w, part)
    return h.reshape(B, S, D)
```

```python
import functools

import jax
import jax.numpy as jnp
import numpy as np
from jax import lax
from jax.experimental import pallas as pl
from jax.experimental.pallas import tpu as pltpu
from jax.experimental.pallas import tpu_sc as plsc

F32 = jnp.float32
BF16 = jnp.bfloat16

D_MODEL = 1024
EPS = 1e-6
D_PLE = 256

GLA_HEADS = 4
GLA_DK = 32
GLA_DV = 64
GLA_GATE_RANK = 16
GLA_TAU = 16.0
GLA_CHUNK = 64
GLA_K = GLA_HEADS * GLA_DK
GLA_W = GLA_HEADS * GLA_DV

MLA_HEADS = 8
MLA_Q_RANK = 256
MLA_KV_RANK = 128
MLA_NOPE = 64
MLA_ROPE = 32
MLA_QK = MLA_NOPE + MLA_ROPE
MLA_V = 64
MLA_W = MLA_HEADS * MLA_V
ROPE_BASE = 10000.0
HEAD_PAD = 128
MLA_QK_PAD = MLA_HEADS * HEAD_PAD

POOL_W = 256
POOL_HALO = 16

N_EXPERTS = 32
TOP_K = 4
D_FF = 1024
SWIGLU_LIMIT = 7.0
SWIGLU_ALPHA = 1.702

COL_GQ, COL_GK, COL_GV, COL_GR, COL_CQ, COL_POOL, COL_CKV, COL_MISC = 0, 128, 256, 512, 768, 1024, 1280, 1408
D_IN_PAD = 1536
MISC_GLOW = 0
MISC_ROPE = 16

LOG2E = 1.4426950408889634
TOKEN_TILE = 1024
MIX_SUB, ROUTER_SUB, PLE_SUB = 512, 512, 256
ATTN_TILE = 2048
ATTN_SUB = 512
MOE_BLOCK = 1024
MOE_CAST_ROWS = 256
COMBINE_PARTS = 4
DISPATCH_ROWS = 128
DISPATCH_SLABS = 2
VMEM_LIMIT = 56 * 1024 * 1024
NEG_BIG = -1e30


def _cparams(n_axes):
    return pltpu.CompilerParams(dimension_semantics=("arbitrary",) * n_axes, vmem_limit_bytes=VMEM_LIMIT)


def _rms(x, g):
    return x * lax.rsqrt(jnp.mean(x * x, axis=-1, keepdims=True) + EPS) * g


def _dot(a, b):
    return jnp.dot(a, b, preferred_element_type=F32)


def _dot_nt(a, b):
    return lax.dot_general(a, b, (((1,), (1,)), ((), ())), preferred_element_type=F32)


def _dot_tn(a, b):
    return lax.dot_general(a, b, (((0,), (0,)), ((), ())), preferred_element_type=F32)


def _split3(x):
    hi = x.astype(BF16)
    r = x - hi.astype(F32)
    mid = r.astype(BF16)
    lo = (r - mid.astype(F32)).astype(BF16)
    return hi, mid, lo


def _split2(x):
    hi = x.astype(BF16)
    lo = (x - hi.astype(F32)).astype(BF16)
    return hi, lo


def _pack_bf16_pairs(x):
    m = x.shape[1] // 2
    bits = lax.bitcast_convert_type(x.astype(BF16).astype(F32), jnp.uint32)
    return (bits[:, :m] >> 16) | (bits[:, m:] & jnp.uint32(0xFFFF0000))


def _unpack_bf16_pairs(w):
    lo = lax.bitcast_convert_type(w << 16, F32)
    hi = lax.bitcast_convert_type(w & jnp.uint32(0xFFFF0000), F32)
    return lo, hi


def _skewed(stages, n_rows, sub):
    states = [{"rows": slice(r0, r0 + sub)} for r0 in range(0, n_rows, sub)]
    for step in range(len(states) + len(stages) - 1):
        for s, stage in enumerate(stages):
            t = step - s
            if 0 <= t < len(states):
                stage(states[t])


def _full(shape):
    nd = len(shape)
    return pl.BlockSpec(shape, lambda *_: (0,) * nd)


def _rope(x, c, s1, s2):
    return x * c + pltpu.roll(x, HEAD_PAD - 16, 1) * s1 + pltpu.roll(x, 16, 1) * s2


def _mix_pre_kernel(h_ref, mixn_ref, win_ref, wgate_ref, bgate_ref, qn_ref, wuq_ref, kvn_ref,
                    wukvk_ref, wukvv_ref, gq_ref, gk_ref, rc_ref, rs1_ref, rs2_ref,
                    wpool_ref, pscale_ref, gn_ref,
                    yg_ref, q_ref, k_ref, v_ref, yp_ref, carry_ref, gla_state_ref, *, tiles_per_seq):
    tm = h_ref.shape[0]
    sub = MIX_SUB
    seq_tile = pl.program_id(0) % tiles_per_seq

    @pl.when(seq_tile == 0)
    def _():
        carry_ref[...] = jnp.zeros_like(carry_ref)
        gla_state_ref[...] = jnp.zeros_like(gla_state_ref)

    gla_consts = _gla_consts()
    gla_state = [gla_state_ref[...]]

    lane = lax.broadcasted_iota(jnp.int32, (sub, HEAD_PAD), 1)
    in_rope = (lane >= MLA_NOPE) & (lane < MLA_QK)
    lane_p = lax.broadcasted_iota(jnp.int32, (sub, POOL_W), 1)
    row_p = lax.broadcasted_iota(jnp.int32, (sub, POOL_W), 0)
    g0, g1, g2 = lane_p < 64, lane_p < 128, lane_p < 192
    win = jnp.where(g0, 2.0, jnp.where(g1, 4.0, jnp.where(g2, 8.0, 16.0)))
    gq, gq_sw, gk = gq_ref[0:1, :], gq_ref[1:2, :], gk_ref[...]

    def norm_in(st):
        st["hn"] = _rms(h_ref[st["rows"], :], mixn_ref[...]).astype(BF16)

    def project_in(st):
        st["z"] = _dot(st["hn"], win_ref[...])

    def norm_latents(st):
        z = st["z"]
        st["cqn"] = _rms(z[:, COL_CQ:COL_CQ + MLA_Q_RANK], qn_ref[...]).astype(BF16)
        st["ckvn"] = _rms(z[:, COL_CKV:COL_CKV + MLA_KV_RANK], kvn_ref[...]).astype(BF16)

    def project_up(st):
        zm = st["z"][:, COL_MISC:COL_MISC + 128]
        st["logit"] = _dot(zm.astype(BF16), wgate_ref[...]) + bgate_ref[...]
        st["qf"] = _dot(st["cqn"], wuq_ref[...])
        st["kn"] = _dot(st["ckvn"], wukvk_ref[...])
        st["v"] = _dot(st["ckvn"], wukvv_ref[...])

    def heads_and_pool(st):
        rows, z, qf, kn, logit = st["rows"], st["z"], st["qf"], st["kn"], st["logit"]
        zm = z[:, COL_MISC:COL_MISC + 128]
        st["la"] = (jnp.minimum(logit, 0.0) - jnp.log(1.0 + jnp.exp(-jnp.abs(logit)))) * (1.0 / GLA_TAU)
        v_ref[rows, :] = st["v"].astype(BF16)

        rc, rs1, rs2 = rc_ref[rows, :], rs1_ref[rows, :], rs2_ref[rows, :]
        kr = jnp.where(in_rope, pltpu.roll(zm, MLA_NOPE - MISC_ROPE, 1), 0.0)
        kr_ss = jnp.sum(kr * kr, axis=-1, keepdims=True)
        krr = _rope(kr * gk, rc, rs1, rs2)
        cq = rc * gq
        sq_tab = (rs1 + rs2) * gq_sw
        for hh in range(MLA_HEADS):
            sl = slice(hh * HEAD_PAD, (hh + 1) * HEAD_PAD)
            qh = qf[:, sl]
            qsw = qf[:, MLA_QK_PAD + hh * HEAD_PAD:MLA_QK_PAD + (hh + 1) * HEAD_PAD]
            sq = lax.rsqrt(jnp.sum(qh * qh, axis=-1, keepdims=True) * (1.0 / MLA_QK) + EPS)
            q_ref[rows, sl] = ((qh * cq + qsw * sq_tab) * sq).astype(BF16)
            kh = kn[:, sl]
            sk = lax.rsqrt((jnp.sum(kh * kh, axis=-1, keepdims=True) + kr_ss) * (1.0 / MLA_QK) + EPS)
            k_ref[rows, sl] = (sk * (kh * gk + krr)).astype(BF16)

        u = z[:, COL_POOL:COL_POOL + POOL_W]
        xe = jnp.concatenate([carry_ref[...], u], axis=0)
        carry_ref[...] = u[sub - POOL_HALO:, :]
        s2 = xe + pltpu.roll(xe, 1, 0)
        s4 = s2 + pltpu.roll(s2, 2, 0)
        s8 = s4 + pltpu.roll(s4, 4, 0)
        s16 = s8 + pltpu.roll(s8, 8, 0)
        pooled = jnp.where(g0, s2[POOL_HALO:], jnp.where(g1, s4[POOL_HALO:],
                           jnp.where(g2, s8[POOL_HALO:], s16[POOL_HALO:])))
        cnt = jnp.minimum((seq_tile * tm + rows.start + row_p + 1).astype(F32), win)
        st["d"] = (pooled / cnt - u).astype(BF16)

    def project_pool(st):
        yp_ref[st["rows"], :] = (_dot(st["d"], wpool_ref[...]) * pscale_ref[...]).astype(BF16)

    def gla(st):
        yg_ref[st["rows"], :] = _gla_rows(st["z"], st["la"], gn_ref[...], gla_consts, gla_state)

    _skewed([norm_in, project_in, norm_latents, project_up, heads_and_pool, project_pool, gla], tm, sub)
    gla_state_ref[...] = gla_state[0]


def _mix_pre(h, w, rope_c, rope_s1, rope_s2, seq_len):
    T = h.shape[0]
    tm = TOKEN_TILE
    row = lambda n: pl.BlockSpec((tm, n), lambda i: (i, 0))
    ins = [h, w["mix_norm"], w["w_in"], w["gla_w_gate"], w["gla_b_gate"], w["mla_q_norm"], w["mla_w_uq"],
           w["mla_kv_norm"], w["mla_w_ukv_k"], w["mla_w_ukv_v"], w["mla_gq"], w["mla_gk"],
           rope_c, rope_s1, rope_s2, w["pool_w"], w["pool_scale"], w["gla_out_norm"]]
    in_specs = [row(D_MODEL)] + [_full(a.shape) for a in ins[1:12]] + [row(HEAD_PAD)] * 3 + \
               [_full(a.shape) for a in ins[15:]]
    out_shape = [jax.ShapeDtypeStruct((T, GLA_W), BF16),
                 jax.ShapeDtypeStruct((T, MLA_QK_PAD), BF16), jax.ShapeDtypeStruct((T, MLA_QK_PAD), BF16),
                 jax.ShapeDtypeStruct((T, MLA_W), BF16), jax.ShapeDtypeStruct((T, POOL_W), BF16)]
    out_specs = [row(GLA_W), row(MLA_QK_PAD), row(MLA_QK_PAD), row(MLA_W), row(POOL_W)]
    return pl.pallas_call(
        functools.partial(_mix_pre_kernel, tiles_per_seq=seq_len // tm),
        grid=(T // tm,), in_specs=in_specs, out_specs=out_specs, out_shape=out_shape,
        scratch_shapes=[pltpu.VMEM((POOL_HALO, POOL_W), F32), pltpu.VMEM((GLA_K, GLA_W), F32)],
        compiler_params=_cparams(1), name="mix_pre")(*ins)


def _gla_consts():
    C = GLA_CHUNK
    r_i = lax.broadcasted_iota(jnp.int32, (C, C), 0)
    c_i = lax.broadcasted_iota(jnp.int32, (C, C), 1)
    ar = lax.broadcasted_iota(jnp.int32, (GLA_HEADS * C, C), 0)
    ac = lax.broadcasted_iota(jnp.int32, (GLA_HEADS * C, C), 1)
    sk = lax.broadcasted_iota(jnp.int32, (GLA_K, GLA_W), 0) // GLA_DK
    sv = lax.broadcasted_iota(jnp.int32, (GLA_K, GLA_W), 1) // GLA_DV
    gr = lax.broadcasted_iota(jnp.int32, (GLA_W, GLA_W), 0) // GLA_DV
    gc = lax.broadcasted_iota(jnp.int32, (GLA_W, GLA_W), 1) // GLA_DV
    return {
        "tri": (r_i >= c_i).astype(BF16),
        "ones": jnp.ones((C, GLA_W), BF16),
        "head_k": lax.broadcasted_iota(jnp.int32, (C, GLA_K), 1) // GLA_DK,
        "head_v": lax.broadcasted_iota(jnp.int32, (C, GLA_W), 1) // GLA_DV,
        "causal": (ar % C) >= ac,
        "blockdiag": sk == sv,
        "group": (gr == gc).astype(BF16),
    }


def _gla_rows(z, la, gn, consts, state):
    C = GLA_CHUNK
    tri, ones, head_k, head_v = consts["tri"], consts["ones"], consts["head_k"], consts["head_v"]
    causal, blockdiag = consts["causal"], consts["blockdiag"]
    outs = []

    def log_decay(st):
        la3 = _split3(la[st["rows"], :])
        st["bc"] = _dot(tri, la3[0]) + _dot(tri, la3[1]) + _dot(tri, la3[2])
        st["dsum"] = _dot_tn(la3[0], ones) + _dot_tn(la3[1], ones) + _dot_tn(la3[2], ones)

    def scores(st):
        rows, bc = st["rows"], st["bc"]
        q = z[rows, COL_GQ:COL_GQ + GLA_K] * (GLA_DK ** -0.5)
        k = z[rows, COL_GK:COL_GK + GLA_K]
        b_last = bc[C - 1:C, :]
        q_dec = (q * jnp.exp(bc)).astype(BF16)
        k_dec = (k * jnp.exp(-bc)).astype(BF16)
        st["k_end"] = (k * jnp.exp(b_last - bc)).astype(BF16)
        st["decay"] = jnp.exp(st["dsum"])
        zero = jnp.zeros_like(q_dec)
        qs = jnp.concatenate([jnp.where(head_k == hh, q_dec, zero) for hh in range(GLA_HEADS)], axis=0)
        st["q_dec"] = q_dec
        st["att"] = _dot_nt(qs, k_dec)

    def values(st):
        v = z[st["rows"], COL_GV:COL_GV + GLA_W].astype(BF16)
        st["o_full"] = _dot(jnp.where(causal, st["att"], 0.0).astype(BF16), v)
        st["upd"] = jnp.where(blockdiag, _dot_tn(st["k_end"], v), 0.0)

    def recur(st):
        o_full = st["o_full"]
        o = _dot(st["q_dec"], state[0].astype(BF16))
        for hh in range(GLA_HEADS):
            o = o + jnp.where(head_v == hh, o_full[hh * C:(hh + 1) * C, :], 0.0)
        outs.append(o)
        state[0] = st["decay"] * state[0] + st["upd"]

    _skewed([log_decay, scores, values, recur], z.shape[0], C)

    o = jnp.concatenate(outs, axis=0)
    oo = _split2(o * o)
    ms = (_dot(oo[0], consts["group"]) + _dot(oo[1], consts["group"])) * (1.0 / GLA_DV)
    r = z[:, COL_GR:COL_GR + GLA_W]
    return (o * lax.rsqrt(ms + EPS) * gn * (r / (1.0 + jnp.exp(-r)))).astype(BF16)


def _attn_kernel(q_ref, k_ref, v_ref, o_ref, m_ref, acc_ref):
    tq = q_ref.shape[0]
    ts = ATTN_SUB
    i = pl.program_id(2)
    m_ref[...] = jnp.full_like(m_ref, NEG_BIG)
    acc_ref[...] = jnp.zeros_like(acc_ref)

    def sub_block(hh, start, r0, mask_off):
        hs = slice(hh * HEAD_PAD, (hh + 1) * HEAD_PAD)
        kj = k_ref[pl.ds(start, ts), hs]
        vp = v_ref[pl.ds(start, ts), :]
        if hh == 1:
            vp = jnp.concatenate([vp[:, MLA_V:], vp[:, :MLA_V]], axis=1)
        lane_v = lax.broadcasted_iota(jnp.int32, vp.shape, 1)
        vj = jnp.where(lane_v < MLA_V, vp, jnp.where(lane_v == MLA_V, 1.0, 0.0).astype(BF16))
        s = _dot_nt(q_ref[r0:, hs], kj)
        if mask_off is not None:
            row = lax.broadcasted_iota(jnp.int32, s.shape, 0) + r0
            col = lax.broadcasted_iota(jnp.int32, s.shape, 1) + mask_off
            s = jnp.where(col <= row, s, NEG_BIG)
        m_old = m_ref[hh, r0:, :]
        parts = [s[:, c * 128:(c + 1) * 128] for c in range(ts // 128)]
        m_new = jnp.maximum(m_old, jnp.max(functools.reduce(jnp.maximum, parts), axis=-1, keepdims=True))
        p = jnp.concatenate([jnp.exp2((x - m_new).astype(BF16)) for x in parts], axis=1)
        acc_ref[hh, r0:, :] = jnp.exp2(m_old - m_new) * acc_ref[hh, r0:, :] + _dot(p, vj)
        m_ref[hh, r0:, :] = m_new

    def body(j, carry):
        base = pl.multiple_of(j * tq, tq)
        for sb in range(tq // ts):
            for hh in range(2):
                sub_block(hh, base + sb * ts, 0, None)
        return carry

    lax.fori_loop(0, i, body, 0)
    base = pl.multiple_of(i * tq, tq)
    for sb in range(tq // ts):
        for hh in range(2):
            sub_block(hh, base + sb * ts, sb * ts, sb * ts)
    outs = []
    for hh in range(2):
        a = acc_ref[hh]
        outs.append(a / a[:, MLA_V:MLA_V + 1])
    lane = lax.broadcasted_iota(jnp.int32, (tq, HEAD_PAD), 1)
    o_ref[...] = jnp.where(lane < MLA_V, outs[0], pltpu.roll(outs[1], MLA_V, 1)).astype(BF16)


def _attn(q, k, v, batch, seq_len):
    T = q.shape[0]
    tq = ATTN_TILE
    nq = seq_len // tq
    pairs = MLA_HEADS // 2
    return pl.pallas_call(
        _attn_kernel, grid=(batch, pairs, nq),
        in_specs=[pl.BlockSpec((tq, 2 * HEAD_PAD), lambda b, p, i: (b * nq + i, p)),
                  pl.BlockSpec((seq_len, 2 * HEAD_PAD), lambda b, p, i: (b, p)),
                  pl.BlockSpec((seq_len, 2 * MLA_V), lambda b, p, i: (b, p))],
        out_specs=pl.BlockSpec((tq, 2 * MLA_V), lambda b, p, i: (b * nq + i, p)),
        out_shape=jax.ShapeDtypeStruct((T, MLA_W), BF16),
        scratch_shapes=[pltpu.VMEM((2, tq, HEAD_PAD), F32), pltpu.VMEM((2, tq, HEAD_PAD), F32)],
        compiler_params=_cparams(3), name="attn")(q, k, v)


def _out_router_kernel(h_ref, yg_ref, ym_ref, yp_ref, wo_ref, fn_ref, rw_ref, rb_ref,
                       h1_ref, hn0_ref, hn1_ref, idx_ref, gate_ref, cnt_ref, carry_ref):
    tm = h_ref.shape[0]

    @pl.when(pl.program_id(0) == 0)
    def _():
        carry_ref[...] = jnp.zeros_like(carry_ref)

    sub = ROUTER_SUB
    lane = lax.broadcasted_iota(jnp.int32, (sub, 128), 1)
    r_i = lax.broadcasted_iota(jnp.int32, (sub, sub), 0)
    c_i = lax.broadcasted_iota(jnp.int32, (sub, sub), 1)
    tri = (r_i >= c_i).astype(BF16)
    def project(st):
        rows = st["rows"]
        st["h1"] = (h_ref[rows, :] + _dot(yg_ref[rows, :], wo_ref[0:GLA_W, :])
                    + _dot(ym_ref[rows, :], wo_ref[GLA_W:GLA_W + MLA_W, :])
                    + _dot(yp_ref[rows, :], wo_ref[GLA_W + MLA_W:, :]))

    def normalize(st):
        rows = st["rows"]
        h1_ref[rows, :] = st["h1"]
        hn = _rms(st["h1"], fn_ref[...])
        st["hi"], st["lo"] = _split2(hn)
        packed = _pack_bf16_pairs(hn)
        slab = packed.shape[1] // DISPATCH_SLABS
        hn0_ref[rows, :] = packed[:, :slab]
        hn1_ref[rows, :] = packed[:, slab:]

    def score(st):
        r2 = _dot(st["hi"], rw_ref[...])
        st["logits"] = r2[:, :128] + r2[:, 128:] + _dot(st["lo"], rw_ref[:, 0:128]) + rb_ref[...]

    def select(st):
        rows = st["rows"]
        cur = jnp.where(lane < N_EXPERTS, st["logits"], NEG_BIG)
        idx_out = jnp.zeros((sub, 128), jnp.int32)
        val_out = jnp.zeros((sub, 128), F32)
        chosen = jnp.zeros((sub, 128), F32)
        top0 = None
        sels = []
        for kk in range(TOP_K):
            m = jnp.max(cur, axis=-1, keepdims=True)
            sel = jnp.min(jnp.where(cur == m, lane, 128), axis=-1, keepdims=True)
            if kk == 0:
                top0 = m
            sels.append(sel)
            idx_out = jnp.where(lane == kk, sel, idx_out)
            val_out = jnp.where(lane == kk, jnp.exp(m - top0), val_out)
            chosen = jnp.where(lane == sel, 1.0, chosen)
            cur = jnp.where(lane == sel, NEG_BIG, cur)
        gate_ref[rows, :] = val_out / jnp.sum(val_out, axis=-1, keepdims=True)

        incl = _dot(tri, chosen.astype(BF16))
        before = carry_ref[0:1, :] + incl - chosen
        for kk in range(TOP_K):
            rank = jnp.sum(jnp.where(lane == sels[kk], before, 0.0), axis=-1, keepdims=True)
            idx_out = jnp.where(lane == TOP_K + kk, rank.astype(jnp.int32), idx_out)
        idx_ref[rows, :] = idx_out
        carry_ref[...] = carry_ref[...] + incl[sub - 1:sub, :]

    _skewed([project, normalize, score, select], tm, sub)
    cnt_ref[...] = carry_ref[...].astype(jnp.int32)


def _out_router(h, yg, ym, yp, w):
    T = h.shape[0]
    tm = TOKEN_TILE
    row = lambda n: pl.BlockSpec((tm, n), lambda i: (i, 0))
    slab = D_MODEL // 2 // DISPATCH_SLABS
    ins = [h, yg, ym, yp, w["w_out"], w["ffn_norm"], w["router_w"], w["router_b"]]
    return pl.pallas_call(
        _out_router_kernel, grid=(T // tm,),
        in_specs=[row(D_MODEL), row(GLA_W), row(MLA_W), row(POOL_W)] + [_full(a.shape) for a in ins[4:]],
        out_specs=[row(D_MODEL), row(slab), row(slab), row(128), row(128), _full((8, 128))],
        out_shape=[jax.ShapeDtypeStruct((T, D_MODEL), F32), jax.ShapeDtypeStruct((T, slab), jnp.uint32),
                   jax.ShapeDtypeStruct((T, slab), jnp.uint32),
                   jax.ShapeDtypeStruct((T, 128), jnp.int32), jax.ShapeDtypeStruct((T, 128), F32),
                   jax.ShapeDtypeStruct((8, 128), jnp.int32)],
        scratch_shapes=[pltpu.VMEM((8, 128), F32)],
        compiler_params=_cparams(1), name="out_router")(*ins)


def _moe_kernel(be_ref, nb_ref, x0_ref, x1_ref, wg_ref, bg_ref, wu_ref, bu_ref, wd_ref, bd_ref,
                y0_ref, y1_ref, wg_bf, wu_bf, wd_bf):
    i = pl.program_id(0)
    used = i < nb_ref[0]
    new_expert = (i == 0) | (be_ref[i] != be_ref[jnp.maximum(i - 1, 0)])

    @pl.when(used & new_expert)
    def _():
        for src, dst in ((wg_ref, wg_bf), (wu_ref, wu_bf), (wd_ref, wd_bf)):
            for r in range(0, src.shape[2], MOE_CAST_ROWS):
                dst[r:r + MOE_CAST_ROWS, :] = src[0, 0, r:r + MOE_CAST_ROWS, :].astype(BF16)

    @pl.when(used)
    def _():
        halves = [_unpack_bf16_pairs(r[...]) for r in (x0_ref, x1_ref)]
        x = jnp.concatenate([h[0] for h in halves] + [h[1] for h in halves], axis=1).astype(BF16)
        g = jnp.minimum(_dot(x, wg_bf[...]) + bg_ref[0], SWIGLU_LIMIT)
        up = jnp.clip(_dot(x, wu_bf[...]) + bu_ref[0], -SWIGLU_LIMIT, SWIGLU_LIMIT)
        hb = (up + 1.0) * (g / (1.0 + jnp.exp(-SWIGLU_ALPHA * g)))
        packed = _pack_bf16_pairs(_dot(hb.astype(BF16), wd_bf[...]) + bd_ref[0])
        slab = y0_ref.shape[1]
        y0_ref[...] = packed[:, :slab]
        y1_ref[...] = packed[:, slab:]

    @pl.when(jnp.logical_not(used))
    def _():
        y0_ref[...] = jnp.zeros_like(y0_ref)
        y1_ref[...] = jnp.zeros_like(y1_ref)


def _moe(xs, block_e, n_used, w):
    n_rows, slab = xs[0].shape
    bm = MOE_BLOCK
    layer = w["layer"]
    wspec = lambda shp: pl.BlockSpec((1, 1) + shp, lambda i, be, nb: (layer, be[i], 0, 0))
    bspec = lambda shp: pl.BlockSpec((1,) + shp, lambda i, be, nb: (be[i], 0, 0))
    grid_spec = pltpu.PrefetchScalarGridSpec(
        num_scalar_prefetch=2, grid=(n_rows // bm,),
        in_specs=[pl.BlockSpec((bm, slab), lambda i, be, nb: (i, 0))] * DISPATCH_SLABS + [
                  wspec((D_MODEL, D_FF)), bspec((1, D_FF)), wspec((D_MODEL, D_FF)), bspec((1, D_FF)),
                  wspec((D_FF, D_MODEL)), bspec((1, D_MODEL))],
        out_specs=[pl.BlockSpec((bm, slab), lambda i, be, nb: (i, 0))] * DISPATCH_SLABS,
        scratch_shapes=[pltpu.VMEM((D_MODEL, D_FF), BF16), pltpu.VMEM((D_MODEL, D_FF), BF16),
                        pltpu.VMEM((D_FF, D_MODEL), BF16)])
    return pl.pallas_call(
        _moe_kernel, grid_spec=grid_spec,
        out_shape=[jax.ShapeDtypeStruct((n_rows, slab), jnp.uint32)] * DISPATCH_SLABS,
        compiler_params=_cparams(1), name="moe")(
            block_e, n_used, *xs, w["moe_w_gate"], w["moe_b_gate"], w["moe_w_up"], w["moe_b_up"],
            w["moe_w_down"], w["moe_b_down"])


def _ple_kernel(h1_ref, ya0_ref, ya1_ref, ya2_ref, ya3_ref, yb0_ref, yb1_ref, yb2_ref, yb3_ref, gate_ref, p_ref,
                wple_ref, gn_ref, wpg_ref, pn_ref, o_ref):
    def combine(st):
        rows = st["rows"]
        gates = gate_ref[rows, :]
        h2 = h1_ref[rows, :]
        for kk, (ya_ref, yb_ref) in enumerate(((ya0_ref, yb0_ref), (ya1_ref, yb1_ref), (ya2_ref, yb2_ref),
                                               (ya3_ref, yb3_ref))):
            lo_a, hi_a = _unpack_bf16_pairs(ya_ref[rows, :])
            lo_b, hi_b = _unpack_bf16_pairs(yb_ref[rows, :])
            h2 = h2 + gates[:, kk:kk + 1] * jnp.concatenate([lo_a, lo_b, hi_a, hi_b], axis=1)
        st["h2"] = h2
        st["hn"] = _rms(h2, gn_ref[...]).astype(BF16)

    def project(st):
        st["e"] = _dot(p_ref[0, st["rows"], :].astype(BF16), wple_ref[...])
        st["a"] = _dot(st["hn"], wpg_ref[...])

    def finish(st):
        gate = 1.0 / (1.0 + jnp.exp(-st["a"]))
        o_ref[st["rows"], :] = st["h2"] + _rms(st["e"] * gate, pn_ref[...])

    _skewed([combine, project, finish], h1_ref.shape[0], PLE_SUB)


def _ple(h1, y_slabs, gates, p, w, part):
    T = h1.shape[0]
    tm = TOKEN_TILE
    steps = T // COMBINE_PARTS // tm
    off = part * steps
    row = lambda n: pl.BlockSpec((tm, n), lambda i: (i + off, 0))
    slab = y_slabs[0].shape[1]
    gathered = lambda kk: pl.BlockSpec((tm, slab), lambda i: (kk * steps + i, 0))
    weights = [w["ple_w_proj"], w["ple_gate_norm"], w["ple_w_gate"], w["ple_post_norm"]]
    layer = w["layer"]
    p_spec = pl.BlockSpec((1, tm, D_PLE), lambda i: (layer, i + off, 0))
    ins = [h1] + [y for y in y_slabs for _ in range(TOP_K)] + [gates, p, *weights]
    in_specs = ([row(D_MODEL)] + [gathered(kk) for _ in y_slabs for kk in range(TOP_K)] + [row(128), p_spec]
                + [_full(a.shape) for a in weights])
    return pl.pallas_call(
        _ple_kernel, grid=(steps,), in_specs=in_specs,
        out_specs=row(D_MODEL), out_shape=jax.ShapeDtypeStruct((T, D_MODEL), F32),
        input_output_aliases={0: 0}, compiler_params=_cparams(1), name="ple")(*ins)


def _pad_heads(wm, per_head, n_heads=MLA_HEADS):
    kdim = wm.shape[0]
    w3 = wm.reshape(kdim, n_heads, per_head)
    return jnp.pad(w3, ((0, 0), (0, 0), (0, HEAD_PAD - per_head))).reshape(kdim, n_heads * HEAD_PAD)


def _swap_rope_halves(a):
    a3 = a.reshape(a.shape[0], -1, HEAD_PAD)
    half = MLA_ROPE // 2
    x1 = a3[:, :, MLA_NOPE:MLA_NOPE + half]
    x2 = a3[:, :, MLA_NOPE + half:MLA_QK]
    out = jnp.zeros_like(a3).at[:, :, MLA_NOPE:MLA_NOPE + half].set(x2).at[:, :, MLA_NOPE + half:MLA_QK].set(x1)
    return out.reshape(a.shape)


def _layer_params(i, mix_norm, w_in, gla_w_gate, gla_b_gate, gla_out_norm, mla_q_norm, mla_w_uq, mla_kv_norm,
                  mla_w_ukv, mla_qk_q_norm, mla_qk_k_norm, pool_w, pool_scale, w_out, ffn_norm, router_w,
                  router_b, moe_w_gate, moe_b_gate, moe_w_up, moe_b_up, moe_w_down, moe_b_down,
                  ple_w_proj, ple_gate_norm, ple_w_gate, ple_post_norm):
    wi = w_in[i]
    c = np.cumsum((0, 128, 128, 256, 16, 256, 256, 128, 32, 256))
    gq, gk, gv, glow, gr, cq, ckv, krope, upool = [wi[:, c[j]:c[j + 1]] for j in range(9)]
    misc = jnp.concatenate([glow, krope, jnp.zeros((D_MODEL, 128 - 48), F32)], axis=1)
    w_in_p = jnp.concatenate([gq, gk, gv, gr, cq, upool, ckv, misc], axis=1).astype(BF16)
    wgate_p = jnp.zeros((128, GLA_K), F32).at[MISC_GLOW:MISC_GLOW + GLA_GATE_RANK].set(gla_w_gate[i]).astype(BF16)
    ukv = mla_w_ukv[i].reshape(MLA_KV_RANK, MLA_HEADS, MLA_NOPE + MLA_V)
    ukv_k = _pad_heads(ukv[:, :, :MLA_NOPE].reshape(MLA_KV_RANK, MLA_HEADS * MLA_NOPE), MLA_NOPE)
    ukv_v = ukv[:, :, MLA_NOPE:].reshape(MLA_KV_RANK, MLA_W)
    pw = pool_w[i]
    pool_bd = jnp.zeros((POOL_W, POOL_W), F32)
    for g in range(4):
        pool_bd = pool_bd.at[g * 64:(g + 1) * 64, g * 64:(g + 1) * 64].set(pw[g])
    rw = jnp.pad(router_w[i], ((0, 0), (0, 128 - N_EXPERTS)))
    rw_hi = rw.astype(BF16)
    rw_lo = (rw - rw_hi.astype(F32)).astype(BF16)
    row = lambda a: a.reshape(1, -1)
    pad96 = lambda a: jnp.pad(a, (0, HEAD_PAD - MLA_QK)).reshape(1, HEAD_PAD)
    wuq_p = _pad_heads(mla_w_uq[i], MLA_QK)
    gq_p = pad96(mla_qk_q_norm[i] * (MLA_QK ** -0.5 * LOG2E))
    return {
        "mix_norm": row(mix_norm[i]), "w_in": w_in_p, "gla_w_gate": wgate_p, "gla_b_gate": row(gla_b_gate[i]),
        "gla_out_norm": row(jnp.tile(gla_out_norm[i], GLA_HEADS)),
        "mla_q_norm": row(mla_q_norm[i]),
        "mla_w_uq": jnp.concatenate([wuq_p, _swap_rope_halves(wuq_p)], axis=1).astype(BF16),
        "mla_kv_norm": row(mla_kv_norm[i]), "mla_w_ukv_k": ukv_k.astype(BF16), "mla_w_ukv_v": ukv_v.astype(BF16),
        "mla_gq": jnp.concatenate([gq_p, _swap_rope_halves(gq_p)], axis=0), "mla_gk": pad96(mla_qk_k_norm[i]),
        "pool_w": pool_bd.astype(BF16), "pool_scale": row(pool_scale[i]),
        "w_out": w_out[i].astype(BF16), "ffn_norm": row(ffn_norm[i]),
        "router_w": jnp.concatenate([rw_hi, rw_lo], axis=1),
        "router_b": row(jnp.pad(router_b[i], (0, 128 - N_EXPERTS))),
        "layer": i,
        "moe_w_gate": moe_w_gate, "moe_b_gate": moe_b_gate[i].reshape(N_EXPERTS, 1, D_FF),
        "moe_w_up": moe_w_up, "moe_b_up": moe_b_up[i].reshape(N_EXPERTS, 1, D_FF),
        "moe_w_down": moe_w_down, "moe_b_down": moe_b_down[i].reshape(N_EXPERTS, 1, D_MODEL),
        "ple_w_proj": ple_w_proj[i].astype(BF16), "ple_gate_norm": row(ple_gate_norm[i]),
        "ple_w_gate": ple_w_gate[i].astype(BF16), "ple_post_norm": row(ple_post_norm[i]),
    }


def _rope_tables(positions):
    T = positions.size
    inv = ROPE_BASE ** (-jnp.arange(0, MLA_ROPE, 2, dtype=F32) / MLA_ROPE)
    ang = positions.reshape(T, 1).astype(F32) * inv
    cos, sin = jnp.cos(ang), jnp.sin(ang)
    z16 = jnp.zeros((T, 16), F32)
    tail = jnp.zeros((T, HEAD_PAD - MLA_QK), F32)
    c = jnp.concatenate([jnp.ones((T, MLA_NOPE), F32), cos, cos, tail], axis=1)
    s1 = jnp.concatenate([jnp.zeros((T, MLA_NOPE), F32), -sin, z16, tail], axis=1)
    s2 = jnp.concatenate([jnp.zeros((T, MLA_NOPE), F32), z16, sin, tail], axis=1)
    return c, s1, s2


def _route(top_idx, rank, counts, T):
    bm = MOE_BLOCK
    A = T * TOP_K
    padded = (counts + bm - 1) // bm * bm
    pad_end = jnp.cumsum(padded)
    pad_start = pad_end - padded
    experts = jnp.arange(N_EXPERTS, dtype=jnp.int32)
    dest = rank + jnp.sum(jnp.where(top_idx[:, :, None] == experts, pad_start, 0), axis=-1)
    n_blocks = (A + N_EXPERTS * (bm - 1) + bm - 1) // bm
    n_rows = n_blocks * bm
    block_start = jnp.arange(n_blocks, dtype=jnp.int32) * bm
    block_e = jnp.minimum(jnp.sum((pad_end[None, :] <= block_start[:, None]).astype(jnp.int32), axis=1),
                          N_EXPERTS - 1)
    n_used = (pad_end[-1] // bm).astype(jnp.int32).reshape(1)
    return dest, n_rows, block_e, n_used


def _dispatch(hn_slabs, dest, n_rows):
    T, width = hn_slabs[0].shape
    win = DISPATCH_ROWS
    dest_t = dest.T
    mesh = plsc.VectorSubcoreMesh(core_axis_name="core", subcore_axis_name="subcore")

    @functools.partial(pl.kernel, out_type=jax.ShapeDtypeStruct((n_rows, width), hn_slabs[0].dtype), mesh=mesh,
                       scratch_types=[], name="dispatch")
    def scatter_rows(x_hbm, i_hbm, o_hbm):
        def body(x_vmem, i_vmem):
            for kk in range(TOP_K):
                pltpu.sync_copy(x_vmem, o_hbm.at[i_vmem.at[kk]])

        pltpu.emit_pipeline(
            body, grid=(T // win,),
            in_specs=[pl.BlockSpec((win, width), lambda i: (i, 0)), pl.BlockSpec((TOP_K, win), lambda i: (0, i))],
            out_specs=[], core_axis_name=("core", "subcore"),
            dimension_semantics=(pltpu.PARALLEL,))(x_hbm, i_hbm)

    return [scatter_rows(slab, dest_t) for slab in hn_slabs]


def _combine_gather(y_slabs, dest_t):
    n_k, n_tok = dest_t.shape
    win = DISPATCH_ROWS
    width = y_slabs[0].shape[1]
    steps = n_tok // win
    mesh = plsc.VectorSubcoreMesh(core_axis_name="core", subcore_axis_name="subcore")

    @functools.partial(pl.kernel, out_type=jax.ShapeDtypeStruct((n_k * n_tok, width), y_slabs[0].dtype),
                       mesh=mesh, scratch_types=[], name="combine")
    def gather_rows(y_hbm, i_hbm, o_hbm):
        def body(i_vmem, o_vmem):
            pltpu.sync_copy(y_hbm.at[i_vmem.at[0]], o_vmem)

        pltpu.emit_pipeline(
            body, grid=(n_k, steps),
            in_specs=[pl.BlockSpec((1, win), lambda k, i: (k, i))],
            out_specs=[pl.BlockSpec((win, width), lambda k, i: (k * steps + i, 0))],
            core_axis_name=("core", "subcore"),
            dimension_semantics=(pltpu.PARALLEL, pltpu.PARALLEL))(i_hbm, o_hbm)

    return [gather_rows(y, dest_t) for y in y_slabs]


def kernel(x, p, positions, mix_norm, w_in, gla_w_gate, gla_b_gate, gla_out_norm, mla_q_norm, mla_w_uq,
           mla_kv_norm, mla_w_ukv, mla_qk_q_norm, mla_qk_k_norm, pool_w, pool_scale, w_out, ffn_norm,
           router_w, router_b, moe_w_gate, moe_b_gate, moe_w_up, moe_b_up, moe_w_down, moe_b_down,
           ple_w_proj, ple_gate_norm, ple_w_gate, ple_post_norm):
    B, S, D = x.shape
    T = B * S
    depth = p.shape[0]
    params = (mix_norm, w_in, gla_w_gate, gla_b_gate, gla_out_norm, mla_q_norm, mla_w_uq, mla_kv_norm,
              mla_w_ukv, mla_qk_q_norm, mla_qk_k_norm, pool_w, pool_scale, w_out, ffn_norm, router_w,
              router_b, moe_w_gate, moe_b_gate, moe_w_up, moe_b_up, moe_w_down, moe_b_down,
              ple_w_proj, ple_gate_norm, ple_w_gate, ple_post_norm)
    rope_c, rope_s1, rope_s2 = _rope_tables(positions)
    p_flat = p.reshape(depth, T, D_PLE)
    h = x.reshape(T, D)
    for i in range(depth):
        w = _layer_params(i, *params)
        y_gla, q, k, v, y_pool = _mix_pre(h, w, rope_c, rope_s1, rope_s2, S)
        y_mla = _attn(q, k, v, B, S)
        h1, hn0, hn1, route, gates, counts = _out_router(h, y_gla, y_mla, y_pool, w)
        dest, n_rows, block_e, n_used = _route(route[:, :TOP_K], route[:, TOP_K:2 * TOP_K],
                                               counts[0, :N_EXPERTS], T)
        ys = _moe(_dispatch([hn0, hn1], dest, n_rows), block_e, n_used, w)
        h = h1
        for part in range(COMBINE_PARTS):
            d = dest[part * (T // COMBINE_PARTS):(part + 1) * (T // COMBINE_PARTS)]
            h = _ple(h, _combine_gather(ys, d.T), gates, p_flat, w, part)
    return h.reshape(B, S, D)
```

```python
import functools

import jax
import jax.numpy as jnp
import numpy as np
from jax import lax
from jax.experimental import pallas as pl
from jax.experimental.pallas import tpu as pltpu
from jax.experimental.pallas import tpu_sc as plsc

F32 = jnp.float32
BF16 = jnp.bfloat16

D_MODEL = 1024
EPS = 1e-6
D_PLE = 256

GLA_HEADS = 4
GLA_DK = 32
GLA_DV = 64
GLA_GATE_RANK = 16
GLA_TAU = 16.0
GLA_CHUNK = 64
GLA_K = GLA_HEADS * GLA_DK
GLA_W = GLA_HEADS * GLA_DV

MLA_HEADS = 8
MLA_Q_RANK = 256
MLA_KV_RANK = 128
MLA_NOPE = 64
MLA_ROPE = 32
MLA_QK = MLA_NOPE + MLA_ROPE
MLA_V = 64
MLA_W = MLA_HEADS * MLA_V
ROPE_BASE = 10000.0
HEAD_PAD = 128
MLA_QK_PAD = MLA_HEADS * HEAD_PAD

POOL_W = 256
POOL_HALO = 16

N_EXPERTS = 32
TOP_K = 4
D_FF = 1024
SWIGLU_LIMIT = 7.0
SWIGLU_ALPHA = 1.702

COL_GQ, COL_GK, COL_GV, COL_GR, COL_CQ, COL_POOL, COL_CKV, COL_MISC = 0, 128, 256, 512, 768, 1024, 1280, 1408
D_IN_PAD = 1536
MISC_GLOW = 0
MISC_ROPE = 16

LOG2E = 1.4426950408889634
TOKEN_TILE = 1024
MIX_SUB, ROUTER_SUB, PLE_SUB = 512, 512, 256
ATTN_TILE = 2048
ATTN_SUB = 512
MOE_BLOCK = 1024
MOE_CAST_ROWS = 256
COMBINE_PARTS = 2
DISPATCH_ROWS = 128
DISPATCH_SLABS = 2
VMEM_LIMIT = 56 * 1024 * 1024
NEG_BIG = -1e30


def _cparams(n_axes):
    return pltpu.CompilerParams(dimension_semantics=("arbitrary",) * n_axes, vmem_limit_bytes=VMEM_LIMIT)


def _rms(x, g):
    return x * lax.rsqrt(jnp.mean(x * x, axis=-1, keepdims=True) + EPS) * g


def _dot(a, b):
    return jnp.dot(a, b, preferred_element_type=F32)


def _dot_nt(a, b):
    return lax.dot_general(a, b, (((1,), (1,)), ((), ())), preferred_element_type=F32)


def _dot_tn(a, b):
    return lax.dot_general(a, b, (((0,), (0,)), ((), ())), preferred_element_type=F32)


def _split3(x):
    hi = x.astype(BF16)
    r = x - hi.astype(F32)
    mid = r.astype(BF16)
    lo = (r - mid.astype(F32)).astype(BF16)
    return hi, mid, lo


def _split2(x):
    hi = x.astype(BF16)
    lo = (x - hi.astype(F32)).astype(BF16)
    return hi, lo


def _pack_bf16_pairs(x):
    m = x.shape[1] // 2
    bits = lax.bitcast_convert_type(x.astype(BF16).astype(F32), jnp.uint32)
    return (bits[:, :m] >> 16) | (bits[:, m:] & jnp.uint32(0xFFFF0000))


def _unpack_bf16_pairs(w):
    lo = lax.bitcast_convert_type(w << 16, F32)
    hi = lax.bitcast_convert_type(w & jnp.uint32(0xFFFF0000), F32)
    return lo, hi


def _skewed(stages, n_rows, sub):
    states = [{"rows": slice(r0, r0 + sub)} for r0 in range(0, n_rows, sub)]
    for step in range(len(states) + len(stages) - 1):
        for s, stage in enumerate(stages):
            t = step - s
            if 0 <= t < len(states):
                stage(states[t])


def _full(shape):
    nd = len(shape)
    return pl.BlockSpec(shape, lambda *_: (0,) * nd)


def _rope(x, c, s1, s2):
    return x * c + pltpu.roll(x, HEAD_PAD - 16, 1) * s1 + pltpu.roll(x, 16, 1) * s2


def _mix_pre_kernel(h_ref, mixn_ref, win_ref, wgate_ref, bgate_ref, qn_ref, wuq_ref, kvn_ref,
                    wukvk_ref, wukvv_ref, gq_ref, gk_ref, rc_ref, rs1_ref, rs2_ref,
                    wpool_ref, pscale_ref, gn_ref,
                    yg_ref, q_ref, k_ref, v_ref, yp_ref, carry_ref, gla_state_ref, *, tiles_per_seq):
    tm = h_ref.shape[0]
    sub = MIX_SUB
    seq_tile = pl.program_id(0) % tiles_per_seq

    @pl.when(seq_tile == 0)
    def _():
        carry_ref[...] = jnp.zeros_like(carry_ref)
        gla_state_ref[...] = jnp.zeros_like(gla_state_ref)

    gla_consts = _gla_consts()
    gla_state = [gla_state_ref[...]]

    lane = lax.broadcasted_iota(jnp.int32, (sub, HEAD_PAD), 1)
    in_rope = (lane >= MLA_NOPE) & (lane < MLA_QK)
    lane_p = lax.broadcasted_iota(jnp.int32, (sub, POOL_W), 1)
    row_p = lax.broadcasted_iota(jnp.int32, (sub, POOL_W), 0)
    g0, g1, g2 = lane_p < 64, lane_p < 128, lane_p < 192
    win = jnp.where(g0, 2.0, jnp.where(g1, 4.0, jnp.where(g2, 8.0, 16.0)))
    gq, gq_sw, gk = gq_ref[0:1, :], gq_ref[1:2, :], gk_ref[...]

    def norm_in(st):
        st["hn"] = _rms(h_ref[st["rows"], :], mixn_ref[...]).astype(BF16)

    def project_in(st):
        st["z"] = _dot(st["hn"], win_ref[...])

    def norm_latents(st):
        z = st["z"]
        st["cqn"] = _rms(z[:, COL_CQ:COL_CQ + MLA_Q_RANK], qn_ref[...]).astype(BF16)
        st["ckvn"] = _rms(z[:, COL_CKV:COL_CKV + MLA_KV_RANK], kvn_ref[...]).astype(BF16)

    def project_up(st):
        zm = st["z"][:, COL_MISC:COL_MISC + 128]
        st["logit"] = _dot(zm.astype(BF16), wgate_ref[...]) + bgate_ref[...]
        st["qf"] = _dot(st["cqn"], wuq_ref[...])
        st["kn"] = _dot(st["ckvn"], wukvk_ref[...])
        st["v"] = _dot(st["ckvn"], wukvv_ref[...])

    def heads_and_pool(st):
        rows, z, qf, kn, logit = st["rows"], st["z"], st["qf"], st["kn"], st["logit"]
        zm = z[:, COL_MISC:COL_MISC + 128]
        st["la"] = (jnp.minimum(logit, 0.0) - jnp.log(1.0 + jnp.exp(-jnp.abs(logit)))) * (1.0 / GLA_TAU)
        v_ref[rows, :] = st["v"].astype(BF16)

        rc, rs1, rs2 = rc_ref[rows, :], rs1_ref[rows, :], rs2_ref[rows, :]
        kr = jnp.where(in_rope, pltpu.roll(zm, MLA_NOPE - MISC_ROPE, 1), 0.0)
        kr_ss = jnp.sum(kr * kr, axis=-1, keepdims=True)
        krr = _rope(kr * gk, rc, rs1, rs2)
        cq = rc * gq
        sq_tab = (rs1 + rs2) * gq_sw
        for hh in range(MLA_HEADS):
            sl = slice(hh * HEAD_PAD, (hh + 1) * HEAD_PAD)
            qh = qf[:, sl]
            qsw = qf[:, MLA_QK_PAD + hh * HEAD_PAD:MLA_QK_PAD + (hh + 1) * HEAD_PAD]
            sq = lax.rsqrt(jnp.sum(qh * qh, axis=-1, keepdims=True) * (1.0 / MLA_QK) + EPS)
            q_ref[rows, sl] = ((qh * cq + qsw * sq_tab) * sq).astype(BF16)
            kh = kn[:, sl]
            sk = lax.rsqrt((jnp.sum(kh * kh, axis=-1, keepdims=True) + kr_ss) * (1.0 / MLA_QK) + EPS)
            k_ref[rows, sl] = (sk * (kh * gk + krr)).astype(BF16)

        u = z[:, COL_POOL:COL_POOL + POOL_W]
        xe = jnp.concatenate([carry_ref[...], u], axis=0)
        carry_ref[...] = u[sub - POOL_HALO:, :]
        s2 = xe + pltpu.roll(xe, 1, 0)
        s4 = s2 + pltpu.roll(s2, 2, 0)
        s8 = s4 + pltpu.roll(s4, 4, 0)
        s16 = s8 + pltpu.roll(s8, 8, 0)
        pooled = jnp.where(g0, s2[POOL_HALO:], jnp.where(g1, s4[POOL_HALO:],
                           jnp.where(g2, s8[POOL_HALO:], s16[POOL_HALO:])))
        cnt = jnp.minimum((seq_tile * tm + rows.start + row_p + 1).astype(F32), win)
        st["d"] = (pooled / cnt - u).astype(BF16)

    def project_pool(st):
        yp_ref[st["rows"], :] = (_dot(st["d"], wpool_ref[...]) * pscale_ref[...]).astype(BF16)

    def gla(st):
        yg_ref[st["rows"], :] = _gla_rows(st["z"], st["la"], gn_ref[...], gla_consts, gla_state)

    _skewed([norm_in, project_in, norm_latents, project_up, heads_and_pool, project_pool, gla], tm, sub)
    gla_state_ref[...] = gla_state[0]


def _mix_pre(h, w, rope_c, rope_s1, rope_s2, seq_len):
    T = h.shape[0]
    tm = TOKEN_TILE
    row = lambda n: pl.BlockSpec((tm, n), lambda i: (i, 0))
    ins = [h, w["mix_norm"], w["w_in"], w["gla_w_gate"], w["gla_b_gate"], w["mla_q_norm"], w["mla_w_uq"],
           w["mla_kv_norm"], w["mla_w_ukv_k"], w["mla_w_ukv_v"], w["mla_gq"], w["mla_gk"],
           rope_c, rope_s1, rope_s2, w["pool_w"], w["pool_scale"], w["gla_out_norm"]]
    in_specs = [row(D_MODEL)] + [_full(a.shape) for a in ins[1:12]] + [row(HEAD_PAD)] * 3 + \
               [_full(a.shape) for a in ins[15:]]
    out_shape = [jax.ShapeDtypeStruct((T, GLA_W), BF16),
                 jax.ShapeDtypeStruct((T, MLA_QK_PAD), BF16), jax.ShapeDtypeStruct((T, MLA_QK_PAD), BF16),
                 jax.ShapeDtypeStruct((T, MLA_W), BF16), jax.ShapeDtypeStruct((T, POOL_W), BF16)]
    out_specs = [row(GLA_W), row(MLA_QK_PAD), row(MLA_QK_PAD), row(MLA_W), row(POOL_W)]
    return pl.pallas_call(
        functools.partial(_mix_pre_kernel, tiles_per_seq=seq_len // tm),
        grid=(T // tm,), in_specs=in_specs, out_specs=out_specs, out_shape=out_shape,
        scratch_shapes=[pltpu.VMEM((POOL_HALO, POOL_W), F32), pltpu.VMEM((GLA_K, GLA_W), F32)],
        compiler_params=_cparams(1), name="mix_pre")(*ins)


def _gla_consts():
    C = GLA_CHUNK
    r_i = lax.broadcasted_iota(jnp.int32, (C, C), 0)
    c_i = lax.broadcasted_iota(jnp.int32, (C, C), 1)
    ar = lax.broadcasted_iota(jnp.int32, (GLA_HEADS * C, C), 0)
    ac = lax.broadcasted_iota(jnp.int32, (GLA_HEADS * C, C), 1)
    sk = lax.broadcasted_iota(jnp.int32, (GLA_K, GLA_W), 0) // GLA_DK
    sv = lax.broadcasted_iota(jnp.int32, (GLA_K, GLA_W), 1) // GLA_DV
    gr = lax.broadcasted_iota(jnp.int32, (GLA_W, GLA_W), 0) // GLA_DV
    gc = lax.broadcasted_iota(jnp.int32, (GLA_W, GLA_W), 1) // GLA_DV
    return {
        "tri": (r_i >= c_i).astype(BF16),
        "ones": jnp.ones((C, GLA_W), BF16),
        "head_k": lax.broadcasted_iota(jnp.int32, (C, GLA_K), 1) // GLA_DK,
        "head_v": lax.broadcasted_iota(jnp.int32, (C, GLA_W), 1) // GLA_DV,
        "causal": (ar % C) >= ac,
        "blockdiag": sk == sv,
        "group": (gr == gc).astype(BF16),
    }


def _gla_rows(z, la, gn, consts, state):
    C = GLA_CHUNK
    tri, ones, head_k, head_v = consts["tri"], consts["ones"], consts["head_k"], consts["head_v"]
    causal, blockdiag = consts["causal"], consts["blockdiag"]
    outs = []

    def log_decay(st):
        la3 = _split3(la[st["rows"], :])
        st["bc"] = _dot(tri, la3[0]) + _dot(tri, la3[1]) + _dot(tri, la3[2])
        st["dsum"] = _dot_tn(la3[0], ones) + _dot_tn(la3[1], ones) + _dot_tn(la3[2], ones)

    def scores(st):
        rows, bc = st["rows"], st["bc"]
        q = z[rows, COL_GQ:COL_GQ + GLA_K] * (GLA_DK ** -0.5)
        k = z[rows, COL_GK:COL_GK + GLA_K]
        b_last = bc[C - 1:C, :]
        q_dec = (q * jnp.exp(bc)).astype(BF16)
        k_dec = (k * jnp.exp(-bc)).astype(BF16)
        st["k_end"] = (k * jnp.exp(b_last - bc)).astype(BF16)
        st["decay"] = jnp.exp(st["dsum"])
        zero = jnp.zeros_like(q_dec)
        qs = jnp.concatenate([jnp.where(head_k == hh, q_dec, zero) for hh in range(GLA_HEADS)], axis=0)
        st["q_dec"] = q_dec
        st["att"] = _dot_nt(qs, k_dec)

    def values(st):
        v = z[st["rows"], COL_GV:COL_GV + GLA_W].astype(BF16)
        st["o_full"] = _dot(jnp.where(causal, st["att"], 0.0).astype(BF16), v)
        st["upd"] = jnp.where(blockdiag, _dot_tn(st["k_end"], v), 0.0)

    def recur(st):
        o_full = st["o_full"]
        o = _dot(st["q_dec"], state[0].astype(BF16))
        for hh in range(GLA_HEADS):
            o = o + jnp.where(head_v == hh, o_full[hh * C:(hh + 1) * C, :], 0.0)
        outs.append(o)
        state[0] = st["decay"] * state[0] + st["upd"]

    _skewed([log_decay, scores, values, recur], z.shape[0], C)

    o = jnp.concatenate(outs, axis=0)
    oo = _split2(o * o)
    ms = (_dot(oo[0], consts["group"]) + _dot(oo[1], consts["group"])) * (1.0 / GLA_DV)
    r = z[:, COL_GR:COL_GR + GLA_W]
    return (o * lax.rsqrt(ms + EPS) * gn * (r / (1.0 + jnp.exp(-r)))).astype(BF16)


def _attn_kernel(q_ref, k_ref, v_ref, o_ref, m_ref, acc_ref):
    tq = q_ref.shape[0]
    ts = ATTN_SUB
    i = pl.program_id(2)
    m_ref[...] = jnp.full_like(m_ref, NEG_BIG)
    acc_ref[...] = jnp.zeros_like(acc_ref)

    def sub_block(hh, start, r0, mask_off):
        hs = slice(hh * HEAD_PAD, (hh + 1) * HEAD_PAD)
        kj = k_ref[pl.ds(start, ts), hs]
        vp = v_ref[pl.ds(start, ts), :]
        if hh == 1:
            vp = jnp.concatenate([vp[:, MLA_V:], vp[:, :MLA_V]], axis=1)
        lane_v = lax.broadcasted_iota(jnp.int32, vp.shape, 1)
        vj = jnp.where(lane_v < MLA_V, vp, jnp.where(lane_v == MLA_V, 1.0, 0.0).astype(BF16))
        s = _dot_nt(q_ref[r0:, hs], kj)
        if mask_off is not None:
            row = lax.broadcasted_iota(jnp.int32, s.shape, 0) + r0
            col = lax.broadcasted_iota(jnp.int32, s.shape, 1) + mask_off
            s = jnp.where(col <= row, s, NEG_BIG)
        m_old = m_ref[hh, r0:, :]
        parts = [s[:, c * 128:(c + 1) * 128] for c in range(ts // 128)]
        m_new = jnp.maximum(m_old, jnp.max(functools.reduce(jnp.maximum, parts), axis=-1, keepdims=True))
        p = jnp.concatenate([jnp.exp2((x - m_new).astype(BF16)) for x in parts], axis=1)
        acc_ref[hh, r0:, :] = jnp.exp2(m_old - m_new) * acc_ref[hh, r0:, :] + _dot(p, vj)
        m_ref[hh, r0:, :] = m_new

    def body(j, carry):
        base = pl.multiple_of(j * tq, tq)
        for sb in range(tq // ts):
            for hh in range(2):
                sub_block(hh, base + sb * ts, 0, None)
        return carry

    lax.fori_loop(0, i, body, 0)
    base = pl.multiple_of(i * tq, tq)
    for sb in range(tq // ts):
        for hh in range(2):
            sub_block(hh, base + sb * ts, sb * ts, sb * ts)
    outs = []
    for hh in range(2):
        a = acc_ref[hh]
        outs.append(a / a[:, MLA_V:MLA_V + 1])
    lane = lax.broadcasted_iota(jnp.int32, (tq, HEAD_PAD), 1)
    o_ref[...] = jnp.where(lane < MLA_V, outs[0], pltpu.roll(outs[1], MLA_V, 1)).astype(BF16)


def _attn(q, k, v, batch, seq_len):
    T = q.shape[0]
    tq = ATTN_TILE
    nq = seq_len // tq
    pairs = MLA_HEADS // 2
    return pl.pallas_call(
        _attn_kernel, grid=(batch, pairs, nq),
        in_specs=[pl.BlockSpec((tq, 2 * HEAD_PAD), lambda b, p, i: (b * nq + i, p)),
                  pl.BlockSpec((seq_len, 2 * HEAD_PAD), lambda b, p, i: (b, p)),
                  pl.BlockSpec((seq_len, 2 * MLA_V), lambda b, p, i: (b, p))],
        out_specs=pl.BlockSpec((tq, 2 * MLA_V), lambda b, p, i: (b * nq + i, p)),
        out_shape=jax.ShapeDtypeStruct((T, MLA_W), BF16),
        scratch_shapes=[pltpu.VMEM((2, tq, HEAD_PAD), F32), pltpu.VMEM((2, tq, HEAD_PAD), F32)],
        compiler_params=_cparams(3), name="attn")(q, k, v)


def _out_router_kernel(h_ref, yg_ref, ym_ref, yp_ref, wo_ref, fn_ref, rw_ref, rb_ref,
                       h1_ref, hn0_ref, hn1_ref, idx_ref, gate_ref, cnt_ref, carry_ref):
    tm = h_ref.shape[0]

    @pl.when(pl.program_id(0) == 0)
    def _():
        carry_ref[...] = jnp.zeros_like(carry_ref)

    sub = ROUTER_SUB
    lane = lax.broadcasted_iota(jnp.int32, (sub, 128), 1)
    r_i = lax.broadcasted_iota(jnp.int32, (sub, sub), 0)
    c_i = lax.broadcasted_iota(jnp.int32, (sub, sub), 1)
    tri = (r_i >= c_i).astype(BF16)
    def project(st):
        rows = st["rows"]
        st["h1"] = (h_ref[rows, :] + _dot(yg_ref[rows, :], wo_ref[0:GLA_W, :])
                    + _dot(ym_ref[rows, :], wo_ref[GLA_W:GLA_W + MLA_W, :])
                    + _dot(yp_ref[rows, :], wo_ref[GLA_W + MLA_W:, :]))

    def normalize(st):
        rows = st["rows"]
        h1_ref[rows, :] = st["h1"]
        hn = _rms(st["h1"], fn_ref[...])
        st["hi"], st["lo"] = _split2(hn)
        packed = _pack_bf16_pairs(hn)
        slab = packed.shape[1] // DISPATCH_SLABS
        hn0_ref[rows, :] = packed[:, :slab]
        hn1_ref[rows, :] = packed[:, slab:]

    def score(st):
        r2 = _dot(st["hi"], rw_ref[...])
        st["logits"] = r2[:, :128] + r2[:, 128:] + _dot(st["lo"], rw_ref[:, 0:128]) + rb_ref[...]

    def select(st):
        rows = st["rows"]
        cur = jnp.where(lane < N_EXPERTS, st["logits"], NEG_BIG)
        idx_out = jnp.zeros((sub, 128), jnp.int32)
        val_out = jnp.zeros((sub, 128), F32)
        chosen = jnp.zeros((sub, 128), F32)
        top0 = None
        sels = []
        for kk in range(TOP_K):
            m = jnp.max(cur, axis=-1, keepdims=True)
            sel = jnp.min(jnp.where(cur == m, lane, 128), axis=-1, keepdims=True)
            if kk == 0:
                top0 = m
            sels.append(sel)
            idx_out = jnp.where(lane == kk, sel, idx_out)
            val_out = jnp.where(lane == kk, jnp.exp(m - top0), val_out)
            chosen = jnp.where(lane == sel, 1.0, chosen)
            cur = jnp.where(lane == sel, NEG_BIG, cur)
        gate_ref[rows, :] = val_out / jnp.sum(val_out, axis=-1, keepdims=True)

        incl = _dot(tri, chosen.astype(BF16))
        before = carry_ref[0:1, :] + incl - chosen
        for kk in range(TOP_K):
            rank = jnp.sum(jnp.where(lane == sels[kk], before, 0.0), axis=-1, keepdims=True)
            idx_out = jnp.where(lane == TOP_K + kk, rank.astype(jnp.int32), idx_out)
        idx_ref[rows, :] = idx_out
        carry_ref[...] = carry_ref[...] + incl[sub - 1:sub, :]

    _skewed([project, normalize, score, select], tm, sub)
    cnt_ref[...] = carry_ref[...].astype(jnp.int32)


def _out_router(h, yg, ym, yp, w):
    T = h.shape[0]
    tm = TOKEN_TILE
    row = lambda n: pl.BlockSpec((tm, n), lambda i: (i, 0))
    slab = D_MODEL // 2 // DISPATCH_SLABS
    ins = [h, yg, ym, yp, w["w_out"], w["ffn_norm"], w["router_w"], w["router_b"]]
    return pl.pallas_call(
        _out_router_kernel, grid=(T // tm,),
        in_specs=[row(D_MODEL), row(GLA_W), row(MLA_W), row(POOL_W)] + [_full(a.shape) for a in ins[4:]],
        out_specs=[row(D_MODEL), row(slab), row(slab), row(128), row(128), _full((8, 128))],
        out_shape=[jax.ShapeDtypeStruct((T, D_MODEL), F32), jax.ShapeDtypeStruct((T, slab), jnp.uint32),
                   jax.ShapeDtypeStruct((T, slab), jnp.uint32),
                   jax.ShapeDtypeStruct((T, 128), jnp.int32), jax.ShapeDtypeStruct((T, 128), F32),
                   jax.ShapeDtypeStruct((8, 128), jnp.int32)],
        scratch_shapes=[pltpu.VMEM((8, 128), F32)],
        compiler_params=_cparams(1), name="out_router")(*ins)


def _moe_kernel(be_ref, nb_ref, x0_ref, x1_ref, wg_ref, bg_ref, wu_ref, bu_ref, wd_ref, bd_ref,
                y0_ref, y1_ref, wg_bf, wu_bf, wd_bf):
    i = pl.program_id(0)
    used = i < nb_ref[0]
    new_expert = (i == 0) | (be_ref[i] != be_ref[jnp.maximum(i - 1, 0)])

    @pl.when(used & new_expert)
    def _():
        for src, dst in ((wg_ref, wg_bf), (wu_ref, wu_bf), (wd_ref, wd_bf)):
            for r in range(0, src.shape[2], MOE_CAST_ROWS):
                dst[r:r + MOE_CAST_ROWS, :] = src[0, 0, r:r + MOE_CAST_ROWS, :].astype(BF16)

    @pl.when(used)
    def _():
        halves = [_unpack_bf16_pairs(r[...]) for r in (x0_ref, x1_ref)]
        x = jnp.concatenate([h[0] for h in halves] + [h[1] for h in halves], axis=1).astype(BF16)
        g = jnp.minimum(_dot(x, wg_bf[...]) + bg_ref[0], SWIGLU_LIMIT)
        up = jnp.clip(_dot(x, wu_bf[...]) + bu_ref[0], -SWIGLU_LIMIT, SWIGLU_LIMIT)
        hb = (up + 1.0) * (g / (1.0 + jnp.exp(-SWIGLU_ALPHA * g)))
        packed = _pack_bf16_pairs(_dot(hb.astype(BF16), wd_bf[...]) + bd_ref[0])
        slab = y0_ref.shape[1]
        y0_ref[...] = packed[:, :slab]
        y1_ref[...] = packed[:, slab:]

    @pl.when(jnp.logical_not(used))
    def _():
        y0_ref[...] = jnp.zeros_like(y0_ref)
        y1_ref[...] = jnp.zeros_like(y1_ref)


def _moe(xs, block_e, n_used, w):
    n_rows, slab = xs[0].shape
    bm = MOE_BLOCK
    layer = w["layer"]
    wspec = lambda shp: pl.BlockSpec((1, 1) + shp, lambda i, be, nb: (layer, be[i], 0, 0))
    bspec = lambda shp: pl.BlockSpec((1,) + shp, lambda i, be, nb: (be[i], 0, 0))
    grid_spec = pltpu.PrefetchScalarGridSpec(
        num_scalar_prefetch=2, grid=(n_rows // bm,),
        in_specs=[pl.BlockSpec((bm, slab), lambda i, be, nb: (i, 0))] * DISPATCH_SLABS + [
                  wspec((D_MODEL, D_FF)), bspec((1, D_FF)), wspec((D_MODEL, D_FF)), bspec((1, D_FF)),
                  wspec((D_FF, D_MODEL)), bspec((1, D_MODEL))],
        out_specs=[pl.BlockSpec((bm, slab), lambda i, be, nb: (i, 0))] * DISPATCH_SLABS,
        scratch_shapes=[pltpu.VMEM((D_MODEL, D_FF), BF16), pltpu.VMEM((D_MODEL, D_FF), BF16),
                        pltpu.VMEM((D_FF, D_MODEL), BF16)])
    return pl.pallas_call(
        _moe_kernel, grid_spec=grid_spec,
        out_shape=[jax.ShapeDtypeStruct((n_rows, slab), jnp.uint32)] * DISPATCH_SLABS,
        compiler_params=_cparams(1), name="moe")(
            block_e, n_used, *xs, w["moe_w_gate"], w["moe_b_gate"], w["moe_w_up"], w["moe_b_up"],
            w["moe_w_down"], w["moe_b_down"])


def _ple_kernel(h1_ref, ya0_ref, ya1_ref, ya2_ref, ya3_ref, yb0_ref, yb1_ref, yb2_ref, yb3_ref, gate_ref, p_ref,
                wple_ref, gn_ref, wpg_ref, pn_ref, o_ref):
    def combine(st):
        rows = st["rows"]
        gates = gate_ref[rows, :]
        h2 = h1_ref[rows, :]
        for kk, (ya_ref, yb_ref) in enumerate(((ya0_ref, yb0_ref), (ya1_ref, yb1_ref), (ya2_ref, yb2_ref),
                                               (ya3_ref, yb3_ref))):
            lo_a, hi_a = _unpack_bf16_pairs(ya_ref[rows, :])
            lo_b, hi_b = _unpack_bf16_pairs(yb_ref[rows, :])
            h2 = h2 + gates[:, kk:kk + 1] * jnp.concatenate([lo_a, lo_b, hi_a, hi_b], axis=1)
        st["h2"] = h2
        st["hn"] = _rms(h2, gn_ref[...]).astype(BF16)

    def project(st):
        st["e"] = _dot(p_ref[0, st["rows"], :].astype(BF16), wple_ref[...])
        st["a"] = _dot(st["hn"], wpg_ref[...])

    def finish(st):
        gate = 1.0 / (1.0 + jnp.exp(-st["a"]))
        o_ref[st["rows"], :] = st["h2"] + _rms(st["e"] * gate, pn_ref[...])

    _skewed([combine, project, finish], h1_ref.shape[0], PLE_SUB)


def _ple(h1, y_slabs, gates, p, w, part):
    T = h1.shape[0]
    tm = TOKEN_TILE
    steps = T // COMBINE_PARTS // tm
    off = part * steps
    row = lambda n: pl.BlockSpec((tm, n), lambda i: (i + off, 0))
    slab = y_slabs[0].shape[1]
    gathered = lambda kk: pl.BlockSpec((tm, slab), lambda i: (kk * steps + i, 0))
    weights = [w["ple_w_proj"], w["ple_gate_norm"], w["ple_w_gate"], w["ple_post_norm"]]
    layer = w["layer"]
    p_spec = pl.BlockSpec((1, tm, D_PLE), lambda i: (layer, i + off, 0))
    ins = [h1] + [y for y in y_slabs for _ in range(TOP_K)] + [gates, p, *weights]
    in_specs = ([row(D_MODEL)] + [gathered(kk) for _ in y_slabs for kk in range(TOP_K)] + [row(128), p_spec]
                + [_full(a.shape) for a in weights])
    return pl.pallas_call(
        _ple_kernel, grid=(steps,), in_specs=in_specs,
        out_specs=row(D_MODEL), out_shape=jax.ShapeDtypeStruct((T, D_MODEL), F32),
        input_output_aliases={0: 0}, compiler_params=_cparams(1), name="ple")(*ins)


def _pad_heads(wm, per_head, n_heads=MLA_HEADS):
    kdim = wm.shape[0]
    w3 = wm.reshape(kdim, n_heads, per_head)
    return jnp.pad(w3, ((0, 0), (0, 0), (0, HEAD_PAD - per_head))).reshape(kdim, n_heads * HEAD_PAD)


def _swap_rope_halves(a):
    a3 = a.reshape(a.shape[0], -1, HEAD_PAD)
    half = MLA_ROPE // 2
    x1 = a3[:, :, MLA_NOPE:MLA_NOPE + half]
    x2 = a3[:, :, MLA_NOPE + half:MLA_QK]
    out = jnp.zeros_like(a3).at[:, :, MLA_NOPE:MLA_NOPE + half].set(x2).at[:, :, MLA_NOPE + half:MLA_QK].set(x1)
    return out.reshape(a.shape)


def _layer_params(i, mix_norm, w_in, gla_w_gate, gla_b_gate, gla_out_norm, mla_q_norm, mla_w_uq, mla_kv_norm,
                  mla_w_ukv, mla_qk_q_norm, mla_qk_k_norm, pool_w, pool_scale, w_out, ffn_norm, router_w,
                  router_b, moe_w_gate, moe_b_gate, moe_w_up, moe_b_up, moe_w_down, moe_b_down,
                  ple_w_proj, ple_gate_norm, ple_w_gate, ple_post_norm):
    wi = w_in[i]
    c = np.cumsum((0, 128, 128, 256, 16, 256, 256, 128, 32, 256))
    gq, gk, gv, glow, gr, cq, ckv, krope, upool = [wi[:, c[j]:c[j + 1]] for j in range(9)]
    misc = jnp.concatenate([glow, krope, jnp.zeros((D_MODEL, 128 - 48), F32)], axis=1)
    w_in_p = jnp.concatenate([gq, gk, gv, gr, cq, upool, ckv, misc], axis=1).astype(BF16)
    wgate_p = jnp.zeros((128, GLA_K), F32).at[MISC_GLOW:MISC_GLOW + GLA_GATE_RANK].set(gla_w_gate[i]).astype(BF16)
    ukv = mla_w_ukv[i].reshape(MLA_KV_RANK, MLA_HEADS, MLA_NOPE + MLA_V)
    ukv_k = _pad_heads(ukv[:, :, :MLA_NOPE].reshape(MLA_KV_RANK, MLA_HEADS * MLA_NOPE), MLA_NOPE)
    ukv_v = ukv[:, :, MLA_NOPE:].reshape(MLA_KV_RANK, MLA_W)
    pw = pool_w[i]
    pool_bd = jnp.zeros((POOL_W, POOL_W), F32)
    for g in range(4):
        pool_bd = pool_bd.at[g * 64:(g + 1) * 64, g * 64:(g + 1) * 64].set(pw[g])
    rw = jnp.pad(router_w[i], ((0, 0), (0, 128 - N_EXPERTS)))
    rw_hi = rw.astype(BF16)
    rw_lo = (rw - rw_hi.astype(F32)).astype(BF16)
    row = lambda a: a.reshape(1, -1)
    pad96 = lambda a: jnp.pad(a, (0, HEAD_PAD - MLA_QK)).reshape(1, HEAD_PAD)
    wuq_p = _pad_heads(mla_w_uq[i], MLA_QK)
    gq_p = pad96(mla_qk_q_norm[i] * (MLA_QK ** -0.5 * LOG2E))
    return {
        "mix_norm": row(mix_norm[i]), "w_in": w_in_p, "gla_w_gate": wgate_p, "gla_b_gate": row(gla_b_gate[i]),
        "gla_out_norm": row(jnp.tile(gla_out_norm[i], GLA_HEADS)),
        "mla_q_norm": row(mla_q_norm[i]),
        "mla_w_uq": jnp.concatenate([wuq_p, _swap_rope_halves(wuq_p)], axis=1).astype(BF16),
        "mla_kv_norm": row(mla_kv_norm[i]), "mla_w_ukv_k": ukv_k.astype(BF16), "mla_w_ukv_v": ukv_v.astype(BF16),
        "mla_gq": jnp.concatenate([gq_p, _swap_rope_halves(gq_p)], axis=0), "mla_gk": pad96(mla_qk_k_norm[i]),
        "pool_w": pool_bd.astype(BF16), "pool_scale": row(pool_scale[i]),
        "w_out": w_out[i].astype(BF16), "ffn_norm": row(ffn_norm[i]),
        "router_w": jnp.concatenate([rw_hi, rw_lo], axis=1),
        "router_b": row(jnp.pad(router_b[i], (0, 128 - N_EXPERTS))),
        "layer": i,
        "moe_w_gate": moe_w_gate, "moe_b_gate": moe_b_gate[i].reshape(N_EXPERTS, 1, D_FF),
        "moe_w_up": moe_w_up, "moe_b_up": moe_b_up[i].reshape(N_EXPERTS, 1, D_FF),
        "moe_w_down": moe_w_down, "moe_b_down": moe_b_down[i].reshape(N_EXPERTS, 1, D_MODEL),
        "ple_w_proj": ple_w_proj[i].astype(BF16), "ple_gate_norm": row(ple_gate_norm[i]),
        "ple_w_gate": ple_w_gate[i].astype(BF16), "ple_post_norm": row(ple_post_norm[i]),
    }


def _rope_tables(positions):
    T = positions.size
    inv = ROPE_BASE ** (-jnp.arange(0, MLA_ROPE, 2, dtype=F32) / MLA_ROPE)
    ang = positions.reshape(T, 1).astype(F32) * inv
    cos, sin = jnp.cos(ang), jnp.sin(ang)
    z16 = jnp.zeros((T, 16), F32)
    tail = jnp.zeros((T, HEAD_PAD - MLA_QK), F32)
    c = jnp.concatenate([jnp.ones((T, MLA_NOPE), F32), cos, cos, tail], axis=1)
    s1 = jnp.concatenate([jnp.zeros((T, MLA_NOPE), F32), -sin, z16, tail], axis=1)
    s2 = jnp.concatenate([jnp.zeros((T, MLA_NOPE), F32), z16, sin, tail], axis=1)
    return c, s1, s2


def _route(top_idx, rank, counts, T):
    bm = MOE_BLOCK
    A = T * TOP_K
    padded = (counts + bm - 1) // bm * bm
    pad_end = jnp.cumsum(padded)
    pad_start = pad_end - padded
    experts = jnp.arange(N_EXPERTS, dtype=jnp.int32)
    dest = rank + jnp.sum(jnp.where(top_idx[:, :, None] == experts, pad_start, 0), axis=-1)
    n_blocks = (A + N_EXPERTS * (bm - 1) + bm - 1) // bm
    n_rows = n_blocks * bm
    block_start = jnp.arange(n_blocks, dtype=jnp.int32) * bm
    block_e = jnp.minimum(jnp.sum((pad_end[None, :] <= block_start[:, None]).astype(jnp.int32), axis=1),
                          N_EXPERTS - 1)
    n_used = (pad_end[-1] // bm).astype(jnp.int32).reshape(1)
    return dest, n_rows, block_e, n_used


def _dispatch(hn_slabs, dest, n_rows):
    T, width = hn_slabs[0].shape
    win = DISPATCH_ROWS
    dest_t = dest.T
    mesh = plsc.VectorSubcoreMesh(core_axis_name="core", subcore_axis_name="subcore")

    @functools.partial(pl.kernel, out_type=jax.ShapeDtypeStruct((n_rows, width), hn_slabs[0].dtype), mesh=mesh,
                       scratch_types=[], name="dispatch")
    def scatter_rows(x_hbm, i_hbm, o_hbm):
        def body(x_vmem, i_vmem):
            for kk in range(TOP_K):
                pltpu.sync_copy(x_vmem, o_hbm.at[i_vmem.at[kk]])

        pltpu.emit_pipeline(
            body, grid=(T // win,),
            in_specs=[pl.BlockSpec((win, width), lambda i: (i, 0)), pl.BlockSpec((TOP_K, win), lambda i: (0, i))],
            out_specs=[], core_axis_name=("core", "subcore"),
            dimension_semantics=(pltpu.PARALLEL,))(x_hbm, i_hbm)

    return [scatter_rows(slab, dest_t) for slab in hn_slabs]


def _combine_gather(y_slabs, dest_t):
    n_k, n_tok = dest_t.shape
    win = DISPATCH_ROWS
    width = y_slabs[0].shape[1]
    steps = n_tok // win
    mesh = plsc.VectorSubcoreMesh(core_axis_name="core", subcore_axis_name="subcore")

    @functools.partial(pl.kernel, out_type=jax.ShapeDtypeStruct((n_k * n_tok, width), y_slabs[0].dtype),
                       mesh=mesh, scratch_types=[], name="combine")
    def gather_rows(y_hbm, i_hbm, o_hbm):
        def body(i_vmem, o_vmem):
            pltpu.sync_copy(y_hbm.at[i_vmem.at[0]], o_vmem)

        pltpu.emit_pipeline(
            body, grid=(n_k, steps),
            in_specs=[pl.BlockSpec((1, win), lambda k, i: (k, i))],
            out_specs=[pl.BlockSpec((win, width), lambda k, i: (k * steps + i, 0))],
            core_axis_name=("core", "subcore"),
            dimension_semantics=(pltpu.PARALLEL, pltpu.PARALLEL))(i_hbm, o_hbm)

    return [gather_rows(y, dest_t) for y in y_slabs]


def kernel(x, p, positions, mix_norm, w_in, gla_w_gate, gla_b_gate, gla_out_norm, mla_q_norm, mla_w_uq,
           mla_kv_norm, mla_w_ukv, mla_qk_q_norm, mla_qk_k_norm, pool_w, pool_scale, w_out, ffn_norm,
           router_w, router_b, moe_w_gate, moe_b_gate, moe_w_up, moe_b_up, moe_w_down, moe_b_down,
           ple_w_proj, ple_gate_norm, ple_w_gate, ple_post_norm):
    B, S, D = x.shape
    T = B * S
    depth = p.shape[0]
    params = (mix_norm, w_in, gla_w_gate, gla_b_gate, gla_out_norm, mla_q_norm, mla_w_uq, mla_kv_norm,
              mla_w_ukv, mla_qk_q_norm, mla_qk_k_norm, pool_w, pool_scale, w_out, ffn_norm, router_w,
              router_b, moe_w_gate, moe_b_gate, moe_w_up, moe_b_up, moe_w_down, moe_b_down,
              ple_w_proj, ple_gate_norm, ple_w_gate, ple_post_norm)
    rope_c, rope_s1, rope_s2 = _rope_tables(positions)
    p_flat = p.reshape(depth, T, D_PLE)
    h = x.reshape(T, D)
    for i in range(depth):
        w = _layer_params(i, *params)
        y_gla, q, k, v, y_pool = _mix_pre(h, w, rope_c, rope_s1, rope_s2, S)
        y_mla = _attn(q, k, v, B, S)
        h1, hn0, hn1, route, gates, counts = _out_router(h, y_gla, y_mla, y_pool, w)
        dest, n_rows, block_e, n_used = _route(route[:, :TOP_K], route[:, TOP_K:2 * TOP_K],
                                               counts[0, :N_EXPERTS], T)
        ys = _moe(_dispatch([hn0, hn1], dest, n_rows), block_e, n_used, w)
        h = h1
        for part in range(COMBINE_PARTS):
            d = dest[part * (T // COMBINE_PARTS):(part + 1) * (T // COMBINE_PARTS)]
            h = _ple(h, _combine_gather(ys, d.T), gates, p_flat, w, part)
    return h.reshape(B, S, D)
```

```python
import functools

import jax
import jax.numpy as jnp
import numpy as np
from jax import lax
from jax.experimental import pallas as pl
from jax.experimental.pallas import tpu as pltpu
from jax.experimental.pallas import tpu_sc as plsc

F32 = jnp.float32
BF16 = jnp.bfloat16

D_MODEL = 1024
EPS = 1e-6
D_PLE = 256

GLA_HEADS = 4
GLA_DK = 32
GLA_DV = 64
GLA_GATE_RANK = 16
GLA_TAU = 16.0
GLA_CHUNK = 64
GLA_K = GLA_HEADS * GLA_DK
GLA_W = GLA_HEADS * GLA_DV

MLA_HEADS = 8
MLA_Q_RANK = 256
MLA_KV_RANK = 128
MLA_NOPE = 64
MLA_ROPE = 32
MLA_QK = MLA_NOPE + MLA_ROPE
MLA_V = 64
MLA_W = MLA_HEADS * MLA_V
ROPE_BASE = 10000.0
HEAD_PAD = 128
MLA_QK_PAD = MLA_HEADS * HEAD_PAD

POOL_W = 256
POOL_HALO = 16

N_EXPERTS = 32
TOP_K = 4
D_FF = 1024
SWIGLU_LIMIT = 7.0
SWIGLU_ALPHA = 1.702

COL_GQ, COL_GK, COL_GV, COL_GR, COL_CQ, COL_POOL, COL_CKV, COL_MISC = 0, 128, 256, 512, 768, 1024, 1280, 1408
D_IN_PAD = 1536
MISC_GLOW = 0
MISC_ROPE = 16

LOG2E = 1.4426950408889634
TOKEN_TILE = 1024
MIX_SUB, ROUTER_SUB, PLE_SUB = 512, 512, 256
ATTN_TILE = 2048
ATTN_SUB = 512
MOE_BLOCK = 1024
MOE_CAST_ROWS = 256
COMBINE_PARTS = 4
DISPATCH_ROWS = 128
DISPATCH_SLABS = 2
VMEM_LIMIT = 56 * 1024 * 1024
NEG_BIG = -1e30


def _cparams(n_axes):
    return pltpu.CompilerParams(dimension_semantics=("arbitrary",) * n_axes, vmem_limit_bytes=VMEM_LIMIT)


def _rms(x, g):
    return x * lax.rsqrt(jnp.mean(x * x, axis=-1, keepdims=True) + EPS) * g


def _dot(a, b):
    return jnp.dot(a, b, preferred_element_type=F32)


def _dot_nt(a, b):
    return lax.dot_general(a, b, (((1,), (1,)), ((), ())), preferred_element_type=F32)


def _dot_tn(a, b):
    return lax.dot_general(a, b, (((0,), (0,)), ((), ())), preferred_element_type=F32)


def _split3(x):
    hi = x.astype(BF16)
    r = x - hi.astype(F32)
    mid = r.astype(BF16)
    lo = (r - mid.astype(F32)).astype(BF16)
    return hi, mid, lo


def _split2(x):
    hi = x.astype(BF16)
    lo = (x - hi.astype(F32)).astype(BF16)
    return hi, lo


def _pack_bf16_pairs(x):
    m = x.shape[1] // 2
    bits = lax.bitcast_convert_type(x.astype(BF16).astype(F32), jnp.uint32)
    return (bits[:, :m] >> 16) | (bits[:, m:] & jnp.uint32(0xFFFF0000))


def _unpack_bf16_pairs(w):
    lo = lax.bitcast_convert_type(w << 16, F32)
    hi = lax.bitcast_convert_type(w & jnp.uint32(0xFFFF0000), F32)
    return lo, hi


def _skewed(stages, n_rows, sub):
    states = [{"rows": slice(r0, r0 + sub)} for r0 in range(0, n_rows, sub)]
    for step in range(len(states) + len(stages) - 1):
        for s, stage in enumerate(stages):
            t = step - s
            if 0 <= t < len(states):
                stage(states[t])


def _full(shape):
    nd = len(shape)
    return pl.BlockSpec(shape, lambda *_: (0,) * nd)


def _rope(x, c, s1, s2):
    return x * c + pltpu.roll(x, HEAD_PAD - 16, 1) * s1 + pltpu.roll(x, 16, 1) * s2


def _mix_pre_kernel(h_ref, mixn_ref, win_ref, wgate_ref, bgate_ref, qn_ref, wuq_ref, kvn_ref,
                    wukvk_ref, wukvv_ref, gq_ref, gk_ref, rc_ref, rs1_ref, rs2_ref,
                    wpool_ref, pscale_ref, gn_ref,
                    yg_ref, q_ref, k_ref, v_ref, yp_ref, carry_ref, gla_state_ref, *, tiles_per_seq):
    tm = h_ref.shape[0]
    sub = MIX_SUB
    seq_tile = pl.program_id(0) % tiles_per_seq

    @pl.when(seq_tile == 0)
    def _():
        carry_ref[...] = jnp.zeros_like(carry_ref)
        gla_state_ref[...] = jnp.zeros_like(gla_state_ref)

    gla_consts = _gla_consts()
    gla_state = [gla_state_ref[...]]

    lane = lax.broadcasted_iota(jnp.int32, (sub, HEAD_PAD), 1)
    in_rope = (lane >= MLA_NOPE) & (lane < MLA_QK)
    lane_p = lax.broadcasted_iota(jnp.int32, (sub, POOL_W), 1)
    row_p = lax.broadcasted_iota(jnp.int32, (sub, POOL_W), 0)
    g0, g1, g2 = lane_p < 64, lane_p < 128, lane_p < 192
    win = jnp.where(g0, 2.0, jnp.where(g1, 4.0, jnp.where(g2, 8.0, 16.0)))
    gq, gq_sw, gk = gq_ref[0:1, :], gq_ref[1:2, :], gk_ref[...]

    def norm_in(st):
        st["hn"] = _rms(h_ref[st["rows"], :], mixn_ref[...]).astype(BF16)

    def project_in(st):
        st["z"] = _dot(st["hn"], win_ref[...])

    def norm_latents(st):
        z = st["z"]
        st["cqn"] = _rms(z[:, COL_CQ:COL_CQ + MLA_Q_RANK], qn_ref[...]).astype(BF16)
        st["ckvn"] = _rms(z[:, COL_CKV:COL_CKV + MLA_KV_RANK], kvn_ref[...]).astype(BF16)

    def project_up(st):
        zm = st["z"][:, COL_MISC:COL_MISC + 128]
        st["logit"] = _dot(zm.astype(BF16), wgate_ref[...]) + bgate_ref[...]
        st["qf"] = _dot(st["cqn"], wuq_ref[...])
        st["kn"] = _dot(st["ckvn"], wukvk_ref[...])
        st["v"] = _dot(st["ckvn"], wukvv_ref[...])

    def heads_and_pool(st):
        rows, z, qf, kn, logit = st["rows"], st["z"], st["qf"], st["kn"], st["logit"]
        zm = z[:, COL_MISC:COL_MISC + 128]
        st["la"] = (jnp.minimum(logit, 0.0) - jnp.log(1.0 + jnp.exp(-jnp.abs(logit)))) * (1.0 / GLA_TAU)
        v_ref[rows, :] = st["v"].astype(BF16)

        rc, rs1, rs2 = rc_ref[rows, :], rs1_ref[rows, :], rs2_ref[rows, :]
        kr = jnp.where(in_rope, pltpu.roll(zm, MLA_NOPE - MISC_ROPE, 1), 0.0)
        kr_ss = jnp.sum(kr * kr, axis=-1, keepdims=True)
        krr = _rope(kr * gk, rc, rs1, rs2)
        cq = rc * gq
        sq_tab = (rs1 + rs2) * gq_sw
        for hh in range(MLA_HEADS):
            sl = slice(hh * HEAD_PAD, (hh + 1) * HEAD_PAD)
            qh = qf[:, sl]
            qsw = qf[:, MLA_QK_PAD + hh * HEAD_PAD:MLA_QK_PAD + (hh + 1) * HEAD_PAD]
            sq = lax.rsqrt(jnp.sum(qh * qh, axis=-1, keepdims=True) * (1.0 / MLA_QK) + EPS)
            q_ref[rows, sl] = ((qh * cq + qsw * sq_tab) * sq).astype(BF16)
            kh = kn[:, sl]
            sk = lax.rsqrt((jnp.sum(kh * kh, axis=-1, keepdims=True) + kr_ss) * (1.0 / MLA_QK) + EPS)
            k_ref[rows, sl] = (sk * (kh * gk + krr)).astype(BF16)

        u = z[:, COL_POOL:COL_POOL + POOL_W]
        xe = jnp.concatenate([carry_ref[...], u], axis=0)
        carry_ref[...] = u[sub - POOL_HALO:, :]
        s2 = xe + pltpu.roll(xe, 1, 0)
        s4 = s2 + pltpu.roll(s2, 2, 0)
        s8 = s4 + pltpu.roll(s4, 4, 0)
        s16 = s8 + pltpu.roll(s8, 8, 0)
        pooled = jnp.where(g0, s2[POOL_HALO:], jnp.where(g1, s4[POOL_HALO:],
                           jnp.where(g2, s8[POOL_HALO:], s16[POOL_HALO:])))
        cnt = jnp.minimum((seq_tile * tm + rows.start + row_p + 1).astype(F32), win)
        st["d"] = (pooled / cnt - u).astype(BF16)

    def project_pool(st):
        yp_ref[st["rows"], :] = (_dot(st["d"], wpool_ref[...]) * pscale_ref[...]).astype(BF16)

    def gla(st):
        yg_ref[st["rows"], :] = _gla_rows(st["z"], st["la"], gn_ref[...], gla_consts, gla_state)

    _skewed([norm_in, project_in, norm_latents, project_up, heads_and_pool, project_pool, gla], tm, sub)
    gla_state_ref[...] = gla_state[0]


def _mix_pre(h, w, rope_c, rope_s1, rope_s2, seq_len):
    T = h.shape[0]
    tm = TOKEN_TILE
    row = lambda n: pl.BlockSpec((tm, n), lambda i: (i, 0))
    ins = [h, w["mix_norm"], w["w_in"], w["gla_w_gate"], w["gla_b_gate"], w["mla_q_norm"], w["mla_w_uq"],
           w["mla_kv_norm"], w["mla_w_ukv_k"], w["mla_w_ukv_v"], w["mla_gq"], w["mla_gk"],
           rope_c, rope_s1, rope_s2, w["pool_w"], w["pool_scale"], w["gla_out_norm"]]
    in_specs = [row(D_MODEL)] + [_full(a.shape) for a in ins[1:12]] + [row(HEAD_PAD)] * 3 + \
               [_full(a.shape) for a in ins[15:]]
    out_shape = [jax.ShapeDtypeStruct((T, GLA_W), BF16),
                 jax.ShapeDtypeStruct((T, MLA_QK_PAD), BF16), jax.ShapeDtypeStruct((T, MLA_QK_PAD), BF16),
                 jax.ShapeDtypeStruct((T, MLA_W), BF16), jax.ShapeDtypeStruct((T, POOL_W), BF16)]
    out_specs = [row(GLA_W), row(MLA_QK_PAD), row(MLA_QK_PAD), row(MLA_W), row(POOL_W)]
    return pl.pallas_call(
        functools.partial(_mix_pre_kernel, tiles_per_seq=seq_len // tm),
        grid=(T // tm,), in_specs=in_specs, out_specs=out_specs, out_shape=out_shape,
        scratch_shapes=[pltpu.VMEM((POOL_HALO, POOL_W), F32), pltpu.VMEM((GLA_K, GLA_W), F32)],
        compiler_params=_cparams(1), name="mix_pre")(*ins)


def _gla_consts():
    C = GLA_CHUNK
    r_i = lax.broadcasted_iota(jnp.int32, (C, C), 0)
    c_i = lax.broadcasted_iota(jnp.int32, (C, C), 1)
    ar = lax.broadcasted_iota(jnp.int32, (GLA_HEADS * C, C), 0)
    ac = lax.broadcasted_iota(jnp.int32, (GLA_HEADS * C, C), 1)
    sk = lax.broadcasted_iota(jnp.int32, (GLA_K, GLA_W), 0) // GLA_DK
    sv = lax.broadcasted_iota(jnp.int32, (GLA_K, GLA_W), 1) // GLA_DV
    gr = lax.broadcasted_iota(jnp.int32, (GLA_W, GLA_W), 0) // GLA_DV
    gc = lax.broadcasted_iota(jnp.int32, (GLA_W, GLA_W), 1) // GLA_DV
    return {
        "tri": (r_i >= c_i).astype(BF16),
        "ones": jnp.ones((C, GLA_W), BF16),
        "head_k": lax.broadcasted_iota(jnp.int32, (C, GLA_K), 1) // GLA_DK,
        "head_v": lax.broadcasted_iota(jnp.int32, (C, GLA_W), 1) // GLA_DV,
        "causal": (ar % C) >= ac,
        "blockdiag": sk == sv,
        "group": (gr == gc).astype(BF16),
    }


def _gla_rows(z, la, gn, consts, state):
    C = GLA_CHUNK
    tri, ones, head_k, head_v = consts["tri"], consts["ones"], consts["head_k"], consts["head_v"]
    causal, blockdiag = consts["causal"], consts["blockdiag"]
    outs = []

    def log_decay(st):
        la3 = _split3(la[st["rows"], :])
        st["bc"] = _dot(tri, la3[0]) + _dot(tri, la3[1]) + _dot(tri, la3[2])
        st["dsum"] = _dot_tn(la3[0], ones) + _dot_tn(la3[1], ones) + _dot_tn(la3[2], ones)

    def scores(st):
        rows, bc = st["rows"], st["bc"]
        q = z[rows, COL_GQ:COL_GQ + GLA_K] * (GLA_DK ** -0.5)
        k = z[rows, COL_GK:COL_GK + GLA_K]
        b_last = bc[C - 1:C, :]
        q_dec = (q * jnp.exp(bc)).astype(BF16)
        k_dec = (k * jnp.exp(-bc)).astype(BF16)
        st["k_end"] = (k * jnp.exp(b_last - bc)).astype(BF16)
        st["decay"] = jnp.exp(st["dsum"])
        zero = jnp.zeros_like(q_dec)
        qs = jnp.concatenate([jnp.where(head_k == hh, q_dec, zero) for hh in range(GLA_HEADS)], axis=0)
        st["q_dec"] = q_dec
        st["att"] = _dot_nt(qs, k_dec)

    def values(st):
        v = z[st["rows"], COL_GV:COL_GV + GLA_W].astype(BF16)
        st["o_full"] = _dot(jnp.where(causal, st["att"], 0.0).astype(BF16), v)
        st["upd"] = jnp.where(blockdiag, _dot_tn(st["k_end"], v), 0.0)

    def recur(st):
        o_full = st["o_full"]
        o = _dot(st["q_dec"], state[0].astype(BF16))
        for hh in range(GLA_HEADS):
            o = o + jnp.where(head_v == hh, o_full[hh * C:(hh + 1) * C, :], 0.0)
        outs.append(o)
        state[0] = st["decay"] * state[0] + st["upd"]

    _skewed([log_decay, scores, values, recur], z.shape[0], C)

    o = jnp.concatenate(outs, axis=0)
    oo = _split2(o * o)
    ms = (_dot(oo[0], consts["group"]) + _dot(oo[1], consts["group"])) * (1.0 / GLA_DV)
    r = z[:, COL_GR:COL_GR + GLA_W]
    return (o * lax.rsqrt(ms + EPS) * gn * (r / (1.0 + jnp.exp(-r)))).astype(BF16)


def _attn_kernel(q_ref, k_ref, v_ref, o_ref, m_ref, acc_ref):
    tq = q_ref.shape[0]
    ts = ATTN_SUB
    i = pl.program_id(2)
    m_ref[...] = jnp.full_like(m_ref, NEG_BIG)
    acc_ref[...] = jnp.zeros_like(acc_ref)

    def sub_block(hh, start, r0, mask_off):
        hs = slice(hh * HEAD_PAD, (hh + 1) * HEAD_PAD)
        kj = k_ref[pl.ds(start, ts), hs]
        vp = v_ref[pl.ds(start, ts), :]
        if hh == 1:
            vp = jnp.concatenate([vp[:, MLA_V:], vp[:, :MLA_V]], axis=1)
        lane_v = lax.broadcasted_iota(jnp.int32, vp.shape, 1)
        vj = jnp.where(lane_v < MLA_V, vp, jnp.where(lane_v == MLA_V, 1.0, 0.0).astype(BF16))
        s = _dot_nt(q_ref[r0:, hs], kj)
        if mask_off is not None:
            row = lax.broadcasted_iota(jnp.int32, s.shape, 0) + r0
            col = lax.broadcasted_iota(jnp.int32, s.shape, 1) + mask_off
            s = jnp.where(col <= row, s, NEG_BIG)
        m_old = m_ref[hh, r0:, :]
        parts = [s[:, c * 128:(c + 1) * 128] for c in range(ts // 128)]
        m_new = jnp.maximum(m_old, jnp.max(functools.reduce(jnp.maximum, parts), axis=-1, keepdims=True))
        p = jnp.concatenate([jnp.exp2((x - m_new).astype(BF16)) for x in parts], axis=1)
        acc_ref[hh, r0:, :] = jnp.exp2(m_old - m_new) * acc_ref[hh, r0:, :] + _dot(p, vj)
        m_ref[hh, r0:, :] = m_new

    def body(j, carry):
        base = pl.multiple_of(j * tq, tq)
        for sb in range(tq // ts):
            for hh in range(2):
                sub_block(hh, base + sb * ts, 0, None)
        return carry

    lax.fori_loop(0, i, body, 0)
    base = pl.multiple_of(i * tq, tq)
    for sb in range(tq // ts):
        for hh in range(2):
            sub_block(hh, base + sb * ts, sb * ts, sb * ts)
    outs = []
    for hh in range(2):
        a = acc_ref[hh]
        outs.append(a / a[:, MLA_V:MLA_V + 1])
    lane = lax.broadcasted_iota(jnp.int32, (tq, HEAD_PAD), 1)
    o_ref[...] = jnp.where(lane < MLA_V, outs[0], pltpu.roll(outs[1], MLA_V, 1)).astype(BF16)


def _attn(q, k, v, batch, seq_len):
    T = q.shape[0]
    tq = ATTN_TILE
    nq = seq_len // tq
    pairs = MLA_HEADS // 2
    return pl.pallas_call(
        _attn_kernel, grid=(batch, pairs, nq),
        in_specs=[pl.BlockSpec((tq, 2 * HEAD_PAD), lambda b, p, i: (b * nq + i, p)),
                  pl.BlockSpec((seq_len, 2 * HEAD_PAD), lambda b, p, i: (b, p)),
                  pl.BlockSpec((seq_len, 2 * MLA_V), lambda b, p, i: (b, p))],
        out_specs=pl.BlockSpec((tq, 2 * MLA_V), lambda b, p, i: (b * nq + i, p)),
        out_shape=jax.ShapeDtypeStruct((T, MLA_W), BF16),
        scratch_shapes=[pltpu.VMEM((2, tq, HEAD_PAD), F32), pltpu.VMEM((2, tq, HEAD_PAD), F32)],
        compiler_params=_cparams(3), name="attn")(q, k, v)


def _out_router_kernel(h_ref, yg_ref, ym_ref, yp_ref, wo_ref, fn_ref, rw_ref, rb_ref,
                       h1_ref, hn0_ref, hn1_ref, idx_ref, gate_ref, cnt_ref, carry_ref):
    tm = h_ref.shape[0]

    @pl.when(pl.program_id(0) == 0)
    def _():
        carry_ref[...] = jnp.zeros_like(carry_ref)

    sub = ROUTER_SUB
    lane = lax.broadcasted_iota(jnp.int32, (sub, 128), 1)
    r_i = lax.broadcasted_iota(jnp.int32, (sub, sub), 0)
    c_i = lax.broadcasted_iota(jnp.int32, (sub, sub), 1)
    tri = (r_i >= c_i).astype(BF16)
    def project(st):
        rows = st["rows"]
        st["h1"] = (h_ref[rows, :] + _dot(yg_ref[rows, :], wo_ref[0:GLA_W, :])
                    + _dot(ym_ref[rows, :], wo_ref[GLA_W:GLA_W + MLA_W, :])
                    + _dot(yp_ref[rows, :], wo_ref[GLA_W + MLA_W:, :]))

    def normalize(st):
        rows = st["rows"]
        h1_ref[rows, :] = st["h1"]
        hn = _rms(st["h1"], fn_ref[...])
        st["hi"], st["lo"] = _split2(hn)
        packed = _pack_bf16_pairs(hn)
        slab = packed.shape[1] // DISPATCH_SLABS
        hn0_ref[rows, :] = packed[:, :slab]
        hn1_ref[rows, :] = packed[:, slab:]

    def score(st):
        r2 = _dot(st["hi"], rw_ref[...])
        st["logits"] = r2[:, :128] + r2[:, 128:] + _dot(st["lo"], rw_ref[:, 0:128]) + rb_ref[...]

    def select(st):
        rows = st["rows"]
        cur = jnp.where(lane < N_EXPERTS, st["logits"], NEG_BIG)
        idx_out = jnp.zeros((sub, 128), jnp.int32)
        val_out = jnp.zeros((sub, 128), F32)
        chosen = jnp.zeros((sub, 128), F32)
        top0 = None
        sels = []
        for kk in range(TOP_K):
            m = jnp.max(cur, axis=-1, keepdims=True)
            sel = jnp.min(jnp.where(cur == m, lane, 128), axis=-1, keepdims=True)
            if kk == 0:
                top0 = m
            sels.append(sel)
            idx_out = jnp.where(lane == kk, sel, idx_out)
            val_out = jnp.where(lane == kk, jnp.exp(m - top0), val_out)
            chosen = jnp.where(lane == sel, 1.0, chosen)
            cur = jnp.where(lane == sel, NEG_BIG, cur)
        gate_ref[rows, :] = val_out / jnp.sum(val_out, axis=-1, keepdims=True)

        incl = _dot(tri, chosen.astype(BF16))
        before = carry_ref[0:1, :] + incl - chosen
        for kk in range(TOP_K):
            rank = jnp.sum(jnp.where(lane == sels[kk], before, 0.0), axis=-1, keepdims=True)
            idx_out = jnp.where(lane == TOP_K + kk, rank.astype(jnp.int32), idx_out)
        idx_ref[rows, :] = idx_out
        carry_ref[...] = carry_ref[...] + incl[sub - 1:sub, :]

    _skewed([project, normalize, score, select], tm, sub)
    cnt_ref[...] = carry_ref[...].astype(jnp.int32)


def _out_router(h, yg, ym, yp, w):
    T = h.shape[0]
    tm = TOKEN_TILE
    row = lambda n: pl.BlockSpec((tm, n), lambda i: (i, 0))
    slab = D_MODEL // 2 // DISPATCH_SLABS
    ins = [h, yg, ym, yp, w["w_out"], w["ffn_norm"], w["router_w"], w["router_b"]]
    return pl.pallas_call(
        _out_router_kernel, grid=(T // tm,),
        in_specs=[row(D_MODEL), row(GLA_W), row(MLA_W), row(POOL_W)] + [_full(a.shape) for a in ins[4:]],
        out_specs=[row(D_MODEL), row(slab), row(slab), row(128), row(128), _full((8, 128))],
        out_shape=[jax.ShapeDtypeStruct((T, D_MODEL), F32), jax.ShapeDtypeStruct((T, slab), jnp.uint32),
                   jax.ShapeDtypeStruct((T, slab), jnp.uint32),
                   jax.ShapeDtypeStruct((T, 128), jnp.int32), jax.ShapeDtypeStruct((T, 128), F32),
                   jax.ShapeDtypeStruct((8, 128), jnp.int32)],
        scratch_shapes=[pltpu.VMEM((8, 128), F32)],
        compiler_params=_cparams(1), name="out_router")(*ins)


def _moe_kernel(be_ref, nb_ref, x0_ref, x1_ref, wg_ref, bg_ref, wu_ref, bu_ref, wd_ref, bd_ref,
                y0_ref, y1_ref, wg_bf, wu_bf, wd_bf):
    i = pl.program_id(0)
    used = i < nb_ref[0]
    new_expert = (i == 0) | (be_ref[i] != be_ref[jnp.maximum(i - 1, 0)])

    @pl.when(used & new_expert)
    def _():
        for src, dst in ((wg_ref, wg_bf), (wu_ref, wu_bf), (wd_ref, wd_bf)):
            for r in range(0, src.shape[2], MOE_CAST_ROWS):
                dst[r:r + MOE_CAST_ROWS, :] = src[0, 0, r:r + MOE_CAST_ROWS, :].astype(BF16)

    @pl.when(used)
    def _():
        halves = [_unpack_bf16_pairs(r[...]) for r in (x0_ref, x1_ref)]
        x = jnp.concatenate([h[0] for h in halves] + [h[1] for h in halves], axis=1).astype(BF16)
        g = jnp.minimum(_dot(x, wg_bf[...]) + bg_ref[0], SWIGLU_LIMIT)
        up = jnp.clip(_dot(x, wu_bf[...]) + bu_ref[0], -SWIGLU_LIMIT, SWIGLU_LIMIT)
        hb = (up + 1.0) * (g / (1.0 + jnp.exp(-SWIGLU_ALPHA * g)))
        packed = _pack_bf16_pairs(_dot(hb.astype(BF16), wd_bf[...]) + bd_ref[0])
        slab = y0_ref.shape[1]
        y0_ref[...] = packed[:, :slab]
        y1_ref[...] = packed[:, slab:]

    @pl.when(jnp.logical_not(used))
    def _():
        y0_ref[...] = jnp.zeros_like(y0_ref)
        y1_ref[...] = jnp.zeros_like(y1_ref)


def _moe(xs, block_e, n_used, w):
    n_rows, slab = xs[0].shape
    bm = MOE_BLOCK
    layer = w["layer"]
    wspec = lambda shp: pl.BlockSpec((1, 1) + shp, lambda i, be, nb: (layer, be[i], 0, 0))
    bspec = lambda shp: pl.BlockSpec((1,) + shp, lambda i, be, nb: (be[i], 0, 0))
    grid_spec = pltpu.PrefetchScalarGridSpec(
        num_scalar_prefetch=2, grid=(n_rows // bm,),
        in_specs=[pl.BlockSpec((bm, slab), lambda i, be, nb: (i, 0))] * DISPATCH_SLABS + [
                  wspec((D_MODEL, D_FF)), bspec((1, D_FF)), wspec((D_MODEL, D_FF)), bspec((1, D_FF)),
                  wspec((D_FF, D_MODEL)), bspec((1, D_MODEL))],
        out_specs=[pl.BlockSpec((bm, slab), lambda i, be, nb: (i, 0))] * DISPATCH_SLABS,
        scratch_shapes=[pltpu.VMEM((D_MODEL, D_FF), BF16), pltpu.VMEM((D_MODEL, D_FF), BF16),
                        pltpu.VMEM((D_FF, D_MODEL), BF16)])
    return pl.pallas_call(
        _moe_kernel, grid_spec=grid_spec,
        out_shape=[jax.ShapeDtypeStruct((n_rows, slab), jnp.uint32)] * DISPATCH_SLABS,
        compiler_params=_cparams(1), name="moe")(
            block_e, n_used, *xs, w["moe_w_gate"], w["moe_b_gate"], w["moe_w_up"], w["moe_b_up"],
            w["moe_w_down"], w["moe_b_down"])


def _ple_kernel(h1_ref, ya0_ref, ya1_ref, ya2_ref, ya3_ref, yb0_ref, yb1_ref, yb2_ref, yb3_ref, gate_ref, p_ref,
                wple_ref, gn_ref, wpg_ref, pn_ref, o_ref):
    def combine(st):
        rows = st["rows"]
        gates = gate_ref[rows, :]
        h2 = h1_ref[rows, :]
        for kk, (ya_ref, yb_ref) in enumerate(((ya0_ref, yb0_ref), (ya1_ref, yb1_ref), (ya2_ref, yb2_ref),
                                               (ya3_ref, yb3_ref))):
            lo_a, hi_a = _unpack_bf16_pairs(ya_ref[rows, :])
            lo_b, hi_b = _unpack_bf16_pairs(yb_ref[rows, :])
            h2 = h2 + gates[:, kk:kk + 1] * jnp.concatenate([lo_a, lo_b, hi_a, hi_b], axis=1)
        st["h2"] = h2
        st["hn"] = _rms(h2, gn_ref[...]).astype(BF16)

    def project(st):
        st["e"] = _dot(p_ref[0, st["rows"], :].astype(BF16), wple_ref[...])
        st["a"] = _dot(st["hn"], wpg_ref[...])

    def finish(st):
        gate = 0.5 * jnp.tanh(0.5 * st["a"]) + 0.5
        o_ref[st["rows"], :] = st["h2"] + _rms(st["e"] * gate, pn_ref[...])

    _skewed([combine, project, finish], h1_ref.shape[0], PLE_SUB)


def _ple(h1, y_slabs, gates, p, w, part):
    T = h1.shape[0]
    tm = TOKEN_TILE
    steps = T // COMBINE_PARTS // tm
    off = part * steps
    row = lambda n: pl.BlockSpec((tm, n), lambda i: (i + off, 0))
    slab = y_slabs[0].shape[1]
    gathered = lambda kk: pl.BlockSpec((tm, slab), lambda i: (kk * steps + i, 0))
    weights = [w["ple_w_proj"], w["ple_gate_norm"], w["ple_w_gate"], w["ple_post_norm"]]
    layer = w["layer"]
    p_spec = pl.BlockSpec((1, tm, D_PLE), lambda i: (layer, i + off, 0))
    ins = [h1] + [y for y in y_slabs for _ in range(TOP_K)] + [gates, p, *weights]
    in_specs = ([row(D_MODEL)] + [gathered(kk) for _ in y_slabs for kk in range(TOP_K)] + [row(128), p_spec]
                + [_full(a.shape) for a in weights])
    return pl.pallas_call(
        _ple_kernel, grid=(steps,), in_specs=in_specs,
        out_specs=row(D_MODEL), out_shape=jax.ShapeDtypeStruct((T, D_MODEL), F32),
        input_output_aliases={0: 0}, compiler_params=_cparams(1), name="ple")(*ins)


def _pad_heads(wm, per_head, n_heads=MLA_HEADS):
    kdim = wm.shape[0]
    w3 = wm.reshape(kdim, n_heads, per_head)
    return jnp.pad(w3, ((0, 0), (0, 0), (0, HEAD_PAD - per_head))).reshape(kdim, n_heads * HEAD_PAD)


def _swap_rope_halves(a):
    a3 = a.reshape(a.shape[0], -1, HEAD_PAD)
    half = MLA_ROPE // 2
    x1 = a3[:, :, MLA_NOPE:MLA_NOPE + half]
    x2 = a3[:, :, MLA_NOPE + half:MLA_QK]
    out = jnp.zeros_like(a3).at[:, :, MLA_NOPE:MLA_NOPE + half].set(x2).at[:, :, MLA_NOPE + half:MLA_QK].set(x1)
    return out.reshape(a.shape)


def _layer_params(i, mix_norm, w_in, gla_w_gate, gla_b_gate, gla_out_norm, mla_q_norm, mla_w_uq, mla_kv_norm,
                  mla_w_ukv, mla_qk_q_norm, mla_qk_k_norm, pool_w, pool_scale, w_out, ffn_norm, router_w,
                  router_b, moe_w_gate, moe_b_gate, moe_w_up, moe_b_up, moe_w_down, moe_b_down,
                  ple_w_proj, ple_gate_norm, ple_w_gate, ple_post_norm):
    wi = w_in[i]
    c = np.cumsum((0, 128, 128, 256, 16, 256, 256, 128, 32, 256))
    gq, gk, gv, glow, gr, cq, ckv, krope, upool = [wi[:, c[j]:c[j + 1]] for j in range(9)]
    misc = jnp.concatenate([glow, krope, jnp.zeros((D_MODEL, 128 - 48), F32)], axis=1)
    w_in_p = jnp.concatenate([gq, gk, gv, gr, cq, upool, ckv, misc], axis=1).astype(BF16)
    wgate_p = jnp.zeros((128, GLA_K), F32).at[MISC_GLOW:MISC_GLOW + GLA_GATE_RANK].set(gla_w_gate[i]).astype(BF16)
    ukv = mla_w_ukv[i].reshape(MLA_KV_RANK, MLA_HEADS, MLA_NOPE + MLA_V)
    ukv_k = _pad_heads(ukv[:, :, :MLA_NOPE].reshape(MLA_KV_RANK, MLA_HEADS * MLA_NOPE), MLA_NOPE)
    ukv_v = ukv[:, :, MLA_NOPE:].reshape(MLA_KV_RANK, MLA_W)
    pw = pool_w[i]
    pool_bd = jnp.zeros((POOL_W, POOL_W), F32)
    for g in range(4):
        pool_bd = pool_bd.at[g * 64:(g + 1) * 64, g * 64:(g + 1) * 64].set(pw[g])
    rw = jnp.pad(router_w[i], ((0, 0), (0, 128 - N_EXPERTS)))
    rw_hi = rw.astype(BF16)
    rw_lo = (rw - rw_hi.astype(F32)).astype(BF16)
    row = lambda a: a.reshape(1, -1)
    pad96 = lambda a: jnp.pad(a, (0, HEAD_PAD - MLA_QK)).reshape(1, HEAD_PAD)
    wuq_p = _pad_heads(mla_w_uq[i], MLA_QK)
    gq_p = pad96(mla_qk_q_norm[i] * (MLA_QK ** -0.5 * LOG2E))
    return {
        "mix_norm": row(mix_norm[i]), "w_in": w_in_p, "gla_w_gate": wgate_p, "gla_b_gate": row(gla_b_gate[i]),
        "gla_out_norm": row(jnp.tile(gla_out_norm[i], GLA_HEADS)),
        "mla_q_norm": row(mla_q_norm[i]),
        "mla_w_uq": jnp.concatenate([wuq_p, _swap_rope_halves(wuq_p)], axis=1).astype(BF16),
        "mla_kv_norm": row(mla_kv_norm[i]), "mla_w_ukv_k": ukv_k.astype(BF16), "mla_w_ukv_v": ukv_v.astype(BF16),
        "mla_gq": jnp.concatenate([gq_p, _swap_rope_halves(gq_p)], axis=0), "mla_gk": pad96(mla_qk_k_norm[i]),
        "pool_w": pool_bd.astype(BF16), "pool_scale": row(pool_scale[i]),
        "w_out": w_out[i].astype(BF16), "ffn_norm": row(ffn_norm[i]),
        "router_w": jnp.concatenate([rw_hi, rw_lo], axis=1),
        "router_b": row(jnp.pad(router_b[i], (0, 128 - N_EXPERTS))),
        "layer": i,
        "moe_w_gate": moe_w_gate, "moe_b_gate": moe_b_gate[i].reshape(N_EXPERTS, 1, D_FF),
        "moe_w_up": moe_w_up, "moe_b_up": moe_b_up[i].reshape(N_EXPERTS, 1, D_FF),
        "moe_w_down": moe_w_down, "moe_b_down": moe_b_down[i].reshape(N_EXPERTS, 1, D_MODEL),
        "ple_w_proj": ple_w_proj[i].astype(BF16), "ple_gate_norm": row(ple_gate_norm[i]),
        "ple_w_gate": ple_w_gate[i].astype(BF16), "ple_post_norm": row(ple_post_norm[i]),
    }


def _rope_tables(positions):
    T = positions.size
    inv = ROPE_BASE ** (-jnp.arange(0, MLA_ROPE, 2, dtype=F32) / MLA_ROPE)
    ang = positions.reshape(T, 1).astype(F32) * inv
    cos, sin = jnp.cos(ang), jnp.sin(ang)
    z16 = jnp.zeros((T, 16), F32)
    tail = jnp.zeros((T, HEAD_PAD - MLA_QK), F32)
    c = jnp.concatenate([jnp.ones((T, MLA_NOPE), F32), cos, cos, tail], axis=1)
    s1 = jnp.concatenate([jnp.zeros((T, MLA_NOPE), F32), -sin, z16, tail], axis=1)
    s2 = jnp.concatenate([jnp.zeros((T, MLA_NOPE), F32), z16, sin, tail], axis=1)
    return c, s1, s2


def _route(top_idx, rank, counts, T):
    bm = MOE_BLOCK
    A = T * TOP_K
    padded = (counts + bm - 1) // bm * bm
    pad_end = jnp.cumsum(padded)
    pad_start = pad_end - padded
    experts = jnp.arange(N_EXPERTS, dtype=jnp.int32)
    dest = rank + jnp.sum(jnp.where(top_idx[:, :, None] == experts, pad_start, 0), axis=-1)
    n_blocks = (A + N_EXPERTS * (bm - 1) + bm - 1) // bm
    n_rows = n_blocks * bm
    block_start = jnp.arange(n_blocks, dtype=jnp.int32) * bm
    block_e = jnp.minimum(jnp.sum((pad_end[None, :] <= block_start[:, None]).astype(jnp.int32), axis=1),
                          N_EXPERTS - 1)
    n_used = (pad_end[-1] // bm).astype(jnp.int32).reshape(1)
    return dest, n_rows, block_e, n_used


def _dispatch(hn_slabs, dest, n_rows):
    T, width = hn_slabs[0].shape
    win = DISPATCH_ROWS
    dest_t = dest.T
    mesh = plsc.VectorSubcoreMesh(core_axis_name="core", subcore_axis_name="subcore")

    @functools.partial(pl.kernel, out_type=jax.ShapeDtypeStruct((n_rows, width), hn_slabs[0].dtype), mesh=mesh,
                       scratch_types=[], name="dispatch")
    def scatter_rows(x_hbm, i_hbm, o_hbm):
        def body(x_vmem, i_vmem):
            for kk in range(TOP_K):
                pltpu.sync_copy(x_vmem, o_hbm.at[i_vmem.at[kk]])

        pltpu.emit_pipeline(
            body, grid=(T // win,),
            in_specs=[pl.BlockSpec((win, width), lambda i: (i, 0)), pl.BlockSpec((TOP_K, win), lambda i: (0, i))],
            out_specs=[], core_axis_name=("core", "subcore"),
            dimension_semantics=(pltpu.PARALLEL,))(x_hbm, i_hbm)

    return [scatter_rows(slab, dest_t) for slab in hn_slabs]


def _combine_gather(y_slabs, dest_t):
    n_k, n_tok = dest_t.shape
    win = DISPATCH_ROWS
    width = y_slabs[0].shape[1]
    steps = n_tok // win
    mesh = plsc.VectorSubcoreMesh(core_axis_name="core", subcore_axis_name="subcore")

    @functools.partial(pl.kernel, out_type=jax.ShapeDtypeStruct((n_k * n_tok, width), y_slabs[0].dtype),
                       mesh=mesh, scratch_types=[], name="combine")
    def gather_rows(y_hbm, i_hbm, o_hbm):
        def body(i_vmem, o_vmem):
            pltpu.sync_copy(y_hbm.at[i_vmem.at[0]], o_vmem)

        pltpu.emit_pipeline(
            body, grid=(n_k, steps),
            in_specs=[pl.BlockSpec((1, win), lambda k, i: (k, i))],
            out_specs=[pl.BlockSpec((win, width), lambda k, i: (k * steps + i, 0))],
            core_axis_name=("core", "subcore"),
            dimension_semantics=(pltpu.PARALLEL, pltpu.PARALLEL))(i_hbm, o_hbm)

    return [gather_rows(y, dest_t) for y in y_slabs]


def kernel(x, p, positions, mix_norm, w_in, gla_w_gate, gla_b_gate, gla_out_norm, mla_q_norm, mla_w_uq,
           mla_kv_norm, mla_w_ukv, mla_qk_q_norm, mla_qk_k_norm, pool_w, pool_scale, w_out, ffn_norm,
           router_w, router_b, moe_w_gate, moe_b_gate, moe_w_up, moe_b_up, moe_w_down, moe_b_down,
           ple_w_proj, ple_gate_norm, ple_w_gate, ple_post_norm):
    B, S, D = x.shape
    T = B * S
    depth = p.shape[0]
    params = (mix_norm, w_in, gla_w_gate, gla_b_gate, gla_out_norm, mla_q_norm, mla_w_uq, mla_kv_norm,
              mla_w_ukv, mla_qk_q_norm, mla_qk_k_norm, pool_w, pool_scale, w_out, ffn_norm, router_w,
              router_b, moe_w_gate, moe_b_gate, moe_w_up, moe_b_up, moe_w_down, moe_b_down,
              ple_w_proj, ple_gate_norm, ple_w_gate, ple_post_norm)
    rope_c, rope_s1, rope_s2 = _rope_tables(positions)
    p_flat = p.reshape(depth, T, D_PLE)
    h = x.reshape(T, D)
    for i in range(depth):
        w = _layer_params(i, *params)
        y_gla, q, k, v, y_pool = _mix_pre(h, w, rope_c, rope_s1, rope_s2, S)
        y_mla = _attn(q, k, v, B, S)
        h1, hn0, hn1, route, gates, counts = _out_router(h, y_gla, y_mla, y_pool, w)
        dest, n_rows, block_e, n_used = _route(route[:, :TOP_K], route[:, TOP_K:2 * TOP_K],
                                               counts[0, :N_EXPERTS], T)
        ys = _moe(_dispatch([hn0, hn1], dest, n_rows), block_e, n_used, w)
        h = h1
        for part in range(COMBINE_PARTS):
            d = dest[part * (T // COMBINE_PARTS):(part + 1) * (T // COMBINE_PARTS)]
            h = _ple(h, _combine_gather(ys, d.T), gates, p_flat, w, part)
    return h.reshape(B, S, D)
```
